```python
import math
import jax, jax.numpy as jnp
from jax import lax
import numpy as np

D_MODEL = 1024
BATCH = 4
SEQ = 8192
DEPTH = 2

ATTN_HEADS = 8
HEAD_DIM = 64
ATTN_WIDTH = ATTN_HEADS * HEAD_DIM
LRU_BLOCKS = 8
LRU_BLOCK_W = 64
LRU_WIDTH = LRU_BLOCKS * LRU_BLOCK_W
MIX_WIDTH = ATTN_WIDTH + LRU_WIDTH
IN_COLS = 3 * ATTN_WIDTH + 2 * LRU_WIDTH

DILATED_PAIRS = ((128, 1), (512, 4), (2048, 16))
Q_BLOCK = 128
ROT_DIM = HEAD_DIM // 4
ROPE_THETA = 500000.0
ATTN_SCALE = 1.0 / math.sqrt(HEAD_DIM)
NEG_INF = -1e30

CONV_W = 4
LRU_C = 8.0

D_FF = 3584
N_EXPERTS = 8
TOP_K = 2
N_DENSE = (DEPTH + 1) // 2
N_MOE = DEPTH // 2

RMS_EPS = 1e-6

kernel_name = "hymba_dilated_rglru_moe_trunk"


def rmsnorm(x, g):
    xf = x.astype(jnp.float32)
    var = jnp.mean(xf * xf, axis=-1, keepdims=True)
    return (xf * lax.rsqrt(var + RMS_EPS) * g.astype(jnp.float32)).astype(x.dtype)


def partial_rotary(t, positions):
    half = ROT_DIM // 2
    inv_freq = ROPE_THETA ** (-jnp.arange(half, dtype=jnp.float32) / half)
    ang = positions.astype(jnp.float32)[..., None] * inv_freq
    cos = jnp.cos(ang)[:, :, None, :]
    sin = jnp.sin(ang)[:, :, None, :]
    tf = t.astype(jnp.float32)
    t1 = tf[..., :half]
    t2 = tf[..., half:ROT_DIM]
    out = jnp.concatenate([t1 * cos - t2 * sin, t2 * cos + t1 * sin, tf[..., ROT_DIM:]], axis=-1)
    return out.astype(t.dtype)


def dilated_branch(q, k, v, window, dilation):
    B, S, H, Dh = q.shape
    L = S // dilation
    w_sub = window // dilation
    nb = -(-L // Q_BLOCK)
    Lp = nb * Q_BLOCK

    def to_sub(t):
        t = t.reshape(B, L, dilation, H, Dh).transpose(0, 2, 1, 3, 4)
        return jnp.pad(t, ((0, 0), (0, 0), (0, Lp - L), (0, 0), (0, 0)))

    qs, ks, vs = to_sub(q), to_sub(k), to_sub(v)
    blk = (B, dilation, nb, Q_BLOCK, H, Dh)
    qb = qs.reshape(blk)

    def band_keys(t):
        prev = jnp.pad(t, ((0, 0), (0, 0), (Q_BLOCK, 0), (0, 0), (0, 0)))[:, :, :Lp]
        return jnp.concatenate([prev.reshape(blk), t.reshape(blk)], axis=3)

    kb, vb = band_keys(ks), band_keys(vs)

    s = jnp.einsum('bgnqhd,bgnkhd->bgnhqk', qb, kb) * ATTN_SCALE
    qi = jnp.arange(Q_BLOCK)[:, None]
    kj = jnp.arange(2 * Q_BLOCK)[None, :]
    delta = Q_BLOCK + qi - kj
    band = (delta >= 0) & (delta <= w_sub)
    blk_id = jnp.arange(nb)[:, None, None]
    valid = band[None] & ((blk_id > 0) | (kj[None] >= Q_BLOCK))
    s = jnp.where(valid[None, None, :, None], s, NEG_INF)

    m = jnp.max(s, axis=-1, keepdims=True)
    p = jnp.exp(s - m)
    den = jnp.sum(p, axis=-1)
    o = jnp.einsum('bgnhqk,bgnkhd->bgnqhd', p, vb)
    o = o / jnp.swapaxes(den, -1, -2)[..., None]
    lse = jnp.swapaxes(m[..., 0] + jnp.log(den), -1, -2)

    o = o.reshape(B, dilation, Lp, H, Dh)[:, :, :L].transpose(0, 2, 1, 3, 4).reshape(B, S, H, Dh)
    lse = lse.reshape(B, dilation, Lp, H)[:, :, :L].transpose(0, 2, 1, 3).reshape(B, S, H)
    return o, lse


def dilated_mixture_attention(q, k, v):
    qf, kf, vf = q.astype(jnp.float32), k.astype(jnp.float32), v.astype(jnp.float32)
    outs, lses = [], []
    for window, dilation in DILATED_PAIRS:
        o, lse = dilated_branch(qf, kf, vf, window, dilation)
        outs.append(o)
        lses.append(lse)
    w = jax.nn.softmax(jnp.stack(lses, axis=0), axis=0)
    o = jnp.sum(w[..., None] * jnp.stack(outs, axis=0), axis=0)
    return o.astype(q.dtype)


def causal_dwconv(u, w, b):
    S = u.shape[1]
    up = jnp.pad(u, ((0, 0), (CONV_W - 1, 0), (0, 0)))
    y = b
    for tap in range(CONV_W):
        y = y + up[:, tap:tap + S] * w[tap]
    return y


def rg_lru(u, w_r, b_r, w_i, b_i, lam):
    B, S, _ = u.shape
    uf = u.astype(jnp.float32)
    ub = uf.reshape(B, S, LRU_BLOCKS, LRU_BLOCK_W)
    r = jax.nn.sigmoid(jnp.einsum('bsgi,gij->bsgj', ub, w_r.astype(jnp.float32)).reshape(B, S, -1)
                       + b_r.astype(jnp.float32))
    i = jax.nn.sigmoid(jnp.einsum('bsgi,gij->bsgj', ub, w_i.astype(jnp.float32)).reshape(B, S, -1)
                       + b_i.astype(jnp.float32))
    log_a = -LRU_C * r * jax.nn.softplus(-lam.astype(jnp.float32))
    a = jnp.exp(log_a)
    mult = jnp.sqrt(-jnp.expm1(2.0 * log_a))
    bterm = mult * i * uf

    def combine(left, right):
        a1, b1 = left
        a2, b2 = right
        return a1 * a2, a2 * b1 + b2

    _, h = lax.associative_scan(combine, (a, bterm), axis=1)
    return h.astype(u.dtype)


def swiglu(t, wg, wu, wd):
    return (jax.nn.silu(t @ wg) * (t @ wu)) @ wd


def moe_ffn(h, router_w, wg, wu, wd):
    B, S, D = h.shape
    t = h.reshape(-1, D)
    logits = t.astype(jnp.float32) @ router_w.astype(jnp.float32)
    vals, idx = lax.top_k(logits, TOP_K)
    probs = jax.nn.softmax(vals, axis=-1)
    gates = jnp.sum(jax.nn.one_hot(idx, N_EXPERTS, dtype=jnp.float32) * probs[..., None], axis=1)
    y = jnp.zeros(t.shape, jnp.float32)
    for e in range(N_EXPERTS):
        y = y + gates[:, e:e + 1] * swiglu(t, wg[e], wu[e], wd[e]).astype(jnp.float32)
    return y.astype(h.dtype).reshape(B, S, D)


def setup_inputs(seed: int = 0) -> dict:
    key = jax.random.key(seed)
    ks = jax.random.split(key, 26)
    f32 = jnp.float32

    def nrm(k, shape, fan_in):
        return jax.random.normal(k, shape, f32) * (fan_in ** -0.5)

    def gain(k, shape):
        return 1.0 + 0.02 * jax.random.normal(k, shape, f32)

    x = jax.random.normal(ks[0], (BATCH, SEQ, D_MODEL), f32)
    offset = jax.random.randint(ks[1], (BATCH, 1), 0, 4096, dtype=jnp.int32)
    positions = (offset + jnp.arange(SEQ, dtype=jnp.int32)[None, :]).astype(jnp.int32)

    u = jax.random.uniform(ks[2], (DEPTH, LRU_WIDTH), f32, 0.9, 0.999)
    a0 = u ** (1.0 / LRU_C)
    lru_lambda = jnp.log(a0) - jnp.log1p(-a0)

    return {
        "x": x,
        "positions": positions,
        "pre_mix_g": gain(ks[3], (DEPTH, D_MODEL)),
        "w_in": nrm(ks[4], (DEPTH, D_MODEL, IN_COLS), D_MODEL),
        "conv_w": nrm(ks[5], (DEPTH, CONV_W, LRU_WIDTH), CONV_W),
        "conv_b": 0.02 * jax.random.normal(ks[6], (DEPTH, LRU_WIDTH), f32),
        "w_rgate": nrm(ks[7], (DEPTH, LRU_BLOCKS, LRU_BLOCK_W, LRU_BLOCK_W), LRU_BLOCK_W),
        "b_rgate": 0.02 * jax.random.normal(ks[8], (DEPTH, LRU_WIDTH), f32),
        "w_igate": nrm(ks[9], (DEPTH, LRU_BLOCKS, LRU_BLOCK_W, LRU_BLOCK_W), LRU_BLOCK_W),
        "b_igate": 0.02 * jax.random.normal(ks[10], (DEPTH, LRU_WIDTH), f32),
        "lru_lambda": lru_lambda,
        "attn_out_g": gain(ks[11], (DEPTH, ATTN_WIDTH)),
        "lru_out_g": gain(ks[12], (DEPTH, LRU_WIDTH)),
        "w_out": nrm(ks[13], (DEPTH, MIX_WIDTH, D_MODEL), MIX_WIDTH),
        "post_mix_g": gain(ks[14], (DEPTH, D_MODEL)),
        "pre_ffn_g": gain(ks[15], (DEPTH, D_MODEL)),
        "post_ffn_g": gain(ks[16], (DEPTH, D_MODEL)),
        "dense_w_gate": nrm(ks[17], (N_DENSE, D_MODEL, D_FF), D_MODEL),
        "dense_w_up": nrm(ks[18], (N_DENSE, D_MODEL, D_FF), D_MODEL),
        "dense_w_down": nrm(ks[19], (N_DENSE, D_FF, D_MODEL), D_FF),
        "router_w": nrm(ks[20], (N_MOE, D_MODEL, N_EXPERTS), D_MODEL),
        "moe_w_gate": nrm(ks[21], (N_MOE, N_EXPERTS, D_MODEL, D_FF), D_MODEL),
        "moe_w_up": nrm(ks[22], (N_MOE, N_EXPERTS, D_MODEL, D_FF), D_MODEL),
        "moe_w_down": nrm(ks[23], (N_MOE, N_EXPERTS, D_FF, D_MODEL), D_FF),
    }


def reference(x, positions, pre_mix_g, w_in, conv_w, conv_b, w_rgate, b_rgate, w_igate, b_igate,
              lru_lambda, attn_out_g, lru_out_g, w_out, post_mix_g, pre_ffn_g, post_ffn_g,
              dense_w_gate, dense_w_up, dense_w_down, router_w, moe_w_gate, moe_w_up, moe_w_down):
    B, S, _ = x.shape
    c1 = ATTN_WIDTH
    c2 = 2 * ATTN_WIDTH
    c3 = 3 * ATTN_WIDTH
    c4 = 3 * ATTN_WIDTH + LRU_WIDTH
    for l in range(DEPTH):
        h = rmsnorm(x, pre_mix_g[l])
        z = h @ w_in[l]
        q = z[..., :c1].reshape(B, S, ATTN_HEADS, HEAD_DIM)
        k = z[..., c1:c2].reshape(B, S, ATTN_HEADS, HEAD_DIM)
        v = z[..., c2:c3].reshape(B, S, ATTN_HEADS, HEAD_DIM)
        u = z[..., c3:c4]
        g = z[..., c4:]

        q = partial_rotary(q, positions)
        k = partial_rotary(k, positions)
        attn = dilated_mixture_attention(q, k, v).reshape(B, S, ATTN_WIDTH)

        uc = causal_dwconv(u, conv_w[l], conv_b[l])
        rec = rg_lru(uc, w_rgate[l], b_rgate[l], w_igate[l], b_igate[l], lru_lambda[l])
        rec = jax.nn.gelu(g) * rec

        mixed = jnp.concatenate([rmsnorm(attn, attn_out_g[l]), rmsnorm(rec, lru_out_g[l])], axis=-1)
        x = x + rmsnorm(mixed @ w_out[l], post_mix_g[l])

        h = rmsnorm(x, pre_ffn_g[l])
        if l % 2 == 0:
            j = l // 2
            f = swiglu(h, dense_w_gate[j], dense_w_up[j], dense_w_down[j])
        else:
            j = l // 2
            f = moe_ffn(h, router_w[j], moe_w_gate[j], moe_w_up[j], moe_w_down[j])
        x = x + rmsnorm(f, post_ffn_g[l])
    return x
```

```python
import functools
import math

import jax
import jax.numpy as jnp
from jax import lax
from jax.experimental import pallas as pl
from jax.experimental.pallas import tpu as pltpu

ATTN_HEADS = 8
HEAD_DIM = 64
ATTN_WIDTH = ATTN_HEADS * HEAD_DIM
LRU_BLOCKS = 8
LRU_BLOCK_W = 64
LRU_WIDTH = LRU_BLOCKS * LRU_BLOCK_W
DILATED_PAIRS = ((128, 1), (512, 4), (2048, 16))
Q_BLOCK = 128
ROT_DIM = HEAD_DIM // 4
ROT_HALF = ROT_DIM // 2
ROPE_THETA = 500000.0
ATTN_SCALE = 1.0 / math.sqrt(HEAD_DIM)
NEG_INF = -1e30
CONV_W = 4
LRU_C = 8.0
TOP_K = 2
RMS_EPS = 1e-6

LANES = 128
SUBLANES = 8
VMEM_LIMIT_BYTES = 56 * 1024 * 1024

F32 = jnp.float32
BF16 = jnp.bfloat16


def _params(semantics):
    return pltpu.CompilerParams(dimension_semantics=semantics,
                                vmem_limit_bytes=VMEM_LIMIT_BYTES)


def _resident(block_shape, index_map):
    return pl.BlockSpec(block_shape, index_map, pipeline_mode=pl.Buffered(1))


def _rms(x, g):
    var = jnp.mean(x * x, axis=-1, keepdims=True)
    return x * lax.rsqrt(var + RMS_EPS) * g


def _in_proj_kernel(x_ref, pos_ref, g_ref, w_ref, rope_ref,
                    q_ref, k_ref, v_ref, u_ref, gate_ref):
    h = _rms(x_ref[...], g_ref[...]).astype(BF16)
    ang = pos_ref[...].astype(F32) * rope_ref[0:1, :]
    cs = jnp.cos(ang)
    sn = jnp.sin(ang)
    s_lo = sn * rope_ref[1:2, :]
    s_hi = sn * rope_ref[2:3, :]

    def rotary(z):
        return (z * cs + pltpu.roll(z, ROT_HALF, 1) * s_lo
                + pltpu.roll(z, LANES - ROT_HALF, 1) * s_hi)

    w = ATTN_WIDTH
    for sec, out in ((0, q_ref), (1, k_ref)):
        z = jnp.dot(h, w_ref[:, sec * w:(sec + 1) * w], preferred_element_type=F32)
        for cb in range(w // LANES):
            sl = slice(cb * LANES, (cb + 1) * LANES)
            out[:, sl] = rotary(z[:, sl]).astype(BF16)
    v_ref[...] = jnp.dot(h, w_ref[:, 2 * w:3 * w], preferred_element_type=F32).astype(BF16)
    u_ref[...] = jnp.dot(h, w_ref[:, 3 * w:3 * w + LRU_WIDTH], preferred_element_type=F32)
    gate_ref[...] = jnp.dot(h, w_ref[:, 3 * w + LRU_WIDTH:], preferred_element_type=F32)


def _rope_table():
    lane = jnp.arange(LANES, dtype=jnp.int32) % HEAD_DIM
    inv_freq = ROPE_THETA ** (-jnp.arange(ROT_HALF, dtype=F32) / ROT_HALF)
    freq = jnp.where(lane < ROT_DIM, inv_freq[lane % ROT_HALF], 0.0)
    lo = jnp.where((lane >= ROT_HALF) & (lane < ROT_DIM), 1.0, 0.0)
    hi = jnp.where(lane < ROT_HALF, -1.0, 0.0)
    return jnp.stack([freq, lo, hi]).astype(F32)


def _in_proj(x, pos, g, w_in, tm):
    t, d = x.shape
    ncol = w_in.shape[1]
    row = lambda i: (i, 0)
    const = lambda i: (0, 0)
    outs = pl.pallas_call(
        _in_proj_kernel,
        grid=(t // tm,),
        in_specs=[
            pl.BlockSpec((tm, d), row),
            pl.BlockSpec((tm, 1), row),
            pl.BlockSpec((1, d), const),
            _resident((d, ncol), const),
            pl.BlockSpec((3, LANES), const),
        ],
        out_specs=[pl.BlockSpec((tm, ATTN_WIDTH), row)] * 3
        + [pl.BlockSpec((tm, LRU_WIDTH), row)] * 2,
        out_shape=[jax.ShapeDtypeStruct((t, ATTN_WIDTH), BF16)] * 3
        + [jax.ShapeDtypeStruct((t, LRU_WIDTH), F32)] * 2,
        compiler_params=_params(("parallel",)),
        name="in_proj",
    )(x, pos, g, w_in, _rope_table())
    return outs


def _attn_kernel(q_ref, kc_ref, vc_ref, kp_ref, vp_ref, o_ref, lse_ref, *, blocks_per_step):
    n = pl.program_id(2)
    lane = lax.broadcasted_iota(jnp.int32, (1, LANES), 1)
    head0 = lane < HEAD_DIM
    row = lax.broadcasted_iota(jnp.int32, (Q_BLOCK, Q_BLOCK), 0)
    col = lax.broadcasted_iota(jnp.int32, (Q_BLOCK, Q_BLOCK), 1)
    cur_ok = col <= row
    nt = (((1,), (1,)), ((), ()))

    for i in range(blocks_per_step):
        sl = slice(i * Q_BLOCK, (i + 1) * Q_BLOCK)
        q = q_ref[sl, :]
        k_cur, v_cur = kc_ref[sl, :], vc_ref[sl, :]
        if i == 0:
            k_prev, v_prev = kp_ref[...], vp_ref[...]
        else:
            psl = slice((i - 1) * Q_BLOCK, i * Q_BLOCK)
            k_prev, v_prev = kc_ref[psl, :], vc_ref[psl, :]
        prev_ok = jnp.logical_and(col >= row, n * blocks_per_step + i > 0)

        outs, lses = [], []
        for hmask in (head0, jnp.logical_not(head0)):
            qh = jnp.where(hmask, q, jnp.zeros_like(q))
            s_c = lax.dot_general(qh, k_cur, nt, preferred_element_type=F32) * ATTN_SCALE
            s_p = lax.dot_general(qh, k_prev, nt, preferred_element_type=F32) * ATTN_SCALE
            s_c = jnp.where(cur_ok, s_c, NEG_INF)
            s_p = jnp.where(prev_ok, s_p, NEG_INF)
            m = jnp.maximum(jnp.max(s_c, axis=-1, keepdims=True),
                            jnp.max(s_p, axis=-1, keepdims=True))
            p_c = jnp.exp(s_c - m)
            p_p = jnp.exp(s_p - m)
            den = jnp.sum(p_c, axis=-1, keepdims=True) + jnp.sum(p_p, axis=-1, keepdims=True)
            o = (jnp.dot(p_p.astype(BF16), v_prev, preferred_element_type=F32)
                 + jnp.dot(p_c.astype(BF16), v_cur, preferred_element_type=F32))
            outs.append(o / den)
            lses.append(jnp.broadcast_to(m + jnp.log(den), (Q_BLOCK, LANES)))
        o_ref[sl, :] = jnp.where(head0, outs[0], outs[1])
        lse_ref[sl, :] = jnp.where(head0, lses[0], lses[1])


def _attn_branch(q, k, v, batch, seq, dilation):
    sub_len = seq // dilation
    rows = min(4 * Q_BLOCK, sub_len)
    assert sub_len % rows == 0 and rows % Q_BLOCK == 0
    bps = rows // Q_BLOCK
    width = dilation * ATTN_WIDTH
    view = lambda a: a.reshape(batch, sub_len, width)
    cur = pl.BlockSpec((None, rows, LANES), lambda b, c, n: (b, n, c))
    prev = pl.BlockSpec((None, Q_BLOCK, LANES),
                        lambda b, c, n: (b, jnp.maximum(n * bps - 1, 0), c))
    o, lse = pl.pallas_call(
        functools.partial(_attn_kernel, blocks_per_step=bps),
        grid=(batch, width // LANES, sub_len // rows),
        in_specs=[cur, cur, cur, prev, prev],
        out_specs=[cur, cur],
        out_shape=[jax.ShapeDtypeStruct((batch, sub_len, width), F32)] * 2,
        compiler_params=_params(("parallel", "parallel", "parallel")),
        name=f"attn_d{dilation}",
    )(view(q), view(k), view(v), view(k), view(v))
    return o.reshape(batch * seq, ATTN_WIDTH), lse.reshape(batch * seq, ATTN_WIDTH)


def _lru_kernel(u_ref, gate_ref, cw_ref, cb_ref, wr_ref, br_ref, wi_ref, bi_ref, lam_ref,
                out_ref, ubuf, hcarry):
    ts, c = u_ref.shape

    @pl.when(pl.program_id(1) == 0)
    def _():
        ubuf[0:SUBLANES, :] = jnp.zeros((SUBLANES, c), F32)
        hcarry[...] = jnp.zeros_like(hcarry)

    ubuf[SUBLANES:SUBLANES + ts, :] = u_ref[...]
    uc = cb_ref[...]
    for tap in range(CONV_W):
        off = SUBLANES - (CONV_W - 1) + tap
        uc = uc + ubuf[off:off + ts, :] * cw_ref[tap:tap + 1, :]
    ubuf[0:SUBLANES, :] = ubuf[ts:ts + SUBLANES, :]

    ucb = uc.astype(BF16)
    r = jax.nn.sigmoid(jnp.dot(ucb, wr_ref[...], preferred_element_type=F32) + br_ref[...])
    ig = jax.nn.sigmoid(jnp.dot(ucb, wi_ref[...], preferred_element_type=F32) + bi_ref[...])
    nl = -lam_ref[...]
    softplus = jnp.maximum(nl, 0.0) + jnp.log1p(jnp.exp(-jnp.abs(nl)))
    log_a = -LRU_C * r * softplus
    a = jnp.exp(log_a)
    mult = jnp.sqrt(1.0 - jnp.exp(2.0 * log_a))
    b = mult * ig * uc

    row = lax.broadcasted_iota(jnp.int32, (ts, c), 0)
    shift = 1
    while shift < ts:
        valid = row >= shift
        a_sh = jnp.where(valid, pltpu.roll(a, shift, 0), 1.0)
        b_sh = jnp.where(valid, pltpu.roll(b, shift, 0), 0.0)
        b = a * b_sh + b
        a = a * a_sh
        shift *= 2
    h = a * hcarry[0:1, :] + b
    hcarry[...] = jnp.broadcast_to(h[ts - 1:ts, :], hcarry.shape)
    out_ref[...] = jax.nn.gelu(gate_ref[...]) * h


def _block_diag(w):
    nb, bw, _ = w.shape
    eye = jnp.eye(nb, dtype=w.dtype)
    return jnp.einsum("gij,gh->gihj", w, eye).reshape(nb * bw, nb * bw)


def _lru(u, gate, conv_w, conv_b, w_r, b_r, w_i, b_i, lam, batch, seq, ts):
    c = LRU_WIDTH
    nblk = seq // ts
    row = lambda b, j: (b * nblk + j, 0)
    const = lambda b, j: (0, 0)
    vec = lambda a: a.reshape(1, c).astype(F32)
    return pl.pallas_call(
        _lru_kernel,
        grid=(batch, nblk),
        in_specs=[
            pl.BlockSpec((ts, c), row),
            pl.BlockSpec((ts, c), row),
            pl.BlockSpec((CONV_W, c), const),
            pl.BlockSpec((1, c), const),
            pl.BlockSpec((c, c), const),
            pl.BlockSpec((1, c), const),
            pl.BlockSpec((c, c), const),
            pl.BlockSpec((1, c), const),
            pl.BlockSpec((1, c), const),
        ],
        out_specs=pl.BlockSpec((ts, c), row),
        out_shape=jax.ShapeDtypeStruct((batch * seq, c), F32),
        scratch_shapes=[pltpu.VMEM((ts + SUBLANES, c), F32), pltpu.VMEM((SUBLANES, c), F32)],
        compiler_params=_params(("parallel", "arbitrary")),
        name="rg_lru",
    )(u, gate, conv_w.astype(F32), vec(conv_b), _block_diag(w_r).astype(BF16), vec(b_r),
      _block_diag(w_i).astype(BF16), vec(b_i), vec(lam))


def _mix_kernel(*refs, n_exp):
    with_router = n_exp > 0
    (o1, o2, o3, l1, l2, l3, rec_ref, x_ref, ga_ref, gr_ref, wo_ref, gpost_ref,
     gpre_ref) = refs[:13]
    if with_router:
        rw_ref, x1_ref, h2_ref, ri_ref, rp_ref = refs[13:]
    else:
        x1_ref, h2_ref = refs[13:]

    la, lb, lc = l1[...], l2[...], l3[...]
    m = jnp.maximum(jnp.maximum(la, lb), lc)
    ea, eb, ec = jnp.exp(la - m), jnp.exp(lb - m), jnp.exp(lc - m)
    attn = (ea * o1[...] + eb * o2[...] + ec * o3[...]) / (ea + eb + ec)

    na = _rms(attn, ga_ref[...]).astype(BF16)
    nr = _rms(rec_ref[...], gr_ref[...]).astype(BF16)
    mixed = (jnp.dot(na, wo_ref[0:ATTN_WIDTH, :], preferred_element_type=F32)
             + jnp.dot(nr, wo_ref[ATTN_WIDTH:, :], preferred_element_type=F32))
    x1 = x_ref[...] + _rms(mixed, gpost_ref[...])
    x1_ref[...] = x1
    hn = _rms(x1, gpre_ref[...])
    h2_ref[...] = hn.astype(h2_ref.dtype)

    if with_router:
        logits = jnp.dot(hn, rw_ref[...], precision=lax.Precision.HIGHEST,
                         preferred_element_type=F32)
        tm = logits.shape[0]
        lane = lax.broadcasted_iota(jnp.int32, (tm, LANES), 1)
        logits = jnp.where(lane < n_exp, logits, -jnp.inf)
        m1 = jnp.max(logits, axis=-1, keepdims=True)
        i1 = jnp.min(jnp.where(logits == m1, lane, n_exp), axis=-1, keepdims=True)
        rest = jnp.where(lane == i1, -jnp.inf, logits)
        m2 = jnp.max(rest, axis=-1, keepdims=True)
        i2 = jnp.min(jnp.where(rest == m2, lane, n_exp), axis=-1, keepdims=True)
        e2 = jnp.exp(m2 - m1)
        p1 = 1.0 / (1.0 + e2)
        p2 = e2 / (1.0 + e2)
        ri_ref[...] = jnp.where(lane == 0, i1, jnp.where(lane == 1, i2, 0))
        rp_ref[...] = jnp.where(lane == 0, p1, jnp.where(lane == 1, p2, 0.0))


def _mix_out(branches, rec, x, ga, gr, w_out, gpost, gpre, router_w, h2_dtype, tm):
    t, d = x.shape
    row = lambda i: (i, 0)
    const = lambda i: (0, 0)
    aw = ATTN_WIDTH
    with_router = router_w is not None
    n_exp = 0
    (o1, l1), (o2, l2), (o3, l3) = branches
    args = [o1, o2, o3, l1, l2, l3, rec, x, ga.reshape(1, aw), gr.reshape(1, LRU_WIDTH),
            w_out.astype(BF16), gpost.reshape(1, d), gpre.reshape(1, d)]
    in_specs = ([pl.BlockSpec((tm, aw), row)] * 7 + [pl.BlockSpec((tm, d), row),
                pl.BlockSpec((1, aw), const), pl.BlockSpec((1, LRU_WIDTH), const),
                _resident(w_out.shape, const), pl.BlockSpec((1, d), const),
                pl.BlockSpec((1, d), const)])
    out_specs = [pl.BlockSpec((tm, d), row), pl.BlockSpec((tm, d), row)]
    out_shape = [jax.ShapeDtypeStruct((t, d), F32), jax.ShapeDtypeStruct((t, d), h2_dtype)]
    if with_router:
        n_exp = router_w.shape[1]
        rw = jnp.pad(router_w.astype(F32), ((0, 0), (0, LANES - n_exp)))
        args.append(rw)
        in_specs.append(pl.BlockSpec(rw.shape, const))
        out_specs += [pl.BlockSpec((tm, LANES), row)] * 2
        out_shape += [jax.ShapeDtypeStruct((t, LANES), jnp.int32),
                      jax.ShapeDtypeStruct((t, LANES), F32)]
    return pl.pallas_call(
        functools.partial(_mix_kernel, n_exp=n_exp),
        grid=(t // tm,),
        in_specs=in_specs,
        out_specs=out_specs,
        out_shape=out_shape,
        compiler_params=_params(("parallel",)),
        name="mix_out_router" if with_router else "mix_out",
    )(*args)


def _ffn_kernel(te_ref, nv_ref, x_ref, wg_ref, wu_ref, wd_ref, o_ref, *, f_chunk):
    i = pl.program_id(0)
    d_ff = wg_ref.shape[1]

    @pl.when(i < nv_ref[0])
    def _():
        x = x_ref[...].astype(BF16)
        for c in range(d_ff // f_chunk):
            sl = slice(c * f_chunk, (c + 1) * f_chunk)
            g = jnp.dot(x, wg_ref[:, sl], preferred_element_type=F32)
            u = jnp.dot(x, wu_ref[:, sl], preferred_element_type=F32)
            a = (g * jax.nn.sigmoid(g) * u).astype(BF16)
            y = jnp.dot(a, wd_ref[sl, :], preferred_element_type=F32)
            if c == 0:
                o_ref[...] = y
            else:
                o_ref[...] += y

    @pl.when(i >= nv_ref[0])
    def _():
        o_ref[...] = jnp.zeros_like(o_ref)


def _ffn(xs, tile_expert, n_valid, wg, wu, wd, tm, f_chunk):
    rows, d = xs.shape
    d_ff = wg.shape[2]
    assert d_ff % f_chunk == 0 and rows % tm == 0
    xmap = lambda i, te, nv: (jnp.minimum(i, nv[0] - 1), 0)
    wmap = lambda i, te, nv: (te[i], 0, 0)
    grid_spec = pltpu.PrefetchScalarGridSpec(
        num_scalar_prefetch=2,
        grid=(rows // tm,),
        in_specs=[
            pl.BlockSpec((tm, d), xmap),
            _resident((None, d, d_ff), wmap),
            _resident((None, d, d_ff), wmap),
            _resident((None, d_ff, d), wmap),
        ],
        out_specs=pl.BlockSpec((tm, d), lambda i, te, nv: (i, 0)),
    )
    return pl.pallas_call(
        functools.partial(_ffn_kernel, f_chunk=f_chunk),
        grid_spec=grid_spec,
        out_shape=jax.ShapeDtypeStruct((rows, d), F32),
        compiler_params=_params(("arbitrary",)),
        name="ffn",
    )(tile_expert, n_valid, xs, wg, wu, wd)


def _post_kernel(y_ref, x_ref, g_ref, o_ref):
    o_ref[...] = x_ref[...] + _rms(y_ref[...], g_ref[...])


def _post_dense(y, x, g, tm):
    t, d = x.shape
    row = lambda i: (i, 0)
    return pl.pallas_call(
        _post_kernel,
        grid=(t // tm,),
        in_specs=[pl.BlockSpec((tm, d), row), pl.BlockSpec((tm, d), row),
                  pl.BlockSpec((1, d), lambda i: (0, 0))],
        out_specs=pl.BlockSpec((tm, d), row),
        out_shape=jax.ShapeDtypeStruct((t, d), F32),
        compiler_params=_params(("parallel",)),
        name="post_dense",
    )(y, x, g.reshape(1, d))


def _gather_kernel(src_ref, *refs):
    del src_ref
    o_ref = refs[-1]
    for j, r in enumerate(refs[:-1]):
        o_ref[j:j + 1, :] = r[...]


def _gather_rows(h, src, rows_per_step):
    t, d = h.shape
    n_out = src.shape[0]
    g = rows_per_step
    in_specs = [pl.BlockSpec((None, 1, d), functools.partial(
        lambda i, s, j: (s[i * g + j], 0, 0), j=j)) for j in range(g)]
    grid_spec = pltpu.PrefetchScalarGridSpec(
        num_scalar_prefetch=1,
        grid=(n_out // g,),
        in_specs=in_specs,
        out_specs=pl.BlockSpec((g, d), lambda i, s: (i, 0)),
    )
    return pl.pallas_call(
        _gather_kernel,
        grid_spec=grid_spec,
        out_shape=jax.ShapeDtypeStruct((n_out, d), h.dtype),
        compiler_params=_params(("arbitrary",)),
        name="moe_gather",
    )(src, *([h.reshape(t, 1, d)] * g))


def _combine_kernel(d1_ref, d2_ref, *refs, tokens):
    del d1_ref, d2_ref
    rows = refs[:2 * tokens]
    rp_ref, x_ref, g_ref, o_ref, ybuf = refs[2 * tokens:]
    for j in range(tokens):
        p1 = rp_ref[j:j + 1, 0:1]
        p2 = rp_ref[j:j + 1, 1:2]
        ybuf[j:j + 1, :] = p1 * rows[2 * j][...] + p2 * rows[2 * j + 1][...]
    o_ref[...] = x_ref[...] + _rms(ybuf[...], g_ref[...])


def _combine(ys, d1, d2, route_p, x, g, tokens):
    t, d = x.shape
    p = ys.shape[0]
    in_specs = []
    for j in range(tokens):
        in_specs.append(pl.BlockSpec((None, 1, d), functools.partial(
            lambda i, a, b, j: (a[i * tokens + j], 0, 0), j=j)))
        in_specs.append(pl.BlockSpec((None, 1, d), functools.partial(
            lambda i, a, b, j: (b[i * tokens + j], 0, 0), j=j)))
    row = lambda i, a, b: (i, 0)
    in_specs += [pl.BlockSpec((tokens, LANES), row), pl.BlockSpec((tokens, d), row),
                 pl.BlockSpec((1, d), lambda i, a, b: (0, 0))]
    grid_spec = pltpu.PrefetchScalarGridSpec(
        num_scalar_prefetch=2,
        grid=(t // tokens,),
        in_specs=in_specs,
        out_specs=pl.BlockSpec((tokens, d), row),
        scratch_shapes=[pltpu.VMEM((tokens, d), F32)],
    )
    ys3 = ys.reshape(p, 1, d)
    return pl.pallas_call(
        functools.partial(_combine_kernel, tokens=tokens),
        grid_spec=grid_spec,
        out_shape=jax.ShapeDtypeStruct((t, d), F32),
        compiler_params=_params(("arbitrary",)),
        name="moe_combine",
    )(d1, d2, *([ys3] * (2 * tokens)), route_p, x, g.reshape(1, d))


def _routing_tables(route_i, n_exp, tm):
    t = route_i.shape[0]
    e_flat = jnp.concatenate([route_i[:, 0], route_i[:, 1]])
    onehot = (e_flat[:, None] == jnp.arange(n_exp, dtype=jnp.int32)[None, :]).astype(jnp.int32)
    counts = jnp.sum(onehot, axis=0)
    rank = jnp.sum((jnp.cumsum(onehot, axis=0) - onehot) * onehot, axis=1)
    padded = ((counts + tm - 1) // tm) * tm
    pad_end = jnp.cumsum(padded)
    pad_start = pad_end - padded
    dest = pad_start[e_flat] + rank

    n_rows = TOP_K * t + n_exp * tm
    n_tiles = n_rows // tm
    tile_first_row = jnp.arange(n_tiles, dtype=jnp.int32) * tm
    tile_expert = jnp.minimum(jnp.searchsorted(pad_end, tile_first_row, side="right"),
                              n_exp - 1).astype(jnp.int32)
    n_valid = (pad_end[-1] // tm).astype(jnp.int32).reshape(1)
    last_expert = tile_expert[jnp.maximum(n_valid[0] - 1, 0)]
    tile_expert = jnp.where(jnp.arange(n_tiles) < n_valid[0], tile_expert, last_expert)

    order = jnp.argsort(e_flat, stable=True).astype(jnp.int32)
    cnt_start = jnp.cumsum(counts) - counts
    p = jnp.arange(n_rows, dtype=jnp.int32)
    pe = tile_expert[p // tm]
    off = p - pad_start[pe]
    ok = (off < counts[pe]) & (p < pad_end[-1])
    sorted_idx = jnp.clip(cnt_start[pe] + off, 0, TOP_K * t - 1)
    src = jnp.where(ok, order[sorted_idx] % t, 0).astype(jnp.int32)
    return src, dest[:t].astype(jnp.int32), dest[t:].astype(jnp.int32), tile_expert, n_valid


def _tile(n, want):
    t = min(n, want)
    while n % t:
        t -= SUBLANES
    return t


def kernel(x, positions, pre_mix_g, w_in, conv_w, conv_b, w_rgate, b_rgate, w_igate, b_igate,
           lru_lambda, attn_out_g, lru_out_g, w_out, post_mix_g, pre_ffn_g, post_ffn_g,
           dense_w_gate, dense_w_up, dense_w_down, router_w, moe_w_gate, moe_w_up, moe_w_down):
    batch, seq, d = x.shape
    depth = w_in.shape[0]
    t = batch * seq
    d_ff = dense_w_gate.shape[-1]
    n_exp = moe_w_gate.shape[1]
    assert w_in.shape[2] == 3 * ATTN_WIDTH + 2 * LRU_WIDTH
    assert seq % (DILATED_PAIRS[-1][1] * Q_BLOCK) == 0

    tm_proj = _tile(t, 512)
    tm_mix = _tile(t, 256)
    tm_ffn = _tile(t, 512)
    ts_lru = _tile(seq, 512)
    f_chunk = 512 if d_ff % 512 == 0 else d_ff

    xf = x.reshape(t, d).astype(F32)
    pos = positions.reshape(t, 1).astype(jnp.int32)

    for l in range(depth):
        q, k, v, u, gate = _in_proj(xf, pos, pre_mix_g[l].reshape(1, d), w_in[l].astype(BF16),
                                    tm_proj)
        branches = [_attn_branch(q, k, v, batch, seq, dil) for _, dil in DILATED_PAIRS]
        rec = _lru(u, gate, conv_w[l], conv_b[l], w_rgate[l], b_rgate[l], w_igate[l], b_igate[l],
                   lru_lambda[l], batch, seq, ts_lru)
        j = l // 2
        if l % 2 == 0:
            x1, h2 = _mix_out(branches, rec, xf, attn_out_g[l], lru_out_g[l], w_out[l],
                              post_mix_g[l], pre_ffn_g[l], None, BF16, tm_mix)
            n_tiles = t // tm_ffn
            y = _ffn(h2, jnp.zeros((n_tiles,), jnp.int32), jnp.full((1,), n_tiles, jnp.int32),
                     dense_w_gate[j][None].astype(BF16), dense_w_up[j][None].astype(BF16),
                     dense_w_down[j][None].astype(BF16), tm_ffn, f_chunk)
            xf = _post_dense(y, x1, post_ffn_g[l], tm_mix)
        else:
            x1, h2, route_i, route_p = _mix_out(
                branches, rec, xf, attn_out_g[l], lru_out_g[l], w_out[l], post_mix_g[l],
                pre_ffn_g[l], router_w[j], F32, tm_mix)
            src, d1, d2, tile_expert, n_valid = _routing_tables(route_i, n_exp, tm_ffn)
            xs = _gather_rows(h2, src, SUBLANES)
            ys = _ffn(xs, tile_expert, n_valid, moe_w_gate[j].astype(BF16),
                      moe_w_up[j].astype(BF16), moe_w_down[j].astype(BF16), tm_ffn, f_chunk)
            xf = _combine(ys, d1, d2, route_p, x1, post_ffn_g[l], SUBLANES)
    return xf.reshape(batch, seq, d).astype(x.dtype)
```

```python
import functools
import math

import jax
import jax.numpy as jnp
from jax import lax
from jax.experimental import pallas as pl
from jax.experimental.pallas import tpu as pltpu

ATTN_HEADS = 8
HEAD_DIM = 64
ATTN_WIDTH = ATTN_HEADS * HEAD_DIM
LRU_BLOCKS = 8
LRU_BLOCK_W = 64
LRU_WIDTH = LRU_BLOCKS * LRU_BLOCK_W
DILATED_PAIRS = ((128, 1), (512, 4), (2048, 16))
Q_BLOCK = 128
ROT_DIM = HEAD_DIM // 4
ROT_HALF = ROT_DIM // 2
ROPE_THETA = 500000.0
ATTN_SCALE = 1.0 / math.sqrt(HEAD_DIM)
NEG_INF = -1e30
CONV_W = 4
LRU_C = 8.0
TOP_K = 2
RMS_EPS = 1e-6

LANES = 128
SUBLANES = 8
SMEM_BLOCK_WORDS = 1024
VMEM_LIMIT_BYTES = 56 * 1024 * 1024

F32 = jnp.float32
BF16 = jnp.bfloat16


def _params(semantics):
    return pltpu.CompilerParams(dimension_semantics=semantics,
                                vmem_limit_bytes=VMEM_LIMIT_BYTES)


def _resident(block_shape, index_map):
    return pl.BlockSpec(block_shape, index_map, pipeline_mode=pl.Buffered(1))


def _rms(x, g):
    var = jnp.mean(x * x, axis=-1, keepdims=True)
    return x * lax.rsqrt(var + RMS_EPS) * g


def _in_proj_kernel(x_ref, pos_ref, g_ref, w_ref, rope_ref,
                    q_ref, k_ref, v_ref, u_ref, gate_ref):
    h = _rms(x_ref[...], g_ref[...]).astype(BF16)
    ang = pos_ref[...].astype(F32) * rope_ref[0:1, :]
    cs = jnp.cos(ang)
    sn = jnp.sin(ang)
    s_lo = sn * rope_ref[1:2, :]
    s_hi = sn * rope_ref[2:3, :]

    def rotary(z):
        return (z * cs + pltpu.roll(z, ROT_HALF, 1) * s_lo
                + pltpu.roll(z, LANES - ROT_HALF, 1) * s_hi)

    w = ATTN_WIDTH
    for sec, out in ((0, q_ref), (1, k_ref)):
        z = jnp.dot(h, w_ref[:, sec * w:(sec + 1) * w], preferred_element_type=F32)
        for cb in range(w // LANES):
            sl = slice(cb * LANES, (cb + 1) * LANES)
            out[:, sl] = rotary(z[:, sl]).astype(BF16)
    v_ref[...] = jnp.dot(h, w_ref[:, 2 * w:3 * w], preferred_element_type=F32).astype(BF16)
    u_ref[...] = jnp.dot(h, w_ref[:, 3 * w:3 * w + LRU_WIDTH], preferred_element_type=F32)
    gate_ref[...] = jnp.dot(h, w_ref[:, 3 * w + LRU_WIDTH:], preferred_element_type=F32)


def _rope_table():
    lane = jnp.arange(LANES, dtype=jnp.int32) % HEAD_DIM
    inv_freq = ROPE_THETA ** (-jnp.arange(ROT_HALF, dtype=F32) / ROT_HALF)
    freq = jnp.where(lane < ROT_DIM, inv_freq[lane % ROT_HALF], 0.0)
    lo = jnp.where((lane >= ROT_HALF) & (lane < ROT_DIM), 1.0, 0.0)
    hi = jnp.where(lane < ROT_HALF, -1.0, 0.0)
    return jnp.stack([freq, lo, hi]).astype(F32)


def _in_proj(x, pos, g, w_in, tm):
    t, d = x.shape
    ncol = w_in.shape[1]
    row = lambda i: (i, 0)
    const = lambda i: (0, 0)
    outs = pl.pallas_call(
        _in_proj_kernel,
        grid=(t // tm,),
        in_specs=[
            pl.BlockSpec((tm, d), row),
            pl.BlockSpec((tm, 1), row),
            pl.BlockSpec((1, d), const),
            _resident((d, ncol), const),
            pl.BlockSpec((3, LANES), const),
        ],
        out_specs=[pl.BlockSpec((tm, ATTN_WIDTH), row)] * 3
        + [pl.BlockSpec((tm, LRU_WIDTH), row)] * 2,
        out_shape=[jax.ShapeDtypeStruct((t, ATTN_WIDTH), BF16)] * 3
        + [jax.ShapeDtypeStruct((t, LRU_WIDTH), F32)] * 2,
        compiler_params=_params(("parallel",)),
        name="in_proj",
    )(x, pos, g, w_in, _rope_table())
    return outs


def _attn_kernel(q_ref, kc_ref, vc_ref, kp_ref, vp_ref, o_ref, lse_ref, *, blocks_per_step):
    n = pl.program_id(2)
    lane = lax.broadcasted_iota(jnp.int32, (1, LANES), 1)
    head0 = lane < HEAD_DIM
    row = lax.broadcasted_iota(jnp.int32, (Q_BLOCK, Q_BLOCK), 0)
    col = lax.broadcasted_iota(jnp.int32, (Q_BLOCK, Q_BLOCK), 1)
    cur_ok = col <= row
    nt = (((1,), (1,)), ((), ()))

    for i in range(blocks_per_step):
        sl = slice(i * Q_BLOCK, (i + 1) * Q_BLOCK)
        q = q_ref[sl, :]
        k_cur, v_cur = kc_ref[sl, :], vc_ref[sl, :]
        if i == 0:
            k_prev, v_prev = kp_ref[...], vp_ref[...]
        else:
            psl = slice((i - 1) * Q_BLOCK, i * Q_BLOCK)
            k_prev, v_prev = kc_ref[psl, :], vc_ref[psl, :]
        prev_ok = jnp.logical_and(col >= row, n * blocks_per_step + i > 0)

        outs, lses = [], []
        for hmask in (head0, jnp.logical_not(head0)):
            qh = jnp.where(hmask, q, jnp.zeros_like(q))
            s_c = lax.dot_general(qh, k_cur, nt, preferred_element_type=F32) * ATTN_SCALE
            s_p = lax.dot_general(qh, k_prev, nt, preferred_element_type=F32) * ATTN_SCALE
            s_c = jnp.where(cur_ok, s_c, NEG_INF)
            s_p = jnp.where(prev_ok, s_p, NEG_INF)
            m = jnp.maximum(jnp.max(s_c, axis=-1, keepdims=True),
                            jnp.max(s_p, axis=-1, keepdims=True))
            p_c = jnp.exp(s_c - m)
            p_p = jnp.exp(s_p - m)
            den = jnp.sum(p_c, axis=-1, keepdims=True) + jnp.sum(p_p, axis=-1, keepdims=True)
            o = (jnp.dot(p_p.astype(BF16), v_prev, preferred_element_type=F32)
                 + jnp.dot(p_c.astype(BF16), v_cur, preferred_element_type=F32))
            outs.append(o / den)
            lses.append(jnp.broadcast_to(m + jnp.log(den), (Q_BLOCK, LANES)))
        o_ref[sl, :] = jnp.where(head0, outs[0], outs[1])
        lse_ref[sl, :] = jnp.where(head0, lses[0], lses[1])


def _attn_branch(q, k, v, batch, seq, dilation):
    sub_len = seq // dilation
    rows = min(4 * Q_BLOCK, sub_len)
    assert sub_len % rows == 0 and rows % Q_BLOCK == 0
    bps = rows // Q_BLOCK
    width = dilation * ATTN_WIDTH
    view = lambda a: a.reshape(batch, sub_len, width)
    cur = pl.BlockSpec((None, rows, LANES), lambda b, c, n: (b, n, c))
    prev = pl.BlockSpec((None, Q_BLOCK, LANES),
                        lambda b, c, n: (b, jnp.maximum(n * bps - 1, 0), c))
    o, lse = pl.pallas_call(
        functools.partial(_attn_kernel, blocks_per_step=bps),
        grid=(batch, width // LANES, sub_len // rows),
        in_specs=[cur, cur, cur, prev, prev],
        out_specs=[cur, cur],
        out_shape=[jax.ShapeDtypeStruct((batch, sub_len, width), F32)] * 2,
        compiler_params=_params(("parallel", "parallel", "parallel")),
        name=f"attn_d{dilation}",
    )(view(q), view(k), view(v), view(k), view(v))
    return o.reshape(batch * seq, ATTN_WIDTH), lse.reshape(batch * seq, ATTN_WIDTH)


def _lru_kernel(u_ref, gate_ref, cw_ref, cb_ref, wr_ref, br_ref, wi_ref, bi_ref, lam_ref,
                out_ref, ubuf, hcarry):
    ts, c = u_ref.shape

    @pl.when(pl.program_id(1) == 0)
    def _():
        ubuf[0:SUBLANES, :] = jnp.zeros((SUBLANES, c), F32)
        hcarry[...] = jnp.zeros_like(hcarry)

    ubuf[SUBLANES:SUBLANES + ts, :] = u_ref[...]
    uc = cb_ref[...]
    for tap in range(CONV_W):
        off = SUBLANES - (CONV_W - 1) + tap
        uc = uc + ubuf[off:off + ts, :] * cw_ref[tap:tap + 1, :]
    ubuf[0:SUBLANES, :] = ubuf[ts:ts + SUBLANES, :]

    ucb = uc.astype(BF16)
    r = jax.nn.sigmoid(jnp.dot(ucb, wr_ref[...], preferred_element_type=F32) + br_ref[...])
    ig = jax.nn.sigmoid(jnp.dot(ucb, wi_ref[...], preferred_element_type=F32) + bi_ref[...])
    nl = -lam_ref[...]
    softplus = jnp.maximum(nl, 0.0) + jnp.log1p(jnp.exp(-jnp.abs(nl)))
    log_a = -LRU_C * r * softplus
    a = jnp.exp(log_a)
    mult = jnp.sqrt(1.0 - jnp.exp(2.0 * log_a))
    b = mult * ig * uc

    row = lax.broadcasted_iota(jnp.int32, (ts, c), 0)
    shift = 1
    while shift < ts:
        valid = row >= shift
        a_sh = jnp.where(valid, pltpu.roll(a, shift, 0), 1.0)
        b_sh = jnp.where(valid, pltpu.roll(b, shift, 0), 0.0)
        b = a * b_sh + b
        a = a * a_sh
        shift *= 2
    h = a * hcarry[0:1, :] + b
    hcarry[...] = jnp.broadcast_to(h[ts - 1:ts, :], hcarry.shape)
    out_ref[...] = jax.nn.gelu(gate_ref[...]) * h


def _block_diag(w):
    nb, bw, _ = w.shape
    eye = jnp.eye(nb, dtype=w.dtype)
    return jnp.einsum("gij,gh->gihj", w, eye).reshape(nb * bw, nb * bw)


def _lru(u, gate, conv_w, conv_b, w_r, b_r, w_i, b_i, lam, batch, seq, ts):
    c = LRU_WIDTH
    nblk = seq // ts
    row = lambda b, j: (b * nblk + j, 0)
    const = lambda b, j: (0, 0)
    vec = lambda a: a.reshape(1, c).astype(F32)
    return pl.pallas_call(
        _lru_kernel,
        grid=(batch, nblk),
        in_specs=[
            pl.BlockSpec((ts, c), row),
            pl.BlockSpec((ts, c), row),
            pl.BlockSpec((CONV_W, c), const),
            pl.BlockSpec((1, c), const),
            pl.BlockSpec((c, c), const),
            pl.BlockSpec((1, c), const),
            pl.BlockSpec((c, c), const),
            pl.BlockSpec((1, c), const),
            pl.BlockSpec((1, c), const),
        ],
        out_specs=pl.BlockSpec((ts, c), row),
        out_shape=jax.ShapeDtypeStruct((batch * seq, c), F32),
        scratch_shapes=[pltpu.VMEM((ts + SUBLANES, c), F32), pltpu.VMEM((SUBLANES, c), F32)],
        compiler_params=_params(("parallel", "arbitrary")),
        name="rg_lru",
    )(u, gate, conv_w.astype(F32), vec(conv_b), _block_diag(w_r).astype(BF16), vec(b_r),
      _block_diag(w_i).astype(BF16), vec(b_i), vec(lam))


def _mix_kernel(*refs, n_exp):
    with_router = n_exp > 0
    (o1, o2, o3, l1, l2, l3, rec_ref, x_ref, ga_ref, gr_ref, wo_ref, gpost_ref,
     gpre_ref) = refs[:13]
    if with_router:
        rw_ref, x1_ref, h2_ref, ri_ref, rp_ref = refs[13:]
    else:
        x1_ref, h2_ref = refs[13:]

    la, lb, lc = l1[...], l2[...], l3[...]
    m = jnp.maximum(jnp.maximum(la, lb), lc)
    ea, eb, ec = jnp.exp(la - m), jnp.exp(lb - m), jnp.exp(lc - m)
    attn = (ea * o1[...] + eb * o2[...] + ec * o3[...]) / (ea + eb + ec)

    na = _rms(attn, ga_ref[...]).astype(BF16)
    nr = _rms(rec_ref[...], gr_ref[...]).astype(BF16)
    mixed = (jnp.dot(na, wo_ref[0:ATTN_WIDTH, :], preferred_element_type=F32)
             + jnp.dot(nr, wo_ref[ATTN_WIDTH:, :], preferred_element_type=F32))
    x1 = x_ref[...] + _rms(mixed, gpost_ref[...])
    x1_ref[...] = x1
    hn = _rms(x1, gpre_ref[...])
    h2_ref[...] = hn.astype(h2_ref.dtype)

    if with_router:
        logits = jnp.dot(hn, rw_ref[...], precision=lax.Precision.HIGHEST,
                         preferred_element_type=F32)
        tm = logits.shape[0]
        lane = lax.broadcasted_iota(jnp.int32, (tm, LANES), 1)
        logits = jnp.where(lane < n_exp, logits, -jnp.inf)
        m1 = jnp.max(logits, axis=-1, keepdims=True)
        i1 = jnp.min(jnp.where(logits == m1, lane, n_exp), axis=-1, keepdims=True)
        rest = jnp.where(lane == i1, -jnp.inf, logits)
        m2 = jnp.max(rest, axis=-1, keepdims=True)
        i2 = jnp.min(jnp.where(rest == m2, lane, n_exp), axis=-1, keepdims=True)
        e2 = jnp.exp(m2 - m1)
        p1 = 1.0 / (1.0 + e2)
        p2 = e2 / (1.0 + e2)
        ri_ref[...] = jnp.where(lane == 0, i1, jnp.where(lane == 1, i2, 0))
        rp_ref[...] = jnp.where(lane == 0, p1, jnp.where(lane == 1, p2, 0.0))


def _mix_out(branches, rec, x, ga, gr, w_out, gpost, gpre, router_w, h2_dtype, tm):
    t, d = x.shape
    row = lambda i: (i, 0)
    const = lambda i: (0, 0)
    aw = ATTN_WIDTH
    with_router = router_w is not None
    n_exp = 0
    (o1, l1), (o2, l2), (o3, l3) = branches
    args = [o1, o2, o3, l1, l2, l3, rec, x, ga.reshape(1, aw), gr.reshape(1, LRU_WIDTH),
            w_out.astype(BF16), gpost.reshape(1, d), gpre.reshape(1, d)]
    in_specs = ([pl.BlockSpec((tm, aw), row)] * 7 + [pl.BlockSpec((tm, d), row),
                pl.BlockSpec((1, aw), const), pl.BlockSpec((1, LRU_WIDTH), const),
                _resident(w_out.shape, const), pl.BlockSpec((1, d), const),
                pl.BlockSpec((1, d), const)])
    out_specs = [pl.BlockSpec((tm, d), row), pl.BlockSpec((tm, d), row)]
    out_shape = [jax.ShapeDtypeStruct((t, d), F32), jax.ShapeDtypeStruct((t, d), h2_dtype)]
    if with_router:
        n_exp = router_w.shape[1]
        rw = jnp.pad(router_w.astype(F32), ((0, 0), (0, LANES - n_exp)))
        args.append(rw)
        in_specs.append(pl.BlockSpec(rw.shape, const))
        out_specs += [pl.BlockSpec((tm, LANES), row)] * 2
        out_shape += [jax.ShapeDtypeStruct((t, LANES), jnp.int32),
                      jax.ShapeDtypeStruct((t, LANES), F32)]
    return pl.pallas_call(
        functools.partial(_mix_kernel, n_exp=n_exp),
        grid=(t // tm,),
        in_specs=in_specs,
        out_specs=out_specs,
        out_shape=out_shape,
        compiler_params=_params(("parallel",)),
        name="mix_out_router" if with_router else "mix_out",
    )(*args)


def _ffn_kernel(te_ref, nv_ref, x_ref, wg_ref, wu_ref, wd_ref, o_ref, *, f_chunk):
    i = pl.program_id(0)
    d_ff = wg_ref.shape[1]

    @pl.when(i < nv_ref[0])
    def _():
        x = x_ref[...].astype(BF16)
        for c in range(d_ff // f_chunk):
            sl = slice(c * f_chunk, (c + 1) * f_chunk)
            g = jnp.dot(x, wg_ref[:, sl], preferred_element_type=F32)
            u = jnp.dot(x, wu_ref[:, sl], preferred_element_type=F32)
            a = (g * jax.nn.sigmoid(g) * u).astype(BF16)
            y = jnp.dot(a, wd_ref[sl, :], preferred_element_type=F32)
            if c == 0:
                o_ref[...] = y
            else:
                o_ref[...] += y

    @pl.when(i >= nv_ref[0])
    def _():
        o_ref[...] = jnp.zeros_like(o_ref)


def _ffn(xs, tile_expert, n_valid, wg, wu, wd, tm, f_chunk):
    rows, d = xs.shape
    d_ff = wg.shape[2]
    assert d_ff % f_chunk == 0 and rows % tm == 0
    xmap = lambda i, te, nv: (jnp.minimum(i, nv[0] - 1), 0)
    wmap = lambda i, te, nv: (te[i], 0, 0)
    grid_spec = pltpu.PrefetchScalarGridSpec(
        num_scalar_prefetch=2,
        grid=(rows // tm,),
        in_specs=[
            pl.BlockSpec((tm, d), xmap),
            _resident((None, d, d_ff), wmap),
            _resident((None, d, d_ff), wmap),
            _resident((None, d_ff, d), wmap),
        ],
        out_specs=pl.BlockSpec((tm, d), lambda i, te, nv: (i, 0)),
    )
    return pl.pallas_call(
        functools.partial(_ffn_kernel, f_chunk=f_chunk),
        grid_spec=grid_spec,
        out_shape=jax.ShapeDtypeStruct((rows, d), F32),
        compiler_params=_params(("arbitrary",)),
        name="ffn",
    )(tile_expert, n_valid, xs, wg, wu, wd)


def _post_kernel(y_ref, x_ref, g_ref, o_ref):
    o_ref[...] = x_ref[...] + _rms(y_ref[...], g_ref[...])


def _post_dense(y, x, g, tm):
    t, d = x.shape
    row = lambda i: (i, 0)
    return pl.pallas_call(
        _post_kernel,
        grid=(t // tm,),
        in_specs=[pl.BlockSpec((tm, d), row), pl.BlockSpec((tm, d), row),
                  pl.BlockSpec((1, d), lambda i: (0, 0))],
        out_specs=pl.BlockSpec((tm, d), row),
        out_shape=jax.ShapeDtypeStruct((t, d), F32),
        compiler_params=_params(("parallel",)),
        name="post_dense",
    )(y, x, g.reshape(1, d))


def _row_copy_kernel(*refs, has_init):
    if has_init:
        src_ref, dst_ref, table_ref, _, out_ref, sem = refs
    else:
        src_ref, dst_ref, table_ref, out_ref, sem = refs

    def row_copy(j):
        return pltpu.make_async_copy(table_ref.at[pl.ds(src_ref[j], 1)],
                                     out_ref.at[pl.ds(dst_ref[j], 1)], sem)

    def start(j, carry):
        row_copy(j).start()
        return carry

    def wait(j, carry):
        row_copy(j).wait()
        return carry

    n = src_ref.shape[0]
    lax.fori_loop(0, n, start, 0, unroll=8)
    lax.fori_loop(0, n, wait, 0, unroll=8)


def _row_copy(table, src, dst, n_out, zero_fill):
    d = table.shape[1]
    n = src.shape[0]
    step = min(n, 4 * SMEM_BLOCK_WORDS)
    assert n % step == 0 and step % SMEM_BLOCK_WORDS == 0
    idx_spec = pl.BlockSpec((step,), lambda i: (i,), memory_space=pltpu.SMEM)
    any_spec = pl.BlockSpec(memory_space=pl.ANY)
    args = [src, dst, table]
    in_specs = [idx_spec, idx_spec, any_spec]
    aliases = {}
    if zero_fill:
        args.append(jnp.zeros((n_out, d), table.dtype))
        in_specs.append(any_spec)
        aliases = {3: 0}
    return pl.pallas_call(
        functools.partial(_row_copy_kernel, has_init=zero_fill),
        grid=(n // step,),
        in_specs=in_specs,
        out_specs=any_spec,
        out_shape=jax.ShapeDtypeStruct((n_out, d), table.dtype),
        scratch_shapes=[pltpu.SemaphoreType.DMA(())],
        input_output_aliases=aliases,
        compiler_params=_params(("arbitrary",)),
        name="moe_scatter_rows" if zero_fill else "moe_gather_rows",
    )(*args)


def _post_moe_kernel(ya_ref, yb_ref, rp_ref, x_ref, g_ref, o_ref):
    y = rp_ref[:, 0:1] * ya_ref[...] + rp_ref[:, 1:2] * yb_ref[...]
    o_ref[...] = x_ref[...] + _rms(y, g_ref[...])


def _post_moe(yab, route_p, x, g, tm):
    t, d = x.shape
    nblk = t // tm
    row = lambda i: (i, 0)
    return pl.pallas_call(
        _post_moe_kernel,
        grid=(nblk,),
        in_specs=[pl.BlockSpec((tm, d), row), pl.BlockSpec((tm, d), lambda i: (i + nblk, 0)),
                  pl.BlockSpec((tm, LANES), row), pl.BlockSpec((tm, d), row),
                  pl.BlockSpec((1, d), lambda i: (0, 0))],
        out_specs=pl.BlockSpec((tm, d), row),
        out_shape=jax.ShapeDtypeStruct((t, d), F32),
        compiler_params=_params(("parallel",)),
        name="post_moe",
    )(yab, yab, route_p, x, g.reshape(1, d))


def _routing_tables(route_i, n_exp, tm):
    t = route_i.shape[0]
    e_flat = jnp.concatenate([route_i[:, 0], route_i[:, 1]])
    onehot = (e_flat[:, None] == jnp.arange(n_exp, dtype=jnp.int32)[None, :]).astype(jnp.int32)
    counts = jnp.sum(onehot, axis=0)
    padded = ((counts + tm - 1) // tm) * tm
    pad_end = jnp.cumsum(padded)
    pad_start = pad_end - padded
    dest = jnp.sum((jnp.cumsum(onehot, axis=0) - onehot + pad_start[None, :]) * onehot, axis=1)

    n_rows = TOP_K * t + n_exp * tm
    n_tiles = n_rows // tm
    tile_first_row = jnp.arange(n_tiles, dtype=jnp.int32) * tm
    n_valid = pad_end[-1] // tm
    tile_first_row = jnp.minimum(tile_first_row, (n_valid - 1) * tm)
    tile_expert = jnp.sum((pad_end[None, :] <= tile_first_row[:, None]).astype(jnp.int32), axis=1)
    return (dest.astype(jnp.int32), jnp.minimum(tile_expert, n_exp - 1).astype(jnp.int32),
            n_valid.astype(jnp.int32).reshape(1), n_rows)


def _tile(n, want):
    t = min(n, want)
    while n % t:
        t -= SUBLANES
    return t


def kernel(x, positions, pre_mix_g, w_in, conv_w, conv_b, w_rgate, b_rgate, w_igate, b_igate,
           lru_lambda, attn_out_g, lru_out_g, w_out, post_mix_g, pre_ffn_g, post_ffn_g,
           dense_w_gate, dense_w_up, dense_w_down, router_w, moe_w_gate, moe_w_up, moe_w_down):
    batch, seq, d = x.shape
    depth = w_in.shape[0]
    t = batch * seq
    d_ff = dense_w_gate.shape[-1]
    n_exp = moe_w_gate.shape[1]
    assert w_in.shape[2] == 3 * ATTN_WIDTH + 2 * LRU_WIDTH
    assert seq % (DILATED_PAIRS[-1][1] * Q_BLOCK) == 0

    tm_proj = _tile(t, 512)
    tm_mix = _tile(t, 256)
    tm_ffn = _tile(t, 512)
    ts_lru = _tile(seq, 512)
    f_chunk = 512 if d_ff % 512 == 0 else d_ff

    xf = x.reshape(t, d).astype(F32)
    pos = positions.reshape(t, 1).astype(jnp.int32)

    for l in range(depth):
        q, k, v, u, gate = _in_proj(xf, pos, pre_mix_g[l].reshape(1, d), w_in[l].astype(BF16),
                                    tm_proj)
        branches = [_attn_branch(q, k, v, batch, seq, dil) for _, dil in DILATED_PAIRS]
        rec = _lru(u, gate, conv_w[l], conv_b[l], w_rgate[l], b_rgate[l], w_igate[l], b_igate[l],
                   lru_lambda[l], batch, seq, ts_lru)
        j = l // 2
        if l % 2 == 0:
            x1, h2 = _mix_out(branches, rec, xf, attn_out_g[l], lru_out_g[l], w_out[l],
                              post_mix_g[l], pre_ffn_g[l], None, BF16, tm_mix)
            n_tiles = t // tm_ffn
            y = _ffn(h2, jnp.zeros((n_tiles,), jnp.int32), jnp.full((1,), n_tiles, jnp.int32),
                     dense_w_gate[j][None].astype(BF16), dense_w_up[j][None].astype(BF16),
                     dense_w_down[j][None].astype(BF16), tm_ffn, f_chunk)
            xf = _post_dense(y, x1, post_ffn_g[l], tm_mix)
        else:
            x1, h2, route_i, route_p = _mix_out(
                branches, rec, xf, attn_out_g[l], lru_out_g[l], w_out[l], post_mix_g[l],
                pre_ffn_g[l], router_w[j], F32, tm_mix)
            dest, tile_expert, n_valid, n_rows = _routing_tables(route_i, n_exp, tm_ffn)
            slot = jnp.arange(TOP_K * t, dtype=jnp.int32)
            xs = _row_copy(h2, slot % t, dest, n_rows, zero_fill=True)
            ys = _ffn(xs, tile_expert, n_valid, moe_w_gate[j].astype(BF16),
                      moe_w_up[j].astype(BF16), moe_w_down[j].astype(BF16), tm_ffn, f_chunk)
            yab = _row_copy(ys, dest, slot, TOP_K * t, zero_fill=False)
            xf = _post_moe(yab, route_p, x1, post_ffn_g[l], tm_mix)
    return xf.reshape(batch, seq, d).astype(x.dtype)
```

```python
import functools
import math

import jax
import jax.numpy as jnp
from jax import lax
from jax.experimental import pallas as pl
from jax.experimental.pallas import tpu as pltpu

ATTN_HEADS = 8
HEAD_DIM = 64
ATTN_WIDTH = ATTN_HEADS * HEAD_DIM
LRU_BLOCKS = 8
LRU_BLOCK_W = 64
LRU_WIDTH = LRU_BLOCKS * LRU_BLOCK_W
DILATED_PAIRS = ((128, 1), (512, 4), (2048, 16))
Q_BLOCK = 128
ROT_DIM = HEAD_DIM // 4
ROT_HALF = ROT_DIM // 2
ROPE_THETA = 500000.0
ATTN_SCALE = 1.0 / math.sqrt(HEAD_DIM)
NEG_INF = -1e30
CONV_W = 4
LRU_C = 8.0
TOP_K = 2
RMS_EPS = 1e-6

LANES = 128
SUBLANES = 8
SMEM_BLOCK_WORDS = 1024
VMEM_LIMIT_BYTES = 56 * 1024 * 1024

F32 = jnp.float32
BF16 = jnp.bfloat16


def _params(semantics):
    return pltpu.CompilerParams(dimension_semantics=semantics,
                                vmem_limit_bytes=VMEM_LIMIT_BYTES)


def _resident(block_shape, index_map):
    return pl.BlockSpec(block_shape, index_map, pipeline_mode=pl.Buffered(1))


def _rms(x, g):
    var = jnp.mean(x * x, axis=-1, keepdims=True)
    return x * lax.rsqrt(var + RMS_EPS) * g


def _in_proj_kernel(x_ref, pos_ref, g_ref, w_ref, rope_ref,
                    q_ref, k_ref, v_ref, u_ref, gate_ref):
    h = _rms(x_ref[...], g_ref[...]).astype(BF16)
    ang = pos_ref[...].astype(F32) * rope_ref[0:1, :]
    cs = jnp.cos(ang)
    sn = jnp.sin(ang)
    s_lo = sn * rope_ref[1:2, :]
    s_hi = sn * rope_ref[2:3, :]

    def rotary(z):
        return (z * cs + pltpu.roll(z, ROT_HALF, 1) * s_lo
                + pltpu.roll(z, LANES - ROT_HALF, 1) * s_hi)

    w = ATTN_WIDTH
    for sec, out in ((0, q_ref), (1, k_ref)):
        z = jnp.dot(h, w_ref[:, sec * w:(sec + 1) * w], preferred_element_type=F32)
        for cb in range(w // LANES):
            sl = slice(cb * LANES, (cb + 1) * LANES)
            out[:, sl] = rotary(z[:, sl]).astype(BF16)
    v_ref[...] = jnp.dot(h, w_ref[:, 2 * w:3 * w], preferred_element_type=F32).astype(BF16)
    u_ref[...] = jnp.dot(h, w_ref[:, 3 * w:3 * w + LRU_WIDTH], preferred_element_type=F32)
    gate_ref[...] = jnp.dot(h, w_ref[:, 3 * w + LRU_WIDTH:], preferred_element_type=F32)


def _rope_table():
    lane = jnp.arange(LANES, dtype=jnp.int32) % HEAD_DIM
    inv_freq = ROPE_THETA ** (-jnp.arange(ROT_HALF, dtype=F32) / ROT_HALF)
    freq = jnp.where(lane < ROT_DIM, inv_freq[lane % ROT_HALF], 0.0)
    lo = jnp.where((lane >= ROT_HALF) & (lane < ROT_DIM), 1.0, 0.0)
    hi = jnp.where(lane < ROT_HALF, -1.0, 0.0)
    return jnp.stack([freq, lo, hi]).astype(F32)


def _in_proj(x, pos, g, w_in, tm):
    t, d = x.shape
    ncol = w_in.shape[1]
    row = lambda i: (i, 0)
    const = lambda i: (0, 0)
    outs = pl.pallas_call(
        _in_proj_kernel,
        grid=(t // tm,),
        in_specs=[
            pl.BlockSpec((tm, d), row),
            pl.BlockSpec((tm, 1), row),
            pl.BlockSpec((1, d), const),
            _resident((d, ncol), const),
            pl.BlockSpec((3, LANES), const),
        ],
        out_specs=[pl.BlockSpec((tm, ATTN_WIDTH), row)] * 3
        + [pl.BlockSpec((tm, LRU_WIDTH), row)] * 2,
        out_shape=[jax.ShapeDtypeStruct((t, ATTN_WIDTH), BF16)] * 3
        + [jax.ShapeDtypeStruct((t, LRU_WIDTH), F32)] * 2,
        compiler_params=_params(("parallel",)),
        name="in_proj",
    )(x, pos, g, w_in, _rope_table())
    return outs


def _attn_kernel(q_ref, kc_ref, vc_ref, kp_ref, vp_ref, o_ref, lse_ref, *, blocks_per_step):
    n = pl.program_id(2)
    lane = lax.broadcasted_iota(jnp.int32, (1, LANES), 1)
    head0 = lane < HEAD_DIM
    row = lax.broadcasted_iota(jnp.int32, (Q_BLOCK, Q_BLOCK), 0)
    col = lax.broadcasted_iota(jnp.int32, (Q_BLOCK, Q_BLOCK), 1)
    cur_ok = col <= row
    nt = (((1,), (1,)), ((), ()))

    for i in range(blocks_per_step):
        sl = slice(i * Q_BLOCK, (i + 1) * Q_BLOCK)
        q = q_ref[sl, :]
        k_cur, v_cur = kc_ref[sl, :], vc_ref[sl, :]
        if i == 0:
            k_prev, v_prev = kp_ref[...], vp_ref[...]
        else:
            psl = slice((i - 1) * Q_BLOCK, i * Q_BLOCK)
            k_prev, v_prev = kc_ref[psl, :], vc_ref[psl, :]
        prev_ok = jnp.logical_and(col >= row, n * blocks_per_step + i > 0)

        outs, lses = [], []
        for hmask in (head0, jnp.logical_not(head0)):
            qh = jnp.where(hmask, q, jnp.zeros_like(q))
            s_c = lax.dot_general(qh, k_cur, nt, preferred_element_type=F32) * ATTN_SCALE
            s_p = lax.dot_general(qh, k_prev, nt, preferred_element_type=F32) * ATTN_SCALE
            s_c = jnp.where(cur_ok, s_c, NEG_INF)
            s_p = jnp.where(prev_ok, s_p, NEG_INF)
            m = jnp.maximum(jnp.max(s_c, axis=-1, keepdims=True),
                            jnp.max(s_p, axis=-1, keepdims=True))
            p_c = jnp.exp(s_c - m)
            p_p = jnp.exp(s_p - m)
            den = jnp.sum(p_c, axis=-1, keepdims=True) + jnp.sum(p_p, axis=-1, keepdims=True)
            o = (jnp.dot(p_p.astype(BF16), v_prev, preferred_element_type=F32)
                 + jnp.dot(p_c.astype(BF16), v_cur, preferred_element_type=F32))
            outs.append(o / den)
            lses.append(jnp.broadcast_to(m + jnp.log(den), (Q_BLOCK, LANES)))
        o_ref[sl, :] = jnp.where(head0, outs[0], outs[1])
        lse_ref[sl, :] = jnp.where(head0, lses[0], lses[1])


def _attn_branch(q, k, v, batch, seq, dilation):
    sub_len = seq // dilation
    rows = min(4 * Q_BLOCK, sub_len)
    assert sub_len % rows == 0 and rows % Q_BLOCK == 0
    bps = rows // Q_BLOCK
    width = dilation * ATTN_WIDTH
    view = lambda a: a.reshape(batch, sub_len, width)
    cur = pl.BlockSpec((None, rows, LANES), lambda b, c, n: (b, n, c))
    prev = pl.BlockSpec((None, Q_BLOCK, LANES),
                        lambda b, c, n: (b, jnp.maximum(n * bps - 1, 0), c))
    o, lse = pl.pallas_call(
        functools.partial(_attn_kernel, blocks_per_step=bps),
        grid=(batch, width // LANES, sub_len // rows),
        in_specs=[cur, cur, cur, prev, prev],
        out_specs=[cur, cur],
        out_shape=[jax.ShapeDtypeStruct((batch, sub_len, width), F32)] * 2,
        compiler_params=_params(("parallel", "parallel", "parallel")),
        name=f"attn_d{dilation}",
    )(view(q), view(k), view(v), view(k), view(v))
    return o.reshape(batch * seq, ATTN_WIDTH), lse.reshape(batch * seq, ATTN_WIDTH)


def _lru_kernel(u_ref, gate_ref, cw_ref, cb_ref, wr_ref, br_ref, wi_ref, bi_ref, lam_ref,
                out_ref, ubuf, hcarry):
    ts, c = u_ref.shape

    @pl.when(pl.program_id(1) == 0)
    def _():
        ubuf[0:SUBLANES, :] = jnp.zeros((SUBLANES, c), F32)
        hcarry[...] = jnp.zeros_like(hcarry)

    ubuf[SUBLANES:SUBLANES + ts, :] = u_ref[...]
    uc = cb_ref[...]
    for tap in range(CONV_W):
        off = SUBLANES - (CONV_W - 1) + tap
        uc = uc + ubuf[off:off + ts, :] * cw_ref[tap:tap + 1, :]
    ubuf[0:SUBLANES, :] = ubuf[ts:ts + SUBLANES, :]

    ucb = uc.astype(BF16)
    r = jax.nn.sigmoid(jnp.dot(ucb, wr_ref[...], preferred_element_type=F32) + br_ref[...])
    ig = jax.nn.sigmoid(jnp.dot(ucb, wi_ref[...], preferred_element_type=F32) + bi_ref[...])
    nl = -lam_ref[...]
    softplus = jnp.maximum(nl, 0.0) + jnp.log1p(jnp.exp(-jnp.abs(nl)))
    log_a = -LRU_C * r * softplus
    a = jnp.exp(log_a)
    mult = jnp.sqrt(1.0 - jnp.exp(2.0 * log_a))
    b = mult * ig * uc

    row = lax.broadcasted_iota(jnp.int32, (ts, c), 0)
    shift = 1
    while shift < ts:
        valid = row >= shift
        a_sh = jnp.where(valid, pltpu.roll(a, shift, 0), 1.0)
        b_sh = jnp.where(valid, pltpu.roll(b, shift, 0), 0.0)
        b = a * b_sh + b
        a = a * a_sh
        shift *= 2
    h = a * hcarry[0:1, :] + b
    hcarry[...] = jnp.broadcast_to(h[ts - 1:ts, :], hcarry.shape)
    out_ref[...] = jax.nn.gelu(gate_ref[...]) * h


def _block_diag(w):
    nb, bw, _ = w.shape
    eye = jnp.eye(nb, dtype=w.dtype)
    return jnp.einsum("gij,gh->gihj", w, eye).reshape(nb * bw, nb * bw)


def _lru(u, gate, conv_w, conv_b, w_r, b_r, w_i, b_i, lam, batch, seq, ts):
    c = LRU_WIDTH
    nblk = seq // ts
    row = lambda b, j: (b * nblk + j, 0)
    const = lambda b, j: (0, 0)
    vec = lambda a: a.reshape(1, c).astype(F32)
    return pl.pallas_call(
        _lru_kernel,
        grid=(batch, nblk),
        in_specs=[
            pl.BlockSpec((ts, c), row),
            pl.BlockSpec((ts, c), row),
            pl.BlockSpec((CONV_W, c), const),
            pl.BlockSpec((1, c), const),
            pl.BlockSpec((c, c), const),
            pl.BlockSpec((1, c), const),
            pl.BlockSpec((c, c), const),
            pl.BlockSpec((1, c), const),
            pl.BlockSpec((1, c), const),
        ],
        out_specs=pl.BlockSpec((ts, c), row),
        out_shape=jax.ShapeDtypeStruct((batch * seq, c), F32),
        scratch_shapes=[pltpu.VMEM((ts + SUBLANES, c), F32), pltpu.VMEM((SUBLANES, c), F32)],
        compiler_params=_params(("parallel", "arbitrary")),
        name="rg_lru",
    )(u, gate, conv_w.astype(F32), vec(conv_b), _block_diag(w_r).astype(BF16), vec(b_r),
      _block_diag(w_i).astype(BF16), vec(b_i), vec(lam))


def _mix_kernel(*refs, n_exp):
    with_router = n_exp > 0
    (o1, o2, o3, l1, l2, l3, rec_ref, x_ref, ga_ref, gr_ref, wo_ref, gpost_ref,
     gpre_ref) = refs[:13]
    if with_router:
        rw_ref, x1_ref, h2_ref, ri_ref, rp_ref = refs[13:]
    else:
        x1_ref, h2_ref = refs[13:]

    la, lb, lc = l1[...], l2[...], l3[...]
    m = jnp.maximum(jnp.maximum(la, lb), lc)
    ea, eb, ec = jnp.exp(la - m), jnp.exp(lb - m), jnp.exp(lc - m)
    attn = (ea * o1[...] + eb * o2[...] + ec * o3[...]) / (ea + eb + ec)

    na = _rms(attn, ga_ref[...]).astype(BF16)
    nr = _rms(rec_ref[...], gr_ref[...]).astype(BF16)
    mixed = (jnp.dot(na, wo_ref[0:ATTN_WIDTH, :], preferred_element_type=F32)
             + jnp.dot(nr, wo_ref[ATTN_WIDTH:, :], preferred_element_type=F32))
    x1 = x_ref[...] + _rms(mixed, gpost_ref[...])
    x1_ref[...] = x1
    hn = _rms(x1, gpre_ref[...])
    h2_ref[...] = hn.astype(h2_ref.dtype)

    if with_router:
        logits = jnp.dot(hn, rw_ref[...], precision=lax.Precision.HIGHEST,
                         preferred_element_type=F32)
        tm = logits.shape[0]
        lane = lax.broadcasted_iota(jnp.int32, (tm, LANES), 1)
        logits = jnp.where(lane < n_exp, logits, -jnp.inf)
        m1 = jnp.max(logits, axis=-1, keepdims=True)
        i1 = jnp.min(jnp.where(logits == m1, lane, n_exp), axis=-1, keepdims=True)
        rest = jnp.where(lane == i1, -jnp.inf, logits)
        m2 = jnp.max(rest, axis=-1, keepdims=True)
        i2 = jnp.min(jnp.where(rest == m2, lane, n_exp), axis=-1, keepdims=True)
        e2 = jnp.exp(m2 - m1)
        p1 = 1.0 / (1.0 + e2)
        p2 = e2 / (1.0 + e2)
        ri_ref[...] = jnp.where(lane == 0, i1, jnp.where(lane == 1, i2, 0))
        rp_ref[...] = jnp.where(lane == 0, p1, jnp.where(lane == 1, p2, 0.0))


def _mix_out(branches, rec, x, ga, gr, w_out, gpost, gpre, router_w, h2_dtype, tm):
    t, d = x.shape
    row = lambda i: (i, 0)
    const = lambda i: (0, 0)
    aw = ATTN_WIDTH
    with_router = router_w is not None
    n_exp = 0
    (o1, l1), (o2, l2), (o3, l3) = branches
    args = [o1, o2, o3, l1, l2, l3, rec, x, ga.reshape(1, aw), gr.reshape(1, LRU_WIDTH),
            w_out.astype(BF16), gpost.reshape(1, d), gpre.reshape(1, d)]
    in_specs = ([pl.BlockSpec((tm, aw), row)] * 7 + [pl.BlockSpec((tm, d), row),
                pl.BlockSpec((1, aw), const), pl.BlockSpec((1, LRU_WIDTH), const),
                _resident(w_out.shape, const), pl.BlockSpec((1, d), const),
                pl.BlockSpec((1, d), const)])
    out_specs = [pl.BlockSpec((tm, d), row), pl.BlockSpec((tm, d), row)]
    out_shape = [jax.ShapeDtypeStruct((t, d), F32), jax.ShapeDtypeStruct((t, d), h2_dtype)]
    if with_router:
        n_exp = router_w.shape[1]
        rw = jnp.pad(router_w.astype(F32), ((0, 0), (0, LANES - n_exp)))
        args.append(rw)
        in_specs.append(pl.BlockSpec(rw.shape, const))
        out_specs += [pl.BlockSpec((tm, LANES), row)] * 2
        out_shape += [jax.ShapeDtypeStruct((t, LANES), jnp.int32),
                      jax.ShapeDtypeStruct((t, LANES), F32)]
    return pl.pallas_call(
        functools.partial(_mix_kernel, n_exp=n_exp),
        grid=(t // tm,),
        in_specs=in_specs,
        out_specs=out_specs,
        out_shape=out_shape,
        compiler_params=_params(("parallel",)),
        name="mix_out_router" if with_router else "mix_out",
    )(*args)


def _ffn_kernel(te_ref, nv_ref, x_ref, wg_ref, wu_ref, wd_ref, o_ref, *, f_chunk):
    i = pl.program_id(0)
    d_ff = wg_ref.shape[1]

    @pl.when(i < nv_ref[0])
    def _():
        x = x_ref[...].astype(BF16)
        for c in range(d_ff // f_chunk):
            sl = slice(c * f_chunk, (c + 1) * f_chunk)
            g = jnp.dot(x, wg_ref[:, sl], preferred_element_type=F32)
            u = jnp.dot(x, wu_ref[:, sl], preferred_element_type=F32)
            a = (g * jax.nn.sigmoid(g) * u).astype(BF16)
            y = jnp.dot(a, wd_ref[sl, :], preferred_element_type=F32)
            if c == 0:
                o_ref[...] = y
            else:
                o_ref[...] += y

    @pl.when(i >= nv_ref[0])
    def _():
        o_ref[...] = jnp.zeros_like(o_ref)


def _ffn(xs, tile_expert, n_valid, wg, wu, wd, tm, f_chunk):
    rows, d = xs.shape
    d_ff = wg.shape[2]
    assert d_ff % f_chunk == 0 and rows % tm == 0
    xmap = lambda i, te, nv: (jnp.minimum(i, nv[0] - 1), 0)
    wmap = lambda i, te, nv: (te[i], 0, 0)
    grid_spec = pltpu.PrefetchScalarGridSpec(
        num_scalar_prefetch=2,
        grid=(rows // tm,),
        in_specs=[
            pl.BlockSpec((tm, d), xmap),
            _resident((None, d, d_ff), wmap),
            _resident((None, d, d_ff), wmap),
            _resident((None, d_ff, d), wmap),
        ],
        out_specs=pl.BlockSpec((tm, d), lambda i, te, nv: (i, 0)),
    )
    return pl.pallas_call(
        functools.partial(_ffn_kernel, f_chunk=f_chunk),
        grid_spec=grid_spec,
        out_shape=jax.ShapeDtypeStruct((rows, d), F32),
        compiler_params=_params(("arbitrary",)),
        name="ffn",
    )(tile_expert, n_valid, xs, wg, wu, wd)


def _post_kernel(y_ref, x_ref, g_ref, o_ref):
    o_ref[...] = x_ref[...] + _rms(y_ref[...], g_ref[...])


def _post_dense(y, x, g, tm):
    t, d = x.shape
    row = lambda i: (i, 0)
    return pl.pallas_call(
        _post_kernel,
        grid=(t // tm,),
        in_specs=[pl.BlockSpec((tm, d), row), pl.BlockSpec((tm, d), row),
                  pl.BlockSpec((1, d), lambda i: (0, 0))],
        out_specs=pl.BlockSpec((tm, d), row),
        out_shape=jax.ShapeDtypeStruct((t, d), F32),
        compiler_params=_params(("parallel",)),
        name="post_dense",
    )(y, x, g.reshape(1, d))


def _dispatch_kernel(d1_ref, d2_ref, h_ref, init_ref, out_ref, sem):
    del init_ref

    def row_copies(j):
        src = h_ref.at[pl.ds(j, 1)]
        return (pltpu.make_async_copy(src, out_ref.at[pl.ds(d1_ref[j], 1)], sem),
                pltpu.make_async_copy(src, out_ref.at[pl.ds(d2_ref[j], 1)], sem))

    def start(j, carry):
        for c in row_copies(j):
            c.start()
        return carry

    def wait(j, carry):
        for c in row_copies(j):
            c.wait()
        return carry

    n = h_ref.shape[0]
    lax.fori_loop(0, n, start, 0, unroll=8)
    lax.fori_loop(0, n, wait, 0, unroll=8)


def _dispatch(h, dest, n_rows, tm):
    t, d = h.shape
    assert tm % SMEM_BLOCK_WORDS == 0 and t % tm == 0
    nblk = t // tm
    any_spec = pl.BlockSpec(memory_space=pl.ANY)
    return pl.pallas_call(
        _dispatch_kernel,
        grid=(nblk,),
        in_specs=[pl.BlockSpec((tm,), lambda i: (i,), memory_space=pltpu.SMEM),
                  pl.BlockSpec((tm,), lambda i: (i + nblk,), memory_space=pltpu.SMEM),
                  pl.BlockSpec((tm, d), lambda i: (i, 0)), any_spec],
        out_specs=any_spec,
        out_shape=jax.ShapeDtypeStruct((n_rows, d), h.dtype),
        scratch_shapes=[pltpu.SemaphoreType.DMA(())],
        input_output_aliases={3: 0},
        compiler_params=_params(("arbitrary",)),
        name="moe_dispatch",
    )(dest, dest, h, jnp.zeros((n_rows, d), h.dtype))


def _combine_kernel(d1_ref, d2_ref, ys_ref, rp_ref, x_ref, g_ref, o_ref, ya, yb, sem):
    def row_copies(j):
        dst = pl.ds(j, 1)
        return (pltpu.make_async_copy(ys_ref.at[pl.ds(d1_ref[j], 1)], ya.at[dst], sem),
                pltpu.make_async_copy(ys_ref.at[pl.ds(d2_ref[j], 1)], yb.at[dst], sem))

    def start(j, carry):
        for c in row_copies(j):
            c.start()
        return carry

    def wait(j, carry):
        for c in row_copies(j):
            c.wait()
        return carry

    n = x_ref.shape[0]
    lax.fori_loop(0, n, start, 0, unroll=8)
    lax.fori_loop(0, n, wait, 0, unroll=8)
    y = rp_ref[:, 0:1] * ya[...] + rp_ref[:, 1:2] * yb[...]
    o_ref[...] = x_ref[...] + _rms(y, g_ref[...])


def _combine(ys, dest, route_p, x, g, tm):
    t, d = x.shape
    assert tm % SMEM_BLOCK_WORDS == 0 and t % tm == 0
    nblk = t // tm
    row = lambda i: (i, 0)
    return pl.pallas_call(
        _combine_kernel,
        grid=(nblk,),
        in_specs=[pl.BlockSpec((tm,), lambda i: (i,), memory_space=pltpu.SMEM),
                  pl.BlockSpec((tm,), lambda i: (i + nblk,), memory_space=pltpu.SMEM),
                  pl.BlockSpec(memory_space=pl.ANY),
                  pl.BlockSpec((tm, LANES), row), pl.BlockSpec((tm, d), row),
                  pl.BlockSpec((1, d), lambda i: (0, 0))],
        out_specs=pl.BlockSpec((tm, d), row),
        out_shape=jax.ShapeDtypeStruct((t, d), F32),
        scratch_shapes=[pltpu.VMEM((tm, d), F32), pltpu.VMEM((tm, d), F32),
                        pltpu.SemaphoreType.DMA(())],
        compiler_params=_params(("arbitrary",)),
        name="moe_combine",
    )(dest, dest, ys, route_p, x, g.reshape(1, d))


def _routing_tables(route_i, n_exp, tm):
    t = route_i.shape[0]
    e_flat = jnp.concatenate([route_i[:, 0], route_i[:, 1]])
    onehot = (e_flat[:, None] == jnp.arange(n_exp, dtype=jnp.int32)[None, :]).astype(jnp.int32)
    counts = jnp.sum(onehot, axis=0)
    padded = ((counts + tm - 1) // tm) * tm
    pad_end = jnp.cumsum(padded)
    pad_start = pad_end - padded
    dest = jnp.sum((jnp.cumsum(onehot, axis=0) - onehot + pad_start[None, :]) * onehot, axis=1)

    n_rows = TOP_K * t + n_exp * tm
    n_tiles = n_rows // tm
    tile_first_row = jnp.arange(n_tiles, dtype=jnp.int32) * tm
    n_valid = pad_end[-1] // tm
    tile_first_row = jnp.minimum(tile_first_row, (n_valid - 1) * tm)
    tile_expert = jnp.sum((pad_end[None, :] <= tile_first_row[:, None]).astype(jnp.int32), axis=1)
    return (dest.astype(jnp.int32), jnp.minimum(tile_expert, n_exp - 1).astype(jnp.int32),
            n_valid.astype(jnp.int32).reshape(1), n_rows)


def _tile(n, want):
    t = min(n, want)
    while n % t:
        t -= SUBLANES
    return t


def kernel(x, positions, pre_mix_g, w_in, conv_w, conv_b, w_rgate, b_rgate, w_igate, b_igate,
           lru_lambda, attn_out_g, lru_out_g, w_out, post_mix_g, pre_ffn_g, post_ffn_g,
           dense_w_gate, dense_w_up, dense_w_down, router_w, moe_w_gate, moe_w_up, moe_w_down):
    batch, seq, d = x.shape
    depth = w_in.shape[0]
    t = batch * seq
    d_ff = dense_w_gate.shape[-1]
    n_exp = moe_w_gate.shape[1]
    assert w_in.shape[2] == 3 * ATTN_WIDTH + 2 * LRU_WIDTH
    assert seq % (DILATED_PAIRS[-1][1] * Q_BLOCK) == 0

    tm_proj = _tile(t, 512)
    tm_mix = _tile(t, 256)
    tm_ffn = _tile(t, 512)
    ts_lru = _tile(seq, 512)
    tm_rows = SMEM_BLOCK_WORDS
    f_chunk = 512 if d_ff % 512 == 0 else d_ff

    xf = x.reshape(t, d).astype(F32)
    pos = positions.reshape(t, 1).astype(jnp.int32)

    for l in range(depth):
        q, k, v, u, gate = _in_proj(xf, pos, pre_mix_g[l].reshape(1, d), w_in[l].astype(BF16),
                                    tm_proj)
        branches = [_attn_branch(q, k, v, batch, seq, dil) for _, dil in DILATED_PAIRS]
        rec = _lru(u, gate, conv_w[l], conv_b[l], w_rgate[l], b_rgate[l], w_igate[l], b_igate[l],
                   lru_lambda[l], batch, seq, ts_lru)
        j = l // 2
        if l % 2 == 0:
            x1, h2 = _mix_out(branches, rec, xf, attn_out_g[l], lru_out_g[l], w_out[l],
                              post_mix_g[l], pre_ffn_g[l], None, BF16, tm_mix)
            n_tiles = t // tm_ffn
            y = _ffn(h2, jnp.zeros((n_tiles,), jnp.int32), jnp.full((1,), n_tiles, jnp.int32),
                     dense_w_gate[j][None].astype(BF16), dense_w_up[j][None].astype(BF16),
                     dense_w_down[j][None].astype(BF16), tm_ffn, f_chunk)
            xf = _post_dense(y, x1, post_ffn_g[l], tm_mix)
        else:
            x1, h2, route_i, route_p = _mix_out(
                branches, rec, xf, attn_out_g[l], lru_out_g[l], w_out[l], post_mix_g[l],
                pre_ffn_g[l], router_w[j], F32, tm_mix)
            dest, tile_expert, n_valid, n_rows = _routing_tables(route_i, n_exp, tm_ffn)
            xs = _dispatch(h2, dest, n_rows, tm_rows)
            ys = _ffn(xs, tile_expert, n_valid, moe_w_gate[j].astype(BF16),
                      moe_w_up[j].astype(BF16), moe_w_down[j].astype(BF16), tm_ffn, f_chunk)
            xf = _combine(ys, dest, route_p, x1, post_ffn_g[l], tm_rows)
    return xf.reshape(batch, seq, d).astype(x.dtype)
```

```python
import functools
import math

import jax
import jax.numpy as jnp
from jax import lax
from jax.experimental import pallas as pl
from jax.experimental.pallas import tpu as pltpu

ATTN_HEADS = 8
HEAD_DIM = 64
ATTN_WIDTH = ATTN_HEADS * HEAD_DIM
LRU_BLOCKS = 8
LRU_BLOCK_W = 64
LRU_WIDTH = LRU_BLOCKS * LRU_BLOCK_W
DILATED_PAIRS = ((128, 1), (512, 4), (2048, 16))
Q_BLOCK = 128
ROT_DIM = HEAD_DIM // 4
ROT_HALF = ROT_DIM // 2
ROPE_THETA = 500000.0
ATTN_SCALE = 1.0 / math.sqrt(HEAD_DIM)
NEG_INF = -1e30
CONV_W = 4
LRU_C = 8.0
TOP_K = 2
RMS_EPS = 1e-6

LANES = 128
SUBLANES = 8
SMEM_BLOCK_WORDS = 1024
VMEM_LIMIT_BYTES = 56 * 1024 * 1024

F32 = jnp.float32
BF16 = jnp.bfloat16


def _params(semantics):
    return pltpu.CompilerParams(dimension_semantics=semantics,
                                vmem_limit_bytes=VMEM_LIMIT_BYTES)


def _resident(block_shape, index_map):
    return pl.BlockSpec(block_shape, index_map, pipeline_mode=pl.Buffered(1))


def _rms(x, g):
    var = jnp.mean(x * x, axis=-1, keepdims=True)
    return x * lax.rsqrt(var + RMS_EPS) * g


def _in_proj_kernel(x_ref, pos_ref, g_ref, w_ref, rope_ref,
                    q_ref, k_ref, v_ref, u_ref, gate_ref):
    h = _rms(x_ref[...], g_ref[...]).astype(BF16)
    ang = pos_ref[...].astype(F32) * rope_ref[0:1, :]
    cs = jnp.cos(ang)
    sn = jnp.sin(ang)
    s_lo = sn * rope_ref[1:2, :]
    s_hi = sn * rope_ref[2:3, :]

    def rotary(z):
        return (z * cs + pltpu.roll(z, ROT_HALF, 1) * s_lo
                + pltpu.roll(z, LANES - ROT_HALF, 1) * s_hi)

    w = ATTN_WIDTH
    for sec, out in ((0, q_ref), (1, k_ref)):
        z = jnp.dot(h, w_ref[:, sec * w:(sec + 1) * w], preferred_element_type=F32)
        for cb in range(w // LANES):
            sl = slice(cb * LANES, (cb + 1) * LANES)
            out[:, sl] = rotary(z[:, sl]).astype(BF16)
    v_ref[...] = jnp.dot(h, w_ref[:, 2 * w:3 * w], preferred_element_type=F32).astype(BF16)
    u_ref[...] = jnp.dot(h, w_ref[:, 3 * w:3 * w + LRU_WIDTH], preferred_element_type=F32)
    gate_ref[...] = jnp.dot(h, w_ref[:, 3 * w + LRU_WIDTH:], preferred_element_type=F32)


def _rope_table():
    lane = jnp.arange(LANES, dtype=jnp.int32) % HEAD_DIM
    inv_freq = ROPE_THETA ** (-jnp.arange(ROT_HALF, dtype=F32) / ROT_HALF)
    freq = jnp.where(lane < ROT_DIM, inv_freq[lane % ROT_HALF], 0.0)
    lo = jnp.where((lane >= ROT_HALF) & (lane < ROT_DIM), 1.0, 0.0)
    hi = jnp.where(lane < ROT_HALF, -1.0, 0.0)
    return jnp.stack([freq, lo, hi]).astype(F32)


def _in_proj(x, pos, g, w_in, tm):
    t, d = x.shape
    ncol = w_in.shape[1]
    row = lambda i: (i, 0)
    const = lambda i: (0, 0)
    outs = pl.pallas_call(
        _in_proj_kernel,
        grid=(t // tm,),
        in_specs=[
            pl.BlockSpec((tm, d), row),
            pl.BlockSpec((tm, 1), row),
            pl.BlockSpec((1, d), const),
            _resident((d, ncol), const),
            pl.BlockSpec((3, LANES), const),
        ],
        out_specs=[pl.BlockSpec((tm, ATTN_WIDTH), row)] * 3
        + [pl.BlockSpec((tm, LRU_WIDTH), row)] * 2,
        out_shape=[jax.ShapeDtypeStruct((t, ATTN_WIDTH), BF16)] * 3
        + [jax.ShapeDtypeStruct((t, LRU_WIDTH), F32)] * 2,
        compiler_params=_params(("parallel",)),
        name="in_proj",
    )(x, pos, g, w_in, _rope_table())
    return outs


ATTN_TILE = DILATED_PAIRS[-1][1] * Q_BLOCK


def _rows(start, size, stride):
    return pl.ds(start, size) if stride == 1 else pl.ds(start, size, stride=stride)


def _attn_kernel(q_ref, kc_ref, vc_ref, kp_ref, vp_ref, bias_ref, o_ref,
                 q0_scr, q1_scr, k_scr, v_scr, ob0, ob1, ob2, ls0, ls1, ls2):
    j = pl.program_id(2)
    ts = q_ref.shape[0]
    lane = lax.broadcasted_iota(jnp.int32, (1, LANES), 1)
    head0 = lane < HEAD_DIM

    qf = q_ref[...].astype(F32) * ATTN_SCALE
    q0_scr[...] = jnp.where(head0, qf, 0.0)
    q1_scr[...] = jnp.where(head0, 0.0, qf)
    k_scr[0:ts, :] = kp_ref[...].astype(F32)
    k_scr[ts:2 * ts, :] = kc_ref[...].astype(F32)
    v_scr[0:ts, :] = vp_ref[...].astype(F32)
    v_scr[ts:2 * ts, :] = vc_ref[...].astype(F32)

    nt = (((1,), (1,)), ((), ()))
    for (_, dil), ob, ls in zip(DILATED_PAIRS, (ob0, ob1, ob2), (ls0, ls1, ls2)):
        shift = dil.bit_length() - 1
        blocks = ts // Q_BLOCK

        def unit(u, carry, dil=dil, shift=shift, ob=ob, ls=ls):
            cls = jnp.bitwise_and(u, dil - 1)
            blk = lax.shift_right_logical(u, shift)
            q_start = cls + dil * Q_BLOCK * blk
            q_rows = _rows(q_start, Q_BLOCK, dil)
            kv_rows = _rows(ts + q_start - dil * Q_BLOCK, 2 * Q_BLOCK, dil)
            lhs = jnp.concatenate([q0_scr[q_rows, :], q1_scr[q_rows, :]], axis=0).astype(BF16)
            k2 = k_scr[kv_rows, :].astype(BF16)
            v2 = v_scr[kv_rows, :].astype(BF16)
            no_prev = jnp.logical_and(j == 0, blk == 0).astype(jnp.int32)
            bias = bias_ref[pl.ds(no_prev * 2 * Q_BLOCK, 2 * Q_BLOCK), :]
            s = lax.dot_general(lhs, k2, nt, preferred_element_type=F32) + bias
            m = jnp.max(s, axis=-1, keepdims=True)
            p = jnp.exp(s - m)
            den = jnp.sum(p, axis=-1, keepdims=True)
            o = jnp.dot(p.astype(BF16), v2, preferred_element_type=F32) / den
            lse = jnp.broadcast_to(m + jnp.log(den), (2 * Q_BLOCK, LANES))
            ob[q_rows, :] = jnp.where(head0, o[0:Q_BLOCK], o[Q_BLOCK:])
            ls[q_rows, :] = jnp.where(head0, lse[0:Q_BLOCK], lse[Q_BLOCK:])
            return carry

        lax.fori_loop(0, blocks, unit, 0, unroll=2)

    chunk = 2 * Q_BLOCK

    def mix(c, carry):
        r = pl.ds(pl.multiple_of(c * chunk, chunk), chunk)
        la, lb, lc = ls0[r, :], ls1[r, :], ls2[r, :]
        m = jnp.maximum(jnp.maximum(la, lb), lc)
        ea, eb, ec = jnp.exp(la - m), jnp.exp(lb - m), jnp.exp(lc - m)
        o_ref[r, :] = (ea * ob0[r, :] + eb * ob1[r, :] + ec * ob2[r, :]) / (ea + eb + ec)
        return carry

    lax.fori_loop(0, ts // chunk, mix, 0)


def _attn_bias():
    qi = jnp.arange(2 * Q_BLOCK, dtype=jnp.int32)[:, None] % Q_BLOCK
    kj = jnp.arange(2 * Q_BLOCK, dtype=jnp.int32)[None, :]
    delta = Q_BLOCK + qi - kj
    band = (delta >= 0) & (delta <= Q_BLOCK)
    with_prev = jnp.where(band, 0.0, NEG_INF)
    no_prev = jnp.where(band & (kj >= Q_BLOCK), 0.0, NEG_INF)
    return jnp.concatenate([with_prev, no_prev], axis=0).astype(F32)


def _attention(q, k, v, batch, seq):
    ts = ATTN_TILE
    assert seq % ts == 0
    nt = seq // ts
    cur = pl.BlockSpec((ts, LANES), lambda b, c, j: (b * nt + j, c))
    prev = pl.BlockSpec((ts, LANES), lambda b, c, j: (b * nt + jnp.maximum(j - 1, 0), c))
    tile_scr = pltpu.VMEM((ts, LANES), F32)
    pair_scr = pltpu.VMEM((2 * ts, LANES), F32)
    return pl.pallas_call(
        _attn_kernel,
        grid=(batch, ATTN_WIDTH // LANES, nt),
        in_specs=[cur, cur, cur, prev, prev,
                  pl.BlockSpec((4 * Q_BLOCK, 2 * Q_BLOCK), lambda b, c, j: (0, 0))],
        out_specs=cur,
        out_shape=jax.ShapeDtypeStruct((batch * seq, ATTN_WIDTH), F32),
        scratch_shapes=[tile_scr, tile_scr, pair_scr, pair_scr] + [tile_scr] * 6,
        compiler_params=_params(("parallel", "parallel", "parallel")),
        name="attention",
    )(q, k, v, k, v, _attn_bias())


def _lru_kernel(u_ref, gate_ref, cw_ref, cb_ref, wr_ref, br_ref, wi_ref, bi_ref, lam_ref,
                out_ref, ubuf, hcarry):
    ts, c = u_ref.shape

    @pl.when(pl.program_id(1) == 0)
    def _():
        ubuf[0:SUBLANES, :] = jnp.zeros((SUBLANES, c), F32)
        hcarry[...] = jnp.zeros_like(hcarry)

    ubuf[SUBLANES:SUBLANES + ts, :] = u_ref[...]
    uc = cb_ref[...]
    for tap in range(CONV_W):
        off = SUBLANES - (CONV_W - 1) + tap
        uc = uc + ubuf[off:off + ts, :] * cw_ref[tap:tap + 1, :]
    ubuf[0:SUBLANES, :] = ubuf[ts:ts + SUBLANES, :]

    ucb = uc.astype(BF16)
    r = jax.nn.sigmoid(jnp.dot(ucb, wr_ref[...], preferred_element_type=F32) + br_ref[...])
    ig = jax.nn.sigmoid(jnp.dot(ucb, wi_ref[...], preferred_element_type=F32) + bi_ref[...])
    nl = -lam_ref[...]
    softplus = jnp.maximum(nl, 0.0) + jnp.log1p(jnp.exp(-jnp.abs(nl)))
    log_a = -LRU_C * r * softplus
    a = jnp.exp(log_a)
    mult = jnp.sqrt(1.0 - jnp.exp(2.0 * log_a))
    b = mult * ig * uc

    row = lax.broadcasted_iota(jnp.int32, (ts, c), 0)
    shift = 1
    while shift < ts:
        valid = row >= shift
        a_sh = jnp.where(valid, pltpu.roll(a, shift, 0), 1.0)
        b_sh = jnp.where(valid, pltpu.roll(b, shift, 0), 0.0)
        b = a * b_sh + b
        a = a * a_sh
        shift *= 2
    h = a * hcarry[0:1, :] + b
    hcarry[...] = jnp.broadcast_to(h[ts - 1:ts, :], hcarry.shape)
    out_ref[...] = jax.nn.gelu(gate_ref[...]) * h


def _block_diag(w):
    nb, bw, _ = w.shape
    eye = jnp.eye(nb, dtype=w.dtype)
    return jnp.einsum("gij,gh->gihj", w, eye).reshape(nb * bw, nb * bw)


def _lru(u, gate, conv_w, conv_b, w_r, b_r, w_i, b_i, lam, batch, seq, ts):
    c = LRU_WIDTH
    nblk = seq // ts
    row = lambda b, j: (b * nblk + j, 0)
    const = lambda b, j: (0, 0)
    vec = lambda a: a.reshape(1, c).astype(F32)
    return pl.pallas_call(
        _lru_kernel,
        grid=(batch, nblk),
        in_specs=[
            pl.BlockSpec((ts, c), row),
            pl.BlockSpec((ts, c), row),
            pl.BlockSpec((CONV_W, c), const),
            pl.BlockSpec((1, c), const),
            pl.BlockSpec((c, c), const),
            pl.BlockSpec((1, c), const),
            pl.BlockSpec((c, c), const),
            pl.BlockSpec((1, c), const),
            pl.BlockSpec((1, c), const),
        ],
        out_specs=pl.BlockSpec((ts, c), row),
        out_shape=jax.ShapeDtypeStruct((batch * seq, c), F32),
        scratch_shapes=[pltpu.VMEM((ts + SUBLANES, c), F32), pltpu.VMEM((SUBLANES, c), F32)],
        compiler_params=_params(("parallel", "arbitrary")),
        name="rg_lru",
    )(u, gate, conv_w.astype(F32), vec(conv_b), _block_diag(w_r).astype(BF16), vec(b_r),
      _block_diag(w_i).astype(BF16), vec(b_i), vec(lam))


def _mix_kernel(*refs, n_exp):
    with_router = n_exp > 0
    attn_ref, rec_ref, x_ref, ga_ref, gr_ref, wo_ref, gpost_ref, gpre_ref = refs[:8]
    if with_router:
        rw_ref, x1_ref, h2_ref, ri_ref, rp_ref = refs[8:]
    else:
        x1_ref, h2_ref = refs[8:]

    na = _rms(attn_ref[...], ga_ref[...]).astype(BF16)
    nr = _rms(rec_ref[...], gr_ref[...]).astype(BF16)
    mixed = (jnp.dot(na, wo_ref[0:ATTN_WIDTH, :], preferred_element_type=F32)
             + jnp.dot(nr, wo_ref[ATTN_WIDTH:, :], preferred_element_type=F32))
    x1 = x_ref[...] + _rms(mixed, gpost_ref[...])
    x1_ref[...] = x1
    hn = _rms(x1, gpre_ref[...])
    h2_ref[...] = hn.astype(h2_ref.dtype)

    if with_router:
        logits = jnp.dot(hn, rw_ref[...], precision=lax.Precision.HIGHEST,
                         preferred_element_type=F32)
        tm = logits.shape[0]
        lane = lax.broadcasted_iota(jnp.int32, (tm, LANES), 1)
        logits = jnp.where(lane < n_exp, logits, -jnp.inf)
        m1 = jnp.max(logits, axis=-1, keepdims=True)
        i1 = jnp.min(jnp.where(logits == m1, lane, n_exp), axis=-1, keepdims=True)
        rest = jnp.where(lane == i1, -jnp.inf, logits)
        m2 = jnp.max(rest, axis=-1, keepdims=True)
        i2 = jnp.min(jnp.where(rest == m2, lane, n_exp), axis=-1, keepdims=True)
        e2 = jnp.exp(m2 - m1)
        p1 = 1.0 / (1.0 + e2)
        p2 = e2 / (1.0 + e2)
        ri_ref[...] = jnp.where(lane == 0, i1, jnp.where(lane == 1, i2, 0))
        rp_ref[...] = jnp.where(lane == 0, p1, jnp.where(lane == 1, p2, 0.0))


def _mix_out(attn, rec, x, ga, gr, w_out, gpost, gpre, router_w, h2_dtype, tm):
    t, d = x.shape
    row = lambda i: (i, 0)
    const = lambda i: (0, 0)
    aw = ATTN_WIDTH
    with_router = router_w is not None
    n_exp = 0
    args = [attn, rec, x, ga.reshape(1, aw), gr.reshape(1, LRU_WIDTH),
            w_out.astype(BF16), gpost.reshape(1, d), gpre.reshape(1, d)]
    in_specs = ([pl.BlockSpec((tm, aw), row)] * 2 + [pl.BlockSpec((tm, d), row),
                pl.BlockSpec((1, aw), const), pl.BlockSpec((1, LRU_WIDTH), const),
                _resident(w_out.shape, const), pl.BlockSpec((1, d), const),
                pl.BlockSpec((1, d), const)])
    out_specs = [pl.BlockSpec((tm, d), row), pl.BlockSpec((tm, d), row)]
    out_shape = [jax.ShapeDtypeStruct((t, d), F32), jax.ShapeDtypeStruct((t, d), h2_dtype)]
    if with_router:
        n_exp = router_w.shape[1]
        rw = jnp.pad(router_w.astype(F32), ((0, 0), (0, LANES - n_exp)))
        args.append(rw)
        in_specs.append(pl.BlockSpec(rw.shape, const))
        out_specs += [pl.BlockSpec((tm, LANES), row)] * 2
        out_shape += [jax.ShapeDtypeStruct((t, LANES), jnp.int32),
                      jax.ShapeDtypeStruct((t, LANES), F32)]
    return pl.pallas_call(
        functools.partial(_mix_kernel, n_exp=n_exp),
        grid=(t // tm,),
        in_specs=in_specs,
        out_specs=out_specs,
        out_shape=out_shape,
        compiler_params=_params(("parallel",)),
        name="mix_out_router" if with_router else "mix_out",
    )(*args)


def _ffn_kernel(te_ref, nv_ref, x_ref, wg_ref, wu_ref, wd_ref, o_ref, *, f_chunk):
    i = pl.program_id(0)
    d_ff = wg_ref.shape[1]

    @pl.when(i < nv_ref[0])
    def _():
        x = x_ref[...].astype(BF16)
        for c in range(d_ff // f_chunk):
            sl = slice(c * f_chunk, (c + 1) * f_chunk)
            g = jnp.dot(x, wg_ref[:, sl], preferred_element_type=F32)
            u = jnp.dot(x, wu_ref[:, sl], preferred_element_type=F32)
            a = (g * jax.nn.sigmoid(g) * u).astype(BF16)
            y = jnp.dot(a, wd_ref[sl, :], preferred_element_type=F32)
            if c == 0:
                o_ref[...] = y
            else:
                o_ref[...] += y

    @pl.when(i >= nv_ref[0])
    def _():
        o_ref[...] = jnp.zeros_like(o_ref)


def _ffn(xs, tile_expert, n_valid, wg, wu, wd, tm, f_chunk):
    rows, d = xs.shape
    d_ff = wg.shape[2]
    assert d_ff % f_chunk == 0 and rows % tm == 0
    xmap = lambda i, te, nv: (jnp.minimum(i, nv[0] - 1), 0)
    wmap = lambda i, te, nv: (te[i], 0, 0)
    grid_spec = pltpu.PrefetchScalarGridSpec(
        num_scalar_prefetch=2,
        grid=(rows // tm,),
        in_specs=[
            pl.BlockSpec((tm, d), xmap),
            _resident((None, d, d_ff), wmap),
            _resident((None, d, d_ff), wmap),
            _resident((None, d_ff, d), wmap),
        ],
        out_specs=pl.BlockSpec((tm, d), lambda i, te, nv: (i, 0)),
    )
    return pl.pallas_call(
        functools.partial(_ffn_kernel, f_chunk=f_chunk),
        grid_spec=grid_spec,
        out_shape=jax.ShapeDtypeStruct((rows, d), F32),
        compiler_params=_params(("arbitrary",)),
        name="ffn",
    )(tile_expert, n_valid, xs, wg, wu, wd)


def _post_kernel(y_ref, x_ref, g_ref, o_ref):
    o_ref[...] = x_ref[...] + _rms(y_ref[...], g_ref[...])


def _post_dense(y, x, g, tm):
    t, d = x.shape
    row = lambda i: (i, 0)
    return pl.pallas_call(
        _post_kernel,
        grid=(t // tm,),
        in_specs=[pl.BlockSpec((tm, d), row), pl.BlockSpec((tm, d), row),
                  pl.BlockSpec((1, d), lambda i: (0, 0))],
        out_specs=pl.BlockSpec((tm, d), row),
        out_shape=jax.ShapeDtypeStruct((t, d), F32),
        compiler_params=_params(("parallel",)),
        name="post_dense",
    )(y, x, g.reshape(1, d))


def _dispatch_kernel(d1_ref, d2_ref, h_ref, init_ref, out_ref, sem):
    del init_ref

    def row_copies(j):
        src = h_ref.at[pl.ds(j, 1)]
        return (pltpu.make_async_copy(src, out_ref.at[pl.ds(d1_ref[j], 1)], sem),
                pltpu.make_async_copy(src, out_ref.at[pl.ds(d2_ref[j], 1)], sem))

    def start(j, carry):
        for c in row_copies(j):
            c.start()
        return carry

    def wait(j, carry):
        for c in row_copies(j):
            c.wait()
        return carry

    n = h_ref.shape[0]
    lax.fori_loop(0, n, start, 0, unroll=8)
    lax.fori_loop(0, n, wait, 0, unroll=8)


def _dispatch(h, dest, n_rows, tm):
    t, d = h.shape
    assert tm % SMEM_BLOCK_WORDS == 0 and t % tm == 0
    nblk = t // tm
    any_spec = pl.BlockSpec(memory_space=pl.ANY)
    return pl.pallas_call(
        _dispatch_kernel,
        grid=(nblk,),
        in_specs=[pl.BlockSpec((tm,), lambda i: (i,), memory_space=pltpu.SMEM),
                  pl.BlockSpec((tm,), lambda i: (i + nblk,), memory_space=pltpu.SMEM),
                  pl.BlockSpec((tm, d), lambda i: (i, 0)), any_spec],
        out_specs=any_spec,
        out_shape=jax.ShapeDtypeStruct((n_rows, d), h.dtype),
        scratch_shapes=[pltpu.SemaphoreType.DMA(())],
        input_output_aliases={3: 0},
        compiler_params=_params(("arbitrary",)),
        name="moe_dispatch",
    )(dest, dest, h, jnp.zeros((n_rows, d), h.dtype))


def _combine_kernel(d1_ref, d2_ref, ys_ref, rp_ref, x_ref, g_ref, o_ref, ya, yb, sem):
    def row_copies(j):
        dst = pl.ds(j, 1)
        return (pltpu.make_async_copy(ys_ref.at[pl.ds(d1_ref[j], 1)], ya.at[dst], sem),
                pltpu.make_async_copy(ys_ref.at[pl.ds(d2_ref[j], 1)], yb.at[dst], sem))

    def start(j, carry):
        for c in row_copies(j):
            c.start()
        return carry

    def wait(j, carry):
        for c in row_copies(j):
            c.wait()
        return carry

    n = x_ref.shape[0]
    lax.fori_loop(0, n, start, 0, unroll=8)
    lax.fori_loop(0, n, wait, 0, unroll=8)
    y = rp_ref[:, 0:1] * ya[...] + rp_ref[:, 1:2] * yb[...]
    o_ref[...] = x_ref[...] + _rms(y, g_ref[...])


def _combine(ys, dest, route_p, x, g, tm):
    t, d = x.shape
    assert tm % SMEM_BLOCK_WORDS == 0 and t % tm == 0
    nblk = t // tm
    row = lambda i: (i, 0)
    return pl.pallas_call(
        _combine_kernel,
        grid=(nblk,),
        in_specs=[pl.BlockSpec((tm,), lambda i: (i,), memory_space=pltpu.SMEM),
                  pl.BlockSpec((tm,), lambda i: (i + nblk,), memory_space=pltpu.SMEM),
                  pl.BlockSpec(memory_space=pl.ANY),
                  pl.BlockSpec((tm, LANES), row), pl.BlockSpec((tm, d), row),
                  pl.BlockSpec((1, d), lambda i: (0, 0))],
        out_specs=pl.BlockSpec((tm, d), row),
        out_shape=jax.ShapeDtypeStruct((t, d), F32),
        scratch_shapes=[pltpu.VMEM((tm, d), F32), pltpu.VMEM((tm, d), F32),
                        pltpu.SemaphoreType.DMA(())],
        compiler_params=_params(("arbitrary",)),
        name="moe_combine",
    )(dest, dest, ys, route_p, x, g.reshape(1, d))


def _routing_tables(route_i, n_exp, tm):
    t = route_i.shape[0]
    e_flat = jnp.concatenate([route_i[:, 0], route_i[:, 1]])
    onehot = (e_flat[:, None] == jnp.arange(n_exp, dtype=jnp.int32)[None, :]).astype(jnp.int32)
    counts = jnp.sum(onehot, axis=0)
    padded = ((counts + tm - 1) // tm) * tm
    pad_end = jnp.cumsum(padded)
    pad_start = pad_end - padded
    dest = jnp.sum((jnp.cumsum(onehot, axis=0) - onehot + pad_start[None, :]) * onehot, axis=1)

    n_rows = TOP_K * t + n_exp * tm
    n_tiles = n_rows // tm
    tile_first_row = jnp.arange(n_tiles, dtype=jnp.int32) * tm
    n_valid = pad_end[-1] // tm
    tile_first_row = jnp.minimum(tile_first_row, (n_valid - 1) * tm)
    tile_expert = jnp.sum((pad_end[None, :] <= tile_first_row[:, None]).astype(jnp.int32), axis=1)
    return (dest.astype(jnp.int32), jnp.minimum(tile_expert, n_exp - 1).astype(jnp.int32),
            n_valid.astype(jnp.int32).reshape(1), n_rows)


def _tile(n, want):
    t = min(n, want)
    while n % t:
        t -= SUBLANES
    return t


def kernel(x, positions, pre_mix_g, w_in, conv_w, conv_b, w_rgate, b_rgate, w_igate, b_igate,
           lru_lambda, attn_out_g, lru_out_g, w_out, post_mix_g, pre_ffn_g, post_ffn_g,
           dense_w_gate, dense_w_up, dense_w_down, router_w, moe_w_gate, moe_w_up, moe_w_down):
    batch, seq, d = x.shape
    depth = w_in.shape[0]
    t = batch * seq
    d_ff = dense_w_gate.shape[-1]
    n_exp = moe_w_gate.shape[1]
    assert w_in.shape[2] == 3 * ATTN_WIDTH + 2 * LRU_WIDTH
    assert seq % (DILATED_PAIRS[-1][1] * Q_BLOCK) == 0

    tm_proj = _tile(t, 512)
    tm_mix = _tile(t, 256)
    tm_ffn = _tile(t, 512)
    ts_lru = _tile(seq, 512)
    tm_rows = SMEM_BLOCK_WORDS
    f_chunk = 512 if d_ff % 512 == 0 else d_ff

    xf = x.reshape(t, d).astype(F32)
    pos = positions.reshape(t, 1).astype(jnp.int32)

    for l in range(depth):
        q, k, v, u, gate = _in_proj(xf, pos, pre_mix_g[l].reshape(1, d), w_in[l].astype(BF16),
                                    tm_proj)
        attn = _attention(q, k, v, batch, seq)
        rec = _lru(u, gate, conv_w[l], conv_b[l], w_rgate[l], b_rgate[l], w_igate[l], b_igate[l],
                   lru_lambda[l], batch, seq, ts_lru)
        j = l // 2
        if l % 2 == 0:
            x1, h2 = _mix_out(attn, rec, xf, attn_out_g[l], lru_out_g[l], w_out[l],
                              post_mix_g[l], pre_ffn_g[l], None, BF16, tm_mix)
            n_tiles = t // tm_ffn
            y = _ffn(h2, jnp.zeros((n_tiles,), jnp.int32), jnp.full((1,), n_tiles, jnp.int32),
                     dense_w_gate[j][None].astype(BF16), dense_w_up[j][None].astype(BF16),
                     dense_w_down[j][None].astype(BF16), tm_ffn, f_chunk)
            xf = _post_dense(y, x1, post_ffn_g[l], tm_mix)
        else:
            x1, h2, route_i, route_p = _mix_out(
                attn, rec, xf, attn_out_g[l], lru_out_g[l], w_out[l], post_mix_g[l],
                pre_ffn_g[l], router_w[j], F32, tm_mix)
            dest, tile_expert, n_valid, n_rows = _routing_tables(route_i, n_exp, tm_ffn)
            xs = _dispatch(h2, dest, n_rows, tm_rows)
            ys = _ffn(xs, tile_expert, n_valid, moe_w_gate[j].astype(BF16),
                      moe_w_up[j].astype(BF16), moe_w_down[j].astype(BF16), tm_ffn, f_chunk)
            xf = _combine(ys, dest, route_p, x1, post_ffn_g[l], tm_rows)
    return xf.reshape(batch, seq, d).astype(x.dtype)
```

```python
import functools
import math

import jax
import jax.numpy as jnp
from jax import lax
from jax.experimental import pallas as pl
from jax.experimental.pallas import tpu as pltpu

ATTN_HEADS = 8
HEAD_DIM = 64
ATTN_WIDTH = ATTN_HEADS * HEAD_DIM
LRU_BLOCKS = 8
LRU_BLOCK_W = 64
LRU_WIDTH = LRU_BLOCKS * LRU_BLOCK_W
DILATED_PAIRS = ((128, 1), (512, 4), (2048, 16))
Q_BLOCK = 128
ROT_DIM = HEAD_DIM // 4
ROT_HALF = ROT_DIM // 2
ROPE_THETA = 500000.0
ATTN_SCALE = 1.0 / math.sqrt(HEAD_DIM)
NEG_INF = -1e30
CONV_W = 4
LRU_C = 8.0
TOP_K = 2
RMS_EPS = 1e-6

LANES = 128
SUBLANES = 8
SMEM_BLOCK_WORDS = 1024
VMEM_LIMIT_BYTES = 56 * 1024 * 1024

F32 = jnp.float32
BF16 = jnp.bfloat16


def _params(semantics):
    return pltpu.CompilerParams(dimension_semantics=semantics,
                                vmem_limit_bytes=VMEM_LIMIT_BYTES)


def _resident(block_shape, index_map):
    return pl.BlockSpec(block_shape, index_map, pipeline_mode=pl.Buffered(1))


def _rms(x, g):
    var = jnp.mean(x * x, axis=-1, keepdims=True)
    return x * lax.rsqrt(var + RMS_EPS) * g


def _in_proj_kernel(x_ref, pos_ref, g_ref, w_ref, rope_ref,
                    q_ref, k_ref, v_ref, u_ref, gate_ref):
    h = _rms(x_ref[...], g_ref[...]).astype(BF16)
    ang = pos_ref[...].astype(F32) * rope_ref[0:1, :]
    cs = jnp.cos(ang)
    sn = jnp.sin(ang)
    s_lo = sn * rope_ref[1:2, :]
    s_hi = sn * rope_ref[2:3, :]

    def rotary(z):
        return (z * cs + pltpu.roll(z, ROT_HALF, 1) * s_lo
                + pltpu.roll(z, LANES - ROT_HALF, 1) * s_hi)

    w = ATTN_WIDTH
    for sec, out in ((0, q_ref), (1, k_ref)):
        z = jnp.dot(h, w_ref[:, sec * w:(sec + 1) * w], preferred_element_type=F32)
        for cb in range(w // LANES):
            sl = slice(cb * LANES, (cb + 1) * LANES)
            out[:, sl] = rotary(z[:, sl]).astype(BF16)
    v_ref[...] = jnp.dot(h, w_ref[:, 2 * w:3 * w], preferred_element_type=F32).astype(BF16)
    u_ref[...] = jnp.dot(h, w_ref[:, 3 * w:3 * w + LRU_WIDTH], preferred_element_type=F32)
    gate_ref[...] = jnp.dot(h, w_ref[:, 3 * w + LRU_WIDTH:], preferred_element_type=F32)


def _rope_table():
    lane = jnp.arange(LANES, dtype=jnp.int32) % HEAD_DIM
    inv_freq = ROPE_THETA ** (-jnp.arange(ROT_HALF, dtype=F32) / ROT_HALF)
    freq = jnp.where(lane < ROT_DIM, inv_freq[lane % ROT_HALF], 0.0)
    lo = jnp.where((lane >= ROT_HALF) & (lane < ROT_DIM), 1.0, 0.0)
    hi = jnp.where(lane < ROT_HALF, -1.0, 0.0)
    return jnp.stack([freq, lo, hi]).astype(F32)


def _in_proj(x, pos, g, w_in, tm):
    t, d = x.shape
    ncol = w_in.shape[1]
    row = lambda i: (i, 0)
    const = lambda i: (0, 0)
    outs = pl.pallas_call(
        _in_proj_kernel,
        grid=(t // tm,),
        in_specs=[
            pl.BlockSpec((tm, d), row),
            pl.BlockSpec((tm, 1), row),
            pl.BlockSpec((1, d), const),
            _resident((d, ncol), const),
            pl.BlockSpec((3, LANES), const),
        ],
        out_specs=[pl.BlockSpec((tm, ATTN_WIDTH), row)] * 3
        + [pl.BlockSpec((tm, LRU_WIDTH), row)] * 2,
        out_shape=[jax.ShapeDtypeStruct((t, ATTN_WIDTH), BF16)] * 3
        + [jax.ShapeDtypeStruct((t, LRU_WIDTH), F32)] * 2,
        compiler_params=_params(("parallel",)),
        name="in_proj",
    )(x, pos, g, w_in, _rope_table())
    return outs


ATTN_TILE = DILATED_PAIRS[-1][1] * Q_BLOCK


def _rows(start, size, stride):
    return pl.ds(start, size) if stride == 1 else pl.ds(start, size, stride=stride)


ATTN_PIPE_WIDTH = 4


def _attn_kernel(q_ref, kc_ref, vc_ref, kp_ref, vp_ref, bias_ref, o_ref,
                 q0_scr, q1_scr, k_scr, v_scr, ob0, ob1, ob2, ls0, ls1, ls2, *slots):
    s_scrs, m_scrs = slots[:len(slots) // 2], slots[len(slots) // 2:]
    j = pl.program_id(2)
    ts = q_ref.shape[0]
    lane = lax.broadcasted_iota(jnp.int32, (1, LANES), 1)
    head0 = lane < HEAD_DIM

    qf = q_ref[...].astype(F32) * ATTN_SCALE
    q0_scr[...] = jnp.where(head0, qf, 0.0)
    q1_scr[...] = jnp.where(head0, 0.0, qf)
    k_scr[0:ts, :] = kp_ref[...].astype(F32)
    k_scr[ts:2 * ts, :] = kc_ref[...].astype(F32)
    v_scr[0:ts, :] = vp_ref[...].astype(F32)
    v_scr[ts:2 * ts, :] = vc_ref[...].astype(F32)

    nt = (((1,), (1,)), ((), ()))
    units = ts // Q_BLOCK
    width = len(s_scrs) // 2
    groups = units // width
    assert groups % 2 == 0 and groups >= 2
    for (_, dil), ob, ls in zip(DILATED_PAIRS, (ob0, ob1, ob2), (ls0, ls1, ls2)):
        shift = dil.bit_length() - 1

        def rows_of(u, dil=dil, shift=shift):
            cls = jnp.bitwise_and(u, dil - 1)
            blk = lax.shift_right_logical(u, shift)
            q_start = cls + dil * Q_BLOCK * blk
            q_rows = _rows(q_start, Q_BLOCK, dil)
            kv_rows = _rows(ts + q_start - dil * Q_BLOCK, 2 * Q_BLOCK, dil)
            return blk, q_rows, kv_rows

        def scores(u, slot):
            blk, q_rows, kv_rows = rows_of(u)
            lhs = jnp.concatenate([q0_scr[q_rows, :], q1_scr[q_rows, :]], axis=0).astype(BF16)
            k2 = k_scr[kv_rows, :].astype(BF16)
            no_prev = jnp.logical_and(j == 0, blk == 0).astype(jnp.int32)
            bias = bias_ref[pl.ds(no_prev * 2 * Q_BLOCK, 2 * Q_BLOCK), :]
            s = lax.dot_general(lhs, k2, nt, preferred_element_type=F32) + bias
            s_scrs[slot][...] = s
            m_scrs[slot][...] = jnp.broadcast_to(jnp.max(s, axis=-1, keepdims=True),
                                                 (2 * Q_BLOCK, LANES))

        def finish(u, slot, ob=ob, ls=ls):
            _, q_rows, kv_rows = rows_of(u)
            m = m_scrs[slot][...]
            p = jnp.exp(s_scrs[slot][...] - jnp.concatenate([m, m], axis=1))
            den = jnp.sum(p, axis=-1, keepdims=True)
            v2 = v_scr[kv_rows, :].astype(BF16)
            o = jnp.dot(p.astype(BF16), v2, preferred_element_type=F32) / den
            lse = m + jnp.log(den)
            ob[q_rows, :] = jnp.where(head0, o[0:Q_BLOCK], o[Q_BLOCK:])
            ls[q_rows, :] = jnp.where(head0, lse[0:Q_BLOCK], lse[Q_BLOCK:])

        def scores_group(g, half):
            for w in range(width):
                scores(g * width + w, half * width + w)

        def finish_group(g, half):
            for w in range(width):
                finish(g * width + w, half * width + w)

        scores_group(jnp.int32(0), 0)

        def pair(i, carry):
            scores_group(2 * i + 1, 1)
            finish_group(2 * i, 0)
            scores_group(2 * i + 2, 0)
            finish_group(2 * i + 1, 1)
            return carry

        lax.fori_loop(0, groups // 2 - 1, pair, 0)
        scores_group(jnp.int32(groups - 1), 1)
        finish_group(jnp.int32(groups - 2), 0)
        finish_group(jnp.int32(groups - 1), 1)

    chunk = 2 * Q_BLOCK

    def mix(c, carry):
        r = pl.ds(pl.multiple_of(c * chunk, chunk), chunk)
        la, lb, lc = ls0[r, :], ls1[r, :], ls2[r, :]
        m = jnp.maximum(jnp.maximum(la, lb), lc)
        ea, eb, ec = jnp.exp(la - m), jnp.exp(lb - m), jnp.exp(lc - m)
        o_ref[r, :] = (ea * ob0[r, :] + eb * ob1[r, :] + ec * ob2[r, :]) / (ea + eb + ec)
        return carry

    lax.fori_loop(0, ts // chunk, mix, 0)


def _attn_bias():
    qi = jnp.arange(2 * Q_BLOCK, dtype=jnp.int32)[:, None] % Q_BLOCK
    kj = jnp.arange(2 * Q_BLOCK, dtype=jnp.int32)[None, :]
    delta = Q_BLOCK + qi - kj
    band = (delta >= 0) & (delta <= Q_BLOCK)
    with_prev = jnp.where(band, 0.0, NEG_INF)
    no_prev = jnp.where(band & (kj >= Q_BLOCK), 0.0, NEG_INF)
    return jnp.concatenate([with_prev, no_prev], axis=0).astype(F32)


def _attention(q, k, v, batch, seq):
    ts = ATTN_TILE
    assert seq % ts == 0
    nt = seq // ts
    cur = pl.BlockSpec((ts, LANES), lambda b, c, j: (b * nt + j, c))
    prev = pl.BlockSpec((ts, LANES), lambda b, c, j: (b * nt + jnp.maximum(j - 1, 0), c))
    tile_scr = pltpu.VMEM((ts, LANES), F32)
    pair_scr = pltpu.VMEM((2 * ts, LANES), F32)
    return pl.pallas_call(
        _attn_kernel,
        grid=(batch, ATTN_WIDTH // LANES, nt),
        in_specs=[cur, cur, cur, prev, prev,
                  pl.BlockSpec((4 * Q_BLOCK, 2 * Q_BLOCK), lambda b, c, j: (0, 0))],
        out_specs=cur,
        out_shape=jax.ShapeDtypeStruct((batch * seq, ATTN_WIDTH), F32),
        scratch_shapes=[tile_scr, tile_scr, pair_scr, pair_scr] + [tile_scr] * 6
        + [pltpu.VMEM((2 * Q_BLOCK, 2 * Q_BLOCK), F32)] * (2 * ATTN_PIPE_WIDTH)
        + [pltpu.VMEM((2 * Q_BLOCK, LANES), F32)] * (2 * ATTN_PIPE_WIDTH),
        compiler_params=_params(("parallel", "parallel", "parallel")),
        name="attention",
    )(q, k, v, k, v, _attn_bias())


def _lru_kernel(u_ref, gate_ref, cw_ref, cb_ref, wr_ref, br_ref, wi_ref, bi_ref, lam_ref,
                out_ref, ubuf, hcarry):
    ts, c = u_ref.shape

    @pl.when(pl.program_id(1) == 0)
    def _():
        ubuf[0:SUBLANES, :] = jnp.zeros((SUBLANES, c), F32)
        hcarry[...] = jnp.zeros_like(hcarry)

    ubuf[SUBLANES:SUBLANES + ts, :] = u_ref[...]
    uc = cb_ref[...]
    for tap in range(CONV_W):
        off = SUBLANES - (CONV_W - 1) + tap
        uc = uc + ubuf[off:off + ts, :] * cw_ref[tap:tap + 1, :]
    ubuf[0:SUBLANES, :] = ubuf[ts:ts + SUBLANES, :]

    ucb = uc.astype(BF16)
    r = jax.nn.sigmoid(jnp.dot(ucb, wr_ref[...], preferred_element_type=F32) + br_ref[...])
    ig = jax.nn.sigmoid(jnp.dot(ucb, wi_ref[...], preferred_element_type=F32) + bi_ref[...])
    nl = -lam_ref[...]
    softplus = jnp.maximum(nl, 0.0) + jnp.log1p(jnp.exp(-jnp.abs(nl)))
    log_a = -LRU_C * r * softplus
    a = jnp.exp(log_a)
    mult = jnp.sqrt(1.0 - jnp.exp(2.0 * log_a))
    b = mult * ig * uc

    row = lax.broadcasted_iota(jnp.int32, (ts, c), 0)
    shift = 1
    while shift < ts:
        valid = row >= shift
        a_sh = jnp.where(valid, pltpu.roll(a, shift, 0), 1.0)
        b_sh = jnp.where(valid, pltpu.roll(b, shift, 0), 0.0)
        b = a * b_sh + b
        a = a * a_sh
        shift *= 2
    h = a * hcarry[0:1, :] + b
    hcarry[...] = jnp.broadcast_to(h[ts - 1:ts, :], hcarry.shape)
    out_ref[...] = jax.nn.gelu(gate_ref[...]) * h


def _block_diag(w):
    nb, bw, _ = w.shape
    eye = jnp.eye(nb, dtype=w.dtype)
    return jnp.einsum("gij,gh->gihj", w, eye).reshape(nb * bw, nb * bw)


def _lru(u, gate, conv_w, conv_b, w_r, b_r, w_i, b_i, lam, batch, seq, ts):
    c = LRU_WIDTH
    nblk = seq // ts
    row = lambda b, j: (b * nblk + j, 0)
    const = lambda b, j: (0, 0)
    vec = lambda a: a.reshape(1, c).astype(F32)
    return pl.pallas_call(
        _lru_kernel,
        grid=(batch, nblk),
        in_specs=[
            pl.BlockSpec((ts, c), row),
            pl.BlockSpec((ts, c), row),
            pl.BlockSpec((CONV_W, c), const),
            pl.BlockSpec((1, c), const),
            pl.BlockSpec((c, c), const),
            pl.BlockSpec((1, c), const),
            pl.BlockSpec((c, c), const),
            pl.BlockSpec((1, c), const),
            pl.BlockSpec((1, c), const),
        ],
        out_specs=pl.BlockSpec((ts, c), row),
        out_shape=jax.ShapeDtypeStruct((batch * seq, c), F32),
        scratch_shapes=[pltpu.VMEM((ts + SUBLANES, c), F32), pltpu.VMEM((SUBLANES, c), F32)],
        compiler_params=_params(("parallel", "arbitrary")),
        name="rg_lru",
    )(u, gate, conv_w.astype(F32), vec(conv_b), _block_diag(w_r).astype(BF16), vec(b_r),
      _block_diag(w_i).astype(BF16), vec(b_i), vec(lam))


def _mix_kernel(*refs, n_exp):
    with_router = n_exp > 0
    attn_ref, rec_ref, x_ref, ga_ref, gr_ref, wo_ref, gpost_ref, gpre_ref = refs[:8]
    if with_router:
        rw_ref, x1_ref, h2_ref, ri_ref, rp_ref = refs[8:]
    else:
        x1_ref, h2_ref = refs[8:]

    na = _rms(attn_ref[...], ga_ref[...]).astype(BF16)
    nr = _rms(rec_ref[...], gr_ref[...]).astype(BF16)
    mixed = (jnp.dot(na, wo_ref[0:ATTN_WIDTH, :], preferred_element_type=F32)
             + jnp.dot(nr, wo_ref[ATTN_WIDTH:, :], preferred_element_type=F32))
    x1 = x_ref[...] + _rms(mixed, gpost_ref[...])
    x1_ref[...] = x1
    hn = _rms(x1, gpre_ref[...])
    h2_ref[...] = hn.astype(h2_ref.dtype)

    if with_router:
        rw = rw_ref[...]
        h_hi, r_hi = hn.astype(BF16), rw.astype(BF16)
        h_lo = (hn - h_hi.astype(F32)).astype(BF16)
        r_lo = (rw - r_hi.astype(F32)).astype(BF16)
        logits = (jnp.dot(h_hi, r_hi, preferred_element_type=F32)
                  + jnp.dot(h_hi, r_lo, preferred_element_type=F32)
                  + jnp.dot(h_lo, r_hi, preferred_element_type=F32))
        tm = logits.shape[0]
        lane = lax.broadcasted_iota(jnp.int32, (tm, LANES), 1)
        logits = jnp.where(lane < n_exp, logits, -jnp.inf)
        m1 = jnp.max(logits, axis=-1, keepdims=True)
        i1 = jnp.min(jnp.where(logits == m1, lane, n_exp), axis=-1, keepdims=True)
        rest = jnp.where(lane == i1, -jnp.inf, logits)
        m2 = jnp.max(rest, axis=-1, keepdims=True)
        i2 = jnp.min(jnp.where(rest == m2, lane, n_exp), axis=-1, keepdims=True)
        e2 = jnp.exp(m2 - m1)
        p1 = 1.0 / (1.0 + e2)
        p2 = e2 / (1.0 + e2)
        ri_ref[...] = jnp.where(lane == 0, i1, jnp.where(lane == 1, i2, 0))
        rp_ref[...] = jnp.where(lane == 0, p1, jnp.where(lane == 1, p2, 0.0))


def _mix_out(attn, rec, x, ga, gr, w_out, gpost, gpre, router_w, h2_dtype, tm):
    t, d = x.shape
    row = lambda i: (i, 0)
    const = lambda i: (0, 0)
    aw = ATTN_WIDTH
    with_router = router_w is not None
    n_exp = 0
    args = [attn, rec, x, ga.reshape(1, aw), gr.reshape(1, LRU_WIDTH),
            w_out.astype(BF16), gpost.reshape(1, d), gpre.reshape(1, d)]
    in_specs = ([pl.BlockSpec((tm, aw), row)] * 2 + [pl.BlockSpec((tm, d), row),
                pl.BlockSpec((1, aw), const), pl.BlockSpec((1, LRU_WIDTH), const),
                _resident(w_out.shape, const), pl.BlockSpec((1, d), const),
                pl.BlockSpec((1, d), const)])
    out_specs = [pl.BlockSpec((tm, d), row), pl.BlockSpec((tm, d), row)]
    out_shape = [jax.ShapeDtypeStruct((t, d), F32), jax.ShapeDtypeStruct((t, d), h2_dtype)]
    if with_router:
        n_exp = router_w.shape[1]
        rw = jnp.pad(router_w.astype(F32), ((0, 0), (0, LANES - n_exp)))
        args.append(rw)
        in_specs.append(pl.BlockSpec(rw.shape, const))
        out_specs += [pl.BlockSpec((tm, LANES), row)] * 2
        out_shape += [jax.ShapeDtypeStruct((t, LANES), jnp.int32),
                      jax.ShapeDtypeStruct((t, LANES), F32)]
    return pl.pallas_call(
        functools.partial(_mix_kernel, n_exp=n_exp),
        grid=(t // tm,),
        in_specs=in_specs,
        out_specs=out_specs,
        out_shape=out_shape,
        compiler_params=_params(("parallel",)),
        name="mix_out_router" if with_router else "mix_out",
    )(*args)


def _ffn_kernel(te_ref, nv_ref, x_ref, wg_ref, wu_ref, wd_ref, *refs, f_chunk):
    o_ref = refs[-1]
    i = pl.program_id(0)
    d_ff = wg_ref.shape[1]

    @pl.when(i < nv_ref[0])
    def _():
        x = x_ref[...].astype(BF16)
        for c in range(d_ff // f_chunk):
            sl = slice(c * f_chunk, (c + 1) * f_chunk)
            g = jnp.dot(x, wg_ref[:, sl], preferred_element_type=F32)
            u = jnp.dot(x, wu_ref[:, sl], preferred_element_type=F32)
            a = (g * jax.nn.sigmoid(g) * u).astype(BF16)
            y = jnp.dot(a, wd_ref[sl, :], preferred_element_type=F32)
            if c == 0:
                o_ref[...] = y
            else:
                o_ref[...] += y
        if len(refs) == 3:
            res_ref, gain_ref, _ = refs
            o_ref[...] = res_ref[...] + _rms(o_ref[...], gain_ref[...])

    @pl.when(i >= nv_ref[0])
    def _():
        o_ref[...] = jnp.zeros_like(o_ref)


def _ffn(xs, tile_expert, n_valid, wg, wu, wd, tm, f_chunk, residual=None, gain=None):
    rows, d = xs.shape
    d_ff = wg.shape[2]
    assert d_ff % f_chunk == 0 and rows % tm == 0
    xmap = lambda i, te, nv: (jnp.minimum(i, nv[0] - 1), 0)
    wmap = lambda i, te, nv: (te[i], 0, 0)
    in_specs = [
        pl.BlockSpec((tm, d), xmap),
        _resident((None, d, d_ff), wmap),
        _resident((None, d, d_ff), wmap),
        _resident((None, d_ff, d), wmap),
    ]
    args = [tile_expert, n_valid, xs, wg, wu, wd]
    if residual is not None:
        in_specs += [pl.BlockSpec((tm, d), xmap), pl.BlockSpec((1, d), lambda i, te, nv: (0, 0))]
        args += [residual, gain.reshape(1, d)]
    grid_spec = pltpu.PrefetchScalarGridSpec(
        num_scalar_prefetch=2,
        grid=(rows // tm,),
        in_specs=in_specs,
        out_specs=pl.BlockSpec((tm, d), lambda i, te, nv: (i, 0)),
    )
    return pl.pallas_call(
        functools.partial(_ffn_kernel, f_chunk=f_chunk),
        grid_spec=grid_spec,
        out_shape=jax.ShapeDtypeStruct((rows, d), F32),
        compiler_params=_params(("arbitrary",)),
        name="ffn",
    )(*args)


def _dispatch_kernel(d1_ref, d2_ref, h_ref, init_ref, out_ref, sem):
    del init_ref

    def row_copies(j):
        src = h_ref.at[pl.ds(j, 1)]
        return (pltpu.make_async_copy(src, out_ref.at[pl.ds(d1_ref[j], 1)], sem),
                pltpu.make_async_copy(src, out_ref.at[pl.ds(d2_ref[j], 1)], sem))

    def start(j, carry):
        for c in row_copies(j):
            c.start()
        return carry

    def wait(j, carry):
        for c in row_copies(j):
            c.wait()
        return carry

    n = h_ref.shape[0]
    lax.fori_loop(0, n, start, 0, unroll=8)
    lax.fori_loop(0, n, wait, 0, unroll=8)


def _dispatch(h, dest, n_rows, tm):
    t, d = h.shape
    assert tm % SMEM_BLOCK_WORDS == 0 and t % tm == 0
    nblk = t // tm
    any_spec = pl.BlockSpec(memory_space=pl.ANY)
    return pl.pallas_call(
        _dispatch_kernel,
        grid=(nblk,),
        in_specs=[pl.BlockSpec((tm,), lambda i: (i,), memory_space=pltpu.SMEM),
                  pl.BlockSpec((tm,), lambda i: (i + nblk,), memory_space=pltpu.SMEM),
                  pl.BlockSpec((tm, d), lambda i: (i, 0)), any_spec],
        out_specs=any_spec,
        out_shape=jax.ShapeDtypeStruct((n_rows, d), h.dtype),
        scratch_shapes=[pltpu.SemaphoreType.DMA(())],
        input_output_aliases={3: 0},
        compiler_params=_params(("arbitrary",)),
        name="moe_dispatch",
    )(dest, dest, h, jnp.zeros((n_rows, d), h.dtype))


def _combine_kernel(d1_ref, d2_ref, ys_ref, rp_ref, x_ref, g_ref, o_ref, ya, yb, sem):
    def row_copies(j):
        dst = pl.ds(j, 1)
        return (pltpu.make_async_copy(ys_ref.at[pl.ds(d1_ref[j], 1)], ya.at[dst], sem),
                pltpu.make_async_copy(ys_ref.at[pl.ds(d2_ref[j], 1)], yb.at[dst], sem))

    def start(j, carry):
        for c in row_copies(j):
            c.start()
        return carry

    def wait(j, carry):
        for c in row_copies(j):
            c.wait()
        return carry

    n = x_ref.shape[0]
    lax.fori_loop(0, n, start, 0, unroll=8)
    lax.fori_loop(0, n, wait, 0, unroll=8)
    y = rp_ref[:, 0:1] * ya[...] + rp_ref[:, 1:2] * yb[...]
    o_ref[...] = x_ref[...] + _rms(y, g_ref[...])


def _combine(ys, dest, route_p, x, g, tm):
    t, d = x.shape
    assert tm % SMEM_BLOCK_WORDS == 0 and t % tm == 0
    nblk = t // tm
    row = lambda i: (i, 0)
    return pl.pallas_call(
        _combine_kernel,
        grid=(nblk,),
        in_specs=[pl.BlockSpec((tm,), lambda i: (i,), memory_space=pltpu.SMEM),
                  pl.BlockSpec((tm,), lambda i: (i + nblk,), memory_space=pltpu.SMEM),
                  pl.BlockSpec(memory_space=pl.ANY),
                  pl.BlockSpec((tm, LANES), row), pl.BlockSpec((tm, d), row),
                  pl.BlockSpec((1, d), lambda i: (0, 0))],
        out_specs=pl.BlockSpec((tm, d), row),
        out_shape=jax.ShapeDtypeStruct((t, d), F32),
        scratch_shapes=[pltpu.VMEM((tm, d), F32), pltpu.VMEM((tm, d), F32),
                        pltpu.SemaphoreType.DMA(())],
        compiler_params=_params(("arbitrary",)),
        name="moe_combine",
    )(dest, dest, ys, route_p, x, g.reshape(1, d))


def _routing_tables(route_i, n_exp, tm):
    t = route_i.shape[0]
    e_flat = jnp.concatenate([route_i[:, 0], route_i[:, 1]])
    onehot = (e_flat[:, None] == jnp.arange(n_exp, dtype=jnp.int32)[None, :]).astype(jnp.int32)
    counts = jnp.sum(onehot, axis=0)
    padded = ((counts + tm - 1) // tm) * tm
    pad_end = jnp.cumsum(padded)
    pad_start = pad_end - padded
    dest = jnp.sum((jnp.cumsum(onehot, axis=0) - onehot + pad_start[None, :]) * onehot, axis=1)

    n_rows = TOP_K * t + n_exp * tm
    n_tiles = n_rows // tm
    tile_first_row = jnp.arange(n_tiles, dtype=jnp.int32) * tm
    n_valid = pad_end[-1] // tm
    tile_first_row = jnp.minimum(tile_first_row, (n_valid - 1) * tm)
    tile_expert = jnp.sum((pad_end[None, :] <= tile_first_row[:, None]).astype(jnp.int32), axis=1)
    return (dest.astype(jnp.int32), jnp.minimum(tile_expert, n_exp - 1).astype(jnp.int32),
            n_valid.astype(jnp.int32).reshape(1), n_rows)


def _tile(n, want):
    t = min(n, want)
    while n % t:
        t -= SUBLANES
    return t


def kernel(x, positions, pre_mix_g, w_in, conv_w, conv_b, w_rgate, b_rgate, w_igate, b_igate,
           lru_lambda, attn_out_g, lru_out_g, w_out, post_mix_g, pre_ffn_g, post_ffn_g,
           dense_w_gate, dense_w_up, dense_w_down, router_w, moe_w_gate, moe_w_up, moe_w_down):
    batch, seq, d = x.shape
    depth = w_in.shape[0]
    t = batch * seq
    d_ff = dense_w_gate.shape[-1]
    n_exp = moe_w_gate.shape[1]
    assert w_in.shape[2] == 3 * ATTN_WIDTH + 2 * LRU_WIDTH
    assert seq % (DILATED_PAIRS[-1][1] * Q_BLOCK) == 0

    tm_proj = _tile(t, 512)
    tm_mix = _tile(t, 256)
    tm_ffn = _tile(t, 512)
    ts_lru = _tile(seq, 512)
    tm_rows = SMEM_BLOCK_WORDS
    f_chunk = 512 if d_ff % 512 == 0 else d_ff

    xf = x.reshape(t, d).astype(F32)
    pos = positions.reshape(t, 1).astype(jnp.int32)

    for l in range(depth):
        q, k, v, u, gate = _in_proj(xf, pos, pre_mix_g[l].reshape(1, d), w_in[l].astype(BF16),
                                    tm_proj)
        attn = _attention(q, k, v, batch, seq)
        rec = _lru(u, gate, conv_w[l], conv_b[l], w_rgate[l], b_rgate[l], w_igate[l], b_igate[l],
                   lru_lambda[l], batch, seq, ts_lru)
        j = l // 2
        if l % 2 == 0:
            x1, h2 = _mix_out(attn, rec, xf, attn_out_g[l], lru_out_g[l], w_out[l],
                              post_mix_g[l], pre_ffn_g[l], None, BF16, tm_mix)
            n_tiles = t // tm_ffn
            xf = _ffn(h2, jnp.zeros((n_tiles,), jnp.int32), jnp.full((1,), n_tiles, jnp.int32),
                      dense_w_gate[j][None].astype(BF16), dense_w_up[j][None].astype(BF16),
                      dense_w_down[j][None].astype(BF16), tm_ffn, f_chunk,
                      residual=x1, gain=post_ffn_g[l])
        else:
            x1, h2, route_i, route_p = _mix_out(
                attn, rec, xf, attn_out_g[l], lru_out_g[l], w_out[l], post_mix_g[l],
                pre_ffn_g[l], router_w[j], F32, tm_mix)
            dest, tile_expert, n_valid, n_rows = _routing_tables(route_i, n_exp, tm_ffn)
            xs = _dispatch(h2, dest, n_rows, tm_rows)
            ys = _ffn(xs, tile_expert, n_valid, moe_w_gate[j].astype(BF16),
                      moe_w_up[j].astype(BF16), moe_w_down[j].astype(BF16), tm_ffn, f_chunk)
            xf = _combine(ys, dest, route_p, x1, post_ffn_g[l], tm_rows)
    return xf.reshape(batch, seq, d).astype(x.dtype)
```

```python
import functools
import math

import jax
import jax.numpy as jnp
from jax import lax
from jax.experimental import pallas as pl
from jax.experimental.pallas import tpu as pltpu

ATTN_HEADS = 8
HEAD_DIM = 64
ATTN_WIDTH = ATTN_HEADS * HEAD_DIM
LRU_BLOCKS = 8
LRU_BLOCK_W = 64
LRU_WIDTH = LRU_BLOCKS * LRU_BLOCK_W
DILATED_PAIRS = ((128, 1), (512, 4), (2048, 16))
Q_BLOCK = 128
ROT_DIM = HEAD_DIM // 4
ROT_HALF = ROT_DIM // 2
ROPE_THETA = 500000.0
ATTN_SCALE = 1.0 / math.sqrt(HEAD_DIM)
NEG_INF = -1e30
CONV_W = 4
LRU_C = 8.0
TOP_K = 2
RMS_EPS = 1e-6

LANES = 128
SUBLANES = 8
SMEM_BLOCK_WORDS = 1024
VMEM_LIMIT_BYTES = 56 * 1024 * 1024

F32 = jnp.float32
BF16 = jnp.bfloat16


def _params(semantics):
    return pltpu.CompilerParams(dimension_semantics=semantics,
                                vmem_limit_bytes=VMEM_LIMIT_BYTES)


def _resident(block_shape, index_map):
    return pl.BlockSpec(block_shape, index_map, pipeline_mode=pl.Buffered(1))


def _rms(x, g):
    var = jnp.mean(x * x, axis=-1, keepdims=True)
    return x * lax.rsqrt(var + RMS_EPS) * g


def _in_proj_kernel(x_ref, pos_ref, g_ref, w_ref, rope_ref,
                    q_ref, k_ref, v_ref, u_ref, gate_ref):
    h = _rms(x_ref[...], g_ref[...]).astype(BF16)
    ang = pos_ref[...].astype(F32) * rope_ref[0:1, :]
    cs = jnp.cos(ang)
    sn = jnp.sin(ang)
    s_lo = sn * rope_ref[1:2, :]
    s_hi = sn * rope_ref[2:3, :]

    def rotary(z):
        return (z * cs + pltpu.roll(z, ROT_HALF, 1) * s_lo
                + pltpu.roll(z, LANES - ROT_HALF, 1) * s_hi)

    w = ATTN_WIDTH
    for sec, out in ((0, q_ref), (1, k_ref)):
        z = jnp.dot(h, w_ref[:, sec * w:(sec + 1) * w], preferred_element_type=F32)
        for cb in range(w // LANES):
            sl = slice(cb * LANES, (cb + 1) * LANES)
            out[:, sl] = rotary(z[:, sl]).astype(BF16)
    v_ref[...] = jnp.dot(h, w_ref[:, 2 * w:3 * w], preferred_element_type=F32).astype(BF16)
    u_ref[...] = jnp.dot(h, w_ref[:, 3 * w:3 * w + LRU_WIDTH], preferred_element_type=F32)
    gate_ref[...] = jnp.dot(h, w_ref[:, 3 * w + LRU_WIDTH:], preferred_element_type=F32)


def _rope_table():
    lane = jnp.arange(LANES, dtype=jnp.int32) % HEAD_DIM
    inv_freq = ROPE_THETA ** (-jnp.arange(ROT_HALF, dtype=F32) / ROT_HALF)
    freq = jnp.where(lane < ROT_DIM, inv_freq[lane % ROT_HALF], 0.0)
    lo = jnp.where((lane >= ROT_HALF) & (lane < ROT_DIM), 1.0, 0.0)
    hi = jnp.where(lane < ROT_HALF, -1.0, 0.0)
    return jnp.stack([freq, lo, hi]).astype(F32)


def _in_proj(x, pos, g, w_in, tm):
    t, d = x.shape
    ncol = w_in.shape[1]
    row = lambda i: (i, 0)
    const = lambda i: (0, 0)
    outs = pl.pallas_call(
        _in_proj_kernel,
        grid=(t // tm,),
        in_specs=[
            pl.BlockSpec((tm, d), row),
            pl.BlockSpec((tm, 1), row),
            pl.BlockSpec((1, d), const),
            _resident((d, ncol), const),
            pl.BlockSpec((3, LANES), const),
        ],
        out_specs=[pl.BlockSpec((tm, ATTN_WIDTH), row)] * 3
        + [pl.BlockSpec((tm, LRU_WIDTH), row)] * 2,
        out_shape=[jax.ShapeDtypeStruct((t, ATTN_WIDTH), BF16)] * 3
        + [jax.ShapeDtypeStruct((t, LRU_WIDTH), F32)] * 2,
        compiler_params=_params(("parallel",)),
        name="in_proj",
    )(x, pos, g, w_in, _rope_table())
    return outs


ATTN_TILE = DILATED_PAIRS[-1][1] * Q_BLOCK


def _rows(start, size, stride):
    return pl.ds(start, size) if stride == 1 else pl.ds(start, size, stride=stride)


ATTN_PIPE_WIDTH = 4


def _attn_kernel(q_ref, kc_ref, vc_ref, kp_ref, vp_ref, bias_ref, o_ref,
                 q0_scr, q1_scr, k_scr, v_scr, ob0, ob1, ob2, ls0, ls1, ls2, *slots):
    s_scrs, m_scrs = slots[:len(slots) // 2], slots[len(slots) // 2:]
    j = pl.program_id(2)
    ts = q_ref.shape[0]
    lane = lax.broadcasted_iota(jnp.int32, (1, LANES), 1)
    head0 = lane < HEAD_DIM

    qf = q_ref[...].astype(F32) * ATTN_SCALE
    q0_scr[...] = jnp.where(head0, qf, 0.0)
    q1_scr[...] = jnp.where(head0, 0.0, qf)
    k_scr[0:ts, :] = kp_ref[...].astype(F32)
    k_scr[ts:2 * ts, :] = kc_ref[...].astype(F32)
    v_scr[0:ts, :] = vp_ref[...].astype(F32)
    v_scr[ts:2 * ts, :] = vc_ref[...].astype(F32)

    nt = (((1,), (1,)), ((), ()))
    units = ts // Q_BLOCK
    width = len(s_scrs) // 2
    groups = units // width
    assert groups % 2 == 0 and groups >= 2
    for (_, dil), ob, ls in zip(DILATED_PAIRS, (ob0, ob1, ob2), (ls0, ls1, ls2)):
        shift = dil.bit_length() - 1

        def rows_of(u, dil=dil, shift=shift):
            cls = jnp.bitwise_and(u, dil - 1)
            blk = lax.shift_right_logical(u, shift)
            q_start = cls + dil * Q_BLOCK * blk
            q_rows = _rows(q_start, Q_BLOCK, dil)
            kv_rows = _rows(ts + q_start - dil * Q_BLOCK, 2 * Q_BLOCK, dil)
            return blk, q_rows, kv_rows

        def scores(u, slot):
            blk, q_rows, kv_rows = rows_of(u)
            lhs = jnp.concatenate([q0_scr[q_rows, :], q1_scr[q_rows, :]], axis=0).astype(BF16)
            k2 = k_scr[kv_rows, :].astype(BF16)
            no_prev = jnp.logical_and(j == 0, blk == 0).astype(jnp.int32)
            bias = bias_ref[pl.ds(no_prev * 2 * Q_BLOCK, 2 * Q_BLOCK), :]
            s = lax.dot_general(lhs, k2, nt, preferred_element_type=F32) + bias
            s_scrs[slot][...] = s
            m_scrs[slot][...] = jnp.broadcast_to(jnp.max(s, axis=-1, keepdims=True),
                                                 (2 * Q_BLOCK, LANES))

        def finish(u, slot, ob=ob, ls=ls):
            _, q_rows, kv_rows = rows_of(u)
            m = m_scrs[slot][...]
            p = jnp.exp(s_scrs[slot][...] - jnp.concatenate([m, m], axis=1))
            den = jnp.sum(p, axis=-1, keepdims=True)
            v2 = v_scr[kv_rows, :].astype(BF16)
            o = jnp.dot(p.astype(BF16), v2, preferred_element_type=F32) / den
            lse = m + jnp.log(den)
            ob[q_rows, :] = jnp.where(head0, o[0:Q_BLOCK], o[Q_BLOCK:])
            ls[q_rows, :] = jnp.where(head0, lse[0:Q_BLOCK], lse[Q_BLOCK:])

        def scores_group(g, half):
            for w in range(width):
                scores(g * width + w, half * width + w)

        def finish_group(g, half):
            for w in range(width):
                finish(g * width + w, half * width + w)

        scores_group(jnp.int32(0), 0)

        def pair(i, carry):
            scores_group(2 * i + 1, 1)
            finish_group(2 * i, 0)
            scores_group(2 * i + 2, 0)
            finish_group(2 * i + 1, 1)
            return carry

        lax.fori_loop(0, groups // 2 - 1, pair, 0)
        scores_group(jnp.int32(groups - 1), 1)
        finish_group(jnp.int32(groups - 2), 0)
        finish_group(jnp.int32(groups - 1), 1)

    chunk = 2 * Q_BLOCK

    def mix(c, carry):
        r = pl.ds(pl.multiple_of(c * chunk, chunk), chunk)
        la, lb, lc = ls0[r, :], ls1[r, :], ls2[r, :]
        m = jnp.maximum(jnp.maximum(la, lb), lc)
        ea, eb, ec = jnp.exp(la - m), jnp.exp(lb - m), jnp.exp(lc - m)
        o_ref[r, :] = (ea * ob0[r, :] + eb * ob1[r, :] + ec * ob2[r, :]) / (ea + eb + ec)
        return carry

    lax.fori_loop(0, ts // chunk, mix, 0)


def _attn_bias():
    qi = jnp.arange(2 * Q_BLOCK, dtype=jnp.int32)[:, None] % Q_BLOCK
    kj = jnp.arange(2 * Q_BLOCK, dtype=jnp.int32)[None, :]
    delta = Q_BLOCK + qi - kj
    band = (delta >= 0) & (delta <= Q_BLOCK)
    with_prev = jnp.where(band, 0.0, NEG_INF)
    no_prev = jnp.where(band & (kj >= Q_BLOCK), 0.0, NEG_INF)
    return jnp.concatenate([with_prev, no_prev], axis=0).astype(F32)


def _attention(q, k, v, batch, seq):
    ts = ATTN_TILE
    assert seq % ts == 0
    nt = seq // ts
    cur = pl.BlockSpec((ts, LANES), lambda b, c, j: (b * nt + j, c))
    prev = pl.BlockSpec((ts, LANES), lambda b, c, j: (b * nt + jnp.maximum(j - 1, 0), c))
    tile_scr = pltpu.VMEM((ts, LANES), F32)
    pair_scr = pltpu.VMEM((2 * ts, LANES), F32)
    return pl.pallas_call(
        _attn_kernel,
        grid=(batch, ATTN_WIDTH // LANES, nt),
        in_specs=[cur, cur, cur, prev, prev,
                  pl.BlockSpec((4 * Q_BLOCK, 2 * Q_BLOCK), lambda b, c, j: (0, 0))],
        out_specs=cur,
        out_shape=jax.ShapeDtypeStruct((batch * seq, ATTN_WIDTH), F32),
        scratch_shapes=[tile_scr, tile_scr, pair_scr, pair_scr] + [tile_scr] * 6
        + [pltpu.VMEM((2 * Q_BLOCK, 2 * Q_BLOCK), F32)] * (2 * ATTN_PIPE_WIDTH)
        + [pltpu.VMEM((2 * Q_BLOCK, LANES), F32)] * (2 * ATTN_PIPE_WIDTH),
        compiler_params=_params(("parallel", "parallel", "parallel")),
        name="attention",
    )(q, k, v, k, v, _attn_bias())


def _lru_kernel(u_ref, gate_ref, cw_ref, cb_ref, wr_ref, br_ref, wi_ref, bi_ref, lam_ref,
                out_ref, ubuf, hcarry, a_scr, b_scr):
    ts, c = u_ref.shape

    @pl.when(pl.program_id(1) == 0)
    def _():
        ubuf[0:SUBLANES, :] = jnp.zeros((SUBLANES, c), F32)
        hcarry[...] = jnp.zeros_like(hcarry)

    ubuf[SUBLANES:SUBLANES + ts, :] = u_ref[...]
    ext = ubuf[...]
    uc = cb_ref[...]
    for tap in range(CONV_W):
        back = CONV_W - 1 - tap
        shifted = pltpu.roll(ext, back, 0) if back else ext
        uc = uc + shifted[SUBLANES:SUBLANES + ts, :] * cw_ref[tap:tap + 1, :]
    ubuf[0:SUBLANES, :] = ubuf[ts:ts + SUBLANES, :]

    ucb = uc.astype(BF16)
    r = jax.nn.sigmoid(jnp.dot(ucb, wr_ref[...], preferred_element_type=F32) + br_ref[...])
    ig = jax.nn.sigmoid(jnp.dot(ucb, wi_ref[...], preferred_element_type=F32) + bi_ref[...])
    nl = -lam_ref[...]
    softplus = jnp.maximum(nl, 0.0) + jnp.log1p(jnp.exp(-jnp.abs(nl)))
    log_a = -LRU_C * r * softplus
    a = jnp.exp(log_a)
    mult = jnp.sqrt(1.0 - jnp.exp(2.0 * log_a))
    b = mult * ig * uc

    groups = ts // SUBLANES
    a3 = a.reshape(groups, SUBLANES, c)
    b3 = b.reshape(groups, SUBLANES, c)
    sub = lax.broadcasted_iota(jnp.int32, (groups, SUBLANES, c), 1)
    shift = 1
    while shift < SUBLANES:
        valid = sub >= shift
        a_sh = jnp.where(valid, pltpu.roll(a3, shift, 1), 1.0)
        b_sh = jnp.where(valid, pltpu.roll(b3, shift, 1), 0.0)
        b3 = a3 * b_sh + b3
        a3 = a3 * a_sh
        shift *= 2
    a_scr[...] = a3.reshape(ts, c)
    b_scr[...] = b3.reshape(ts, c)

    def group(g, h_prev):
        rows = pl.ds(pl.multiple_of(g * SUBLANES, SUBLANES), SUBLANES)
        h = a_scr[rows, :] * h_prev + b_scr[rows, :]
        b_scr[rows, :] = h
        return jnp.broadcast_to(h[SUBLANES - 1:SUBLANES, :], (SUBLANES, c))

    hcarry[...] = lax.fori_loop(0, groups, group, hcarry[...], unroll=8)
    out_ref[...] = jax.nn.gelu(gate_ref[...]) * b_scr[...]


def _block_diag(w):
    nb, bw, _ = w.shape
    eye = jnp.eye(nb, dtype=w.dtype)
    return jnp.einsum("gij,gh->gihj", w, eye).reshape(nb * bw, nb * bw)


def _lru(u, gate, conv_w, conv_b, w_r, b_r, w_i, b_i, lam, batch, seq, ts):
    c = LRU_WIDTH
    nblk = seq // ts
    row = lambda b, j: (b * nblk + j, 0)
    const = lambda b, j: (0, 0)
    vec = lambda a: a.reshape(1, c).astype(F32)
    return pl.pallas_call(
        _lru_kernel,
        grid=(batch, nblk),
        in_specs=[
            pl.BlockSpec((ts, c), row),
            pl.BlockSpec((ts, c), row),
            pl.BlockSpec((CONV_W, c), const),
            pl.BlockSpec((1, c), const),
            pl.BlockSpec((c, c), const),
            pl.BlockSpec((1, c), const),
            pl.BlockSpec((c, c), const),
            pl.BlockSpec((1, c), const),
            pl.BlockSpec((1, c), const),
        ],
        out_specs=pl.BlockSpec((ts, c), row),
        out_shape=jax.ShapeDtypeStruct((batch * seq, c), F32),
        scratch_shapes=[pltpu.VMEM((ts + SUBLANES, c), F32), pltpu.VMEM((SUBLANES, c), F32),
                        pltpu.VMEM((ts, c), F32), pltpu.VMEM((ts, c), F32)],
        compiler_params=_params(("parallel", "arbitrary")),
        name="rg_lru",
    )(u, gate, conv_w.astype(F32), vec(conv_b), _block_diag(w_r).astype(BF16), vec(b_r),
      _block_diag(w_i).astype(BF16), vec(b_i), vec(lam))


def _mix_kernel(*refs, n_exp):
    with_router = n_exp > 0
    attn_ref, rec_ref, x_ref, ga_ref, gr_ref, wo_ref, gpost_ref, gpre_ref = refs[:8]
    if with_router:
        rw_ref, x1_ref, h2_ref, ri_ref, rp_ref = refs[8:]
    else:
        x1_ref, h2_ref = refs[8:]

    na = _rms(attn_ref[...], ga_ref[...]).astype(BF16)
    nr = _rms(rec_ref[...], gr_ref[...]).astype(BF16)
    mixed = (jnp.dot(na, wo_ref[0:ATTN_WIDTH, :], preferred_element_type=F32)
             + jnp.dot(nr, wo_ref[ATTN_WIDTH:, :], preferred_element_type=F32))
    x1 = x_ref[...] + _rms(mixed, gpost_ref[...])
    x1_ref[...] = x1
    hn = _rms(x1, gpre_ref[...])
    h2_ref[...] = hn.astype(h2_ref.dtype)

    if with_router:
        rw = rw_ref[...]
        h_hi, r_hi = hn.astype(BF16), rw.astype(BF16)
        h_lo = (hn - h_hi.astype(F32)).astype(BF16)
        r_lo = (rw - r_hi.astype(F32)).astype(BF16)
        logits = (jnp.dot(h_hi, r_hi, preferred_element_type=F32)
                  + jnp.dot(h_hi, r_lo, preferred_element_type=F32)
                  + jnp.dot(h_lo, r_hi, preferred_element_type=F32))
        tm = logits.shape[0]
        lane = lax.broadcasted_iota(jnp.int32, (tm, LANES), 1)
        logits = jnp.where(lane < n_exp, logits, -jnp.inf)
        m1 = jnp.max(logits, axis=-1, keepdims=True)
        i1 = jnp.min(jnp.where(logits == m1, lane, n_exp), axis=-1, keepdims=True)
        rest = jnp.where(lane == i1, -jnp.inf, logits)
        m2 = jnp.max(rest, axis=-1, keepdims=True)
        i2 = jnp.min(jnp.where(rest == m2, lane, n_exp), axis=-1, keepdims=True)
        e2 = jnp.exp(m2 - m1)
        p1 = 1.0 / (1.0 + e2)
        p2 = e2 / (1.0 + e2)
        ri_ref[...] = jnp.where(lane == 0, i1, jnp.where(lane == 1, i2, 0))
        rp_ref[...] = jnp.where(lane == 0, p1, jnp.where(lane == 1, p2, 0.0))


def _mix_out(attn, rec, x, ga, gr, w_out, gpost, gpre, router_w, h2_dtype, tm):
    t, d = x.shape
    row = lambda i: (i, 0)
    const = lambda i: (0, 0)
    aw = ATTN_WIDTH
    with_router = router_w is not None
    n_exp = 0
    args = [attn, rec, x, ga.reshape(1, aw), gr.reshape(1, LRU_WIDTH),
            w_out.astype(BF16), gpost.reshape(1, d), gpre.reshape(1, d)]
    in_specs = ([pl.BlockSpec((tm, aw), row)] * 2 + [pl.BlockSpec((tm, d), row),
                pl.BlockSpec((1, aw), const), pl.BlockSpec((1, LRU_WIDTH), const),
                _resident(w_out.shape, const), pl.BlockSpec((1, d), const),
                pl.BlockSpec((1, d), const)])
    out_specs = [pl.BlockSpec((tm, d), row), pl.BlockSpec((tm, d), row)]
    out_shape = [jax.ShapeDtypeStruct((t, d), F32), jax.ShapeDtypeStruct((t, d), h2_dtype)]
    if with_router:
        n_exp = router_w.shape[1]
        rw = jnp.pad(router_w.astype(F32), ((0, 0), (0, LANES - n_exp)))
        args.append(rw)
        in_specs.append(pl.BlockSpec(rw.shape, const))
        out_specs += [pl.BlockSpec((tm, LANES), row)] * 2
        out_shape += [jax.ShapeDtypeStruct((t, LANES), jnp.int32),
                      jax.ShapeDtypeStruct((t, LANES), F32)]
    return pl.pallas_call(
        functools.partial(_mix_kernel, n_exp=n_exp),
        grid=(t // tm,),
        in_specs=in_specs,
        out_specs=out_specs,
        out_shape=out_shape,
        compiler_params=_params(("parallel",)),
        name="mix_out_router" if with_router else "mix_out",
    )(*args)


def _ffn_kernel(te_ref, nv_ref, x_ref, wg_ref, wu_ref, wd_ref, *refs, f_chunk):
    o_ref = refs[-1]
    i = pl.program_id(0)
    d_ff = wg_ref.shape[1]

    @pl.when(i < nv_ref[0])
    def _():
        x = x_ref[...].astype(BF16)
        for c in range(d_ff // f_chunk):
            sl = slice(c * f_chunk, (c + 1) * f_chunk)
            g = jnp.dot(x, wg_ref[:, sl], preferred_element_type=F32)
            u = jnp.dot(x, wu_ref[:, sl], preferred_element_type=F32)
            a = (g * jax.nn.sigmoid(g) * u).astype(BF16)
            y = jnp.dot(a, wd_ref[sl, :], preferred_element_type=F32)
            if c == 0:
                o_ref[...] = y
            else:
                o_ref[...] += y
        if len(refs) == 3:
            res_ref, gain_ref, _ = refs
            o_ref[...] = res_ref[...] + _rms(o_ref[...], gain_ref[...])

    @pl.when(i >= nv_ref[0])
    def _():
        o_ref[...] = jnp.zeros_like(o_ref)


def _ffn(xs, tile_expert, n_valid, wg, wu, wd, tm, f_chunk, residual=None, gain=None):
    rows, d = xs.shape
    d_ff = wg.shape[2]
    assert d_ff % f_chunk == 0 and rows % tm == 0
    xmap = lambda i, te, nv: (jnp.minimum(i, nv[0] - 1), 0)
    wmap = lambda i, te, nv: (te[i], 0, 0)
    in_specs = [
        pl.BlockSpec((tm, d), xmap),
        _resident((None, d, d_ff), wmap),
        _resident((None, d, d_ff), wmap),
        _resident((None, d_ff, d), wmap),
    ]
    args = [tile_expert, n_valid, xs, wg, wu, wd]
    if residual is not None:
        in_specs += [pl.BlockSpec((tm, d), xmap), pl.BlockSpec((1, d), lambda i, te, nv: (0, 0))]
        args += [residual, gain.reshape(1, d)]
    grid_spec = pltpu.PrefetchScalarGridSpec(
        num_scalar_prefetch=2,
        grid=(rows // tm,),
        in_specs=in_specs,
        out_specs=pl.BlockSpec((tm, d), lambda i, te, nv: (i, 0)),
    )
    return pl.pallas_call(
        functools.partial(_ffn_kernel, f_chunk=f_chunk),
        grid_spec=grid_spec,
        out_shape=jax.ShapeDtypeStruct((rows, d), F32),
        compiler_params=_params(("arbitrary",)),
        name="ffn",
    )(*args)


def _dispatch_kernel(d1_ref, d2_ref, h_ref, init_ref, out_ref, sem):
    del init_ref

    def row_copies(j):
        src = h_ref.at[pl.ds(j, 1)]
        return (pltpu.make_async_copy(src, out_ref.at[pl.ds(d1_ref[j], 1)], sem),
                pltpu.make_async_copy(src, out_ref.at[pl.ds(d2_ref[j], 1)], sem))

    def start(j, carry):
        for c in row_copies(j):
            c.start()
        return carry

    def wait(j, carry):
        for c in row_copies(j):
            c.wait()
        return carry

    n = h_ref.shape[0]
    lax.fori_loop(0, n, start, 0, unroll=8)
    lax.fori_loop(0, n, wait, 0, unroll=8)


def _dispatch(h, dest, n_rows, tm):
    t, d = h.shape
    assert tm % SMEM_BLOCK_WORDS == 0 and t % tm == 0
    nblk = t // tm
    any_spec = pl.BlockSpec(memory_space=pl.ANY)
    return pl.pallas_call(
        _dispatch_kernel,
        grid=(nblk,),
        in_specs=[pl.BlockSpec((tm,), lambda i: (i,), memory_space=pltpu.SMEM),
                  pl.BlockSpec((tm,), lambda i: (i + nblk,), memory_space=pltpu.SMEM),
                  pl.BlockSpec((tm, d), lambda i: (i, 0)), any_spec],
        out_specs=any_spec,
        out_shape=jax.ShapeDtypeStruct((n_rows, d), h.dtype),
        scratch_shapes=[pltpu.SemaphoreType.DMA(())],
        input_output_aliases={3: 0},
        compiler_params=_params(("arbitrary",)),
        name="moe_dispatch",
    )(dest, dest, h, jnp.zeros((n_rows, d), h.dtype))


def _combine_kernel(d1_ref, d2_ref, ys_ref, rp_ref, x_ref, g_ref, o_ref, ya, yb, sem):
    def row_copies(j):
        dst = pl.ds(j, 1)
        return (pltpu.make_async_copy(ys_ref.at[pl.ds(d1_ref[j], 1)], ya.at[dst], sem),
                pltpu.make_async_copy(ys_ref.at[pl.ds(d2_ref[j], 1)], yb.at[dst], sem))

    def start(j, carry):
        for c in row_copies(j):
            c.start()
        return carry

    def wait(j, carry):
        for c in row_copies(j):
            c.wait()
        return carry

    n = x_ref.shape[0]
    lax.fori_loop(0, n, start, 0, unroll=8)
    lax.fori_loop(0, n, wait, 0, unroll=8)
    y = rp_ref[:, 0:1] * ya[...] + rp_ref[:, 1:2] * yb[...]
    o_ref[...] = x_ref[...] + _rms(y, g_ref[...])


def _combine(ys, dest, route_p, x, g, tm):
    t, d = x.shape
    assert tm % SMEM_BLOCK_WORDS == 0 and t % tm == 0
    nblk = t // tm
    row = lambda i: (i, 0)
    return pl.pallas_call(
        _combine_kernel,
        grid=(nblk,),
        in_specs=[pl.BlockSpec((tm,), lambda i: (i,), memory_space=pltpu.SMEM),
                  pl.BlockSpec((tm,), lambda i: (i + nblk,), memory_space=pltpu.SMEM),
                  pl.BlockSpec(memory_space=pl.ANY),
                  pl.BlockSpec((tm, LANES), row), pl.BlockSpec((tm, d), row),
                  pl.BlockSpec((1, d), lambda i: (0, 0))],
        out_specs=pl.BlockSpec((tm, d), row),
        out_shape=jax.ShapeDtypeStruct((t, d), F32),
        scratch_shapes=[pltpu.VMEM((tm, d), F32), pltpu.VMEM((tm, d), F32),
                        pltpu.SemaphoreType.DMA(())],
        compiler_params=_params(("arbitrary",)),
        name="moe_combine",
    )(dest, dest, ys, route_p, x, g.reshape(1, d))


def _routing_tables(route_i, n_exp, tm):
    t = route_i.shape[0]
    e_flat = jnp.concatenate([route_i[:, 0], route_i[:, 1]])
    onehot = (e_flat[:, None] == jnp.arange(n_exp, dtype=jnp.int32)[None, :]).astype(jnp.int32)
    counts = jnp.sum(onehot, axis=0)
    padded = ((counts + tm - 1) // tm) * tm
    pad_end = jnp.cumsum(padded)
    pad_start = pad_end - padded
    dest = jnp.sum((jnp.cumsum(onehot, axis=0) - onehot + pad_start[None, :]) * onehot, axis=1)

    n_rows = TOP_K * t + n_exp * tm
    n_tiles = n_rows // tm
    tile_first_row = jnp.arange(n_tiles, dtype=jnp.int32) * tm
    n_valid = pad_end[-1] // tm
    tile_first_row = jnp.minimum(tile_first_row, (n_valid - 1) * tm)
    tile_expert = jnp.sum((pad_end[None, :] <= tile_first_row[:, None]).astype(jnp.int32), axis=1)
    return (dest.astype(jnp.int32), jnp.minimum(tile_expert, n_exp - 1).astype(jnp.int32),
            n_valid.astype(jnp.int32).reshape(1), n_rows)


def _tile(n, want):
    t = min(n, want)
    while n % t:
        t -= SUBLANES
    return t


def kernel(x, positions, pre_mix_g, w_in, conv_w, conv_b, w_rgate, b_rgate, w_igate, b_igate,
           lru_lambda, attn_out_g, lru_out_g, w_out, post_mix_g, pre_ffn_g, post_ffn_g,
           dense_w_gate, dense_w_up, dense_w_down, router_w, moe_w_gate, moe_w_up, moe_w_down):
    batch, seq, d = x.shape
    depth = w_in.shape[0]
    t = batch * seq
    d_ff = dense_w_gate.shape[-1]
    n_exp = moe_w_gate.shape[1]
    assert w_in.shape[2] == 3 * ATTN_WIDTH + 2 * LRU_WIDTH
    assert seq % (DILATED_PAIRS[-1][1] * Q_BLOCK) == 0

    tm_proj = _tile(t, 512)
    tm_mix = _tile(t, 256)
    tm_ffn = _tile(t, 512)
    ts_lru = _tile(seq, 512)
    tm_rows = SMEM_BLOCK_WORDS
    f_chunk = 512 if d_ff % 512 == 0 else d_ff

    xf = x.reshape(t, d).astype(F32)
    pos = positions.reshape(t, 1).astype(jnp.int32)

    for l in range(depth):
        q, k, v, u, gate = _in_proj(xf, pos, pre_mix_g[l].reshape(1, d), w_in[l].astype(BF16),
                                    tm_proj)
        attn = _attention(q, k, v, batch, seq)
        rec = _lru(u, gate, conv_w[l], conv_b[l], w_rgate[l], b_rgate[l], w_igate[l], b_igate[l],
                   lru_lambda[l], batch, seq, ts_lru)
        j = l // 2
        if l % 2 == 0:
            x1, h2 = _mix_out(attn, rec, xf, attn_out_g[l], lru_out_g[l], w_out[l],
                              post_mix_g[l], pre_ffn_g[l], None, BF16, tm_mix)
            n_tiles = t // tm_ffn
            xf = _ffn(h2, jnp.zeros((n_tiles,), jnp.int32), jnp.full((1,), n_tiles, jnp.int32),
                      dense_w_gate[j][None].astype(BF16), dense_w_up[j][None].astype(BF16),
                      dense_w_down[j][None].astype(BF16), tm_ffn, f_chunk,
                      residual=x1, gain=post_ffn_g[l])
        else:
            x1, h2, route_i, route_p = _mix_out(
                attn, rec, xf, attn_out_g[l], lru_out_g[l], w_out[l], post_mix_g[l],
                pre_ffn_g[l], router_w[j], F32, tm_mix)
            dest, tile_expert, n_valid, n_rows = _routing_tables(route_i, n_exp, tm_ffn)
            xs = _dispatch(h2, dest, n_rows, tm_rows)
            ys = _ffn(xs, tile_expert, n_valid, moe_w_gate[j].astype(BF16),
                      moe_w_up[j].astype(BF16), moe_w_down[j].astype(BF16), tm_ffn, f_chunk)
            xf = _combine(ys, dest, route_p, x1, post_ffn_g[l], tm_rows)
    return xf.reshape(batch, seq, d).astype(x.dtype)
```

```python
import functools
import math

import jax
import jax.numpy as jnp
from jax import lax
from jax.experimental import pallas as pl
from jax.experimental.pallas import tpu as pltpu

ATTN_HEADS = 8
HEAD_DIM = 64
ATTN_WIDTH = ATTN_HEADS * HEAD_DIM
LRU_BLOCKS = 8
LRU_BLOCK_W = 64
LRU_WIDTH = LRU_BLOCKS * LRU_BLOCK_W
DILATED_PAIRS = ((128, 1), (512, 4), (2048, 16))
Q_BLOCK = 128
ROT_DIM = HEAD_DIM // 4
ROT_HALF = ROT_DIM // 2
ROPE_THETA = 500000.0
ATTN_SCALE = 1.0 / math.sqrt(HEAD_DIM)
NEG_INF = -1e30
CONV_W = 4
LRU_C = 8.0
TOP_K = 2
RMS_EPS = 1e-6

LANES = 128
SUBLANES = 8
SMEM_BLOCK_WORDS = 1024
VMEM_LIMIT_BYTES = 56 * 1024 * 1024

F32 = jnp.float32
BF16 = jnp.bfloat16


def _params(semantics):
    return pltpu.CompilerParams(dimension_semantics=semantics,
                                vmem_limit_bytes=VMEM_LIMIT_BYTES)


def _resident(block_shape, index_map):
    return pl.BlockSpec(block_shape, index_map, pipeline_mode=pl.Buffered(1))


def _rms(x, g):
    var = jnp.mean(x * x, axis=-1, keepdims=True)
    return x * lax.rsqrt(var + RMS_EPS) * g


def _lru_block(u, gate, cw_ref, cb_ref, wr_ref, br_ref, wi_ref, bi_ref, lam_ref,
               ubuf, hcarry, a_scr, b_scr):
    ts, c = u.shape
    ubuf[SUBLANES:SUBLANES + ts, :] = u
    ext = ubuf[...]
    uc = cb_ref[...]
    for tap in range(CONV_W):
        back = CONV_W - 1 - tap
        shifted = pltpu.roll(ext, back, 0) if back else ext
        uc = uc + shifted[SUBLANES:SUBLANES + ts, :] * cw_ref[tap:tap + 1, :]
    ubuf[0:SUBLANES, :] = ubuf[ts:ts + SUBLANES, :]

    ucb = uc.astype(BF16)
    r = jax.nn.sigmoid(jnp.dot(ucb, wr_ref[...], preferred_element_type=F32) + br_ref[...])
    ig = jax.nn.sigmoid(jnp.dot(ucb, wi_ref[...], preferred_element_type=F32) + bi_ref[...])
    nl = -lam_ref[...]
    softplus = jnp.maximum(nl, 0.0) + jnp.log1p(jnp.exp(-jnp.abs(nl)))
    log_a = -LRU_C * r * softplus
    a = jnp.exp(log_a)
    mult = jnp.sqrt(1.0 - jnp.exp(2.0 * log_a))
    b = mult * ig * uc

    groups = ts // SUBLANES
    a3 = a.reshape(groups, SUBLANES, c)
    b3 = b.reshape(groups, SUBLANES, c)
    sub = lax.broadcasted_iota(jnp.int32, (groups, SUBLANES, c), 1)
    shift = 1
    while shift < SUBLANES:
        valid = sub >= shift
        a_sh = jnp.where(valid, pltpu.roll(a3, shift, 1), 1.0)
        b_sh = jnp.where(valid, pltpu.roll(b3, shift, 1), 0.0)
        b3 = a3 * b_sh + b3
        a3 = a3 * a_sh
        shift *= 2
    a_scr[...] = a3.reshape(ts, c)
    b_scr[...] = b3.reshape(ts, c)

    h_prev = hcarry[...]
    for g in range(groups):
        rows = slice(g * SUBLANES, (g + 1) * SUBLANES)
        h = a_scr[rows, :] * h_prev + b_scr[rows, :]
        b_scr[rows, :] = h
        h_prev = jnp.broadcast_to(h[SUBLANES - 1:SUBLANES, :], (SUBLANES, c))
    hcarry[...] = h_prev
    return jax.nn.gelu(gate) * b_scr[...]


def _in_proj_kernel(x_ref, pos_ref, g_ref, w_ref, rope_ref,
                    cw_ref, cb_ref, wr_ref, br_ref, wi_ref, bi_ref, lam_ref,
                    q_ref, k_ref, v_ref, rec_ref, ubuf, hcarry, a_scr, b_scr):
    @pl.when(pl.program_id(1) == 0)
    def _():
        ubuf[0:SUBLANES, :] = jnp.zeros((SUBLANES, ubuf.shape[1]), F32)
        hcarry[...] = jnp.zeros_like(hcarry)

    h = _rms(x_ref[...], g_ref[...]).astype(BF16)
    w = ATTN_WIDTH
    u = jnp.dot(h, w_ref[:, 3 * w:3 * w + LRU_WIDTH], preferred_element_type=F32)
    gate = jnp.dot(h, w_ref[:, 3 * w + LRU_WIDTH:], preferred_element_type=F32)
    rec_ref[...] = _lru_block(u, gate, cw_ref, cb_ref, wr_ref, br_ref, wi_ref, bi_ref, lam_ref,
                              ubuf, hcarry, a_scr, b_scr)

    ang = pos_ref[...].astype(F32) * rope_ref[0:1, :]
    cs = jnp.cos(ang)
    sn = jnp.sin(ang)
    s_lo = sn * rope_ref[1:2, :]
    s_hi = sn * rope_ref[2:3, :]

    def rotary(z):
        return (z * cs + pltpu.roll(z, ROT_HALF, 1) * s_lo
                + pltpu.roll(z, LANES - ROT_HALF, 1) * s_hi)

    for sec, out in ((0, q_ref), (1, k_ref)):
        z = jnp.dot(h, w_ref[:, sec * w:(sec + 1) * w], preferred_element_type=F32)
        for cb in range(w // LANES):
            sl = slice(cb * LANES, (cb + 1) * LANES)
            out[:, sl] = rotary(z[:, sl]).astype(BF16)
    v_ref[...] = jnp.dot(h, w_ref[:, 2 * w:3 * w], preferred_element_type=F32).astype(BF16)


def _rope_table():
    lane = jnp.arange(LANES, dtype=jnp.int32) % HEAD_DIM
    inv_freq = ROPE_THETA ** (-jnp.arange(ROT_HALF, dtype=F32) / ROT_HALF)
    freq = jnp.where(lane < ROT_DIM, inv_freq[lane % ROT_HALF], 0.0)
    lo = jnp.where((lane >= ROT_HALF) & (lane < ROT_DIM), 1.0, 0.0)
    hi = jnp.where(lane < ROT_HALF, -1.0, 0.0)
    return jnp.stack([freq, lo, hi]).astype(F32)


def _block_diag(w):
    nb, bw, _ = w.shape
    eye = jnp.eye(nb, dtype=w.dtype)
    return jnp.einsum("gij,gh->gihj", w, eye).reshape(nb * bw, nb * bw)


def _in_proj(x, pos, g, w_in, conv_w, conv_b, w_r, b_r, w_i, b_i, lam, batch, seq, ts):
    t, d = x.shape
    ncol = w_in.shape[1]
    c = LRU_WIDTH
    nblk = seq // ts
    row = lambda b, j: (b * nblk + j, 0)
    const = lambda b, j: (0, 0)
    vec = lambda a: a.reshape(1, c).astype(F32)
    return pl.pallas_call(
        _in_proj_kernel,
        grid=(batch, nblk),
        in_specs=[
            pl.BlockSpec((ts, d), row),
            pl.BlockSpec((ts, 1), row),
            pl.BlockSpec((1, d), const),
            _resident((d, ncol), const),
            pl.BlockSpec((3, LANES), const),
            pl.BlockSpec((CONV_W, c), const),
            pl.BlockSpec((1, c), const),
            pl.BlockSpec((c, c), const),
            pl.BlockSpec((1, c), const),
            pl.BlockSpec((c, c), const),
            pl.BlockSpec((1, c), const),
            pl.BlockSpec((1, c), const),
        ],
        out_specs=[pl.BlockSpec((ts, ATTN_WIDTH), row)] * 3 + [pl.BlockSpec((ts, c), row)],
        out_shape=[jax.ShapeDtypeStruct((t, ATTN_WIDTH), BF16)] * 3
        + [jax.ShapeDtypeStruct((t, c), F32)],
        scratch_shapes=[pltpu.VMEM((ts + SUBLANES, c), F32), pltpu.VMEM((SUBLANES, c), F32),
                        pltpu.VMEM((ts, c), F32), pltpu.VMEM((ts, c), F32)],
        compiler_params=_params(("parallel", "arbitrary")),
        name="in_proj_lru",
    )(x, pos, g, w_in, _rope_table(), conv_w.astype(F32), vec(conv_b),
      _block_diag(w_r).astype(BF16), vec(b_r), _block_diag(w_i).astype(BF16), vec(b_i), vec(lam))


ATTN_TILE = DILATED_PAIRS[-1][1] * Q_BLOCK


def _rows(start, size, stride):
    return pl.ds(start, size) if stride == 1 else pl.ds(start, size, stride=stride)


ATTN_PIPE_WIDTH = 4


def _attn_kernel(q_ref, kc_ref, vc_ref, kp_ref, vp_ref, bias_ref, o_ref,
                 q0_scr, q1_scr, k_scr, v_scr, ob0, ob1, ob2, ls0, ls1, ls2, *slots):
    s_scrs, m_scrs = slots[:len(slots) // 2], slots[len(slots) // 2:]
    j = pl.program_id(2)
    ts = q_ref.shape[0]
    lane = lax.broadcasted_iota(jnp.int32, (1, LANES), 1)
    head0 = lane < HEAD_DIM

    qf = q_ref[...].astype(F32) * ATTN_SCALE
    q0_scr[...] = jnp.where(head0, qf, 0.0)
    q1_scr[...] = jnp.where(head0, 0.0, qf)
    k_scr[0:ts, :] = kp_ref[...].astype(F32)
    k_scr[ts:2 * ts, :] = kc_ref[...].astype(F32)
    v_scr[0:ts, :] = vp_ref[...].astype(F32)
    v_scr[ts:2 * ts, :] = vc_ref[...].astype(F32)

    nt = (((1,), (1,)), ((), ()))
    units = ts // Q_BLOCK
    width = len(s_scrs) // 2
    groups = units // width
    assert groups % 2 == 0 and groups >= 2
    for (_, dil), ob, ls in zip(DILATED_PAIRS, (ob0, ob1, ob2), (ls0, ls1, ls2)):
        shift = dil.bit_length() - 1

        def rows_of(u, dil=dil, shift=shift):
            cls = jnp.bitwise_and(u, dil - 1)
            blk = lax.shift_right_logical(u, shift)
            q_start = cls + dil * Q_BLOCK * blk
            q_rows = _rows(q_start, Q_BLOCK, dil)
            kv_rows = _rows(ts + q_start - dil * Q_BLOCK, 2 * Q_BLOCK, dil)
            return blk, q_rows, kv_rows

        def scores(u, slot):
            blk, q_rows, kv_rows = rows_of(u)
            lhs = jnp.concatenate([q0_scr[q_rows, :], q1_scr[q_rows, :]], axis=0).astype(BF16)
            k2 = k_scr[kv_rows, :].astype(BF16)
            no_prev = jnp.logical_and(j == 0, blk == 0).astype(jnp.int32)
            bias = bias_ref[pl.ds(no_prev * 2 * Q_BLOCK, 2 * Q_BLOCK), :]
            s = lax.dot_general(lhs, k2, nt, preferred_element_type=F32) + bias
            s_scrs[slot][...] = s
            m_scrs[slot][...] = jnp.broadcast_to(jnp.max(s, axis=-1, keepdims=True),
                                                 (2 * Q_BLOCK, LANES))

        def finish(u, slot, ob=ob, ls=ls):
            _, q_rows, kv_rows = rows_of(u)
            m = m_scrs[slot][...]
            p = jnp.exp(s_scrs[slot][...] - jnp.concatenate([m, m], axis=1))
            den = jnp.sum(p, axis=-1, keepdims=True)
            v2 = v_scr[kv_rows, :].astype(BF16)
            o = jnp.dot(p.astype(BF16), v2, preferred_element_type=F32) / den
            lse = m + jnp.log(den)
            ob[q_rows, :] = jnp.where(head0, o[0:Q_BLOCK], o[Q_BLOCK:])
            ls[q_rows, :] = jnp.where(head0, lse[0:Q_BLOCK], lse[Q_BLOCK:])

        def scores_group(g, half):
            for w in range(width):
                scores(g * width + w, half * width + w)

        def finish_group(g, half):
            for w in range(width):
                finish(g * width + w, half * width + w)

        scores_group(jnp.int32(0), 0)

        def pair(i, carry):
            scores_group(2 * i + 1, 1)
            finish_group(2 * i, 0)
            scores_group(2 * i + 2, 0)
            finish_group(2 * i + 1, 1)
            return carry

        lax.fori_loop(0, groups // 2 - 1, pair, 0)
        scores_group(jnp.int32(groups - 1), 1)
        finish_group(jnp.int32(groups - 2), 0)
        finish_group(jnp.int32(groups - 1), 1)

    chunk = 2 * Q_BLOCK

    def mix(c, carry):
        r = pl.ds(pl.multiple_of(c * chunk, chunk), chunk)
        la, lb, lc = ls0[r, :], ls1[r, :], ls2[r, :]
        m = jnp.maximum(jnp.maximum(la, lb), lc)
        ea, eb, ec = jnp.exp(la - m), jnp.exp(lb - m), jnp.exp(lc - m)
        o_ref[r, :] = (ea * ob0[r, :] + eb * ob1[r, :] + ec * ob2[r, :]) / (ea + eb + ec)
        return carry

    lax.fori_loop(0, ts // chunk, mix, 0)


def _attn_bias():
    qi = jnp.arange(2 * Q_BLOCK, dtype=jnp.int32)[:, None] % Q_BLOCK
    kj = jnp.arange(2 * Q_BLOCK, dtype=jnp.int32)[None, :]
    delta = Q_BLOCK + qi - kj
    band = (delta >= 0) & (delta <= Q_BLOCK)
    with_prev = jnp.where(band, 0.0, NEG_INF)
    no_prev = jnp.where(band & (kj >= Q_BLOCK), 0.0, NEG_INF)
    return jnp.concatenate([with_prev, no_prev], axis=0).astype(F32)


def _attention(q, k, v, batch, seq):
    ts = ATTN_TILE
    assert seq % ts == 0
    nt = seq // ts
    cur = pl.BlockSpec((ts, LANES), lambda b, c, j: (b * nt + j, c))
    prev = pl.BlockSpec((ts, LANES), lambda b, c, j: (b * nt + jnp.maximum(j - 1, 0), c))
    tile_scr = pltpu.VMEM((ts, LANES), F32)
    pair_scr = pltpu.VMEM((2 * ts, LANES), F32)
    return pl.pallas_call(
        _attn_kernel,
        grid=(batch, ATTN_WIDTH // LANES, nt),
        in_specs=[cur, cur, cur, prev, prev,
                  pl.BlockSpec((4 * Q_BLOCK, 2 * Q_BLOCK), lambda b, c, j: (0, 0))],
        out_specs=cur,
        out_shape=jax.ShapeDtypeStruct((batch * seq, ATTN_WIDTH), F32),
        scratch_shapes=[tile_scr, tile_scr, pair_scr, pair_scr] + [tile_scr] * 6
        + [pltpu.VMEM((2 * Q_BLOCK, 2 * Q_BLOCK), F32)] * (2 * ATTN_PIPE_WIDTH)
        + [pltpu.VMEM((2 * Q_BLOCK, LANES), F32)] * (2 * ATTN_PIPE_WIDTH),
        compiler_params=_params(("parallel", "parallel", "parallel")),
        name="attention",
    )(q, k, v, k, v, _attn_bias())


def _mix_kernel(*refs, n_exp):
    with_router = n_exp > 0
    attn_ref, rec_ref, x_ref, ga_ref, gr_ref, wo_ref, gpost_ref, gpre_ref = refs[:8]
    if with_router:
        rw_ref, x1_ref, h2_ref, ri_ref, rp_ref = refs[8:]
    else:
        x1_ref, h2_ref = refs[8:]

    na = _rms(attn_ref[...], ga_ref[...]).astype(BF16)
    nr = _rms(rec_ref[...], gr_ref[...]).astype(BF16)
    mixed = (jnp.dot(na, wo_ref[0:ATTN_WIDTH, :], preferred_element_type=F32)
             + jnp.dot(nr, wo_ref[ATTN_WIDTH:, :], preferred_element_type=F32))
    x1 = x_ref[...] + _rms(mixed, gpost_ref[...])
    x1_ref[...] = x1
    hn = _rms(x1, gpre_ref[...])
    h2_ref[...] = hn.astype(h2_ref.dtype)

    if with_router:
        rw = rw_ref[...]
        h_hi, r_hi = hn.astype(BF16), rw.astype(BF16)
        h_lo = (hn - h_hi.astype(F32)).astype(BF16)
        r_lo = (rw - r_hi.astype(F32)).astype(BF16)
        logits = (jnp.dot(h_hi, r_hi, preferred_element_type=F32)
                  + jnp.dot(h_hi, r_lo, preferred_element_type=F32)
                  + jnp.dot(h_lo, r_hi, preferred_element_type=F32))
        tm = logits.shape[0]
        lane = lax.broadcasted_iota(jnp.int32, (tm, LANES), 1)
        logits = jnp.where(lane < n_exp, logits, -jnp.inf)
        m1 = jnp.max(logits, axis=-1, keepdims=True)
        i1 = jnp.min(jnp.where(logits == m1, lane, n_exp), axis=-1, keepdims=True)
        rest = jnp.where(lane == i1, -jnp.inf, logits)
        m2 = jnp.max(rest, axis=-1, keepdims=True)
        i2 = jnp.min(jnp.where(rest == m2, lane, n_exp), axis=-1, keepdims=True)
        e2 = jnp.exp(m2 - m1)
        p1 = 1.0 / (1.0 + e2)
        p2 = e2 / (1.0 + e2)
        ri_ref[...] = jnp.where(lane == 0, i1, jnp.where(lane == 1, i2, 0))
        rp_ref[...] = jnp.where(lane == 0, p1, jnp.where(lane == 1, p2, 0.0))


def _mix_out(attn, rec, x, ga, gr, w_out, gpost, gpre, router_w, h2_dtype, tm):
    t, d = x.shape
    row = lambda i: (i, 0)
    const = lambda i: (0, 0)
    aw = ATTN_WIDTH
    with_router = router_w is not None
    n_exp = 0
    args = [attn, rec, x, ga.reshape(1, aw), gr.reshape(1, LRU_WIDTH),
            w_out.astype(BF16), gpost.reshape(1, d), gpre.reshape(1, d)]
    in_specs = ([pl.BlockSpec((tm, aw), row)] * 2 + [pl.BlockSpec((tm, d), row),
                pl.BlockSpec((1, aw), const), pl.BlockSpec((1, LRU_WIDTH), const),
                _resident(w_out.shape, const), pl.BlockSpec((1, d), const),
                pl.BlockSpec((1, d), const)])
    out_specs = [pl.BlockSpec((tm, d), row), pl.BlockSpec((tm, d), row)]
    out_shape = [jax.ShapeDtypeStruct((t, d), F32), jax.ShapeDtypeStruct((t, d), h2_dtype)]
    if with_router:
        n_exp = router_w.shape[1]
        rw = jnp.pad(router_w.astype(F32), ((0, 0), (0, LANES - n_exp)))
        args.append(rw)
        in_specs.append(pl.BlockSpec(rw.shape, const))
        out_specs += [pl.BlockSpec((tm, LANES), row)] * 2
        out_shape += [jax.ShapeDtypeStruct((t, LANES), jnp.int32),
                      jax.ShapeDtypeStruct((t, LANES), F32)]
    return pl.pallas_call(
        functools.partial(_mix_kernel, n_exp=n_exp),
        grid=(t // tm,),
        in_specs=in_specs,
        out_specs=out_specs,
        out_shape=out_shape,
        compiler_params=_params(("parallel",)),
        name="mix_out_router" if with_router else "mix_out",
    )(*args)


def _ffn_kernel(te_ref, nv_ref, x_ref, wg_ref, wu_ref, wd_ref, *refs, f_chunk):
    o_ref = refs[-1]
    i = pl.program_id(0)
    d_ff = wg_ref.shape[1]

    @pl.when(i < nv_ref[0])
    def _():
        x = x_ref[...].astype(BF16)
        for c in range(d_ff // f_chunk):
            sl = slice(c * f_chunk, (c + 1) * f_chunk)
            g = jnp.dot(x, wg_ref[:, sl], preferred_element_type=F32)
            u = jnp.dot(x, wu_ref[:, sl], preferred_element_type=F32)
            a = (g * jax.nn.sigmoid(g) * u).astype(BF16)
            y = jnp.dot(a, wd_ref[sl, :], preferred_element_type=F32)
            if c == 0:
                o_ref[...] = y
            else:
                o_ref[...] += y
        if len(refs) == 3:
            res_ref, gain_ref, _ = refs
            o_ref[...] = res_ref[...] + _rms(o_ref[...], gain_ref[...])

    @pl.when(i >= nv_ref[0])
    def _():
        o_ref[...] = jnp.zeros_like(o_ref)


def _ffn(xs, tile_expert, n_valid, wg, wu, wd, tm, f_chunk, residual=None, gain=None):
    rows, d = xs.shape
    d_ff = wg.shape[2]
    assert d_ff % f_chunk == 0 and rows % tm == 0
    xmap = lambda i, te, nv: (jnp.minimum(i, nv[0] - 1), 0)
    wmap = lambda i, te, nv: (te[i], 0, 0)
    in_specs = [
        pl.BlockSpec((tm, d), xmap),
        _resident((None, d, d_ff), wmap),
        _resident((None, d, d_ff), wmap),
        _resident((None, d_ff, d), wmap),
    ]
    args = [tile_expert, n_valid, xs, wg, wu, wd]
    if residual is not None:
        in_specs += [pl.BlockSpec((tm, d), xmap), pl.BlockSpec((1, d), lambda i, te, nv: (0, 0))]
        args += [residual, gain.reshape(1, d)]
    grid_spec = pltpu.PrefetchScalarGridSpec(
        num_scalar_prefetch=2,
        grid=(rows // tm,),
        in_specs=in_specs,
        out_specs=pl.BlockSpec((tm, d), lambda i, te, nv: (i, 0)),
    )
    return pl.pallas_call(
        functools.partial(_ffn_kernel, f_chunk=f_chunk),
        grid_spec=grid_spec,
        out_shape=jax.ShapeDtypeStruct((rows, d), F32),
        compiler_params=_params(("arbitrary",)),
        name="ffn",
    )(*args)


def _dispatch_kernel(d1_ref, d2_ref, h_ref, init_ref, out_ref, sem):
    del init_ref

    def row_copies(j):
        src = h_ref.at[pl.ds(j, 1)]
        return (pltpu.make_async_copy(src, out_ref.at[pl.ds(d1_ref[j], 1)], sem),
                pltpu.make_async_copy(src, out_ref.at[pl.ds(d2_ref[j], 1)], sem))

    def start(j, carry):
        for c in row_copies(j):
            c.start()
        return carry

    def wait(j, carry):
        for c in row_copies(j):
            c.wait()
        return carry

    n = h_ref.shape[0]
    lax.fori_loop(0, n, start, 0, unroll=8)
    lax.fori_loop(0, n, wait, 0, unroll=8)


def _dispatch(h, dest, n_rows, tm):
    t, d = h.shape
    assert tm % SMEM_BLOCK_WORDS == 0 and t % tm == 0
    nblk = t // tm
    any_spec = pl.BlockSpec(memory_space=pl.ANY)
    return pl.pallas_call(
        _dispatch_kernel,
        grid=(nblk,),
        in_specs=[pl.BlockSpec((tm,), lambda i: (i,), memory_space=pltpu.SMEM),
                  pl.BlockSpec((tm,), lambda i: (i + nblk,), memory_space=pltpu.SMEM),
                  pl.BlockSpec((tm, d), lambda i: (i, 0)), any_spec],
        out_specs=any_spec,
        out_shape=jax.ShapeDtypeStruct((n_rows, d), h.dtype),
        scratch_shapes=[pltpu.SemaphoreType.DMA(())],
        input_output_aliases={3: 0},
        compiler_params=_params(("arbitrary",)),
        name="moe_dispatch",
    )(dest, dest, h, jnp.zeros((n_rows, d), h.dtype))


def _combine_kernel(d1_ref, d2_ref, ys_ref, rp_ref, x_ref, g_ref, o_ref, ya, yb, sem):
    def row_copies(j):
        dst = pl.ds(j, 1)
        return (pltpu.make_async_copy(ys_ref.at[pl.ds(d1_ref[j], 1)], ya.at[dst], sem),
                pltpu.make_async_copy(ys_ref.at[pl.ds(d2_ref[j], 1)], yb.at[dst], sem))

    def start(j, carry):
        for c in row_copies(j):
            c.start()
        return carry

    def wait(j, carry):
        for c in row_copies(j):
            c.wait()
        return carry

    n = x_ref.shape[0]
    lax.fori_loop(0, n, start, 0, unroll=8)
    lax.fori_loop(0, n, wait, 0, unroll=8)
    y = rp_ref[:, 0:1] * ya[...] + rp_ref[:, 1:2] * yb[...]
    o_ref[...] = x_ref[...] + _rms(y, g_ref[...])


def _combine(ys, dest, route_p, x, g, tm):
    t, d = x.shape
    assert tm % SMEM_BLOCK_WORDS == 0 and t % tm == 0
    nblk = t // tm
    row = lambda i: (i, 0)
    return pl.pallas_call(
        _combine_kernel,
        grid=(nblk,),
        in_specs=[pl.BlockSpec((tm,), lambda i: (i,), memory_space=pltpu.SMEM),
                  pl.BlockSpec((tm,), lambda i: (i + nblk,), memory_space=pltpu.SMEM),
                  pl.BlockSpec(memory_space=pl.ANY),
                  pl.BlockSpec((tm, LANES), row), pl.BlockSpec((tm, d), row),
                  pl.BlockSpec((1, d), lambda i: (0, 0))],
        out_specs=pl.BlockSpec((tm, d), row),
        out_shape=jax.ShapeDtypeStruct((t, d), F32),
        scratch_shapes=[pltpu.VMEM((tm, d), F32), pltpu.VMEM((tm, d), F32),
                        pltpu.SemaphoreType.DMA(())],
        compiler_params=_params(("arbitrary",)),
        name="moe_combine",
    )(dest, dest, ys, route_p, x, g.reshape(1, d))


def _routing_tables(route_i, n_exp, tm):
    t = route_i.shape[0]
    e_flat = jnp.concatenate([route_i[:, 0], route_i[:, 1]])
    onehot = (e_flat[:, None] == jnp.arange(n_exp, dtype=jnp.int32)[None, :]).astype(jnp.int32)
    counts = jnp.sum(onehot, axis=0)
    padded = ((counts + tm - 1) // tm) * tm
    pad_end = jnp.cumsum(padded)
    pad_start = pad_end - padded
    dest = jnp.sum((jnp.cumsum(onehot, axis=0) - onehot + pad_start[None, :]) * onehot, axis=1)

    n_rows = TOP_K * t + n_exp * tm
    n_tiles = n_rows // tm
    tile_first_row = jnp.arange(n_tiles, dtype=jnp.int32) * tm
    n_valid = pad_end[-1] // tm
    tile_first_row = jnp.minimum(tile_first_row, (n_valid - 1) * tm)
    tile_expert = jnp.sum((pad_end[None, :] <= tile_first_row[:, None]).astype(jnp.int32), axis=1)
    return (dest.astype(jnp.int32), jnp.minimum(tile_expert, n_exp - 1).astype(jnp.int32),
            n_valid.astype(jnp.int32).reshape(1), n_rows)


def _tile(n, want):
    t = min(n, want)
    while n % t:
        t -= SUBLANES
    return t


def kernel(x, positions, pre_mix_g, w_in, conv_w, conv_b, w_rgate, b_rgate, w_igate, b_igate,
           lru_lambda, attn_out_g, lru_out_g, w_out, post_mix_g, pre_ffn_g, post_ffn_g,
           dense_w_gate, dense_w_up, dense_w_down, router_w, moe_w_gate, moe_w_up, moe_w_down):
    batch, seq, d = x.shape
    depth = w_in.shape[0]
    t = batch * seq
    d_ff = dense_w_gate.shape[-1]
    n_exp = moe_w_gate.shape[1]
    assert w_in.shape[2] == 3 * ATTN_WIDTH + 2 * LRU_WIDTH
    assert seq % (DILATED_PAIRS[-1][1] * Q_BLOCK) == 0

    tm_mix = _tile(t, 512)
    tm_ffn = _tile(t, 512)
    ts_proj = _tile(seq, 512)
    tm_rows = SMEM_BLOCK_WORDS
    f_chunk = 512 if d_ff % 512 == 0 else d_ff

    xf = x.reshape(t, d).astype(F32)
    pos = positions.reshape(t, 1).astype(jnp.int32)

    for l in range(depth):
        q, k, v, rec = _in_proj(xf, pos, pre_mix_g[l].reshape(1, d), w_in[l].astype(BF16),
                                conv_w[l], conv_b[l], w_rgate[l], b_rgate[l], w_igate[l],
                                b_igate[l], lru_lambda[l], batch, seq, ts_proj)
        attn = _attention(q, k, v, batch, seq)
        j = l // 2
        if l % 2 == 0:
            x1, h2 = _mix_out(attn, rec, xf, attn_out_g[l], lru_out_g[l], w_out[l],
                              post_mix_g[l], pre_ffn_g[l], None, BF16, tm_mix)
            n_tiles = t // tm_ffn
            xf = _ffn(h2, jnp.zeros((n_tiles,), jnp.int32), jnp.full((1,), n_tiles, jnp.int32),
                      dense_w_gate[j][None].astype(BF16), dense_w_up[j][None].astype(BF16),
                      dense_w_down[j][None].astype(BF16), tm_ffn, f_chunk,
                      residual=x1, gain=post_ffn_g[l])
        else:
            x1, h2, route_i, route_p = _mix_out(
                attn, rec, xf, attn_out_g[l], lru_out_g[l], w_out[l], post_mix_g[l],
                pre_ffn_g[l], router_w[j], F32, tm_mix)
            dest, tile_expert, n_valid, n_rows = _routing_tables(route_i, n_exp, tm_ffn)
            xs = _dispatch(h2, dest, n_rows, tm_rows)
            ys = _ffn(xs, tile_expert, n_valid, moe_w_gate[j].astype(BF16),
                      moe_w_up[j].astype(BF16), moe_w_down[j].astype(BF16), tm_ffn, f_chunk)
            xf = _combine(ys, dest, route_p, x1, post_ffn_g[l], tm_rows)
    return xf.reshape(batch, seq, d).astype(x.dtype)
```

```python
import functools
import math

import jax
import jax.numpy as jnp
from jax import lax
from jax.experimental import pallas as pl
from jax.experimental.pallas import tpu as pltpu

ATTN_HEADS = 8
HEAD_DIM = 64
ATTN_WIDTH = ATTN_HEADS * HEAD_DIM
LRU_BLOCKS = 8
LRU_BLOCK_W = 64
LRU_WIDTH = LRU_BLOCKS * LRU_BLOCK_W
DILATED_PAIRS = ((128, 1), (512, 4), (2048, 16))
Q_BLOCK = 128
ROT_DIM = HEAD_DIM // 4
ROT_HALF = ROT_DIM // 2
ROPE_THETA = 500000.0
ATTN_SCALE = 1.0 / math.sqrt(HEAD_DIM)
NEG_INF = -1e30
CONV_W = 4
LRU_C = 8.0
TOP_K = 2
RMS_EPS = 1e-6

LANES = 128
SUBLANES = 8
SMEM_BLOCK_WORDS = 1024
VMEM_LIMIT_BYTES = 56 * 1024 * 1024

F32 = jnp.float32
BF16 = jnp.bfloat16


def _params(semantics):
    return pltpu.CompilerParams(dimension_semantics=semantics,
                                vmem_limit_bytes=VMEM_LIMIT_BYTES)


def _resident(block_shape, index_map):
    return pl.BlockSpec(block_shape, index_map, pipeline_mode=pl.Buffered(1))


def _rms(x, g):
    var = jnp.mean(x * x, axis=-1, keepdims=True)
    return x * lax.rsqrt(var + RMS_EPS) * g


def _lru_block(u, gate, cw_ref, cb_ref, wr_ref, br_ref, wi_ref, bi_ref, lam_ref,
               ubuf, hcarry, a_scr, b_scr):
    ts, c = u.shape
    ubuf[SUBLANES:SUBLANES + ts, :] = u
    ext = ubuf[...]
    uc = cb_ref[...]
    for tap in range(CONV_W):
        back = CONV_W - 1 - tap
        shifted = pltpu.roll(ext, back, 0) if back else ext
        uc = uc + shifted[SUBLANES:SUBLANES + ts, :] * cw_ref[tap:tap + 1, :]
    ubuf[0:SUBLANES, :] = ubuf[ts:ts + SUBLANES, :]

    ucb = uc.astype(BF16)
    r = jax.nn.sigmoid(jnp.dot(ucb, wr_ref[...], preferred_element_type=F32) + br_ref[...])
    ig = jax.nn.sigmoid(jnp.dot(ucb, wi_ref[...], preferred_element_type=F32) + bi_ref[...])
    nl = -lam_ref[...]
    softplus = jnp.maximum(nl, 0.0) + jnp.log1p(jnp.exp(-jnp.abs(nl)))
    log_a = -LRU_C * r * softplus
    a = jnp.exp(log_a)
    mult = jnp.sqrt(1.0 - jnp.exp(2.0 * log_a))
    b = mult * ig * uc

    groups = ts // SUBLANES
    a3 = a.reshape(groups, SUBLANES, c)
    b3 = b.reshape(groups, SUBLANES, c)
    sub = lax.broadcasted_iota(jnp.int32, (groups, SUBLANES, c), 1)
    shift = 1
    while shift < SUBLANES:
        valid = sub >= shift
        a_sh = jnp.where(valid, pltpu.roll(a3, shift, 1), 1.0)
        b_sh = jnp.where(valid, pltpu.roll(b3, shift, 1), 0.0)
        b3 = a3 * b_sh + b3
        a3 = a3 * a_sh
        shift *= 2
    a_scr[...] = a3.reshape(ts, c)
    b_scr[...] = b3.reshape(ts, c)

    h_prev = hcarry[...]
    for g in range(groups):
        rows = slice(g * SUBLANES, (g + 1) * SUBLANES)
        h = a_scr[rows, :] * h_prev + b_scr[rows, :]
        b_scr[rows, :] = h
        h_prev = jnp.broadcast_to(h[SUBLANES - 1:SUBLANES, :], (SUBLANES, c))
    hcarry[...] = h_prev
    return jax.nn.gelu(gate) * b_scr[...]


def _in_proj_kernel(x_ref, pos_ref, g_ref, w_ref, rope_ref,
                    cw_ref, cb_ref, wr_ref, br_ref, wi_ref, bi_ref, lam_ref,
                    q_ref, k_ref, v_ref, rec_ref, ubuf, hcarry, a_scr, b_scr):
    @pl.when(pl.program_id(1) == 0)
    def _():
        ubuf[0:SUBLANES, :] = jnp.zeros((SUBLANES, ubuf.shape[1]), F32)
        hcarry[...] = jnp.zeros_like(hcarry)

    h = _rms(x_ref[...], g_ref[...]).astype(BF16)
    w = ATTN_WIDTH
    u = jnp.dot(h, w_ref[:, 3 * w:3 * w + LRU_WIDTH], preferred_element_type=F32)
    gate = jnp.dot(h, w_ref[:, 3 * w + LRU_WIDTH:], preferred_element_type=F32)
    rec_ref[...] = _lru_block(u, gate, cw_ref, cb_ref, wr_ref, br_ref, wi_ref, bi_ref, lam_ref,
                              ubuf, hcarry, a_scr, b_scr)

    ang = pos_ref[...].astype(F32) * rope_ref[0:1, :]
    cs = jnp.cos(ang)
    sn = jnp.sin(ang)
    s_lo = sn * rope_ref[1:2, :]
    s_hi = sn * rope_ref[2:3, :]

    def rotary(z):
        return (z * cs + pltpu.roll(z, ROT_HALF, 1) * s_lo
                + pltpu.roll(z, LANES - ROT_HALF, 1) * s_hi)

    for sec, out in ((0, q_ref), (1, k_ref)):
        z = jnp.dot(h, w_ref[:, sec * w:(sec + 1) * w], preferred_element_type=F32)
        for cb in range(w // LANES):
            sl = slice(cb * LANES, (cb + 1) * LANES)
            out[:, sl] = rotary(z[:, sl]).astype(BF16)
    v_ref[...] = jnp.dot(h, w_ref[:, 2 * w:3 * w], preferred_element_type=F32).astype(BF16)


def _rope_table():
    lane = jnp.arange(LANES, dtype=jnp.int32) % HEAD_DIM
    inv_freq = ROPE_THETA ** (-jnp.arange(ROT_HALF, dtype=F32) / ROT_HALF)
    freq = jnp.where(lane < ROT_DIM, inv_freq[lane % ROT_HALF], 0.0)
    lo = jnp.where((lane >= ROT_HALF) & (lane < ROT_DIM), 1.0, 0.0)
    hi = jnp.where(lane < ROT_HALF, -1.0, 0.0)
    return jnp.stack([freq, lo, hi]).astype(F32)


def _block_diag(w):
    nb, bw, _ = w.shape
    eye = jnp.eye(nb, dtype=w.dtype)
    return jnp.einsum("gij,gh->gihj", w, eye).reshape(nb * bw, nb * bw)


def _in_proj(x, pos, g, w_in, conv_w, conv_b, w_r, b_r, w_i, b_i, lam, batch, seq, ts):
    t, d = x.shape
    ncol = w_in.shape[1]
    c = LRU_WIDTH
    nblk = seq // ts
    row = lambda b, j: (b * nblk + j, 0)
    const = lambda b, j: (0, 0)
    vec = lambda a: a.reshape(1, c).astype(F32)
    return pl.pallas_call(
        _in_proj_kernel,
        grid=(batch, nblk),
        in_specs=[
            pl.BlockSpec((ts, d), row),
            pl.BlockSpec((ts, 1), row),
            pl.BlockSpec((1, d), const),
            _resident((d, ncol), const),
            pl.BlockSpec((3, LANES), const),
            pl.BlockSpec((CONV_W, c), const),
            pl.BlockSpec((1, c), const),
            pl.BlockSpec((c, c), const),
            pl.BlockSpec((1, c), const),
            pl.BlockSpec((c, c), const),
            pl.BlockSpec((1, c), const),
            pl.BlockSpec((1, c), const),
        ],
        out_specs=[pl.BlockSpec((ts, ATTN_WIDTH), row)] * 3 + [pl.BlockSpec((ts, c), row)],
        out_shape=[jax.ShapeDtypeStruct((t, ATTN_WIDTH), BF16)] * 3
        + [jax.ShapeDtypeStruct((t, c), F32)],
        scratch_shapes=[pltpu.VMEM((ts + SUBLANES, c), F32), pltpu.VMEM((SUBLANES, c), F32),
                        pltpu.VMEM((ts, c), F32), pltpu.VMEM((ts, c), F32)],
        compiler_params=_params(("parallel", "arbitrary")),
        name="in_proj_lru",
    )(x, pos, g, w_in, _rope_table(), conv_w.astype(F32), vec(conv_b),
      _block_diag(w_r).astype(BF16), vec(b_r), _block_diag(w_i).astype(BF16), vec(b_i), vec(lam))


ATTN_TILE = DILATED_PAIRS[-1][1] * Q_BLOCK


def _rows(start, size, stride):
    return pl.ds(start, size) if stride == 1 else pl.ds(start, size, stride=stride)


ATTN_PIPE_WIDTH = 4


def _attn_kernel(q_ref, kc_ref, vc_ref, kp_ref, vp_ref, bias_ref, o_ref,
                 q0_scr, q1_scr, k_scr, v_scr, ob0, ob1, ob2, ls0, ls1, ls2, *slots):
    s_scrs, m_scrs = slots[:len(slots) // 2], slots[len(slots) // 2:]
    j = pl.program_id(2)
    ts = q_ref.shape[0]
    lane = lax.broadcasted_iota(jnp.int32, (1, LANES), 1)
    head0 = lane < HEAD_DIM

    qf = q_ref[...].astype(F32) * ATTN_SCALE
    q0_scr[...] = jnp.where(head0, qf, 0.0)
    q1_scr[...] = jnp.where(head0, 0.0, qf)
    k_scr[0:ts, :] = kp_ref[...].astype(F32)
    k_scr[ts:2 * ts, :] = kc_ref[...].astype(F32)
    v_scr[0:ts, :] = vp_ref[...].astype(F32)
    v_scr[ts:2 * ts, :] = vc_ref[...].astype(F32)

    nt = (((1,), (1,)), ((), ()))
    units = ts // Q_BLOCK
    width = len(s_scrs) // 2
    groups = units // width
    assert groups % 2 == 0 and groups >= 2
    for (_, dil), ob, ls in zip(DILATED_PAIRS, (ob0, ob1, ob2), (ls0, ls1, ls2)):
        shift = dil.bit_length() - 1

        def rows_of(u, dil=dil, shift=shift):
            cls = jnp.bitwise_and(u, dil - 1)
            blk = lax.shift_right_logical(u, shift)
            q_start = cls + dil * Q_BLOCK * blk
            q_rows = _rows(q_start, Q_BLOCK, dil)
            kv_rows = _rows(ts + q_start - dil * Q_BLOCK, 2 * Q_BLOCK, dil)
            return blk, q_rows, kv_rows

        def scores(u, slot):
            blk, q_rows, kv_rows = rows_of(u)
            lhs = jnp.concatenate([q0_scr[q_rows, :], q1_scr[q_rows, :]], axis=0).astype(BF16)
            k2 = k_scr[kv_rows, :].astype(BF16)
            no_prev = jnp.logical_and(j == 0, blk == 0).astype(jnp.int32)
            bias = bias_ref[pl.ds(no_prev * 2 * Q_BLOCK, 2 * Q_BLOCK), :]
            s = lax.dot_general(lhs, k2, nt, preferred_element_type=F32) + bias
            s_scrs[slot][...] = s
            m_scrs[slot][...] = jnp.broadcast_to(jnp.max(s, axis=-1, keepdims=True),
                                                 (2 * Q_BLOCK, LANES))

        def finish(u, slot, ob=ob, ls=ls):
            _, q_rows, kv_rows = rows_of(u)
            m = m_scrs[slot][...]
            p = jnp.exp(s_scrs[slot][...] - jnp.concatenate([m, m], axis=1))
            den = jnp.sum(p, axis=-1, keepdims=True)
            v2 = v_scr[kv_rows, :].astype(BF16)
            o = jnp.dot(p.astype(BF16), v2, preferred_element_type=F32) / den
            lse = m + jnp.log(den)
            ob[q_rows, :] = jnp.where(head0, o[0:Q_BLOCK], o[Q_BLOCK:])
            ls[q_rows, :] = jnp.where(head0, lse[0:Q_BLOCK], lse[Q_BLOCK:])

        def scores_group(g, half):
            for w in range(width):
                scores(g * width + w, half * width + w)

        def finish_group(g, half):
            for w in range(width):
                finish(g * width + w, half * width + w)

        scores_group(jnp.int32(0), 0)

        def pair(i, carry):
            scores_group(2 * i + 1, 1)
            finish_group(2 * i, 0)
            scores_group(2 * i + 2, 0)
            finish_group(2 * i + 1, 1)
            return carry

        lax.fori_loop(0, groups // 2 - 1, pair, 0)
        scores_group(jnp.int32(groups - 1), 1)
        finish_group(jnp.int32(groups - 2), 0)
        finish_group(jnp.int32(groups - 1), 1)

    chunk = 2 * Q_BLOCK

    def mix(c, carry):
        r = pl.ds(pl.multiple_of(c * chunk, chunk), chunk)
        la, lb, lc = ls0[r, :], ls1[r, :], ls2[r, :]
        m = jnp.maximum(jnp.maximum(la, lb), lc)
        ea, eb, ec = jnp.exp(la - m), jnp.exp(lb - m), jnp.exp(lc - m)
        o_ref[r, :] = (ea * ob0[r, :] + eb * ob1[r, :] + ec * ob2[r, :]) / (ea + eb + ec)
        return carry

    lax.fori_loop(0, ts // chunk, mix, 0)


def _attn_bias():
    qi = jnp.arange(2 * Q_BLOCK, dtype=jnp.int32)[:, None] % Q_BLOCK
    kj = jnp.arange(2 * Q_BLOCK, dtype=jnp.int32)[None, :]
    delta = Q_BLOCK + qi - kj
    band = (delta >= 0) & (delta <= Q_BLOCK)
    with_prev = jnp.where(band, 0.0, NEG_INF)
    no_prev = jnp.where(band & (kj >= Q_BLOCK), 0.0, NEG_INF)
    return jnp.concatenate([with_prev, no_prev], axis=0).astype(F32)


def _attention(q, k, v, batch, seq):
    ts = ATTN_TILE
    assert seq % ts == 0
    nt = seq // ts
    cur = pl.BlockSpec((ts, LANES), lambda b, c, j: (b * nt + j, c))
    prev = pl.BlockSpec((ts, LANES), lambda b, c, j: (b * nt + jnp.maximum(j - 1, 0), c))
    tile_scr = pltpu.VMEM((ts, LANES), F32)
    pair_scr = pltpu.VMEM((2 * ts, LANES), F32)
    return pl.pallas_call(
        _attn_kernel,
        grid=(batch, ATTN_WIDTH // LANES, nt),
        in_specs=[cur, cur, cur, prev, prev,
                  pl.BlockSpec((4 * Q_BLOCK, 2 * Q_BLOCK), lambda b, c, j: (0, 0))],
        out_specs=cur,
        out_shape=jax.ShapeDtypeStruct((batch * seq, ATTN_WIDTH), F32),
        scratch_shapes=[tile_scr, tile_scr, pair_scr, pair_scr] + [tile_scr] * 6
        + [pltpu.VMEM((2 * Q_BLOCK, 2 * Q_BLOCK), F32)] * (2 * ATTN_PIPE_WIDTH)
        + [pltpu.VMEM((2 * Q_BLOCK, LANES), F32)] * (2 * ATTN_PIPE_WIDTH),
        compiler_params=_params(("parallel", "parallel", "parallel")),
        name="attention",
    )(q, k, v, k, v, _attn_bias())


def _mix_kernel(*refs, n_exp):
    with_router = n_exp > 0
    attn_ref, rec_ref, x_ref, ga_ref, gr_ref, wo_ref, gpost_ref, gpre_ref = refs[:8]
    if with_router:
        rw_ref, x1_ref, h2_ref, ri_ref, rp_ref = refs[8:]
    else:
        x1_ref, h2_ref = refs[8:]

    na = _rms(attn_ref[...], ga_ref[...]).astype(BF16)
    nr = _rms(rec_ref[...], gr_ref[...]).astype(BF16)
    mixed = (jnp.dot(na, wo_ref[0:ATTN_WIDTH, :], preferred_element_type=F32)
             + jnp.dot(nr, wo_ref[ATTN_WIDTH:, :], preferred_element_type=F32))
    x1 = x_ref[...] + _rms(mixed, gpost_ref[...])
    x1_ref[...] = x1
    hn = _rms(x1, gpre_ref[...])
    h2_ref[...] = hn.astype(h2_ref.dtype)

    if with_router:
        rw = rw_ref[...]
        h_hi, r_hi = hn.astype(BF16), rw.astype(BF16)
        h_lo = (hn - h_hi.astype(F32)).astype(BF16)
        r_lo = (rw - r_hi.astype(F32)).astype(BF16)
        logits = (jnp.dot(h_hi, r_hi, preferred_element_type=F32)
                  + jnp.dot(h_hi, r_lo, preferred_element_type=F32)
                  + jnp.dot(h_lo, r_hi, preferred_element_type=F32))
        tm = logits.shape[0]
        lane = lax.broadcasted_iota(jnp.int32, (tm, LANES), 1)
        logits = jnp.where(lane < n_exp, logits, -jnp.inf)
        m1 = jnp.max(logits, axis=-1, keepdims=True)
        i1 = jnp.min(jnp.where(logits == m1, lane, n_exp), axis=-1, keepdims=True)
        rest = jnp.where(lane == i1, -jnp.inf, logits)
        m2 = jnp.max(rest, axis=-1, keepdims=True)
        i2 = jnp.min(jnp.where(rest == m2, lane, n_exp), axis=-1, keepdims=True)
        e2 = jnp.exp(m2 - m1)
        p1 = 1.0 / (1.0 + e2)
        p2 = e2 / (1.0 + e2)
        ri_ref[...] = jnp.where(lane == 0, i1, jnp.where(lane == 1, i2, 0))
        rp_ref[...] = jnp.where(lane == 0, p1, jnp.where(lane == 1, p2, 0.0))


def _mix_out(attn, rec, x, ga, gr, w_out, gpost, gpre, router_w, h2_dtype, tm):
    t, d = x.shape
    row = lambda i: (i, 0)
    const = lambda i: (0, 0)
    aw = ATTN_WIDTH
    with_router = router_w is not None
    n_exp = 0
    args = [attn, rec, x, ga.reshape(1, aw), gr.reshape(1, LRU_WIDTH),
            w_out.astype(BF16), gpost.reshape(1, d), gpre.reshape(1, d)]
    in_specs = ([pl.BlockSpec((tm, aw), row)] * 2 + [pl.BlockSpec((tm, d), row),
                pl.BlockSpec((1, aw), const), pl.BlockSpec((1, LRU_WIDTH), const),
                _resident(w_out.shape, const), pl.BlockSpec((1, d), const),
                pl.BlockSpec((1, d), const)])
    out_specs = [pl.BlockSpec((tm, d), row), pl.BlockSpec((tm, d), row)]
    out_shape = [jax.ShapeDtypeStruct((t, d), F32), jax.ShapeDtypeStruct((t, d), h2_dtype)]
    if with_router:
        n_exp = router_w.shape[1]
        rw = jnp.pad(router_w.astype(F32), ((0, 0), (0, LANES - n_exp)))
        args.append(rw)
        in_specs.append(pl.BlockSpec(rw.shape, const))
        out_specs += [pl.BlockSpec((tm, LANES), row)] * 2
        out_shape += [jax.ShapeDtypeStruct((t, LANES), jnp.int32),
                      jax.ShapeDtypeStruct((t, LANES), F32)]
    return pl.pallas_call(
        functools.partial(_mix_kernel, n_exp=n_exp),
        grid=(t // tm,),
        in_specs=in_specs,
        out_specs=out_specs,
        out_shape=out_shape,
        compiler_params=_params(("parallel",)),
        name="mix_out_router" if with_router else "mix_out",
    )(*args)


def _ffn_kernel(te_ref, nv_ref, x_ref, wg_ref, wu_ref, wd_ref, *refs, f_chunk):
    o_ref = refs[-1]
    i = pl.program_id(0)
    d_ff = wg_ref.shape[1]

    @pl.when(i < nv_ref[0])
    def _():
        x = x_ref[...].astype(BF16)
        for c in range(d_ff // f_chunk):
            sl = slice(c * f_chunk, (c + 1) * f_chunk)
            g = jnp.dot(x, wg_ref[:, sl], preferred_element_type=F32)
            u = jnp.dot(x, wu_ref[:, sl], preferred_element_type=F32)
            a = (g * jax.nn.sigmoid(g) * u).astype(BF16)
            y = jnp.dot(a, wd_ref[sl, :], preferred_element_type=F32)
            if c == 0:
                o_ref[...] = y
            else:
                o_ref[...] += y
        if len(refs) == 3:
            res_ref, gain_ref, _ = refs
            o_ref[...] = res_ref[...] + _rms(o_ref[...], gain_ref[...])

    @pl.when(i >= nv_ref[0])
    def _():
        o_ref[...] = jnp.zeros_like(o_ref)


def _ffn(xs, tile_expert, n_valid, wg, wu, wd, tm, f_chunk, residual=None, gain=None):
    rows, d = xs.shape
    d_ff = wg.shape[2]
    assert d_ff % f_chunk == 0 and rows % tm == 0
    xmap = lambda i, te, nv: (jnp.minimum(i, nv[0] - 1), 0)
    wmap = lambda i, te, nv: (te[i], 0, 0)
    in_specs = [
        pl.BlockSpec((tm, d), xmap),
        _resident((None, d, d_ff), wmap),
        _resident((None, d, d_ff), wmap),
        _resident((None, d_ff, d), wmap),
    ]
    args = [tile_expert, n_valid, xs, wg, wu, wd]
    if residual is not None:
        in_specs += [pl.BlockSpec((tm, d), xmap), pl.BlockSpec((1, d), lambda i, te, nv: (0, 0))]
        args += [residual, gain.reshape(1, d)]
    grid_spec = pltpu.PrefetchScalarGridSpec(
        num_scalar_prefetch=2,
        grid=(rows // tm,),
        in_specs=in_specs,
        out_specs=pl.BlockSpec((tm, d), lambda i, te, nv: (i, 0)),
    )
    return pl.pallas_call(
        functools.partial(_ffn_kernel, f_chunk=f_chunk),
        grid_spec=grid_spec,
        out_shape=jax.ShapeDtypeStruct((rows, d), F32),
        compiler_params=_params(("arbitrary",)),
        name="ffn",
    )(*args)


def _dispatch_kernel(pad_end_ref, d1_ref, d2_ref, h_ref, out_ref, zbuf, sem, zsem):
    tile = zbuf.shape[0]

    @pl.when(pl.program_id(0) == 0)
    def _():
        zbuf[...] = jnp.zeros_like(zbuf)

        def zero_copy(e):
            start = pl.multiple_of(pad_end_ref[e] - tile, tile)
            return pltpu.make_async_copy(zbuf, out_ref.at[pl.ds(start, tile)], zsem)

        def has_rows(e):
            prev_end = pad_end_ref[e - 1] if e else 0
            return pad_end_ref[e] > prev_end

        for e in range(pad_end_ref.shape[0]):
            @pl.when(has_rows(e))
            def _():
                zero_copy(e).start()
        for e in range(pad_end_ref.shape[0]):
            @pl.when(has_rows(e))
            def _():
                zero_copy(e).wait()

    def row_copies(j):
        src = h_ref.at[pl.ds(j, 1)]
        return (pltpu.make_async_copy(src, out_ref.at[pl.ds(d1_ref[j], 1)], sem),
                pltpu.make_async_copy(src, out_ref.at[pl.ds(d2_ref[j], 1)], sem))

    def start(j, carry):
        for c in row_copies(j):
            c.start()
        return carry

    def wait(j, carry):
        for c in row_copies(j):
            c.wait()
        return carry

    n = h_ref.shape[0]
    lax.fori_loop(0, n, start, 0, unroll=8)
    lax.fori_loop(0, n, wait, 0, unroll=8)


def _dispatch(h, dest, pad_end, n_rows, tm, tile):
    t, d = h.shape
    assert tm % SMEM_BLOCK_WORDS == 0 and t % tm == 0
    nblk = t // tm
    grid_spec = pltpu.PrefetchScalarGridSpec(
        num_scalar_prefetch=1,
        grid=(nblk,),
        in_specs=[pl.BlockSpec((tm,), lambda i, pe: (i,), memory_space=pltpu.SMEM),
                  pl.BlockSpec((tm,), lambda i, pe: (i + nblk,), memory_space=pltpu.SMEM),
                  pl.BlockSpec((tm, d), lambda i, pe: (i, 0))],
        out_specs=pl.BlockSpec(memory_space=pl.ANY),
        scratch_shapes=[pltpu.VMEM((tile, d), h.dtype), pltpu.SemaphoreType.DMA(()),
                        pltpu.SemaphoreType.DMA(())],
    )
    return pl.pallas_call(
        _dispatch_kernel,
        grid_spec=grid_spec,
        out_shape=jax.ShapeDtypeStruct((n_rows, d), h.dtype),
        compiler_params=_params(("arbitrary",)),
        name="moe_dispatch",
    )(pad_end, dest, dest, h)


def _combine_kernel(d1_ref, d2_ref, ys_ref, rp_ref, x_ref, g_ref, o_ref, ya, yb, sem):
    def row_copies(j):
        dst = pl.ds(j, 1)
        return (pltpu.make_async_copy(ys_ref.at[pl.ds(d1_ref[j], 1)], ya.at[dst], sem),
                pltpu.make_async_copy(ys_ref.at[pl.ds(d2_ref[j], 1)], yb.at[dst], sem))

    def start(j, carry):
        for c in row_copies(j):
            c.start()
        return carry

    def wait(j, carry):
        for c in row_copies(j):
            c.wait()
        return carry

    n = x_ref.shape[0]
    lax.fori_loop(0, n, start, 0, unroll=8)
    lax.fori_loop(0, n, wait, 0, unroll=8)
    y = rp_ref[:, 0:1] * ya[...] + rp_ref[:, 1:2] * yb[...]
    o_ref[...] = x_ref[...] + _rms(y, g_ref[...])


def _combine(ys, dest, route_p, x, g, tm):
    t, d = x.shape
    assert tm % SMEM_BLOCK_WORDS == 0 and t % tm == 0
    nblk = t // tm
    row = lambda i: (i, 0)
    return pl.pallas_call(
        _combine_kernel,
        grid=(nblk,),
        in_specs=[pl.BlockSpec((tm,), lambda i: (i,), memory_space=pltpu.SMEM),
                  pl.BlockSpec((tm,), lambda i: (i + nblk,), memory_space=pltpu.SMEM),
                  pl.BlockSpec(memory_space=pl.ANY),
                  pl.BlockSpec((tm, LANES), row), pl.BlockSpec((tm, d), row),
                  pl.BlockSpec((1, d), lambda i: (0, 0))],
        out_specs=pl.BlockSpec((tm, d), row),
        out_shape=jax.ShapeDtypeStruct((t, d), F32),
        scratch_shapes=[pltpu.VMEM((tm, d), F32), pltpu.VMEM((tm, d), F32),
                        pltpu.SemaphoreType.DMA(())],
        compiler_params=_params(("arbitrary",)),
        name="moe_combine",
    )(dest, dest, ys, route_p, x, g.reshape(1, d))


def _routing_tables(route_i, n_exp, tm):
    t = route_i.shape[0]
    e_flat = jnp.concatenate([route_i[:, 0], route_i[:, 1]])
    onehot = (e_flat[:, None] == jnp.arange(n_exp, dtype=jnp.int32)[None, :]).astype(jnp.int32)
    counts = jnp.sum(onehot, axis=0)
    padded = ((counts + tm - 1) // tm) * tm
    pad_end = jnp.cumsum(padded)
    pad_start = pad_end - padded
    dest = jnp.sum((jnp.cumsum(onehot, axis=0) - onehot + pad_start[None, :]) * onehot, axis=1)

    n_rows = TOP_K * t + n_exp * tm
    n_tiles = n_rows // tm
    tile_first_row = jnp.arange(n_tiles, dtype=jnp.int32) * tm
    n_valid = pad_end[-1] // tm
    tile_first_row = jnp.minimum(tile_first_row, (n_valid - 1) * tm)
    tile_expert = jnp.sum((pad_end[None, :] <= tile_first_row[:, None]).astype(jnp.int32), axis=1)
    return (dest.astype(jnp.int32), pad_end.astype(jnp.int32),
            jnp.minimum(tile_expert, n_exp - 1).astype(jnp.int32),
            n_valid.astype(jnp.int32).reshape(1), n_rows)


def _tile(n, want):
    t = min(n, want)
    while n % t:
        t -= SUBLANES
    return t


def kernel(x, positions, pre_mix_g, w_in, conv_w, conv_b, w_rgate, b_rgate, w_igate, b_igate,
           lru_lambda, attn_out_g, lru_out_g, w_out, post_mix_g, pre_ffn_g, post_ffn_g,
           dense_w_gate, dense_w_up, dense_w_down, router_w, moe_w_gate, moe_w_up, moe_w_down):
    batch, seq, d = x.shape
    depth = w_in.shape[0]
    t = batch * seq
    d_ff = dense_w_gate.shape[-1]
    n_exp = moe_w_gate.shape[1]
    assert w_in.shape[2] == 3 * ATTN_WIDTH + 2 * LRU_WIDTH
    assert seq % (DILATED_PAIRS[-1][1] * Q_BLOCK) == 0

    tm_mix = _tile(t, 512)
    tm_ffn = _tile(t, 512)
    ts_proj = _tile(seq, 512)
    tm_rows = SMEM_BLOCK_WORDS
    f_chunk = 512 if d_ff % 512 == 0 else d_ff

    xf = x.reshape(t, d).astype(F32)
    pos = positions.reshape(t, 1).astype(jnp.int32)

    for l in range(depth):
        q, k, v, rec = _in_proj(xf, pos, pre_mix_g[l].reshape(1, d), w_in[l].astype(BF16),
                                conv_w[l], conv_b[l], w_rgate[l], b_rgate[l], w_igate[l],
                                b_igate[l], lru_lambda[l], batch, seq, ts_proj)
        attn = _attention(q, k, v, batch, seq)
        j = l // 2
        if l % 2 == 0:
            x1, h2 = _mix_out(attn, rec, xf, attn_out_g[l], lru_out_g[l], w_out[l],
                              post_mix_g[l], pre_ffn_g[l], None, BF16, tm_mix)
            n_tiles = t // tm_ffn
            xf = _ffn(h2, jnp.zeros((n_tiles,), jnp.int32), jnp.full((1,), n_tiles, jnp.int32),
                      dense_w_gate[j][None].astype(BF16), dense_w_up[j][None].astype(BF16),
                      dense_w_down[j][None].astype(BF16), tm_ffn, f_chunk,
                      residual=x1, gain=post_ffn_g[l])
        else:
            x1, h2, route_i, route_p = _mix_out(
                attn, rec, xf, attn_out_g[l], lru_out_g[l], w_out[l], post_mix_g[l],
                pre_ffn_g[l], router_w[j], F32, tm_mix // 2)
            dest, pad_end, tile_expert, n_valid, n_rows = _routing_tables(route_i, n_exp, tm_ffn)
            xs = _dispatch(h2, dest, pad_end, n_rows, tm_rows, tm_ffn)
            ys = _ffn(xs, tile_expert, n_valid, moe_w_gate[j].astype(BF16),
                      moe_w_up[j].astype(BF16), moe_w_down[j].astype(BF16), tm_ffn, f_chunk)
            xf = _combine(ys, dest, route_p, x1, post_ffn_g[l], tm_rows)
    return xf.reshape(batch, seq, d).astype(x.dtype)
```

```python
import functools
import math

import jax
import jax.numpy as jnp
from jax import lax
from jax.experimental import pallas as pl
from jax.experimental.pallas import tpu as pltpu

ATTN_HEADS = 8
HEAD_DIM = 64
ATTN_WIDTH = ATTN_HEADS * HEAD_DIM
LRU_BLOCKS = 8
LRU_BLOCK_W = 64
LRU_WIDTH = LRU_BLOCKS * LRU_BLOCK_W
DILATED_PAIRS = ((128, 1), (512, 4), (2048, 16))
Q_BLOCK = 128
ROT_DIM = HEAD_DIM // 4
ROT_HALF = ROT_DIM // 2
ROPE_THETA = 500000.0
ATTN_SCALE = 1.0 / math.sqrt(HEAD_DIM)
NEG_INF = -1e30
CONV_W = 4
LRU_C = 8.0
TOP_K = 2
RMS_EPS = 1e-6

LANES = 128
SUBLANES = 8
SMEM_BLOCK_WORDS = 1024
VMEM_LIMIT_BYTES = 56 * 1024 * 1024

F32 = jnp.float32
BF16 = jnp.bfloat16


def _params(semantics):
    return pltpu.CompilerParams(dimension_semantics=semantics,
                                vmem_limit_bytes=VMEM_LIMIT_BYTES)


def _resident(block_shape, index_map):
    return pl.BlockSpec(block_shape, index_map, pipeline_mode=pl.Buffered(1))


def _rms(x, g):
    var = jnp.mean(x * x, axis=-1, keepdims=True)
    return x * lax.rsqrt(var + RMS_EPS) * g


def _lru_block(u, gate, cw_ref, cb_ref, wr_ref, br_ref, wi_ref, bi_ref, lam_ref,
               ubuf, hcarry, a_scr, b_scr):
    ts, c = u.shape
    ubuf[SUBLANES:SUBLANES + ts, :] = u
    ext = ubuf[...]
    uc = cb_ref[...]
    for tap in range(CONV_W):
        back = CONV_W - 1 - tap
        shifted = pltpu.roll(ext, back, 0) if back else ext
        uc = uc + shifted[SUBLANES:SUBLANES + ts, :] * cw_ref[tap:tap + 1, :]
    ubuf[0:SUBLANES, :] = ubuf[ts:ts + SUBLANES, :]

    ucb = uc.astype(BF16)
    r = jax.nn.sigmoid(jnp.dot(ucb, wr_ref[...], preferred_element_type=F32) + br_ref[...])
    ig = jax.nn.sigmoid(jnp.dot(ucb, wi_ref[...], preferred_element_type=F32) + bi_ref[...])
    nl = -lam_ref[...]
    softplus = jnp.maximum(nl, 0.0) + jnp.log1p(jnp.exp(-jnp.abs(nl)))
    log_a = -LRU_C * r * softplus
    a = jnp.exp(log_a)
    mult = jnp.sqrt(1.0 - jnp.exp(2.0 * log_a))
    b = mult * ig * uc

    groups = ts // SUBLANES
    a3 = a.reshape(groups, SUBLANES, c)
    b3 = b.reshape(groups, SUBLANES, c)
    sub = lax.broadcasted_iota(jnp.int32, (groups, SUBLANES, c), 1)
    shift = 1
    while shift < SUBLANES:
        valid = sub >= shift
        a_sh = jnp.where(valid, pltpu.roll(a3, shift, 1), 1.0)
        b_sh = jnp.where(valid, pltpu.roll(b3, shift, 1), 0.0)
        b3 = a3 * b_sh + b3
        a3 = a3 * a_sh
        shift *= 2
    a_scr[...] = a3.reshape(ts, c)
    b_scr[...] = b3.reshape(ts, c)

    h_prev = hcarry[...]
    for g in range(groups):
        rows = slice(g * SUBLANES, (g + 1) * SUBLANES)
        h = a_scr[rows, :] * h_prev + b_scr[rows, :]
        b_scr[rows, :] = h
        h_prev = jnp.broadcast_to(h[SUBLANES - 1:SUBLANES, :], (SUBLANES, c))
    hcarry[...] = h_prev
    return jax.nn.gelu(gate) * b_scr[...]


def _in_proj_kernel(x_ref, pos_ref, g_ref, w_ref, rope_ref,
                    cw_ref, cb_ref, wr_ref, br_ref, wi_ref, bi_ref, lam_ref,
                    q_ref, k_ref, v_ref, rec_ref, ubuf, hcarry, a_scr, b_scr):
    @pl.when(pl.program_id(1) == 0)
    def _():
        ubuf[0:SUBLANES, :] = jnp.zeros((SUBLANES, ubuf.shape[1]), F32)
        hcarry[...] = jnp.zeros_like(hcarry)

    ang = pos_ref[...].astype(F32) * rope_ref[0:1, :]
    cs = jnp.cos(ang)
    sn = jnp.sin(ang)
    s_lo = sn * rope_ref[1:2, :]
    s_hi = sn * rope_ref[2:3, :]

    def rotary(z):
        return (z * cs + pltpu.roll(z, ROT_HALF, 1) * s_lo
                + pltpu.roll(z, LANES - ROT_HALF, 1) * s_hi)

    h = _rms(x_ref[...], g_ref[...]).astype(BF16)
    w = ATTN_WIDTH
    u = jnp.dot(h, w_ref[:, 3 * w:3 * w + LRU_WIDTH], preferred_element_type=F32)
    gate = jnp.dot(h, w_ref[:, 3 * w + LRU_WIDTH:], preferred_element_type=F32)
    rec_ref[...] = _lru_block(u, gate, cw_ref, cb_ref, wr_ref, br_ref, wi_ref, bi_ref, lam_ref,
                              ubuf, hcarry, a_scr, b_scr)
    for sec, out in ((0, q_ref), (1, k_ref)):
        z = jnp.dot(h, w_ref[:, sec * w:(sec + 1) * w], preferred_element_type=F32)
        for cb in range(w // LANES):
            sl = slice(cb * LANES, (cb + 1) * LANES)
            out[:, sl] = rotary(z[:, sl]).astype(BF16)
    v_ref[...] = jnp.dot(h, w_ref[:, 2 * w:3 * w], preferred_element_type=F32).astype(BF16)


def _rope_table():
    lane = jnp.arange(LANES, dtype=jnp.int32) % HEAD_DIM
    inv_freq = ROPE_THETA ** (-jnp.arange(ROT_HALF, dtype=F32) / ROT_HALF)
    freq = jnp.where(lane < ROT_DIM, inv_freq[lane % ROT_HALF], 0.0)
    lo = jnp.where((lane >= ROT_HALF) & (lane < ROT_DIM), 1.0, 0.0)
    hi = jnp.where(lane < ROT_HALF, -1.0, 0.0)
    return jnp.stack([freq, lo, hi]).astype(F32)


def _block_diag(w):
    nb, bw, _ = w.shape
    eye = jnp.eye(nb, dtype=w.dtype)
    return jnp.einsum("gij,gh->gihj", w, eye).reshape(nb * bw, nb * bw)


def _in_proj(x, pos, g, w_in, conv_w, conv_b, w_r, b_r, w_i, b_i, lam, batch, seq, ts):
    t, d = x.shape
    ncol = w_in.shape[1]
    c = LRU_WIDTH
    nblk = seq // ts
    row = lambda b, j: (b * nblk + j, 0)
    const = lambda b, j: (0, 0)
    vec = lambda a: a.reshape(1, c).astype(F32)
    return pl.pallas_call(
        _in_proj_kernel,
        grid=(batch, nblk),
        in_specs=[
            pl.BlockSpec((ts, d), row),
            pl.BlockSpec((ts, 1), row),
            pl.BlockSpec((1, d), const),
            _resident((d, ncol), const),
            pl.BlockSpec((3, LANES), const),
            pl.BlockSpec((CONV_W, c), const),
            pl.BlockSpec((1, c), const),
            pl.BlockSpec((c, c), const),
            pl.BlockSpec((1, c), const),
            pl.BlockSpec((c, c), const),
            pl.BlockSpec((1, c), const),
            pl.BlockSpec((1, c), const),
        ],
        out_specs=[pl.BlockSpec((ts, ATTN_WIDTH), row)] * 3 + [pl.BlockSpec((ts, c), row)],
        out_shape=[jax.ShapeDtypeStruct((t, ATTN_WIDTH), BF16)] * 3
        + [jax.ShapeDtypeStruct((t, c), F32)],
        scratch_shapes=[pltpu.VMEM((ts + SUBLANES, c), F32), pltpu.VMEM((SUBLANES, c), F32),
                        pltpu.VMEM((ts, c), F32), pltpu.VMEM((ts, c), F32)],
        compiler_params=_params(("parallel", "arbitrary")),
        name="in_proj_lru",
    )(x, pos, g, w_in, _rope_table(), conv_w.astype(F32), vec(conv_b),
      _block_diag(w_r).astype(BF16), vec(b_r), _block_diag(w_i).astype(BF16), vec(b_i), vec(lam))


ATTN_TILE = DILATED_PAIRS[-1][1] * Q_BLOCK


def _rows(start, size, stride):
    return pl.ds(start, size) if stride == 1 else pl.ds(start, size, stride=stride)


ATTN_PIPE_WIDTH = 4


def _attn_kernel(q_ref, kc_ref, vc_ref, kp_ref, vp_ref, bias_ref, o_ref,
                 q0_scr, q1_scr, k_scr, v_scr, ob0, ob1, ob2, ls0, ls1, ls2, *slots):
    s_scrs, m_scrs = slots[:len(slots) // 2], slots[len(slots) // 2:]
    j = pl.program_id(2)
    ts = q_ref.shape[0]
    lane = lax.broadcasted_iota(jnp.int32, (1, LANES), 1)
    head0 = lane < HEAD_DIM

    qf = q_ref[...].astype(F32) * ATTN_SCALE
    q0_scr[...] = jnp.where(head0, qf, 0.0)
    q1_scr[...] = jnp.where(head0, 0.0, qf)
    k_scr[0:ts, :] = kp_ref[...].astype(F32)
    k_scr[ts:2 * ts, :] = kc_ref[...].astype(F32)
    v_scr[0:ts, :] = vp_ref[...].astype(F32)
    v_scr[ts:2 * ts, :] = vc_ref[...].astype(F32)

    nt = (((1,), (1,)), ((), ()))
    units = ts // Q_BLOCK
    width = len(s_scrs) // 2
    groups = units // width
    assert groups % 2 == 0 and groups >= 2
    for (_, dil), ob, ls in zip(DILATED_PAIRS, (ob0, ob1, ob2), (ls0, ls1, ls2)):
        shift = dil.bit_length() - 1

        def rows_of(u, dil=dil, shift=shift):
            cls = jnp.bitwise_and(u, dil - 1)
            blk = lax.shift_right_logical(u, shift)
            q_start = cls + dil * Q_BLOCK * blk
            q_rows = _rows(q_start, Q_BLOCK, dil)
            kv_rows = _rows(ts + q_start - dil * Q_BLOCK, 2 * Q_BLOCK, dil)
            return blk, q_rows, kv_rows

        def scores(u, slot):
            blk, q_rows, kv_rows = rows_of(u)
            lhs = jnp.concatenate([q0_scr[q_rows, :], q1_scr[q_rows, :]], axis=0).astype(BF16)
            k2 = k_scr[kv_rows, :].astype(BF16)
            no_prev = jnp.logical_and(j == 0, blk == 0).astype(jnp.int32)
            bias = bias_ref[pl.ds(no_prev * 2 * Q_BLOCK, 2 * Q_BLOCK), :]
            s = lax.dot_general(lhs, k2, nt, preferred_element_type=F32) + bias
            s_scrs[slot][...] = s
            m_scrs[slot][...] = jnp.broadcast_to(jnp.max(s, axis=-1, keepdims=True),
                                                 (2 * Q_BLOCK, LANES))

        def finish(u, slot, ob=ob, ls=ls):
            _, q_rows, kv_rows = rows_of(u)
            m = m_scrs[slot][...]
            p = jnp.exp(s_scrs[slot][...] - jnp.concatenate([m, m], axis=1))
            den = jnp.sum(p, axis=-1, keepdims=True)
            v2 = v_scr[kv_rows, :].astype(BF16)
            o = jnp.dot(p.astype(BF16), v2, preferred_element_type=F32) / den
            lse = m + jnp.log(den)
            ob[q_rows, :] = jnp.where(head0, o[0:Q_BLOCK], o[Q_BLOCK:])
            ls[q_rows, :] = jnp.where(head0, lse[0:Q_BLOCK], lse[Q_BLOCK:])

        def scores_group(g, half):
            for w in range(width):
                scores(g * width + w, half * width + w)

        def finish_group(g, half):
            for w in range(width):
                finish(g * width + w, half * width + w)

        scores_group(jnp.int32(0), 0)

        def pair(i, carry):
            scores_group(2 * i + 1, 1)
            finish_group(2 * i, 0)
            scores_group(2 * i + 2, 0)
            finish_group(2 * i + 1, 1)
            return carry

        lax.fori_loop(0, groups // 2 - 1, pair, 0)
        scores_group(jnp.int32(groups - 1), 1)
        finish_group(jnp.int32(groups - 2), 0)
        finish_group(jnp.int32(groups - 1), 1)

    chunk = 2 * Q_BLOCK

    def mix(c, carry):
        r = pl.ds(pl.multiple_of(c * chunk, chunk), chunk)
        la, lb, lc = ls0[r, :], ls1[r, :], ls2[r, :]
        m = jnp.maximum(jnp.maximum(la, lb), lc)
        ea, eb, ec = jnp.exp(la - m), jnp.exp(lb - m), jnp.exp(lc - m)
        o_ref[r, :] = (ea * ob0[r, :] + eb * ob1[r, :] + ec * ob2[r, :]) / (ea + eb + ec)
        return carry

    lax.fori_loop(0, ts // chunk, mix, 0)


def _attn_bias():
    qi = jnp.arange(2 * Q_BLOCK, dtype=jnp.int32)[:, None] % Q_BLOCK
    kj = jnp.arange(2 * Q_BLOCK, dtype=jnp.int32)[None, :]
    delta = Q_BLOCK + qi - kj
    band = (delta >= 0) & (delta <= Q_BLOCK)
    with_prev = jnp.where(band, 0.0, NEG_INF)
    no_prev = jnp.where(band & (kj >= Q_BLOCK), 0.0, NEG_INF)
    return jnp.concatenate([with_prev, no_prev], axis=0).astype(F32)


def _attention(q, k, v, batch, seq):
    ts = ATTN_TILE
    assert seq % ts == 0
    nt = seq // ts
    cur = pl.BlockSpec((ts, LANES), lambda b, c, j: (b * nt + j, c))
    prev = pl.BlockSpec((ts, LANES), lambda b, c, j: (b * nt + jnp.maximum(j - 1, 0), c))
    tile_scr = pltpu.VMEM((ts, LANES), F32)
    pair_scr = pltpu.VMEM((2 * ts, LANES), F32)
    return pl.pallas_call(
        _attn_kernel,
        grid=(batch, ATTN_WIDTH // LANES, nt),
        in_specs=[cur, cur, cur, prev, prev,
                  pl.BlockSpec((4 * Q_BLOCK, 2 * Q_BLOCK), lambda b, c, j: (0, 0))],
        out_specs=cur,
        out_shape=jax.ShapeDtypeStruct((batch * seq, ATTN_WIDTH), F32),
        scratch_shapes=[tile_scr, tile_scr, pair_scr, pair_scr] + [tile_scr] * 6
        + [pltpu.VMEM((2 * Q_BLOCK, 2 * Q_BLOCK), F32)] * (2 * ATTN_PIPE_WIDTH)
        + [pltpu.VMEM((2 * Q_BLOCK, LANES), F32)] * (2 * ATTN_PIPE_WIDTH),
        compiler_params=_params(("parallel", "parallel", "parallel")),
        name="attention",
    )(q, k, v, k, v, _attn_bias())


def _mix(attn_ref, rec_ref, x_ref, ga_ref, gr_ref, wo_ref, gpost_ref, gpre_ref):
    na = _rms(attn_ref[...], ga_ref[...]).astype(BF16)
    nr = _rms(rec_ref[...], gr_ref[...]).astype(BF16)
    mixed = (jnp.dot(na, wo_ref[0:ATTN_WIDTH, :], preferred_element_type=F32)
             + jnp.dot(nr, wo_ref[ATTN_WIDTH:, :], preferred_element_type=F32))
    x1 = x_ref[...] + _rms(mixed, gpost_ref[...])
    return x1, _rms(x1, gpre_ref[...])


def _mix_specs(tm, d, row, const, w_out_shape):
    aw = ATTN_WIDTH
    return ([pl.BlockSpec((tm, aw), row)] * 2 + [pl.BlockSpec((tm, d), row),
            pl.BlockSpec((1, aw), const), pl.BlockSpec((1, LRU_WIDTH), const),
            _resident(w_out_shape, const), pl.BlockSpec((1, d), const),
            pl.BlockSpec((1, d), const)])


def _mix_args(attn, rec, x, ga, gr, w_out, gpost, gpre):
    d = x.shape[1]
    return [attn, rec, x, ga.reshape(1, ATTN_WIDTH), gr.reshape(1, LRU_WIDTH),
            w_out.astype(BF16), gpost.reshape(1, d), gpre.reshape(1, d)]


def _mix_router_kernel(*refs, n_exp):
    rw_ref, x1_ref, h2_ref, ri_ref, rp_ref = refs[8:]
    x1, hn = _mix(*refs[:8])
    x1_ref[...] = x1
    h2_ref[...] = hn

    rw = rw_ref[...]
    h_hi, r_hi = hn.astype(BF16), rw.astype(BF16)
    h_lo = (hn - h_hi.astype(F32)).astype(BF16)
    r_lo = (rw - r_hi.astype(F32)).astype(BF16)
    logits = (jnp.dot(h_hi, r_hi, preferred_element_type=F32)
              + jnp.dot(h_hi, r_lo, preferred_element_type=F32)
              + jnp.dot(h_lo, r_hi, preferred_element_type=F32))
    tm = logits.shape[0]
    lane = lax.broadcasted_iota(jnp.int32, (tm, LANES), 1)
    logits = jnp.where(lane < n_exp, logits, -jnp.inf)
    m1 = jnp.max(logits, axis=-1, keepdims=True)
    i1 = jnp.min(jnp.where(logits == m1, lane, n_exp), axis=-1, keepdims=True)
    rest = jnp.where(lane == i1, -jnp.inf, logits)
    m2 = jnp.max(rest, axis=-1, keepdims=True)
    i2 = jnp.min(jnp.where(rest == m2, lane, n_exp), axis=-1, keepdims=True)
    e2 = jnp.exp(m2 - m1)
    p1 = 1.0 / (1.0 + e2)
    p2 = e2 / (1.0 + e2)
    ri_ref[...] = jnp.where(lane == 0, i1, jnp.where(lane == 1, i2, 0))
    rp_ref[...] = jnp.where(lane == 0, p1, jnp.where(lane == 1, p2, 0.0))


def _mix_out_router(attn, rec, x, ga, gr, w_out, gpost, gpre, router_w, tm):
    t, d = x.shape
    row = lambda i: (i, 0)
    const = lambda i: (0, 0)
    n_exp = router_w.shape[1]
    rw = jnp.pad(router_w.astype(F32), ((0, 0), (0, LANES - n_exp)))
    return pl.pallas_call(
        functools.partial(_mix_router_kernel, n_exp=n_exp),
        grid=(t // tm,),
        in_specs=_mix_specs(tm, d, row, const, w_out.shape) + [pl.BlockSpec(rw.shape, const)],
        out_specs=[pl.BlockSpec((tm, d), row)] * 2 + [pl.BlockSpec((tm, LANES), row)] * 2,
        out_shape=[jax.ShapeDtypeStruct((t, d), F32)] * 2
        + [jax.ShapeDtypeStruct((t, LANES), jnp.int32), jax.ShapeDtypeStruct((t, LANES), F32)],
        compiler_params=_params(("parallel",)),
        name="mix_out_router",
    )(*_mix_args(attn, rec, x, ga, gr, w_out, gpost, gpre), rw)


def _swiglu_into(o_ref, x, wg_ref, wu_ref, wd_ref, f_chunk):
    d_ff = wg_ref.shape[1]
    for c in range(d_ff // f_chunk):
        sl = slice(c * f_chunk, (c + 1) * f_chunk)
        g = jnp.dot(x, wg_ref[:, sl], preferred_element_type=F32)
        u = jnp.dot(x, wu_ref[:, sl], preferred_element_type=F32)
        a = (g * jax.nn.sigmoid(g) * u).astype(BF16)
        y = jnp.dot(a, wd_ref[sl, :], preferred_element_type=F32)
        if c == 0:
            o_ref[...] = y
        else:
            o_ref[...] += y


def _ffn_kernel(te_ref, nv_ref, x_ref, wg_ref, wu_ref, wd_ref, o_ref, *, f_chunk):
    i = pl.program_id(0)

    @pl.when(i < nv_ref[0])
    def _():
        _swiglu_into(o_ref, x_ref[...].astype(BF16), wg_ref, wu_ref, wd_ref, f_chunk)

    @pl.when(i >= nv_ref[0])
    def _():
        o_ref[...] = jnp.zeros_like(o_ref)


def _ffn(xs, tile_expert, n_valid, wg, wu, wd, tm, f_chunk):
    rows, d = xs.shape
    d_ff = wg.shape[2]
    assert d_ff % f_chunk == 0 and rows % tm == 0
    wmap = lambda i, te, nv: (te[i], 0, 0)
    grid_spec = pltpu.PrefetchScalarGridSpec(
        num_scalar_prefetch=2,
        grid=(rows // tm,),
        in_specs=[
            pl.BlockSpec((tm, d), lambda i, te, nv: (jnp.minimum(i, nv[0] - 1), 0)),
            _resident((None, d, d_ff), wmap),
            _resident((None, d, d_ff), wmap),
            _resident((None, d_ff, d), wmap),
        ],
        out_specs=pl.BlockSpec((tm, d), lambda i, te, nv: (i, 0)),
    )
    return pl.pallas_call(
        functools.partial(_ffn_kernel, f_chunk=f_chunk),
        grid_spec=grid_spec,
        out_shape=jax.ShapeDtypeStruct((rows, d), F32),
        compiler_params=_params(("arbitrary",)),
        name="ffn",
    )(tile_expert, n_valid, xs, wg, wu, wd)


def _mix_ffn_kernel(*refs, f_chunk):
    wg_ref, wu_ref, wd_ref, gffn_ref, o_ref, h2_scr, x1_scr = refs[8:]
    s = pl.program_id(0)
    cur = lax.rem(s, 2)
    prev = 1 - cur

    @pl.when(s == 0)
    def _():
        h2_scr[1] = jnp.zeros(h2_scr.shape[1:], h2_scr.dtype)
        x1_scr[1] = jnp.zeros(x1_scr.shape[1:], x1_scr.dtype)

    x1, hn = _mix(*refs[:8])
    x1_scr[cur] = x1
    h2_scr[cur] = hn.astype(BF16)

    _swiglu_into(o_ref, h2_scr[prev], wg_ref, wu_ref, wd_ref, f_chunk)
    o_ref[...] = x1_scr[prev] + _rms(o_ref[...], gffn_ref[...])


def _mix_ffn(attn, rec, x, ga, gr, w_out, gpost, gpre, wg, wu, wd, gffn, tm, f_chunk):
    t, d = x.shape
    d_ff = wg.shape[1]
    n = t // tm
    assert d_ff % f_chunk == 0 and t % tm == 0
    row = lambda s: (jnp.minimum(s, n - 1), 0)
    const = lambda s: (0, 0)
    return pl.pallas_call(
        functools.partial(_mix_ffn_kernel, f_chunk=f_chunk),
        grid=(n + 1,),
        in_specs=_mix_specs(tm, d, row, const, w_out.shape)
        + [_resident((d, d_ff), const), _resident((d, d_ff), const), _resident((d_ff, d), const),
           pl.BlockSpec((1, d), const)],
        out_specs=pl.BlockSpec((tm, d), lambda s: (jnp.maximum(s - 1, 0), 0)),
        out_shape=jax.ShapeDtypeStruct((t, d), F32),
        scratch_shapes=[pltpu.VMEM((2, tm, d), BF16), pltpu.VMEM((2, tm, d), F32)],
        compiler_params=_params(("arbitrary",)),
        name="mix_ffn_dense",
    )(*_mix_args(attn, rec, x, ga, gr, w_out, gpost, gpre),
      wg.astype(BF16), wu.astype(BF16), wd.astype(BF16), gffn.reshape(1, d))


def _dispatch_kernel(pad_end_ref, d1_ref, d2_ref, h_ref, out_ref, zbuf, sem, zsem):
    tile = zbuf.shape[0]

    @pl.when(pl.program_id(0) == 0)
    def _():
        zbuf[...] = jnp.zeros_like(zbuf)

        def zero_copy(e):
            start = pl.multiple_of(pad_end_ref[e] - tile, tile)
            return pltpu.make_async_copy(zbuf, out_ref.at[pl.ds(start, tile)], zsem)

        def has_rows(e):
            prev_end = pad_end_ref[e - 1] if e else 0
            return pad_end_ref[e] > prev_end

        for e in range(pad_end_ref.shape[0]):
            @pl.when(has_rows(e))
            def _():
                zero_copy(e).start()
        for e in range(pad_end_ref.shape[0]):
            @pl.when(has_rows(e))
            def _():
                zero_copy(e).wait()

    def row_copies(j):
        src = h_ref.at[pl.ds(j, 1)]
        return (pltpu.make_async_copy(src, out_ref.at[pl.ds(d1_ref[j], 1)], sem),
                pltpu.make_async_copy(src, out_ref.at[pl.ds(d2_ref[j], 1)], sem))

    def start(j, carry):
        for c in row_copies(j):
            c.start()
        return carry

    def wait(j, carry):
        for c in row_copies(j):
            c.wait()
        return carry

    n = h_ref.shape[0]
    lax.fori_loop(0, n, start, 0, unroll=8)
    lax.fori_loop(0, n, wait, 0, unroll=8)


def _dispatch(h, dest, pad_end, n_rows, tm, tile):
    t, d = h.shape
    assert tm % SMEM_BLOCK_WORDS == 0 and t % tm == 0
    nblk = t // tm
    grid_spec = pltpu.PrefetchScalarGridSpec(
        num_scalar_prefetch=1,
        grid=(nblk,),
        in_specs=[pl.BlockSpec((tm,), lambda i, pe: (i,), memory_space=pltpu.SMEM),
                  pl.BlockSpec((tm,), lambda i, pe: (i + nblk,), memory_space=pltpu.SMEM),
                  pl.BlockSpec((tm, d), lambda i, pe: (i, 0))],
        out_specs=pl.BlockSpec(memory_space=pl.ANY),
        scratch_shapes=[pltpu.VMEM((tile, d), h.dtype), pltpu.SemaphoreType.DMA(()),
                        pltpu.SemaphoreType.DMA(())],
    )
    return pl.pallas_call(
        _dispatch_kernel,
        grid_spec=grid_spec,
        out_shape=jax.ShapeDtypeStruct((n_rows, d), h.dtype),
        compiler_params=_params(("arbitrary",)),
        name="moe_dispatch",
    )(pad_end, dest, dest, h)


def _combine_kernel(d1_ref, d2_ref, ys_ref, rp_ref, x_ref, g_ref, o_ref, ya, yb, sem):
    def row_copies(j):
        dst = pl.ds(j, 1)
        return (pltpu.make_async_copy(ys_ref.at[pl.ds(d1_ref[j], 1)], ya.at[dst], sem),
                pltpu.make_async_copy(ys_ref.at[pl.ds(d2_ref[j], 1)], yb.at[dst], sem))

    def start(j, carry):
        for c in row_copies(j):
            c.start()
        return carry

    def wait(j, carry):
        for c in row_copies(j):
            c.wait()
        return carry

    n = x_ref.shape[0]
    lax.fori_loop(0, n, start, 0, unroll=8)
    lax.fori_loop(0, n, wait, 0, unroll=8)
    y = rp_ref[:, 0:1] * ya[...] + rp_ref[:, 1:2] * yb[...]
    o_ref[...] = x_ref[...] + _rms(y, g_ref[...])


def _combine(ys, dest, route_p, x, g, tm):
    t, d = x.shape
    assert tm % SMEM_BLOCK_WORDS == 0 and t % tm == 0
    nblk = t // tm
    row = lambda i: (i, 0)
    return pl.pallas_call(
        _combine_kernel,
        grid=(nblk,),
        in_specs=[pl.BlockSpec((tm,), lambda i: (i,), memory_space=pltpu.SMEM),
                  pl.BlockSpec((tm,), lambda i: (i + nblk,), memory_space=pltpu.SMEM),
                  pl.BlockSpec(memory_space=pl.ANY),
                  pl.BlockSpec((tm, LANES), row), pl.BlockSpec((tm, d), row),
                  pl.BlockSpec((1, d), lambda i: (0, 0))],
        out_specs=pl.BlockSpec((tm, d), row),
        out_shape=jax.ShapeDtypeStruct((t, d), F32),
        scratch_shapes=[pltpu.VMEM((tm, d), F32), pltpu.VMEM((tm, d), F32),
                        pltpu.SemaphoreType.DMA(())],
        compiler_params=_params(("arbitrary",)),
        name="moe_combine",
    )(dest, dest, ys, route_p, x, g.reshape(1, d))


def _routing_tables(route_i, n_exp, tm):
    t = route_i.shape[0]
    e_flat = jnp.concatenate([route_i[:, 0], route_i[:, 1]])
    onehot = (e_flat[:, None] == jnp.arange(n_exp, dtype=jnp.int32)[None, :]).astype(jnp.int32)
    counts = jnp.sum(onehot, axis=0)
    padded = ((counts + tm - 1) // tm) * tm
    pad_end = jnp.cumsum(padded)
    pad_start = pad_end - padded
    dest = jnp.sum((jnp.cumsum(onehot, axis=0) - onehot + pad_start[None, :]) * onehot, axis=1)

    n_rows = TOP_K * t + n_exp * tm
    n_tiles = n_rows // tm
    tile_first_row = jnp.arange(n_tiles, dtype=jnp.int32) * tm
    n_valid = pad_end[-1] // tm
    tile_first_row = jnp.minimum(tile_first_row, (n_valid - 1) * tm)
    tile_expert = jnp.sum((pad_end[None, :] <= tile_first_row[:, None]).astype(jnp.int32), axis=1)
    return (dest.astype(jnp.int32), pad_end.astype(jnp.int32),
            jnp.minimum(tile_expert, n_exp - 1).astype(jnp.int32),
            n_valid.astype(jnp.int32).reshape(1), n_rows)


def _tile(n, want):
    t = min(n, want)
    while n % t:
        t -= SUBLANES
    return t


def kernel(x, positions, pre_mix_g, w_in, conv_w, conv_b, w_rgate, b_rgate, w_igate, b_igate,
           lru_lambda, attn_out_g, lru_out_g, w_out, post_mix_g, pre_ffn_g, post_ffn_g,
           dense_w_gate, dense_w_up, dense_w_down, router_w, moe_w_gate, moe_w_up, moe_w_down):
    batch, seq, d = x.shape
    depth = w_in.shape[0]
    t = batch * seq
    d_ff = dense_w_gate.shape[-1]
    n_exp = moe_w_gate.shape[1]
    assert w_in.shape[2] == 3 * ATTN_WIDTH + 2 * LRU_WIDTH
    assert seq % (DILATED_PAIRS[-1][1] * Q_BLOCK) == 0

    tm_mix = _tile(t, 256)
    tm_ffn = _tile(t, 512)
    tm_moe = tm_ffn
    ts_proj = _tile(seq, 512)
    tm_rows = SMEM_BLOCK_WORDS
    f_chunk = 512 if d_ff % 512 == 0 else d_ff

    xf = x.reshape(t, d).astype(F32)
    pos = positions.reshape(t, 1).astype(jnp.int32)

    for l in range(depth):
        q, k, v, rec = _in_proj(xf, pos, pre_mix_g[l].reshape(1, d), w_in[l].astype(BF16),
                                conv_w[l], conv_b[l], w_rgate[l], b_rgate[l], w_igate[l],
                                b_igate[l], lru_lambda[l], batch, seq, ts_proj)
        attn = _attention(q, k, v, batch, seq)
        j = l // 2
        if l % 2 == 0:
            xf = _mix_ffn(attn, rec, xf, attn_out_g[l], lru_out_g[l], w_out[l], post_mix_g[l],
                          pre_ffn_g[l], dense_w_gate[j], dense_w_up[j], dense_w_down[j],
                          post_ffn_g[l], tm_ffn, f_chunk)
        else:
            x1, h2, route_i, route_p = _mix_out_router(
                attn, rec, xf, attn_out_g[l], lru_out_g[l], w_out[l], post_mix_g[l],
                pre_ffn_g[l], router_w[j], tm_mix)
            dest, pad_end, tile_expert, n_valid, n_rows = _routing_tables(route_i, n_exp, tm_moe)
            xs = _dispatch(h2, dest, pad_end, n_rows, tm_rows, tm_moe)
            ys = _ffn(xs, tile_expert, n_valid, moe_w_gate[j].astype(BF16),
                      moe_w_up[j].astype(BF16), moe_w_down[j].astype(BF16), tm_moe, f_chunk)
            xf = _combine(ys, dest, route_p, x1, post_ffn_g[l], tm_rows)
    return xf.reshape(batch, seq, d).astype(x.dtype)
```

```python
import functools
import math

import jax
import jax.numpy as jnp
from jax import lax
from jax.experimental import pallas as pl
from jax.experimental.pallas import tpu as pltpu

ATTN_HEADS = 8
HEAD_DIM = 64
ATTN_WIDTH = ATTN_HEADS * HEAD_DIM
LRU_BLOCKS = 8
LRU_BLOCK_W = 64
LRU_WIDTH = LRU_BLOCKS * LRU_BLOCK_W
DILATED_PAIRS = ((128, 1), (512, 4), (2048, 16))
Q_BLOCK = 128
ROT_DIM = HEAD_DIM // 4
ROT_HALF = ROT_DIM // 2
ROPE_THETA = 500000.0
ATTN_SCALE = 1.0 / math.sqrt(HEAD_DIM)
NEG_INF = -1e30
CONV_W = 4
LRU_C = 8.0
TOP_K = 2
RMS_EPS = 1e-6

LANES = 128
SUBLANES = 8
SMEM_BLOCK_WORDS = 1024
VMEM_LIMIT_BYTES = 56 * 1024 * 1024

F32 = jnp.float32
BF16 = jnp.bfloat16


def _params(semantics):
    return pltpu.CompilerParams(dimension_semantics=semantics,
                                vmem_limit_bytes=VMEM_LIMIT_BYTES)


def _resident(block_shape, index_map):
    return pl.BlockSpec(block_shape, index_map, pipeline_mode=pl.Buffered(1))


def _rms(x, g):
    var = jnp.mean(x * x, axis=-1, keepdims=True)
    return x * lax.rsqrt(var + RMS_EPS) * g


def _tile_rows(r):
    return pl.ds(pl.multiple_of(r * SUBLANES, SUBLANES), SUBLANES)


def _store_row_tiles(ref, x):
    rows = x.shape[0]
    for c in range(x.shape[1] // LANES):
        ref[pl.ds(c, rows, stride=SUBLANES), :] = x[:, c * LANES:(c + 1) * LANES]


def _load_row_tiles(ref):
    rows = ref.shape[0] // SUBLANES
    return jnp.concatenate([ref[pl.ds(c, rows, stride=SUBLANES), :] for c in range(SUBLANES)],
                           axis=1)


def _lru_block(u, gate, cw_ref, cb_ref, wr_ref, br_ref, wi_ref, bi_ref, lam_ref,
               ubuf, hcarry, a_scr, b_scr):
    ts, c = u.shape
    ubuf[SUBLANES:SUBLANES + ts, :] = u
    ext = ubuf[...]
    uc = cb_ref[...]
    for tap in range(CONV_W):
        back = CONV_W - 1 - tap
        shifted = pltpu.roll(ext, back, 0) if back else ext
        uc = uc + shifted[SUBLANES:SUBLANES + ts, :] * cw_ref[tap:tap + 1, :]
    ubuf[0:SUBLANES, :] = ubuf[ts:ts + SUBLANES, :]

    ucb = uc.astype(BF16)
    r = jax.nn.sigmoid(jnp.dot(ucb, wr_ref[...], preferred_element_type=F32) + br_ref[...])
    ig = jax.nn.sigmoid(jnp.dot(ucb, wi_ref[...], preferred_element_type=F32) + bi_ref[...])
    nl = -lam_ref[...]
    softplus = jnp.maximum(nl, 0.0) + jnp.log1p(jnp.exp(-jnp.abs(nl)))
    log_a = -LRU_C * r * softplus
    a = jnp.exp(log_a)
    mult = jnp.sqrt(1.0 - jnp.exp(2.0 * log_a))
    b = mult * ig * uc

    groups = ts // SUBLANES
    a3 = a.reshape(groups, SUBLANES, c)
    b3 = b.reshape(groups, SUBLANES, c)
    sub = lax.broadcasted_iota(jnp.int32, (groups, SUBLANES, c), 1)
    shift = 1
    while shift < SUBLANES:
        valid = sub >= shift
        a_sh = jnp.where(valid, pltpu.roll(a3, shift, 1), 1.0)
        b_sh = jnp.where(valid, pltpu.roll(b3, shift, 1), 0.0)
        b3 = a3 * b_sh + b3
        a3 = a3 * a_sh
        shift *= 2
    a_scr[...] = a3.reshape(ts, c)
    b_scr[...] = b3.reshape(ts, c)

    h_prev = hcarry[...]
    for g in range(groups):
        rows = slice(g * SUBLANES, (g + 1) * SUBLANES)
        h = a_scr[rows, :] * h_prev + b_scr[rows, :]
        b_scr[rows, :] = h
        h_prev = jnp.broadcast_to(h[SUBLANES - 1:SUBLANES, :], (SUBLANES, c))
    hcarry[...] = h_prev
    return jax.nn.gelu(gate) * b_scr[...]


def _in_proj_kernel(x_ref, pos_ref, g_ref, w_ref, rope_ref,
                    cw_ref, cb_ref, wr_ref, br_ref, wi_ref, bi_ref, lam_ref,
                    q_ref, k_ref, v_ref, rec_ref, ubuf, hcarry, a_scr, b_scr):
    @pl.when(pl.program_id(1) == 0)
    def _():
        ubuf[0:SUBLANES, :] = jnp.zeros((SUBLANES, ubuf.shape[1]), F32)
        hcarry[...] = jnp.zeros_like(hcarry)

    ang = pos_ref[...].astype(F32) * rope_ref[0:1, :]
    cs = jnp.cos(ang)
    sn = jnp.sin(ang)
    s_lo = sn * rope_ref[1:2, :]
    s_hi = sn * rope_ref[2:3, :]

    def rotary(z):
        return (z * cs + pltpu.roll(z, ROT_HALF, 1) * s_lo
                + pltpu.roll(z, LANES - ROT_HALF, 1) * s_hi)

    h = _rms(x_ref[...], g_ref[...]).astype(BF16)
    w = ATTN_WIDTH
    u = jnp.dot(h, w_ref[:, 3 * w:3 * w + LRU_WIDTH], preferred_element_type=F32)
    gate = jnp.dot(h, w_ref[:, 3 * w + LRU_WIDTH:], preferred_element_type=F32)
    rec_ref[...] = _lru_block(u, gate, cw_ref, cb_ref, wr_ref, br_ref, wi_ref, bi_ref, lam_ref,
                              ubuf, hcarry, a_scr, b_scr)
    for sec, out in ((0, q_ref), (1, k_ref)):
        z = jnp.dot(h, w_ref[:, sec * w:(sec + 1) * w], preferred_element_type=F32)
        for cb in range(w // LANES):
            sl = slice(cb * LANES, (cb + 1) * LANES)
            out[:, sl] = rotary(z[:, sl]).astype(BF16)
    v_ref[...] = jnp.dot(h, w_ref[:, 2 * w:3 * w], preferred_element_type=F32).astype(BF16)


def _rope_table():
    lane = jnp.arange(LANES, dtype=jnp.int32) % HEAD_DIM
    inv_freq = ROPE_THETA ** (-jnp.arange(ROT_HALF, dtype=F32) / ROT_HALF)
    freq = jnp.where(lane < ROT_DIM, inv_freq[lane % ROT_HALF], 0.0)
    lo = jnp.where((lane >= ROT_HALF) & (lane < ROT_DIM), 1.0, 0.0)
    hi = jnp.where(lane < ROT_HALF, -1.0, 0.0)
    return jnp.stack([freq, lo, hi]).astype(F32)


def _block_diag(w):
    nb, bw, _ = w.shape
    eye = jnp.eye(nb, dtype=w.dtype)
    return jnp.einsum("gij,gh->gihj", w, eye).reshape(nb * bw, nb * bw)


def _in_proj(x, pos, g, w_in, conv_w, conv_b, w_r, b_r, w_i, b_i, lam, batch, seq, ts):
    t, d = x.shape
    ncol = w_in.shape[1]
    c = LRU_WIDTH
    nblk = seq // ts
    row = lambda b, j: (b * nblk + j, 0)
    const = lambda b, j: (0, 0)
    vec = lambda a: a.reshape(1, c).astype(F32)
    return pl.pallas_call(
        _in_proj_kernel,
        grid=(batch, nblk),
        in_specs=[
            pl.BlockSpec((ts, d), row),
            pl.BlockSpec((ts, 1), row),
            pl.BlockSpec((1, d), const),
            _resident((d, ncol), const),
            pl.BlockSpec((3, LANES), const),
            pl.BlockSpec((CONV_W, c), const),
            pl.BlockSpec((1, c), const),
            pl.BlockSpec((c, c), const),
            pl.BlockSpec((1, c), const),
            pl.BlockSpec((c, c), const),
            pl.BlockSpec((1, c), const),
            pl.BlockSpec((1, c), const),
        ],
        out_specs=[pl.BlockSpec((ts, ATTN_WIDTH), row)] * 3 + [pl.BlockSpec((ts, c), row)],
        out_shape=[jax.ShapeDtypeStruct((t, ATTN_WIDTH), BF16)] * 3
        + [jax.ShapeDtypeStruct((t, c), F32)],
        scratch_shapes=[pltpu.VMEM((ts + SUBLANES, c), F32), pltpu.VMEM((SUBLANES, c), F32),
                        pltpu.VMEM((ts, c), F32), pltpu.VMEM((ts, c), F32)],
        compiler_params=_params(("parallel", "arbitrary")),
        name="in_proj_lru",
    )(x, pos, g, w_in, _rope_table(), conv_w.astype(F32), vec(conv_b),
      _block_diag(w_r).astype(BF16), vec(b_r), _block_diag(w_i).astype(BF16), vec(b_i), vec(lam))


ATTN_TILE = DILATED_PAIRS[-1][1] * Q_BLOCK


def _rows(start, size, stride):
    return pl.ds(start, size) if stride == 1 else pl.ds(start, size, stride=stride)


ATTN_PIPE_WIDTH = 4


def _attn_kernel(q_ref, kc_ref, vc_ref, kp_ref, vp_ref, bias_ref, o_ref,
                 q0_scr, q1_scr, k_scr, v_scr, ob0, ob1, ob2, ls0, ls1, ls2, *slots):
    s_scrs, m_scrs = slots[:len(slots) // 2], slots[len(slots) // 2:]
    j = pl.program_id(2)
    ts = q_ref.shape[0]
    lane = lax.broadcasted_iota(jnp.int32, (1, LANES), 1)
    head0 = lane < HEAD_DIM

    qf = q_ref[...].astype(F32) * ATTN_SCALE
    q0_scr[...] = jnp.where(head0, qf, 0.0)
    q1_scr[...] = jnp.where(head0, 0.0, qf)
    k_scr[0:ts, :] = kp_ref[...].astype(F32)
    k_scr[ts:2 * ts, :] = kc_ref[...].astype(F32)
    v_scr[0:ts, :] = vp_ref[...].astype(F32)
    v_scr[ts:2 * ts, :] = vc_ref[...].astype(F32)

    nt = (((1,), (1,)), ((), ()))
    units = ts // Q_BLOCK
    width = len(s_scrs) // 2
    groups = units // width
    assert groups % 2 == 0 and groups >= 2
    for (_, dil), ob, ls in zip(DILATED_PAIRS, (ob0, ob1, ob2), (ls0, ls1, ls2)):
        shift = dil.bit_length() - 1

        def rows_of(u, dil=dil, shift=shift):
            cls = jnp.bitwise_and(u, dil - 1)
            blk = lax.shift_right_logical(u, shift)
            q_start = cls + dil * Q_BLOCK * blk
            q_rows = _rows(q_start, Q_BLOCK, dil)
            kv_rows = _rows(ts + q_start - dil * Q_BLOCK, 2 * Q_BLOCK, dil)
            return blk, q_rows, kv_rows

        def scores(u, slot):
            blk, q_rows, kv_rows = rows_of(u)
            lhs = jnp.concatenate([q0_scr[q_rows, :], q1_scr[q_rows, :]], axis=0).astype(BF16)
            k2 = k_scr[kv_rows, :].astype(BF16)
            no_prev = jnp.logical_and(j == 0, blk == 0).astype(jnp.int32)
            bias = bias_ref[pl.ds(no_prev * 2 * Q_BLOCK, 2 * Q_BLOCK), :]
            s = lax.dot_general(lhs, k2, nt, preferred_element_type=F32) + bias
            s_scrs[slot][...] = s
            m_scrs[slot][...] = jnp.broadcast_to(jnp.max(s, axis=-1, keepdims=True),
                                                 (2 * Q_BLOCK, LANES))

        def finish(u, slot, ob=ob, ls=ls):
            _, q_rows, kv_rows = rows_of(u)
            m = m_scrs[slot][...]
            p = jnp.exp(s_scrs[slot][...] - jnp.concatenate([m, m], axis=1))
            den = jnp.sum(p, axis=-1, keepdims=True)
            v2 = v_scr[kv_rows, :].astype(BF16)
            o = jnp.dot(p.astype(BF16), v2, preferred_element_type=F32) / den
            lse = m + jnp.log(den)
            ob[q_rows, :] = jnp.where(head0, o[0:Q_BLOCK], o[Q_BLOCK:])
            ls[q_rows, :] = jnp.where(head0, lse[0:Q_BLOCK], lse[Q_BLOCK:])

        def scores_group(g, half):
            for w in range(width):
                scores(g * width + w, half * width + w)

        def finish_group(g, half):
            for w in range(width):
                finish(g * width + w, half * width + w)

        scores_group(jnp.int32(0), 0)

        def pair(i, carry):
            scores_group(2 * i + 1, 1)
            finish_group(2 * i, 0)
            scores_group(2 * i + 2, 0)
            finish_group(2 * i + 1, 1)
            return carry

        lax.fori_loop(0, groups // 2 - 1, pair, 0)
        scores_group(jnp.int32(groups - 1), 1)
        finish_group(jnp.int32(groups - 2), 0)
        finish_group(jnp.int32(groups - 1), 1)

    chunk = 2 * Q_BLOCK

    def mix(c, carry):
        r = pl.ds(pl.multiple_of(c * chunk, chunk), chunk)
        la, lb, lc = ls0[r, :], ls1[r, :], ls2[r, :]
        m = jnp.maximum(jnp.maximum(la, lb), lc)
        ea, eb, ec = jnp.exp(la - m), jnp.exp(lb - m), jnp.exp(lc - m)
        o_ref[r, :] = (ea * ob0[r, :] + eb * ob1[r, :] + ec * ob2[r, :]) / (ea + eb + ec)
        return carry

    lax.fori_loop(0, ts // chunk, mix, 0)


def _attn_bias():
    qi = jnp.arange(2 * Q_BLOCK, dtype=jnp.int32)[:, None] % Q_BLOCK
    kj = jnp.arange(2 * Q_BLOCK, dtype=jnp.int32)[None, :]
    delta = Q_BLOCK + qi - kj
    band = (delta >= 0) & (delta <= Q_BLOCK)
    with_prev = jnp.where(band, 0.0, NEG_INF)
    no_prev = jnp.where(band & (kj >= Q_BLOCK), 0.0, NEG_INF)
    return jnp.concatenate([with_prev, no_prev], axis=0).astype(F32)


def _attention(q, k, v, batch, seq):
    ts = ATTN_TILE
    assert seq % ts == 0
    nt = seq // ts
    cur = pl.BlockSpec((ts, LANES), lambda b, c, j: (b * nt + j, c))
    prev = pl.BlockSpec((ts, LANES), lambda b, c, j: (b * nt + jnp.maximum(j - 1, 0), c))
    tile_scr = pltpu.VMEM((ts, LANES), F32)
    pair_scr = pltpu.VMEM((2 * ts, LANES), F32)
    return pl.pallas_call(
        _attn_kernel,
        grid=(batch, ATTN_WIDTH // LANES, nt),
        in_specs=[cur, cur, cur, prev, prev,
                  pl.BlockSpec((4 * Q_BLOCK, 2 * Q_BLOCK), lambda b, c, j: (0, 0))],
        out_specs=cur,
        out_shape=jax.ShapeDtypeStruct((batch * seq, ATTN_WIDTH), F32),
        scratch_shapes=[tile_scr, tile_scr, pair_scr, pair_scr] + [tile_scr] * 6
        + [pltpu.VMEM((2 * Q_BLOCK, 2 * Q_BLOCK), F32)] * (2 * ATTN_PIPE_WIDTH)
        + [pltpu.VMEM((2 * Q_BLOCK, LANES), F32)] * (2 * ATTN_PIPE_WIDTH),
        compiler_params=_params(("parallel", "parallel", "parallel")),
        name="attention",
    )(q, k, v, k, v, _attn_bias())


def _mix(attn_ref, rec_ref, x_ref, ga_ref, gr_ref, wo_ref, gpost_ref, gpre_ref):
    na = _rms(attn_ref[...], ga_ref[...]).astype(BF16)
    nr = _rms(rec_ref[...], gr_ref[...]).astype(BF16)
    mixed = (jnp.dot(na, wo_ref[0:ATTN_WIDTH, :], preferred_element_type=F32)
             + jnp.dot(nr, wo_ref[ATTN_WIDTH:, :], preferred_element_type=F32))
    x1 = x_ref[...] + _rms(mixed, gpost_ref[...])
    return x1, _rms(x1, gpre_ref[...])


def _mix_specs(tm, d, row, const, w_out_shape):
    aw = ATTN_WIDTH
    return ([pl.BlockSpec((tm, aw), row)] * 2 + [pl.BlockSpec((tm, d), row),
            pl.BlockSpec((1, aw), const), pl.BlockSpec((1, LRU_WIDTH), const),
            _resident(w_out_shape, const), pl.BlockSpec((1, d), const),
            pl.BlockSpec((1, d), const)])


def _mix_args(attn, rec, x, ga, gr, w_out, gpost, gpre):
    d = x.shape[1]
    return [attn, rec, x, ga.reshape(1, ATTN_WIDTH), gr.reshape(1, LRU_WIDTH),
            w_out.astype(BF16), gpost.reshape(1, d), gpre.reshape(1, d)]


def _mix_router_kernel(*refs, n_exp):
    rw_ref, x1_ref, h2_ref, ri_ref, rp_ref = refs[8:]
    x1, hn = _mix(*refs[:8])
    x1_ref[...] = x1
    _store_row_tiles(h2_ref, hn)

    rw = rw_ref[...]
    h_hi, r_hi = hn.astype(BF16), rw.astype(BF16)
    h_lo = (hn - h_hi.astype(F32)).astype(BF16)
    r_lo = (rw - r_hi.astype(F32)).astype(BF16)
    logits = (jnp.dot(h_hi, r_hi, preferred_element_type=F32)
              + jnp.dot(h_hi, r_lo, preferred_element_type=F32)
              + jnp.dot(h_lo, r_hi, preferred_element_type=F32))
    tm = logits.shape[0]
    lane = lax.broadcasted_iota(jnp.int32, (tm, LANES), 1)
    logits = jnp.where(lane < n_exp, logits, -jnp.inf)
    m1 = jnp.max(logits, axis=-1, keepdims=True)
    i1 = jnp.min(jnp.where(logits == m1, lane, n_exp), axis=-1, keepdims=True)
    rest = jnp.where(lane == i1, -jnp.inf, logits)
    m2 = jnp.max(rest, axis=-1, keepdims=True)
    i2 = jnp.min(jnp.where(rest == m2, lane, n_exp), axis=-1, keepdims=True)
    e2 = jnp.exp(m2 - m1)
    p1 = 1.0 / (1.0 + e2)
    p2 = e2 / (1.0 + e2)
    ri_ref[...] = jnp.where(lane == 0, i1, jnp.where(lane == 1, i2, 0))
    rp_ref[...] = jnp.where(lane == 0, p1, jnp.where(lane == 1, p2, 0.0))


def _mix_out_router(attn, rec, x, ga, gr, w_out, gpost, gpre, router_w, tm):
    t, d = x.shape
    assert d == SUBLANES * LANES
    row = lambda i: (i, 0)
    const = lambda i: (0, 0)
    n_exp = router_w.shape[1]
    rw = jnp.pad(router_w.astype(F32), ((0, 0), (0, LANES - n_exp)))
    return pl.pallas_call(
        functools.partial(_mix_router_kernel, n_exp=n_exp),
        grid=(t // tm,),
        in_specs=_mix_specs(tm, d, row, const, w_out.shape) + [pl.BlockSpec(rw.shape, const)],
        out_specs=[pl.BlockSpec((tm, d), row), pl.BlockSpec((tm * SUBLANES, LANES), row)]
        + [pl.BlockSpec((tm, LANES), row)] * 2,
        out_shape=[jax.ShapeDtypeStruct((t, d), F32),
                   jax.ShapeDtypeStruct((t * SUBLANES, LANES), F32)]
        + [jax.ShapeDtypeStruct((t, LANES), jnp.int32), jax.ShapeDtypeStruct((t, LANES), F32)],
        compiler_params=_params(("parallel",)),
        name="mix_out_router",
    )(*_mix_args(attn, rec, x, ga, gr, w_out, gpost, gpre), rw)


def _swiglu_into(o_ref, x, wg_ref, wu_ref, wd_ref, f_chunk):
    d_ff = wg_ref.shape[1]
    for c in range(d_ff // f_chunk):
        sl = slice(c * f_chunk, (c + 1) * f_chunk)
        g = jnp.dot(x, wg_ref[:, sl], preferred_element_type=F32)
        u = jnp.dot(x, wu_ref[:, sl], preferred_element_type=F32)
        a = (g * jax.nn.sigmoid(g) * u).astype(BF16)
        y = jnp.dot(a, wd_ref[sl, :], preferred_element_type=F32)
        if c == 0:
            o_ref[...] = y
        else:
            o_ref[...] += y


def _ffn_kernel(te_ref, nv_ref, x_ref, wg_ref, wu_ref, wd_ref, o_ref, acc, *, f_chunk):
    i = pl.program_id(0)

    @pl.when(i < nv_ref[0])
    def _():
        _swiglu_into(acc, _load_row_tiles(x_ref).astype(BF16), wg_ref, wu_ref, wd_ref, f_chunk)
        _store_row_tiles(o_ref, acc[...])

    @pl.when(i >= nv_ref[0])
    def _():
        o_ref[...] = jnp.zeros_like(o_ref)


def _ffn(xs, tile_expert, n_valid, wg, wu, wd, tm, f_chunk):
    d, d_ff = wg.shape[1], wg.shape[2]
    rows = xs.shape[0] // SUBLANES
    assert d == SUBLANES * LANES and d_ff % f_chunk == 0 and rows % tm == 0
    wmap = lambda i, te, nv: (te[i], 0, 0)
    grid_spec = pltpu.PrefetchScalarGridSpec(
        num_scalar_prefetch=2,
        grid=(rows // tm,),
        in_specs=[
            pl.BlockSpec((tm * SUBLANES, LANES), lambda i, te, nv: (jnp.minimum(i, nv[0] - 1), 0)),
            _resident((None, d, d_ff), wmap),
            _resident((None, d, d_ff), wmap),
            _resident((None, d_ff, d), wmap),
        ],
        out_specs=pl.BlockSpec((tm * SUBLANES, LANES), lambda i, te, nv: (i, 0)),
        scratch_shapes=[pltpu.VMEM((tm, d), F32)],
    )
    return pl.pallas_call(
        functools.partial(_ffn_kernel, f_chunk=f_chunk),
        grid_spec=grid_spec,
        out_shape=jax.ShapeDtypeStruct(xs.shape, F32),
        compiler_params=_params(("arbitrary",)),
        name="ffn",
    )(tile_expert, n_valid, xs, wg, wu, wd)


def _mix_ffn_kernel(*refs, f_chunk):
    wg_ref, wu_ref, wd_ref, gffn_ref, o_ref, h2_scr, x1_scr = refs[8:]
    s = pl.program_id(0)
    cur = lax.rem(s, 2)
    prev = 1 - cur

    @pl.when(s == 0)
    def _():
        h2_scr[1] = jnp.zeros(h2_scr.shape[1:], h2_scr.dtype)
        x1_scr[1] = jnp.zeros(x1_scr.shape[1:], x1_scr.dtype)

    x1, hn = _mix(*refs[:8])
    x1_scr[cur] = x1
    h2_scr[cur] = hn.astype(BF16)

    _swiglu_into(o_ref, h2_scr[prev], wg_ref, wu_ref, wd_ref, f_chunk)
    o_ref[...] = x1_scr[prev] + _rms(o_ref[...], gffn_ref[...])


def _mix_ffn(attn, rec, x, ga, gr, w_out, gpost, gpre, wg, wu, wd, gffn, tm, f_chunk):
    t, d = x.shape
    d_ff = wg.shape[1]
    n = t // tm
    assert d_ff % f_chunk == 0 and t % tm == 0
    row = lambda s: (jnp.minimum(s, n - 1), 0)
    const = lambda s: (0, 0)
    return pl.pallas_call(
        functools.partial(_mix_ffn_kernel, f_chunk=f_chunk),
        grid=(n + 1,),
        in_specs=_mix_specs(tm, d, row, const, w_out.shape)
        + [_resident((d, d_ff), const), _resident((d, d_ff), const), _resident((d_ff, d), const),
           pl.BlockSpec((1, d), const)],
        out_specs=pl.BlockSpec((tm, d), lambda s: (jnp.maximum(s - 1, 0), 0)),
        out_shape=jax.ShapeDtypeStruct((t, d), F32),
        scratch_shapes=[pltpu.VMEM((2, tm, d), BF16), pltpu.VMEM((2, tm, d), F32)],
        compiler_params=_params(("arbitrary",)),
        name="mix_ffn_dense",
    )(*_mix_args(attn, rec, x, ga, gr, w_out, gpost, gpre),
      wg.astype(BF16), wu.astype(BF16), wd.astype(BF16), gffn.reshape(1, d))


def _dispatch_kernel(pad_end_ref, d1_ref, d2_ref, h_ref, out_ref, zbuf, sem, zsem):
    tile = zbuf.shape[0]

    @pl.when(pl.program_id(0) == 0)
    def _():
        zbuf[...] = jnp.zeros_like(zbuf)

        def zero_copy(e):
            start = pl.multiple_of(pad_end_ref[e] * SUBLANES - tile, tile)
            return pltpu.make_async_copy(zbuf, out_ref.at[pl.ds(start, tile)], zsem)

        def has_rows(e):
            prev_end = pad_end_ref[e - 1] if e else 0
            return pad_end_ref[e] > prev_end

        for e in range(pad_end_ref.shape[0]):
            @pl.when(has_rows(e))
            def _():
                zero_copy(e).start()
        for e in range(pad_end_ref.shape[0]):
            @pl.when(has_rows(e))
            def _():
                zero_copy(e).wait()

    def row_copies(j):
        src = h_ref.at[_tile_rows(j)]
        return (pltpu.make_async_copy(src, out_ref.at[_tile_rows(d1_ref[j])], sem),
                pltpu.make_async_copy(src, out_ref.at[_tile_rows(d2_ref[j])], sem))

    def start(j, carry):
        for c in row_copies(j):
            c.start()
        return carry

    def wait(j, carry):
        for c in row_copies(j):
            c.wait()
        return carry

    n = d1_ref.shape[0]
    lax.fori_loop(0, n, start, 0, unroll=8)
    lax.fori_loop(0, n, wait, 0, unroll=8)


def _dispatch(h, dest, pad_end, n_rows, tm, tile):
    t = h.shape[0] // SUBLANES
    assert tm % SMEM_BLOCK_WORDS == 0 and t % tm == 0
    nblk = t // tm
    grid_spec = pltpu.PrefetchScalarGridSpec(
        num_scalar_prefetch=1,
        grid=(nblk,),
        in_specs=[pl.BlockSpec((tm,), lambda i, pe: (i,), memory_space=pltpu.SMEM),
                  pl.BlockSpec((tm,), lambda i, pe: (i + nblk,), memory_space=pltpu.SMEM),
                  pl.BlockSpec((tm * SUBLANES, LANES), lambda i, pe: (i, 0))],
        out_specs=pl.BlockSpec(memory_space=pl.ANY),
        scratch_shapes=[pltpu.VMEM((tile * SUBLANES, LANES), h.dtype),
                        pltpu.SemaphoreType.DMA(()), pltpu.SemaphoreType.DMA(())],
    )
    return pl.pallas_call(
        _dispatch_kernel,
        grid_spec=grid_spec,
        out_shape=jax.ShapeDtypeStruct((n_rows * SUBLANES, LANES), h.dtype),
        compiler_params=_params(("arbitrary",)),
        name="moe_dispatch",
    )(pad_end, dest, dest, h)


def _combine_kernel(d1_ref, d2_ref, ys_ref, rp_ref, x_ref, g_ref, o_ref, ya, yb, sem):
    def row_copies(j):
        dst = _tile_rows(j)
        return (pltpu.make_async_copy(ys_ref.at[_tile_rows(d1_ref[j])], ya.at[dst], sem),
                pltpu.make_async_copy(ys_ref.at[_tile_rows(d2_ref[j])], yb.at[dst], sem))

    def start(j, carry):
        for c in row_copies(j):
            c.start()
        return carry

    def wait(j, carry):
        for c in row_copies(j):
            c.wait()
        return carry

    n = x_ref.shape[0]
    lax.fori_loop(0, n, start, 0, unroll=8)
    lax.fori_loop(0, n, wait, 0, unroll=8)
    y = rp_ref[:, 0:1] * _load_row_tiles(ya) + rp_ref[:, 1:2] * _load_row_tiles(yb)
    o_ref[...] = x_ref[...] + _rms(y, g_ref[...])


def _combine(ys, dest, route_p, x, g, tm):
    t, d = x.shape
    assert d == SUBLANES * LANES and tm % SMEM_BLOCK_WORDS == 0 and t % tm == 0
    nblk = t // tm
    row = lambda i: (i, 0)
    return pl.pallas_call(
        _combine_kernel,
        grid=(nblk,),
        in_specs=[pl.BlockSpec((tm,), lambda i: (i,), memory_space=pltpu.SMEM),
                  pl.BlockSpec((tm,), lambda i: (i + nblk,), memory_space=pltpu.SMEM),
                  pl.BlockSpec(memory_space=pl.ANY),
                  pl.BlockSpec((tm, LANES), row), pl.BlockSpec((tm, d), row),
                  pl.BlockSpec((1, d), lambda i: (0, 0))],
        out_specs=pl.BlockSpec((tm, d), row),
        out_shape=jax.ShapeDtypeStruct((t, d), F32),
        scratch_shapes=[pltpu.VMEM((tm * SUBLANES, LANES), F32),
                        pltpu.VMEM((tm * SUBLANES, LANES), F32), pltpu.SemaphoreType.DMA(())],
        compiler_params=_params(("arbitrary",)),
        name="moe_combine",
    )(dest, dest, ys, route_p, x, g.reshape(1, d))


def _routing_tables(route_i, n_exp, tm):
    t = route_i.shape[0]
    e_flat = jnp.concatenate([route_i[:, 0], route_i[:, 1]])
    onehot = (e_flat[:, None] == jnp.arange(n_exp, dtype=jnp.int32)[None, :]).astype(jnp.int32)
    counts = jnp.sum(onehot, axis=0)
    padded = ((counts + tm - 1) // tm) * tm
    pad_end = jnp.cumsum(padded)
    pad_start = pad_end - padded
    dest = jnp.sum((jnp.cumsum(onehot, axis=0) - onehot + pad_start[None, :]) * onehot, axis=1)

    n_rows = TOP_K * t + n_exp * tm
    n_tiles = n_rows // tm
    tile_first_row = jnp.arange(n_tiles, dtype=jnp.int32) * tm
    n_valid = pad_end[-1] // tm
    tile_first_row = jnp.minimum(tile_first_row, (n_valid - 1) * tm)
    tile_expert = jnp.sum((pad_end[None, :] <= tile_first_row[:, None]).astype(jnp.int32), axis=1)
    return (dest.astype(jnp.int32), pad_end.astype(jnp.int32),
            jnp.minimum(tile_expert, n_exp - 1).astype(jnp.int32),
            n_valid.astype(jnp.int32).reshape(1), n_rows)


def _tile(n, want):
    t = min(n, want)
    while n % t:
        t -= SUBLANES
    return t


def kernel(x, positions, pre_mix_g, w_in, conv_w, conv_b, w_rgate, b_rgate, w_igate, b_igate,
           lru_lambda, attn_out_g, lru_out_g, w_out, post_mix_g, pre_ffn_g, post_ffn_g,
           dense_w_gate, dense_w_up, dense_w_down, router_w, moe_w_gate, moe_w_up, moe_w_down):
    batch, seq, d = x.shape
    depth = w_in.shape[0]
    t = batch * seq
    d_ff = dense_w_gate.shape[-1]
    n_exp = moe_w_gate.shape[1]
    assert w_in.shape[2] == 3 * ATTN_WIDTH + 2 * LRU_WIDTH
    assert seq % (DILATED_PAIRS[-1][1] * Q_BLOCK) == 0

    tm_mix = _tile(t, 256)
    tm_ffn = _tile(t, 512)
    tm_moe = tm_ffn
    ts_proj = _tile(seq, 512)
    tm_rows = SMEM_BLOCK_WORDS
    f_chunk = 512 if d_ff % 512 == 0 else d_ff

    xf = x.reshape(t, d).astype(F32)
    pos = positions.reshape(t, 1).astype(jnp.int32)

    for l in range(depth):
        q, k, v, rec = _in_proj(xf, pos, pre_mix_g[l].reshape(1, d), w_in[l].astype(BF16),
                                conv_w[l], conv_b[l], w_rgate[l], b_rgate[l], w_igate[l],
                                b_igate[l], lru_lambda[l], batch, seq, ts_proj)
        attn = _attention(q, k, v, batch, seq)
        j = l // 2
        if l % 2 == 0:
            xf = _mix_ffn(attn, rec, xf, attn_out_g[l], lru_out_g[l], w_out[l], post_mix_g[l],
                          pre_ffn_g[l], dense_w_gate[j], dense_w_up[j], dense_w_down[j],
                          post_ffn_g[l], tm_ffn, f_chunk)
        else:
            x1, h2, route_i, route_p = _mix_out_router(
                attn, rec, xf, attn_out_g[l], lru_out_g[l], w_out[l], post_mix_g[l],
                pre_ffn_g[l], router_w[j], tm_mix)
            dest, pad_end, tile_expert, n_valid, n_rows = _routing_tables(route_i, n_exp, tm_moe)
            xs = _dispatch(h2, dest, pad_end, n_rows, tm_rows, tm_moe)
            ys = _ffn(xs, tile_expert, n_valid, moe_w_gate[j].astype(BF16),
                      moe_w_up[j].astype(BF16), moe_w_down[j].astype(BF16), tm_moe, f_chunk)
            xf = _combine(ys, dest, route_p, x1, post_ffn_g[l], tm_rows)
    return xf.reshape(batch, seq, d).astype(x.dtype)
```

```python
import functools
import math

import jax
import jax.numpy as jnp
from jax import lax
from jax.experimental import pallas as pl
from jax.experimental.pallas import tpu as pltpu

ATTN_HEADS = 8
HEAD_DIM = 64
ATTN_WIDTH = ATTN_HEADS * HEAD_DIM
LRU_BLOCKS = 8
LRU_BLOCK_W = 64
LRU_WIDTH = LRU_BLOCKS * LRU_BLOCK_W
DILATED_PAIRS = ((128, 1), (512, 4), (2048, 16))
Q_BLOCK = 128
ROT_DIM = HEAD_DIM // 4
ROT_HALF = ROT_DIM // 2
ROPE_THETA = 500000.0
ATTN_SCALE = 1.0 / math.sqrt(HEAD_DIM)
NEG_INF = -1e30
CONV_W = 4
LRU_C = 8.0
TOP_K = 2
RMS_EPS = 1e-6

LANES = 128
SUBLANES = 8
SMEM_BLOCK_WORDS = 1024
VMEM_LIMIT_BYTES = 56 * 1024 * 1024

F32 = jnp.float32
BF16 = jnp.bfloat16


def _params(semantics):
    return pltpu.CompilerParams(dimension_semantics=semantics,
                                vmem_limit_bytes=VMEM_LIMIT_BYTES)


def _resident(block_shape, index_map):
    return pl.BlockSpec(block_shape, index_map, pipeline_mode=pl.Buffered(1))


def _rms(x, g):
    var = jnp.mean(x * x, axis=-1, keepdims=True)
    return x * lax.rsqrt(var + RMS_EPS) * g


def _tile_rows(r):
    return pl.ds(pl.multiple_of(r * SUBLANES, SUBLANES), SUBLANES)


def _store_row_tiles(ref, x):
    rows = x.shape[0]
    for c in range(x.shape[1] // LANES):
        ref[pl.ds(c, rows, stride=SUBLANES), :] = x[:, c * LANES:(c + 1) * LANES]


def _load_row_tiles(ref):
    rows = ref.shape[0] // SUBLANES
    return jnp.concatenate([ref[pl.ds(c, rows, stride=SUBLANES), :] for c in range(SUBLANES)],
                           axis=1)


def _lru_block(u, gate, cw_ref, cb_ref, wr_ref, br_ref, wi_ref, bi_ref, lam_ref,
               ubuf, hcarry, a_scr, b_scr):
    ts, c = u.shape
    ubuf[SUBLANES:SUBLANES + ts, :] = u
    ext = ubuf[...]
    uc = cb_ref[...]
    for tap in range(CONV_W):
        back = CONV_W - 1 - tap
        shifted = pltpu.roll(ext, back, 0) if back else ext
        uc = uc + shifted[SUBLANES:SUBLANES + ts, :] * cw_ref[tap:tap + 1, :]
    ubuf[0:SUBLANES, :] = ubuf[ts:ts + SUBLANES, :]

    ucb = uc.astype(BF16)
    r = jax.nn.sigmoid(jnp.dot(ucb, wr_ref[...], preferred_element_type=F32) + br_ref[...])
    ig = jax.nn.sigmoid(jnp.dot(ucb, wi_ref[...], preferred_element_type=F32) + bi_ref[...])
    nl = -lam_ref[...]
    softplus = jnp.maximum(nl, 0.0) + jnp.log1p(jnp.exp(-jnp.abs(nl)))
    log_a = -LRU_C * r * softplus
    a = jnp.exp(log_a)
    mult = jnp.sqrt(1.0 - jnp.exp(2.0 * log_a))
    b = mult * ig * uc

    groups = ts // SUBLANES
    a3 = a.reshape(groups, SUBLANES, c)
    b3 = b.reshape(groups, SUBLANES, c)
    sub = lax.broadcasted_iota(jnp.int32, (groups, SUBLANES, c), 1)
    shift = 1
    while shift < SUBLANES:
        valid = sub >= shift
        a_sh = jnp.where(valid, pltpu.roll(a3, shift, 1), 1.0)
        b_sh = jnp.where(valid, pltpu.roll(b3, shift, 1), 0.0)
        b3 = a3 * b_sh + b3
        a3 = a3 * a_sh
        shift *= 2
    a_scr[...] = a3.reshape(ts, c)
    b_scr[...] = b3.reshape(ts, c)

    h_prev = hcarry[...]
    for g in range(groups):
        rows = slice(g * SUBLANES, (g + 1) * SUBLANES)
        h = a_scr[rows, :] * h_prev + b_scr[rows, :]
        b_scr[rows, :] = h
        h_prev = jnp.broadcast_to(h[SUBLANES - 1:SUBLANES, :], (SUBLANES, c))
    hcarry[...] = h_prev
    return jax.nn.gelu(gate) * b_scr[...]


def _in_proj_kernel(x_ref, pos_ref, g_ref, w_ref, rope_ref,
                    cw_ref, cb_ref, wr_ref, br_ref, wi_ref, bi_ref, lam_ref,
                    q_ref, k_ref, v_ref, rec_ref, ubuf, hcarry, a_scr, b_scr):
    @pl.when(pl.program_id(1) == 0)
    def _():
        ubuf[0:SUBLANES, :] = jnp.zeros((SUBLANES, ubuf.shape[1]), F32)
        hcarry[...] = jnp.zeros_like(hcarry)

    ang = pos_ref[...].astype(F32) * rope_ref[0:1, :]
    cs = jnp.cos(ang)
    sn = jnp.sin(ang)
    s_lo = sn * rope_ref[1:2, :]
    s_hi = sn * rope_ref[2:3, :]

    def rotary(z):
        return (z * cs + pltpu.roll(z, ROT_HALF, 1) * s_lo
                + pltpu.roll(z, LANES - ROT_HALF, 1) * s_hi)

    h = _rms(x_ref[...], g_ref[...]).astype(BF16)
    w = ATTN_WIDTH
    u = jnp.dot(h, w_ref[:, 3 * w:3 * w + LRU_WIDTH], preferred_element_type=F32)
    gate = jnp.dot(h, w_ref[:, 3 * w + LRU_WIDTH:], preferred_element_type=F32)
    rec_ref[...] = _lru_block(u, gate, cw_ref, cb_ref, wr_ref, br_ref, wi_ref, bi_ref, lam_ref,
                              ubuf, hcarry, a_scr, b_scr)
    for sec, out in ((0, q_ref), (1, k_ref)):
        z = jnp.dot(h, w_ref[:, sec * w:(sec + 1) * w], preferred_element_type=F32)
        for cb in range(w // LANES):
            sl = slice(cb * LANES, (cb + 1) * LANES)
            out[:, sl] = rotary(z[:, sl]).astype(BF16)
    v_ref[...] = jnp.dot(h, w_ref[:, 2 * w:3 * w], preferred_element_type=F32).astype(BF16)


def _rope_table():
    lane = jnp.arange(LANES, dtype=jnp.int32) % HEAD_DIM
    inv_freq = ROPE_THETA ** (-jnp.arange(ROT_HALF, dtype=F32) / ROT_HALF)
    freq = jnp.where(lane < ROT_DIM, inv_freq[lane % ROT_HALF], 0.0)
    lo = jnp.where((lane >= ROT_HALF) & (lane < ROT_DIM), 1.0, 0.0)
    hi = jnp.where(lane < ROT_HALF, -1.0, 0.0)
    return jnp.stack([freq, lo, hi]).astype(F32)


def _block_diag(w):
    nb, bw, _ = w.shape
    eye = jnp.eye(nb, dtype=w.dtype)
    return jnp.einsum("gij,gh->gihj", w, eye).reshape(nb * bw, nb * bw)


def _in_proj(x, pos, g, w_in, conv_w, conv_b, w_r, b_r, w_i, b_i, lam, batch, seq, ts):
    t, d = x.shape
    ncol = w_in.shape[1]
    c = LRU_WIDTH
    nblk = seq // ts
    row = lambda b, j: (b * nblk + j, 0)
    const = lambda b, j: (0, 0)
    vec = lambda a: a.reshape(1, c).astype(F32)
    return pl.pallas_call(
        _in_proj_kernel,
        grid=(batch, nblk),
        in_specs=[
            pl.BlockSpec((ts, d), row),
            pl.BlockSpec((ts, 1), row),
            pl.BlockSpec((1, d), const),
            _resident((d, ncol), const),
            pl.BlockSpec((3, LANES), const),
            pl.BlockSpec((CONV_W, c), const),
            pl.BlockSpec((1, c), const),
            pl.BlockSpec((c, c), const),
            pl.BlockSpec((1, c), const),
            pl.BlockSpec((c, c), const),
            pl.BlockSpec((1, c), const),
            pl.BlockSpec((1, c), const),
        ],
        out_specs=[pl.BlockSpec((ts, ATTN_WIDTH), row)] * 3 + [pl.BlockSpec((ts, c), row)],
        out_shape=[jax.ShapeDtypeStruct((t, ATTN_WIDTH), BF16)] * 3
        + [jax.ShapeDtypeStruct((t, c), F32)],
        scratch_shapes=[pltpu.VMEM((ts + SUBLANES, c), F32), pltpu.VMEM((SUBLANES, c), F32),
                        pltpu.VMEM((ts, c), F32), pltpu.VMEM((ts, c), F32)],
        compiler_params=_params(("parallel", "arbitrary")),
        name="in_proj_lru",
    )(x, pos, g, w_in, _rope_table(), conv_w.astype(F32), vec(conv_b),
      _block_diag(w_r).astype(BF16), vec(b_r), _block_diag(w_i).astype(BF16), vec(b_i), vec(lam))


ATTN_TILE = DILATED_PAIRS[-1][1] * Q_BLOCK


def _rows(start, size, stride):
    return pl.ds(start, size) if stride == 1 else pl.ds(start, size, stride=stride)


ATTN_PIPE_WIDTH = 4


def _attn_kernel(q_ref, kc_ref, vc_ref, kp_ref, vp_ref, bias_ref, o_ref,
                 q0_scr, q1_scr, k_scr, v_scr, ob0, ob1, ob2, ls0, ls1, ls2, *slots):
    s_scrs, m_scrs = slots[:len(slots) // 2], slots[len(slots) // 2:]
    j = pl.program_id(2)
    ts = q_ref.shape[0]
    lane = lax.broadcasted_iota(jnp.int32, (1, LANES), 1)
    head0 = lane < HEAD_DIM

    qf = q_ref[...].astype(F32) * ATTN_SCALE
    q0_scr[...] = jnp.where(head0, qf, 0.0)
    q1_scr[...] = jnp.where(head0, 0.0, qf)
    k_scr[0:ts, :] = kp_ref[...].astype(F32)
    k_scr[ts:2 * ts, :] = kc_ref[...].astype(F32)
    v_scr[0:ts, :] = vp_ref[...].astype(F32)
    v_scr[ts:2 * ts, :] = vc_ref[...].astype(F32)

    nt = (((1,), (1,)), ((), ()))
    units = ts // Q_BLOCK
    width = len(s_scrs) // 2
    groups = units // width
    assert groups % 2 == 0 and groups >= 2
    for (_, dil), ob, ls in zip(DILATED_PAIRS, (ob0, ob1, ob2), (ls0, ls1, ls2)):
        shift = dil.bit_length() - 1

        def rows_of(u, dil=dil, shift=shift):
            cls = jnp.bitwise_and(u, dil - 1)
            blk = lax.shift_right_logical(u, shift)
            q_start = cls + dil * Q_BLOCK * blk
            q_rows = _rows(q_start, Q_BLOCK, dil)
            kv_rows = _rows(ts + q_start - dil * Q_BLOCK, 2 * Q_BLOCK, dil)
            return blk, q_rows, kv_rows

        def scores(u, slot):
            blk, q_rows, kv_rows = rows_of(u)
            lhs = jnp.concatenate([q0_scr[q_rows, :], q1_scr[q_rows, :]], axis=0).astype(BF16)
            k2 = k_scr[kv_rows, :].astype(BF16)
            no_prev = jnp.logical_and(j == 0, blk == 0).astype(jnp.int32)
            bias = bias_ref[pl.ds(no_prev * 2 * Q_BLOCK, 2 * Q_BLOCK), :]
            s = lax.dot_general(lhs, k2, nt, preferred_element_type=F32) + bias
            s_scrs[slot][...] = s
            m_scrs[slot][...] = jnp.broadcast_to(jnp.max(s, axis=-1, keepdims=True),
                                                 (2 * Q_BLOCK, LANES))

        def finish(u, slot, ob=ob, ls=ls):
            _, q_rows, kv_rows = rows_of(u)
            m = m_scrs[slot][...]
            p = jnp.exp(s_scrs[slot][...] - jnp.concatenate([m, m], axis=1))
            den = jnp.sum(p, axis=-1, keepdims=True)
            v2 = v_scr[kv_rows, :].astype(BF16)
            o = jnp.dot(p.astype(BF16), v2, preferred_element_type=F32) / den
            lse = m + jnp.log(den)
            ob[q_rows, :] = jnp.where(head0, o[0:Q_BLOCK], o[Q_BLOCK:])
            ls[q_rows, :] = jnp.where(head0, lse[0:Q_BLOCK], lse[Q_BLOCK:])

        def scores_group(g, half):
            for w in range(width):
                scores(g * width + w, half * width + w)

        def finish_group(g, half):
            for w in range(width):
                finish(g * width + w, half * width + w)

        scores_group(jnp.int32(0), 0)

        def pair(i, carry):
            scores_group(2 * i + 1, 1)
            finish_group(2 * i, 0)
            scores_group(2 * i + 2, 0)
            finish_group(2 * i + 1, 1)
            return carry

        lax.fori_loop(0, groups // 2 - 1, pair, 0)
        scores_group(jnp.int32(groups - 1), 1)
        finish_group(jnp.int32(groups - 2), 0)
        finish_group(jnp.int32(groups - 1), 1)

    chunk = 2 * Q_BLOCK

    def mix(c, carry):
        r = pl.ds(pl.multiple_of(c * chunk, chunk), chunk)
        la, lb, lc = ls0[r, :], ls1[r, :], ls2[r, :]
        m = jnp.maximum(jnp.maximum(la, lb), lc)
        ea, eb, ec = jnp.exp(la - m), jnp.exp(lb - m), jnp.exp(lc - m)
        o_ref[r, :] = (ea * ob0[r, :] + eb * ob1[r, :] + ec * ob2[r, :]) / (ea + eb + ec)
        return carry

    lax.fori_loop(0, ts // chunk, mix, 0)


def _attn_bias():
    qi = jnp.arange(2 * Q_BLOCK, dtype=jnp.int32)[:, None] % Q_BLOCK
    kj = jnp.arange(2 * Q_BLOCK, dtype=jnp.int32)[None, :]
    delta = Q_BLOCK + qi - kj
    band = (delta >= 0) & (delta <= Q_BLOCK)
    with_prev = jnp.where(band, 0.0, NEG_INF)
    no_prev = jnp.where(band & (kj >= Q_BLOCK), 0.0, NEG_INF)
    return jnp.concatenate([with_prev, no_prev], axis=0).astype(F32)


def _attention(q, k, v, batch, seq):
    ts = ATTN_TILE
    assert seq % ts == 0
    nt = seq // ts
    cur = pl.BlockSpec((ts, LANES), lambda b, c, j: (b * nt + j, c))
    prev = pl.BlockSpec((ts, LANES), lambda b, c, j: (b * nt + jnp.maximum(j - 1, 0), c))
    tile_scr = pltpu.VMEM((ts, LANES), F32)
    pair_scr = pltpu.VMEM((2 * ts, LANES), F32)
    return pl.pallas_call(
        _attn_kernel,
        grid=(batch, ATTN_WIDTH // LANES, nt),
        in_specs=[cur, cur, cur, prev, prev,
                  pl.BlockSpec((4 * Q_BLOCK, 2 * Q_BLOCK), lambda b, c, j: (0, 0))],
        out_specs=cur,
        out_shape=jax.ShapeDtypeStruct((batch * seq, ATTN_WIDTH), F32),
        scratch_shapes=[tile_scr, tile_scr, pair_scr, pair_scr] + [tile_scr] * 6
        + [pltpu.VMEM((2 * Q_BLOCK, 2 * Q_BLOCK), F32)] * (2 * ATTN_PIPE_WIDTH)
        + [pltpu.VMEM((2 * Q_BLOCK, LANES), F32)] * (2 * ATTN_PIPE_WIDTH),
        compiler_params=_params(("parallel", "parallel", "parallel")),
        name="attention",
    )(q, k, v, k, v, _attn_bias())


def _mix(attn_ref, rec_ref, x_ref, ga_ref, gr_ref, wo_ref, gpost_ref, gpre_ref):
    na = _rms(attn_ref[...], ga_ref[...]).astype(BF16)
    nr = _rms(rec_ref[...], gr_ref[...]).astype(BF16)
    mixed = (jnp.dot(na, wo_ref[0:ATTN_WIDTH, :], preferred_element_type=F32)
             + jnp.dot(nr, wo_ref[ATTN_WIDTH:, :], preferred_element_type=F32))
    x1 = x_ref[...] + _rms(mixed, gpost_ref[...])
    return x1, _rms(x1, gpre_ref[...])


def _mix_specs(tm, d, row, const, w_out_shape):
    aw = ATTN_WIDTH
    return ([pl.BlockSpec((tm, aw), row)] * 2 + [pl.BlockSpec((tm, d), row),
            pl.BlockSpec((1, aw), const), pl.BlockSpec((1, LRU_WIDTH), const),
            _resident(w_out_shape, const), pl.BlockSpec((1, d), const),
            pl.BlockSpec((1, d), const)])


def _mix_args(attn, rec, x, ga, gr, w_out, gpost, gpre):
    d = x.shape[1]
    return [attn, rec, x, ga.reshape(1, ATTN_WIDTH), gr.reshape(1, LRU_WIDTH),
            w_out.astype(BF16), gpost.reshape(1, d), gpre.reshape(1, d)]


def _mix_router_kernel(*refs, n_exp):
    rw_ref, x1_ref, h2_ref, ri_ref, rp_ref = refs[8:]
    x1, hn = _mix(*refs[:8])
    x1_ref[...] = x1
    _store_row_tiles(h2_ref, hn)

    rw = rw_ref[...]
    h_hi, r_hi = hn.astype(BF16), rw.astype(BF16)
    h_lo = (hn - h_hi.astype(F32)).astype(BF16)
    r_lo = (rw - r_hi.astype(F32)).astype(BF16)
    logits = (jnp.dot(h_hi, r_hi, preferred_element_type=F32)
              + jnp.dot(h_hi, r_lo, preferred_element_type=F32)
              + jnp.dot(h_lo, r_hi, preferred_element_type=F32))
    tm = logits.shape[0]
    lane = lax.broadcasted_iota(jnp.int32, (tm, LANES), 1)
    logits = jnp.where(lane < n_exp, logits, -jnp.inf)
    m1 = jnp.max(logits, axis=-1, keepdims=True)
    i1 = jnp.min(jnp.where(logits == m1, lane, n_exp), axis=-1, keepdims=True)
    rest = jnp.where(lane == i1, -jnp.inf, logits)
    m2 = jnp.max(rest, axis=-1, keepdims=True)
    i2 = jnp.min(jnp.where(rest == m2, lane, n_exp), axis=-1, keepdims=True)
    e2 = jnp.exp(m2 - m1)
    p1 = 1.0 / (1.0 + e2)
    p2 = e2 / (1.0 + e2)
    ri_ref[...] = jnp.where(lane == 0, i1, jnp.where(lane == 1, i2, 0))
    rp_ref[...] = jnp.where(lane == 0, p1, jnp.where(lane == 1, p2, 0.0))


def _mix_out_router(attn, rec, x, ga, gr, w_out, gpost, gpre, router_w, tm):
    t, d = x.shape
    assert d == SUBLANES * LANES
    row = lambda i: (i, 0)
    const = lambda i: (0, 0)
    n_exp = router_w.shape[1]
    rw = jnp.pad(router_w.astype(F32), ((0, 0), (0, LANES - n_exp)))
    return pl.pallas_call(
        functools.partial(_mix_router_kernel, n_exp=n_exp),
        grid=(t // tm,),
        in_specs=_mix_specs(tm, d, row, const, w_out.shape) + [pl.BlockSpec(rw.shape, const)],
        out_specs=[pl.BlockSpec((tm, d), row), pl.BlockSpec((tm * SUBLANES, LANES), row)]
        + [pl.BlockSpec((tm, LANES), row)] * 2,
        out_shape=[jax.ShapeDtypeStruct((t, d), F32),
                   jax.ShapeDtypeStruct((t * SUBLANES, LANES), F32)]
        + [jax.ShapeDtypeStruct((t, LANES), jnp.int32), jax.ShapeDtypeStruct((t, LANES), F32)],
        compiler_params=_params(("parallel",)),
        name="mix_out_router",
    )(*_mix_args(attn, rec, x, ga, gr, w_out, gpost, gpre), rw)


def _swiglu_into(o_ref, x, wg_ref, wu_ref, wd_ref, f_chunk):
    d_ff = wg_ref.shape[1]
    for c in range(d_ff // f_chunk):
        sl = slice(c * f_chunk, (c + 1) * f_chunk)
        g = jnp.dot(x, wg_ref[:, sl], preferred_element_type=F32)
        u = jnp.dot(x, wu_ref[:, sl], preferred_element_type=F32)
        a = (g * jax.nn.sigmoid(g) * u).astype(BF16)
        y = jnp.dot(a, wd_ref[sl, :], preferred_element_type=F32)
        if c == 0:
            o_ref[...] = y
        else:
            o_ref[...] += y


def _ffn_kernel(te_ref, nv_ref, x_ref, wg_ref, wu_ref, wd_ref, o_ref, acc, *, f_chunk):
    i = pl.program_id(0)

    @pl.when(i < nv_ref[0])
    def _():
        _swiglu_into(acc, _load_row_tiles(x_ref).astype(BF16), wg_ref, wu_ref, wd_ref, f_chunk)
        _store_row_tiles(o_ref, acc[...])

    @pl.when(i >= nv_ref[0])
    def _():
        o_ref[...] = jnp.zeros_like(o_ref)


def _ffn(xs, tile_expert, n_valid, wg, wu, wd, tm, f_chunk):
    d, d_ff = wg.shape[1], wg.shape[2]
    rows = xs.shape[0] // SUBLANES
    assert d == SUBLANES * LANES and d_ff % f_chunk == 0 and rows % tm == 0
    wmap = lambda i, te, nv: (te[i], 0, 0)
    grid_spec = pltpu.PrefetchScalarGridSpec(
        num_scalar_prefetch=2,
        grid=(rows // tm,),
        in_specs=[
            pl.BlockSpec((tm * SUBLANES, LANES), lambda i, te, nv: (jnp.minimum(i, nv[0] - 1), 0)),
            _resident((None, d, d_ff), wmap),
            _resident((None, d, d_ff), wmap),
            _resident((None, d_ff, d), wmap),
        ],
        out_specs=pl.BlockSpec((tm * SUBLANES, LANES), lambda i, te, nv: (i, 0)),
        scratch_shapes=[pltpu.VMEM((tm, d), F32)],
    )
    return pl.pallas_call(
        functools.partial(_ffn_kernel, f_chunk=f_chunk),
        grid_spec=grid_spec,
        out_shape=jax.ShapeDtypeStruct(xs.shape, F32),
        compiler_params=_params(("arbitrary",)),
        name="ffn",
    )(tile_expert, n_valid, xs, wg, wu, wd)


def _mix_ffn_kernel(*refs, f_chunk):
    wg_ref, wu_ref, wd_ref, gffn_ref, o_ref, h2_scr, x1_scr = refs[8:]
    s = pl.program_id(0)
    cur = lax.rem(s, 2)
    prev = 1 - cur

    @pl.when(s == 0)
    def _():
        h2_scr[1] = jnp.zeros(h2_scr.shape[1:], h2_scr.dtype)
        x1_scr[1] = jnp.zeros(x1_scr.shape[1:], x1_scr.dtype)

    x1, hn = _mix(*refs[:8])
    x1_scr[cur] = x1
    h2_scr[cur] = hn.astype(BF16)

    _swiglu_into(o_ref, h2_scr[prev], wg_ref, wu_ref, wd_ref, f_chunk)
    o_ref[...] = x1_scr[prev] + _rms(o_ref[...], gffn_ref[...])


def _mix_ffn(attn, rec, x, ga, gr, w_out, gpost, gpre, wg, wu, wd, gffn, tm, f_chunk):
    t, d = x.shape
    d_ff = wg.shape[1]
    n = t // tm
    assert d_ff % f_chunk == 0 and t % tm == 0
    row = lambda s: (jnp.minimum(s, n - 1), 0)
    const = lambda s: (0, 0)
    return pl.pallas_call(
        functools.partial(_mix_ffn_kernel, f_chunk=f_chunk),
        grid=(n + 1,),
        in_specs=_mix_specs(tm, d, row, const, w_out.shape)
        + [_resident((d, d_ff), const), _resident((d, d_ff), const), _resident((d_ff, d), const),
           pl.BlockSpec((1, d), const)],
        out_specs=pl.BlockSpec((tm, d), lambda s: (jnp.maximum(s - 1, 0), 0)),
        out_shape=jax.ShapeDtypeStruct((t, d), F32),
        scratch_shapes=[pltpu.VMEM((2, tm, d), BF16), pltpu.VMEM((2, tm, d), F32)],
        compiler_params=_params(("arbitrary",)),
        name="mix_ffn_dense",
    )(*_mix_args(attn, rec, x, ga, gr, w_out, gpost, gpre),
      wg.astype(BF16), wu.astype(BF16), wd.astype(BF16), gffn.reshape(1, d))


def _dispatch_kernel(pad_end_ref, d1_ref, d2_ref, h_ref, out_ref, zbuf, sem, zsem):
    tile = zbuf.shape[0]

    @pl.when(pl.program_id(0) == 0)
    def _():
        zbuf[...] = jnp.zeros_like(zbuf)

        def zero_copy(e):
            start = pl.multiple_of(pad_end_ref[e] * SUBLANES - tile, tile)
            return pltpu.make_async_copy(zbuf, out_ref.at[pl.ds(start, tile)], zsem)

        def has_rows(e):
            prev_end = pad_end_ref[e - 1] if e else 0
            return pad_end_ref[e] > prev_end

        for e in range(pad_end_ref.shape[0]):
            @pl.when(has_rows(e))
            def _():
                zero_copy(e).start()
        for e in range(pad_end_ref.shape[0]):
            @pl.when(has_rows(e))
            def _():
                zero_copy(e).wait()

    def row_copies(j):
        src = h_ref.at[_tile_rows(j)]
        return (pltpu.make_async_copy(src, out_ref.at[_tile_rows(d1_ref[j])], sem),
                pltpu.make_async_copy(src, out_ref.at[_tile_rows(d2_ref[j])], sem))

    def start(j, carry):
        for queue, c in enumerate(row_copies(j)):
            c.start(priority=queue)
        return carry

    def wait(j, carry):
        for c in row_copies(j):
            c.wait()
        return carry

    n = d1_ref.shape[0]
    lax.fori_loop(0, n, start, 0, unroll=8)
    lax.fori_loop(0, n, wait, 0, unroll=8)


def _dispatch(h, dest, pad_end, n_rows, tm, tile):
    t = h.shape[0] // SUBLANES
    assert tm % SMEM_BLOCK_WORDS == 0 and t % tm == 0
    nblk = t // tm
    grid_spec = pltpu.PrefetchScalarGridSpec(
        num_scalar_prefetch=1,
        grid=(nblk,),
        in_specs=[pl.BlockSpec((tm,), lambda i, pe: (i,), memory_space=pltpu.SMEM),
                  pl.BlockSpec((tm,), lambda i, pe: (i + nblk,), memory_space=pltpu.SMEM),
                  pl.BlockSpec((tm * SUBLANES, LANES), lambda i, pe: (i, 0))],
        out_specs=pl.BlockSpec(memory_space=pl.ANY),
        scratch_shapes=[pltpu.VMEM((tile * SUBLANES, LANES), h.dtype),
                        pltpu.SemaphoreType.DMA(()), pltpu.SemaphoreType.DMA(())],
    )
    return pl.pallas_call(
        _dispatch_kernel,
        grid_spec=grid_spec,
        out_shape=jax.ShapeDtypeStruct((n_rows * SUBLANES, LANES), h.dtype),
        compiler_params=_params(("arbitrary",)),
        name="moe_dispatch",
    )(pad_end, dest, dest, h)


def _combine_kernel(d1_ref, d2_ref, ys_ref, rp_ref, x_ref, g_ref, o_ref, ya, yb, sem):
    def row_copies(j):
        dst = _tile_rows(j)
        return (pltpu.make_async_copy(ys_ref.at[_tile_rows(d1_ref[j])], ya.at[dst], sem),
                pltpu.make_async_copy(ys_ref.at[_tile_rows(d2_ref[j])], yb.at[dst], sem))

    def start(j, carry):
        for queue, c in enumerate(row_copies(j)):
            c.start(priority=queue)
        return carry

    def wait(j, carry):
        for c in row_copies(j):
            c.wait()
        return carry

    n = x_ref.shape[0]
    lax.fori_loop(0, n, start, 0, unroll=8)
    lax.fori_loop(0, n, wait, 0, unroll=8)
    y = rp_ref[:, 0:1] * _load_row_tiles(ya) + rp_ref[:, 1:2] * _load_row_tiles(yb)
    o_ref[...] = x_ref[...] + _rms(y, g_ref[...])


def _combine(ys, dest, route_p, x, g, tm):
    t, d = x.shape
    assert d == SUBLANES * LANES and tm % SMEM_BLOCK_WORDS == 0 and t % tm == 0
    nblk = t // tm
    row = lambda i: (i, 0)
    return pl.pallas_call(
        _combine_kernel,
        grid=(nblk,),
        in_specs=[pl.BlockSpec((tm,), lambda i: (i,), memory_space=pltpu.SMEM),
                  pl.BlockSpec((tm,), lambda i: (i + nblk,), memory_space=pltpu.SMEM),
                  pl.BlockSpec(memory_space=pl.ANY),
                  pl.BlockSpec((tm, LANES), row), pl.BlockSpec((tm, d), row),
                  pl.BlockSpec((1, d), lambda i: (0, 0))],
        out_specs=pl.BlockSpec((tm, d), row),
        out_shape=jax.ShapeDtypeStruct((t, d), F32),
        scratch_shapes=[pltpu.VMEM((tm * SUBLANES, LANES), F32),
                        pltpu.VMEM((tm * SUBLANES, LANES), F32), pltpu.SemaphoreType.DMA(())],
        compiler_params=_params(("arbitrary",)),
        name="moe_combine",
    )(dest, dest, ys, route_p, x, g.reshape(1, d))


def _routing_tables(route_i, n_exp, tm):
    t = route_i.shape[0]
    e_flat = jnp.concatenate([route_i[:, 0], route_i[:, 1]])
    onehot = (e_flat[:, None] == jnp.arange(n_exp, dtype=jnp.int32)[None, :]).astype(jnp.int32)
    counts = jnp.sum(onehot, axis=0)
    padded = ((counts + tm - 1) // tm) * tm
    pad_end = jnp.cumsum(padded)
    pad_start = pad_end - padded
    dest = jnp.sum((jnp.cumsum(onehot, axis=0) - onehot + pad_start[None, :]) * onehot, axis=1)

    n_rows = TOP_K * t + n_exp * tm
    n_tiles = n_rows // tm
    tile_first_row = jnp.arange(n_tiles, dtype=jnp.int32) * tm
    n_valid = pad_end[-1] // tm
    tile_first_row = jnp.minimum(tile_first_row, (n_valid - 1) * tm)
    tile_expert = jnp.sum((pad_end[None, :] <= tile_first_row[:, None]).astype(jnp.int32), axis=1)
    return (dest.astype(jnp.int32), pad_end.astype(jnp.int32),
            jnp.minimum(tile_expert, n_exp - 1).astype(jnp.int32),
            n_valid.astype(jnp.int32).reshape(1), n_rows)


def _tile(n, want):
    t = min(n, want)
    while n % t:
        t -= SUBLANES
    return t


def kernel(x, positions, pre_mix_g, w_in, conv_w, conv_b, w_rgate, b_rgate, w_igate, b_igate,
           lru_lambda, attn_out_g, lru_out_g, w_out, post_mix_g, pre_ffn_g, post_ffn_g,
           dense_w_gate, dense_w_up, dense_w_down, router_w, moe_w_gate, moe_w_up, moe_w_down):
    batch, seq, d = x.shape
    depth = w_in.shape[0]
    t = batch * seq
    d_ff = dense_w_gate.shape[-1]
    n_exp = moe_w_gate.shape[1]
    assert w_in.shape[2] == 3 * ATTN_WIDTH + 2 * LRU_WIDTH
    assert seq % (DILATED_PAIRS[-1][1] * Q_BLOCK) == 0

    tm_mix = _tile(t, 256)
    tm_ffn = _tile(t, 512)
    tm_moe = tm_ffn
    ts_proj = _tile(seq, 512)
    tm_rows = SMEM_BLOCK_WORDS
    f_chunk = 512 if d_ff % 512 == 0 else d_ff

    xf = x.reshape(t, d).astype(F32)
    pos = positions.reshape(t, 1).astype(jnp.int32)

    for l in range(depth):
        q, k, v, rec = _in_proj(xf, pos, pre_mix_g[l].reshape(1, d), w_in[l].astype(BF16),
                                conv_w[l], conv_b[l], w_rgate[l], b_rgate[l], w_igate[l],
                                b_igate[l], lru_lambda[l], batch, seq, ts_proj)
        attn = _attention(q, k, v, batch, seq)
        j = l // 2
        if l % 2 == 0:
            xf = _mix_ffn(attn, rec, xf, attn_out_g[l], lru_out_g[l], w_out[l], post_mix_g[l],
                          pre_ffn_g[l], dense_w_gate[j], dense_w_up[j], dense_w_down[j],
                          post_ffn_g[l], tm_ffn, f_chunk)
        else:
            x1, h2, route_i, route_p = _mix_out_router(
                attn, rec, xf, attn_out_g[l], lru_out_g[l], w_out[l], post_mix_g[l],
                pre_ffn_g[l], router_w[j], tm_mix)
            dest, pad_end, tile_expert, n_valid, n_rows = _routing_tables(route_i, n_exp, tm_moe)
            xs = _dispatch(h2, dest, pad_end, n_rows, tm_rows, tm_moe)
            ys = _ffn(xs, tile_expert, n_valid, moe_w_gate[j].astype(BF16),
                      moe_w_up[j].astype(BF16), moe_w_down[j].astype(BF16), tm_moe, f_chunk)
            xf = _combine(ys, dest, route_p, x1, post_ffn_g[l], tm_rows)
    return xf.reshape(batch, seq, d).astype(x.dtype)
```

```python
import functools
import math

import jax
import jax.numpy as jnp
from jax import lax
from jax.experimental import pallas as pl
from jax.experimental.pallas import tpu as pltpu

ATTN_HEADS = 8
HEAD_DIM = 64
ATTN_WIDTH = ATTN_HEADS * HEAD_DIM
LRU_BLOCKS = 8
LRU_BLOCK_W = 64
LRU_WIDTH = LRU_BLOCKS * LRU_BLOCK_W
DILATED_PAIRS = ((128, 1), (512, 4), (2048, 16))
Q_BLOCK = 128
ROT_DIM = HEAD_DIM // 4
ROT_HALF = ROT_DIM // 2
ROPE_THETA = 500000.0
ATTN_SCALE = 1.0 / math.sqrt(HEAD_DIM)
NEG_INF = -1e30
CONV_W = 4
LRU_C = 8.0
TOP_K = 2
RMS_EPS = 1e-6

LANES = 128
SUBLANES = 8
SMEM_BLOCK_WORDS = 1024
VMEM_LIMIT_BYTES = 56 * 1024 * 1024

F32 = jnp.float32
BF16 = jnp.bfloat16


def _params(semantics):
    return pltpu.CompilerParams(dimension_semantics=semantics,
                                vmem_limit_bytes=VMEM_LIMIT_BYTES)


def _resident(block_shape, index_map):
    return pl.BlockSpec(block_shape, index_map, pipeline_mode=pl.Buffered(1))


def _rms(x, g):
    var = jnp.mean(x * x, axis=-1, keepdims=True)
    return x * lax.rsqrt(var + RMS_EPS) * g


def _tile_rows(r):
    return pl.ds(pl.multiple_of(r * SUBLANES, SUBLANES), SUBLANES)


def _store_row_tiles(ref, x):
    rows = x.shape[0]
    for c in range(x.shape[1] // LANES):
        ref[pl.ds(c, rows, stride=SUBLANES), :] = x[:, c * LANES:(c + 1) * LANES]


def _load_row_tiles(ref):
    rows = ref.shape[0] // SUBLANES
    return jnp.concatenate([ref[pl.ds(c, rows, stride=SUBLANES), :] for c in range(SUBLANES)],
                           axis=1)


def _lru_block(u, gate, cw_ref, cb_ref, wr_ref, br_ref, wi_ref, bi_ref, lam_ref,
               ubuf, hcarry, a_scr, b_scr):
    ts, c = u.shape
    ubuf[SUBLANES:SUBLANES + ts, :] = u
    ext = ubuf[...]
    uc = cb_ref[...]
    for tap in range(CONV_W):
        back = CONV_W - 1 - tap
        shifted = pltpu.roll(ext, back, 0) if back else ext
        uc = uc + shifted[SUBLANES:SUBLANES + ts, :] * cw_ref[tap:tap + 1, :]
    ubuf[0:SUBLANES, :] = ubuf[ts:ts + SUBLANES, :]

    ucb = uc.astype(BF16)
    r = jax.nn.sigmoid(jnp.dot(ucb, wr_ref[...], preferred_element_type=F32) + br_ref[...])
    ig = jax.nn.sigmoid(jnp.dot(ucb, wi_ref[...], preferred_element_type=F32) + bi_ref[...])
    nl = -lam_ref[...]
    softplus = jnp.maximum(nl, 0.0) + jnp.log1p(jnp.exp(-jnp.abs(nl)))
    log_a = -LRU_C * r * softplus
    a = jnp.exp(log_a)
    mult = jnp.sqrt(1.0 - jnp.exp(2.0 * log_a))
    b = mult * ig * uc

    groups = ts // SUBLANES
    a3 = a.reshape(groups, SUBLANES, c)
    b3 = b.reshape(groups, SUBLANES, c)
    sub = lax.broadcasted_iota(jnp.int32, (groups, SUBLANES, c), 1)
    shift = 1
    while shift < SUBLANES:
        valid = sub >= shift
        a_sh = jnp.where(valid, pltpu.roll(a3, shift, 1), 1.0)
        b_sh = jnp.where(valid, pltpu.roll(b3, shift, 1), 0.0)
        b3 = a3 * b_sh + b3
        a3 = a3 * a_sh
        shift *= 2
    a_scr[...] = a3.reshape(ts, c)
    b_scr[...] = b3.reshape(ts, c)

    h_prev = hcarry[...]
    for g in range(groups):
        rows = slice(g * SUBLANES, (g + 1) * SUBLANES)
        h = a_scr[rows, :] * h_prev + b_scr[rows, :]
        b_scr[rows, :] = h
        h_prev = jnp.broadcast_to(h[SUBLANES - 1:SUBLANES, :], (SUBLANES, c))
    hcarry[...] = h_prev
    return jax.nn.gelu(gate) * b_scr[...]


def _in_proj_kernel(x_ref, pos_ref, g_ref, w_ref, rope_ref,
                    cw_ref, cb_ref, wr_ref, br_ref, wi_ref, bi_ref, lam_ref,
                    q_ref, k_ref, v_ref, rec_ref, ubuf, hcarry, a_scr, b_scr):
    @pl.when(pl.program_id(1) == 0)
    def _():
        ubuf[0:SUBLANES, :] = jnp.zeros((SUBLANES, ubuf.shape[1]), F32)
        hcarry[...] = jnp.zeros_like(hcarry)

    ang = pos_ref[...].astype(F32) * rope_ref[0:1, :]
    cs = jnp.cos(ang)
    sn = jnp.sin(ang)
    s_lo = sn * rope_ref[1:2, :]
    s_hi = sn * rope_ref[2:3, :]

    def rotary(z):
        return (z * cs + pltpu.roll(z, ROT_HALF, 1) * s_lo
                + pltpu.roll(z, LANES - ROT_HALF, 1) * s_hi)

    h = _rms(x_ref[...], g_ref[...]).astype(BF16)
    w = ATTN_WIDTH
    u = jnp.dot(h, w_ref[:, 3 * w:3 * w + LRU_WIDTH], preferred_element_type=F32)
    gate = jnp.dot(h, w_ref[:, 3 * w + LRU_WIDTH:], preferred_element_type=F32)
    rec_ref[...] = _lru_block(u, gate, cw_ref, cb_ref, wr_ref, br_ref, wi_ref, bi_ref, lam_ref,
                              ubuf, hcarry, a_scr, b_scr)
    for sec, out in ((0, q_ref), (1, k_ref)):
        z = jnp.dot(h, w_ref[:, sec * w:(sec + 1) * w], preferred_element_type=F32)
        for cb in range(w // LANES):
            sl = slice(cb * LANES, (cb + 1) * LANES)
            out[:, sl] = rotary(z[:, sl]).astype(BF16)
    v_ref[...] = jnp.dot(h, w_ref[:, 2 * w:3 * w], preferred_element_type=F32).astype(BF16)


def _rope_table():
    lane = jnp.arange(LANES, dtype=jnp.int32) % HEAD_DIM
    inv_freq = ROPE_THETA ** (-jnp.arange(ROT_HALF, dtype=F32) / ROT_HALF)
    freq = jnp.where(lane < ROT_DIM, inv_freq[lane % ROT_HALF], 0.0)
    lo = jnp.where((lane >= ROT_HALF) & (lane < ROT_DIM), 1.0, 0.0)
    hi = jnp.where(lane < ROT_HALF, -1.0, 0.0)
    return jnp.stack([freq, lo, hi]).astype(F32)


def _block_diag(w):
    nb, bw, _ = w.shape
    eye = jnp.eye(nb, dtype=w.dtype)
    return jnp.einsum("gij,gh->gihj", w, eye).reshape(nb * bw, nb * bw)


def _in_proj(x, pos, g, w_in, conv_w, conv_b, w_r, b_r, w_i, b_i, lam, batch, seq, ts):
    t, d = x.shape
    ncol = w_in.shape[1]
    c = LRU_WIDTH
    nblk = seq // ts
    row = lambda b, j: (b * nblk + j, 0)
    const = lambda b, j: (0, 0)
    vec = lambda a: a.reshape(1, c).astype(F32)
    return pl.pallas_call(
        _in_proj_kernel,
        grid=(batch, nblk),
        in_specs=[
            pl.BlockSpec((ts, d), row),
            pl.BlockSpec((ts, 1), row),
            pl.BlockSpec((1, d), const),
            _resident((d, ncol), const),
            pl.BlockSpec((3, LANES), const),
            pl.BlockSpec((CONV_W, c), const),
            pl.BlockSpec((1, c), const),
            pl.BlockSpec((c, c), const),
            pl.BlockSpec((1, c), const),
            pl.BlockSpec((c, c), const),
            pl.BlockSpec((1, c), const),
            pl.BlockSpec((1, c), const),
        ],
        out_specs=[pl.BlockSpec((ts, ATTN_WIDTH), row)] * 3 + [pl.BlockSpec((ts, c), row)],
        out_shape=[jax.ShapeDtypeStruct((t, ATTN_WIDTH), BF16)] * 3
        + [jax.ShapeDtypeStruct((t, c), F32)],
        scratch_shapes=[pltpu.VMEM((ts + SUBLANES, c), F32), pltpu.VMEM((SUBLANES, c), F32),
                        pltpu.VMEM((ts, c), F32), pltpu.VMEM((ts, c), F32)],
        compiler_params=_params(("parallel", "arbitrary")),
        name="in_proj_lru",
    )(x, pos, g, w_in, _rope_table(), conv_w.astype(F32), vec(conv_b),
      _block_diag(w_r).astype(BF16), vec(b_r), _block_diag(w_i).astype(BF16), vec(b_i), vec(lam))


ATTN_TILE = DILATED_PAIRS[-1][1] * Q_BLOCK


def _rows(start, size, stride):
    return pl.ds(start, size) if stride == 1 else pl.ds(start, size, stride=stride)


ATTN_PIPE_WIDTH = 4


def _attn_kernel(q_ref, kc_ref, vc_ref, kp_ref, vp_ref, bias_ref, o_ref,
                 q0_scr, q1_scr, k_scr, v_scr, ob0, ob1, ob2, ls0, ls1, ls2, *slots):
    s_scrs, m_scrs = slots[:len(slots) // 2], slots[len(slots) // 2:]
    j = pl.program_id(2)
    ts = q_ref.shape[0]
    lane = lax.broadcasted_iota(jnp.int32, (1, LANES), 1)
    head0 = lane < HEAD_DIM

    qf = q_ref[...].astype(F32) * ATTN_SCALE
    q0_scr[...] = jnp.where(head0, qf, 0.0)
    q1_scr[...] = jnp.where(head0, 0.0, qf)
    k_scr[0:ts, :] = kp_ref[...].astype(F32)
    k_scr[ts:2 * ts, :] = kc_ref[...].astype(F32)
    v_scr[0:ts, :] = vp_ref[...].astype(F32)
    v_scr[ts:2 * ts, :] = vc_ref[...].astype(F32)

    nt = (((1,), (1,)), ((), ()))
    units = ts // Q_BLOCK
    width = len(s_scrs) // 2
    groups = units // width
    assert groups % 2 == 0 and groups >= 2
    for (_, dil), ob, ls in zip(DILATED_PAIRS, (ob0, ob1, ob2), (ls0, ls1, ls2)):
        shift = dil.bit_length() - 1

        def rows_of(u, dil=dil, shift=shift):
            cls = jnp.bitwise_and(u, dil - 1)
            blk = lax.shift_right_logical(u, shift)
            q_start = cls + dil * Q_BLOCK * blk
            q_rows = _rows(q_start, Q_BLOCK, dil)
            kv_rows = _rows(ts + q_start - dil * Q_BLOCK, 2 * Q_BLOCK, dil)
            return blk, q_rows, kv_rows

        def scores(u, slot):
            blk, q_rows, kv_rows = rows_of(u)
            lhs = jnp.concatenate([q0_scr[q_rows, :], q1_scr[q_rows, :]], axis=0).astype(BF16)
            k2 = k_scr[kv_rows, :].astype(BF16)
            no_prev = jnp.logical_and(j == 0, blk == 0).astype(jnp.int32)
            bias = bias_ref[pl.ds(no_prev * 2 * Q_BLOCK, 2 * Q_BLOCK), :]
            s = lax.dot_general(lhs, k2, nt, preferred_element_type=F32) + bias
            s_scrs[slot][...] = s
            m_scrs[slot][...] = jnp.broadcast_to(jnp.max(s, axis=-1, keepdims=True),
                                                 (2 * Q_BLOCK, LANES))

        def finish(u, slot, ob=ob, ls=ls):
            _, q_rows, kv_rows = rows_of(u)
            m = m_scrs[slot][...]
            p = jnp.exp(s_scrs[slot][...] - jnp.concatenate([m, m], axis=1))
            den = jnp.sum(p, axis=-1, keepdims=True)
            v2 = v_scr[kv_rows, :].astype(BF16)
            o = jnp.dot(p.astype(BF16), v2, preferred_element_type=F32) / den
            lse = m + jnp.log(den)
            ob[q_rows, :] = jnp.where(head0, o[0:Q_BLOCK], o[Q_BLOCK:])
            ls[q_rows, :] = jnp.where(head0, lse[0:Q_BLOCK], lse[Q_BLOCK:])

        def scores_group(g, half):
            for w in range(width):
                scores(g * width + w, half * width + w)

        def finish_group(g, half):
            for w in range(width):
                finish(g * width + w, half * width + w)

        scores_group(jnp.int32(0), 0)

        def pair(i, carry):
            scores_group(2 * i + 1, 1)
            finish_group(2 * i, 0)
            scores_group(2 * i + 2, 0)
            finish_group(2 * i + 1, 1)
            return carry

        lax.fori_loop(0, groups // 2 - 1, pair, 0)
        scores_group(jnp.int32(groups - 1), 1)
        finish_group(jnp.int32(groups - 2), 0)
        finish_group(jnp.int32(groups - 1), 1)

    chunk = 2 * Q_BLOCK

    def mix(c, carry):
        r = pl.ds(pl.multiple_of(c * chunk, chunk), chunk)
        la, lb, lc = ls0[r, :], ls1[r, :], ls2[r, :]
        m = jnp.maximum(jnp.maximum(la, lb), lc)
        ea, eb, ec = jnp.exp(la - m), jnp.exp(lb - m), jnp.exp(lc - m)
        o_ref[r, :] = (ea * ob0[r, :] + eb * ob1[r, :] + ec * ob2[r, :]) / (ea + eb + ec)
        return carry

    lax.fori_loop(0, ts // chunk, mix, 0)


def _attn_bias():
    qi = jnp.arange(2 * Q_BLOCK, dtype=jnp.int32)[:, None] % Q_BLOCK
    kj = jnp.arange(2 * Q_BLOCK, dtype=jnp.int32)[None, :]
    delta = Q_BLOCK + qi - kj
    band = (delta >= 0) & (delta <= Q_BLOCK)
    with_prev = jnp.where(band, 0.0, NEG_INF)
    no_prev = jnp.where(band & (kj >= Q_BLOCK), 0.0, NEG_INF)
    return jnp.concatenate([with_prev, no_prev], axis=0).astype(F32)


def _attention(q, k, v, batch, seq):
    ts = ATTN_TILE
    assert seq % ts == 0
    nt = seq // ts
    cur = pl.BlockSpec((ts, LANES), lambda b, c, j: (b * nt + j, c))
    prev = pl.BlockSpec((ts, LANES), lambda b, c, j: (b * nt + jnp.maximum(j - 1, 0), c))
    tile_scr = pltpu.VMEM((ts, LANES), F32)
    pair_scr = pltpu.VMEM((2 * ts, LANES), F32)
    return pl.pallas_call(
        _attn_kernel,
        grid=(batch, ATTN_WIDTH // LANES, nt),
        in_specs=[cur, cur, cur, prev, prev,
                  pl.BlockSpec((4 * Q_BLOCK, 2 * Q_BLOCK), lambda b, c, j: (0, 0))],
        out_specs=cur,
        out_shape=jax.ShapeDtypeStruct((batch * seq, ATTN_WIDTH), F32),
        scratch_shapes=[tile_scr, tile_scr, pair_scr, pair_scr] + [tile_scr] * 6
        + [pltpu.VMEM((2 * Q_BLOCK, 2 * Q_BLOCK), F32)] * (2 * ATTN_PIPE_WIDTH)
        + [pltpu.VMEM((2 * Q_BLOCK, LANES), F32)] * (2 * ATTN_PIPE_WIDTH),
        compiler_params=_params(("parallel", "parallel", "parallel")),
        name="attention",
    )(q, k, v, k, v, _attn_bias())


def _mix(attn_ref, rec_ref, x_ref, ga_ref, gr_ref, wo_ref, gpost_ref, gpre_ref):
    na = _rms(attn_ref[...], ga_ref[...]).astype(BF16)
    nr = _rms(rec_ref[...], gr_ref[...]).astype(BF16)
    mixed = (jnp.dot(na, wo_ref[0:ATTN_WIDTH, :], preferred_element_type=F32)
             + jnp.dot(nr, wo_ref[ATTN_WIDTH:, :], preferred_element_type=F32))
    x1 = x_ref[...] + _rms(mixed, gpost_ref[...])
    return x1, _rms(x1, gpre_ref[...])


def _mix_specs(tm, d, row, const, w_out_shape):
    aw = ATTN_WIDTH
    return ([pl.BlockSpec((tm, aw), row)] * 2 + [pl.BlockSpec((tm, d), row),
            pl.BlockSpec((1, aw), const), pl.BlockSpec((1, LRU_WIDTH), const),
            _resident(w_out_shape, const), pl.BlockSpec((1, d), const),
            pl.BlockSpec((1, d), const)])


def _mix_args(attn, rec, x, ga, gr, w_out, gpost, gpre):
    d = x.shape[1]
    return [attn, rec, x, ga.reshape(1, ATTN_WIDTH), gr.reshape(1, LRU_WIDTH),
            w_out.astype(BF16), gpost.reshape(1, d), gpre.reshape(1, d)]


def _mix_router_kernel(*refs, n_exp):
    rw_ref, x1_ref, h2_ref, ri_ref, rp_ref = refs[8:]
    x1, hn = _mix(*refs[:8])
    x1_ref[...] = x1
    _store_row_tiles(h2_ref, hn)

    rw = rw_ref[...]
    h_hi, r_hi = hn.astype(BF16), rw.astype(BF16)
    h_lo = (hn - h_hi.astype(F32)).astype(BF16)
    r_lo = (rw - r_hi.astype(F32)).astype(BF16)
    logits = (jnp.dot(h_hi, r_hi, preferred_element_type=F32)
              + jnp.dot(h_hi, r_lo, preferred_element_type=F32)
              + jnp.dot(h_lo, r_hi, preferred_element_type=F32))
    tm = logits.shape[0]
    lane = lax.broadcasted_iota(jnp.int32, (tm, LANES), 1)
    logits = jnp.where(lane < n_exp, logits, -jnp.inf)
    m1 = jnp.max(logits, axis=-1, keepdims=True)
    i1 = jnp.min(jnp.where(logits == m1, lane, n_exp), axis=-1, keepdims=True)
    rest = jnp.where(lane == i1, -jnp.inf, logits)
    m2 = jnp.max(rest, axis=-1, keepdims=True)
    i2 = jnp.min(jnp.where(rest == m2, lane, n_exp), axis=-1, keepdims=True)
    e2 = jnp.exp(m2 - m1)
    p1 = 1.0 / (1.0 + e2)
    p2 = e2 / (1.0 + e2)
    ri_ref[...] = jnp.where(lane == 0, i1, jnp.where(lane == 1, i2, 0))
    rp_ref[...] = jnp.where(lane == 0, p1, jnp.where(lane == 1, p2, 0.0))


def _mix_out_router(attn, rec, x, ga, gr, w_out, gpost, gpre, router_w, tm):
    t, d = x.shape
    assert d == SUBLANES * LANES
    row = lambda i: (i, 0)
    const = lambda i: (0, 0)
    n_exp = router_w.shape[1]
    rw = jnp.pad(router_w.astype(F32), ((0, 0), (0, LANES - n_exp)))
    return pl.pallas_call(
        functools.partial(_mix_router_kernel, n_exp=n_exp),
        grid=(t // tm,),
        in_specs=_mix_specs(tm, d, row, const, w_out.shape) + [pl.BlockSpec(rw.shape, const)],
        out_specs=[pl.BlockSpec((tm, d), row), pl.BlockSpec((tm * SUBLANES, LANES), row)]
        + [pl.BlockSpec((tm, LANES), row)] * 2,
        out_shape=[jax.ShapeDtypeStruct((t, d), F32),
                   jax.ShapeDtypeStruct((t * SUBLANES, LANES), F32)]
        + [jax.ShapeDtypeStruct((t, LANES), jnp.int32), jax.ShapeDtypeStruct((t, LANES), F32)],
        compiler_params=_params(("parallel",)),
        name="mix_out_router",
    )(*_mix_args(attn, rec, x, ga, gr, w_out, gpost, gpre), rw)


def _swiglu_into(o_ref, x, wg_ref, wu_ref, wd_ref, f_chunk):
    d_ff = wg_ref.shape[1]
    for c in range(d_ff // f_chunk):
        sl = slice(c * f_chunk, (c + 1) * f_chunk)
        g = jnp.dot(x, wg_ref[:, sl], preferred_element_type=F32)
        u = jnp.dot(x, wu_ref[:, sl], preferred_element_type=F32)
        a = (g * jax.nn.sigmoid(g) * u).astype(BF16)
        y = jnp.dot(a, wd_ref[sl, :], preferred_element_type=F32)
        if c == 0:
            o_ref[...] = y
        else:
            o_ref[...] += y


def _ffn_kernel(te_ref, nv_ref, x_ref, wg_ref, wu_ref, wd_ref, o_ref, acc, *, f_chunk):
    i = pl.program_id(0)

    @pl.when(i < nv_ref[0])
    def _():
        _swiglu_into(acc, _load_row_tiles(x_ref).astype(BF16), wg_ref, wu_ref, wd_ref, f_chunk)
        _store_row_tiles(o_ref, acc[...])

    @pl.when(i >= nv_ref[0])
    def _():
        o_ref[...] = jnp.zeros_like(o_ref)


def _ffn(xs, tile_expert, n_valid, wg, wu, wd, tm, f_chunk):
    d, d_ff = wg.shape[1], wg.shape[2]
    rows = xs.shape[0] // SUBLANES
    assert d == SUBLANES * LANES and d_ff % f_chunk == 0 and rows % tm == 0
    wmap = lambda i, te, nv: (te[i], 0, 0)
    grid_spec = pltpu.PrefetchScalarGridSpec(
        num_scalar_prefetch=2,
        grid=(rows // tm,),
        in_specs=[
            pl.BlockSpec((tm * SUBLANES, LANES), lambda i, te, nv: (jnp.minimum(i, nv[0] - 1), 0)),
            _resident((None, d, d_ff), wmap),
            _resident((None, d, d_ff), wmap),
            _resident((None, d_ff, d), wmap),
        ],
        out_specs=pl.BlockSpec((tm * SUBLANES, LANES), lambda i, te, nv: (i, 0)),
        scratch_shapes=[pltpu.VMEM((tm, d), F32)],
    )
    return pl.pallas_call(
        functools.partial(_ffn_kernel, f_chunk=f_chunk),
        grid_spec=grid_spec,
        out_shape=jax.ShapeDtypeStruct(xs.shape, F32),
        compiler_params=_params(("arbitrary",)),
        name="ffn",
    )(tile_expert, n_valid, xs, wg, wu, wd)


def _mix_ffn_kernel(*refs, f_chunk):
    wg_ref, wu_ref, wd_ref, gffn_ref, o_ref, h2_scr, x1_scr = refs[8:]
    s = pl.program_id(0)

    @pl.when(s == 0)
    def _():
        h2_scr[1] = jnp.zeros(h2_scr.shape[1:], h2_scr.dtype)
        x1_scr[1] = jnp.zeros(x1_scr.shape[1:], x1_scr.dtype)

    def step(cur, prev):
        x1, hn = _mix(*refs[:8])
        x1_scr[cur] = x1
        h2_scr[cur] = hn.astype(BF16)
        _swiglu_into(o_ref, h2_scr[prev], wg_ref, wu_ref, wd_ref, f_chunk)
        o_ref[...] = x1_scr[prev] + _rms(o_ref[...], gffn_ref[...])

    for parity in (0, 1):
        @pl.when(lax.rem(s, 2) == parity)
        def _():
            step(parity, 1 - parity)


def _mix_ffn(attn, rec, x, ga, gr, w_out, gpost, gpre, wg, wu, wd, gffn, tm, f_chunk):
    t, d = x.shape
    d_ff = wg.shape[1]
    n = t // tm
    assert d_ff % f_chunk == 0 and t % tm == 0
    row = lambda s: (jnp.minimum(s, n - 1), 0)
    const = lambda s: (0, 0)
    return pl.pallas_call(
        functools.partial(_mix_ffn_kernel, f_chunk=f_chunk),
        grid=(n + 1,),
        in_specs=_mix_specs(tm, d, row, const, w_out.shape)
        + [_resident((d, d_ff), const), _resident((d, d_ff), const), _resident((d_ff, d), const),
           pl.BlockSpec((1, d), const)],
        out_specs=pl.BlockSpec((tm, d), lambda s: (jnp.maximum(s - 1, 0), 0)),
        out_shape=jax.ShapeDtypeStruct((t, d), F32),
        scratch_shapes=[pltpu.VMEM((2, tm, d), BF16), pltpu.VMEM((2, tm, d), F32)],
        compiler_params=_params(("arbitrary",)),
        name="mix_ffn_dense",
    )(*_mix_args(attn, rec, x, ga, gr, w_out, gpost, gpre),
      wg.astype(BF16), wu.astype(BF16), wd.astype(BF16), gffn.reshape(1, d))


def _dispatch_kernel(pad_end_ref, d1_ref, d2_ref, h_ref, out_ref, zbuf, sem, zsem):
    tile = zbuf.shape[0]

    @pl.when(pl.program_id(0) == 0)
    def _():
        zbuf[...] = jnp.zeros_like(zbuf)

        def zero_copy(e):
            start = pl.multiple_of(pad_end_ref[e] * SUBLANES - tile, tile)
            return pltpu.make_async_copy(zbuf, out_ref.at[pl.ds(start, tile)], zsem)

        def has_rows(e):
            prev_end = pad_end_ref[e - 1] if e else 0
            return pad_end_ref[e] > prev_end

        for e in range(pad_end_ref.shape[0]):
            @pl.when(has_rows(e))
            def _():
                zero_copy(e).start()
        for e in range(pad_end_ref.shape[0]):
            @pl.when(has_rows(e))
            def _():
                zero_copy(e).wait()

    def row_copies(j):
        src = h_ref.at[_tile_rows(j)]
        return (pltpu.make_async_copy(src, out_ref.at[_tile_rows(d1_ref[j])], sem),
                pltpu.make_async_copy(src, out_ref.at[_tile_rows(d2_ref[j])], sem))

    def start(j, carry):
        for queue, c in enumerate(row_copies(j)):
            c.start(priority=queue)
        return carry

    def wait(j, carry):
        for c in row_copies(j):
            c.wait()
        return carry

    n = d1_ref.shape[0]
    lax.fori_loop(0, n, start, 0, unroll=8)
    lax.fori_loop(0, n, wait, 0, unroll=8)


def _dispatch(h, dest, pad_end, n_rows, tm, tile):
    t = h.shape[0] // SUBLANES
    assert tm % SMEM_BLOCK_WORDS == 0 and t % tm == 0
    nblk = t // tm
    grid_spec = pltpu.PrefetchScalarGridSpec(
        num_scalar_prefetch=1,
        grid=(nblk,),
        in_specs=[pl.BlockSpec((tm,), lambda i, pe: (i,), memory_space=pltpu.SMEM),
                  pl.BlockSpec((tm,), lambda i, pe: (i + nblk,), memory_space=pltpu.SMEM),
                  pl.BlockSpec((tm * SUBLANES, LANES), lambda i, pe: (i, 0))],
        out_specs=pl.BlockSpec(memory_space=pl.ANY),
        scratch_shapes=[pltpu.VMEM((tile * SUBLANES, LANES), h.dtype),
                        pltpu.SemaphoreType.DMA(()), pltpu.SemaphoreType.DMA(())],
    )
    return pl.pallas_call(
        _dispatch_kernel,
        grid_spec=grid_spec,
        out_shape=jax.ShapeDtypeStruct((n_rows * SUBLANES, LANES), h.dtype),
        compiler_params=_params(("arbitrary",)),
        name="moe_dispatch",
    )(pad_end, dest, dest, h)


def _combine_kernel(d1_ref, d2_ref, ys_ref, rp_ref, x_ref, g_ref, o_ref,
                    ya0, yb0, ya1, yb1, sems):
    s = pl.program_id(0)
    last = pl.num_programs(0) - 1
    bufs = ((ya0, yb0), (ya1, yb1))
    n = d1_ref.shape[0]

    def row_copies(j, pair, src_a, src_b):
        ya, yb = bufs[pair]
        dst = _tile_rows(j)
        return (pltpu.make_async_copy(ys_ref.at[_tile_rows(src_a)], ya.at[dst], sems.at[pair]),
                pltpu.make_async_copy(ys_ref.at[_tile_rows(src_b)], yb.at[dst], sems.at[pair]))

    def gather(pair):
        def start(j, carry):
            for queue, c in enumerate(row_copies(j, pair, d1_ref[j], d2_ref[j])):
                c.start(priority=queue)
            return carry
        lax.fori_loop(0, n, start, 0, unroll=8)

    def finish(pair):
        def wait(j, carry):
            for c in row_copies(j, pair, 0, 0):
                c.wait()
            return carry
        lax.fori_loop(0, n, wait, 0, unroll=8)
        ya, yb = bufs[pair]
        y = rp_ref[:, 0:1] * _load_row_tiles(ya) + rp_ref[:, 1:2] * _load_row_tiles(yb)
        o_ref[...] = x_ref[...] + _rms(y, g_ref[...])

    for parity in (0, 1):
        @pl.when(jnp.logical_and(lax.rem(s, 2) == parity, s < last))
        def _():
            gather(parity)

        @pl.when(jnp.logical_and(lax.rem(s, 2) == parity, s > 0))
        def _():
            finish(1 - parity)


def _combine(ys, dest, route_p, x, g, tm):
    t, d = x.shape
    assert d == SUBLANES * LANES and tm % SMEM_BLOCK_WORDS == 0 and t % tm == 0
    nblk = t // tm
    ahead = lambda off: (lambda s: (jnp.minimum(s, nblk - 1) + off,))
    behind = lambda s: (jnp.maximum(s - 1, 0), 0)
    buf = pltpu.VMEM((tm * SUBLANES, LANES), F32)
    return pl.pallas_call(
        _combine_kernel,
        grid=(nblk + 1,),
        in_specs=[pl.BlockSpec((tm,), ahead(0), memory_space=pltpu.SMEM),
                  pl.BlockSpec((tm,), ahead(nblk), memory_space=pltpu.SMEM),
                  pl.BlockSpec(memory_space=pl.ANY),
                  pl.BlockSpec((tm, LANES), behind), pl.BlockSpec((tm, d), behind),
                  pl.BlockSpec((1, d), lambda s: (0, 0))],
        out_specs=pl.BlockSpec((tm, d), behind),
        out_shape=jax.ShapeDtypeStruct((t, d), F32),
        scratch_shapes=[buf, buf, buf, buf, pltpu.SemaphoreType.DMA((2,))],
        compiler_params=_params(("arbitrary",)),
        name="moe_combine",
    )(dest, dest, ys, route_p, x, g.reshape(1, d))


def _routing_tables(route_i, n_exp, tm):
    t = route_i.shape[0]
    e_flat = jnp.concatenate([route_i[:, 0], route_i[:, 1]])
    onehot = (e_flat[:, None] == jnp.arange(n_exp, dtype=jnp.int32)[None, :]).astype(jnp.int32)
    counts = jnp.sum(onehot, axis=0)
    padded = ((counts + tm - 1) // tm) * tm
    pad_end = jnp.cumsum(padded)
    pad_start = pad_end - padded
    dest = jnp.sum((jnp.cumsum(onehot, axis=0) - onehot + pad_start[None, :]) * onehot, axis=1)

    n_rows = TOP_K * t + n_exp * tm
    n_tiles = n_rows // tm
    tile_first_row = jnp.arange(n_tiles, dtype=jnp.int32) * tm
    n_valid = pad_end[-1] // tm
    tile_first_row = jnp.minimum(tile_first_row, (n_valid - 1) * tm)
    tile_expert = jnp.sum((pad_end[None, :] <= tile_first_row[:, None]).astype(jnp.int32), axis=1)
    return (dest.astype(jnp.int32), pad_end.astype(jnp.int32),
            jnp.minimum(tile_expert, n_exp - 1).astype(jnp.int32),
            n_valid.astype(jnp.int32).reshape(1), n_rows)


def _tile(n, want):
    t = min(n, want)
    while n % t:
        t -= SUBLANES
    return t


def kernel(x, positions, pre_mix_g, w_in, conv_w, conv_b, w_rgate, b_rgate, w_igate, b_igate,
           lru_lambda, attn_out_g, lru_out_g, w_out, post_mix_g, pre_ffn_g, post_ffn_g,
           dense_w_gate, dense_w_up, dense_w_down, router_w, moe_w_gate, moe_w_up, moe_w_down):
    batch, seq, d = x.shape
    depth = w_in.shape[0]
    t = batch * seq
    d_ff = dense_w_gate.shape[-1]
    n_exp = moe_w_gate.shape[1]
    assert w_in.shape[2] == 3 * ATTN_WIDTH + 2 * LRU_WIDTH
    assert seq % (DILATED_PAIRS[-1][1] * Q_BLOCK) == 0

    tm_mix = _tile(t, 256)
    tm_ffn = _tile(t, 512)
    tm_moe = tm_ffn
    ts_proj = _tile(seq, 512)
    tm_rows = SMEM_BLOCK_WORDS
    f_chunk = 512 if d_ff % 512 == 0 else d_ff

    xf = x.reshape(t, d).astype(F32)
    pos = positions.reshape(t, 1).astype(jnp.int32)

    for l in range(depth):
        q, k, v, rec = _in_proj(xf, pos, pre_mix_g[l].reshape(1, d), w_in[l].astype(BF16),
                                conv_w[l], conv_b[l], w_rgate[l], b_rgate[l], w_igate[l],
                                b_igate[l], lru_lambda[l], batch, seq, ts_proj)
        attn = _attention(q, k, v, batch, seq)
        j = l // 2
        if l % 2 == 0:
            xf = _mix_ffn(attn, rec, xf, attn_out_g[l], lru_out_g[l], w_out[l], post_mix_g[l],
                          pre_ffn_g[l], dense_w_gate[j], dense_w_up[j], dense_w_down[j],
                          post_ffn_g[l], tm_ffn, f_chunk)
        else:
            x1, h2, route_i, route_p = _mix_out_router(
                attn, rec, xf, attn_out_g[l], lru_out_g[l], w_out[l], post_mix_g[l],
                pre_ffn_g[l], router_w[j], tm_mix)
            dest, pad_end, tile_expert, n_valid, n_rows = _routing_tables(route_i, n_exp, tm_moe)
            xs = _dispatch(h2, dest, pad_end, n_rows, tm_rows, tm_moe)
            ys = _ffn(xs, tile_expert, n_valid, moe_w_gate[j].astype(BF16),
                      moe_w_up[j].astype(BF16), moe_w_down[j].astype(BF16), tm_moe, f_chunk)
            xf = _combine(ys, dest, route_p, x1, post_ffn_g[l], tm_rows)
    return xf.reshape(batch, seq, d).astype(x.dtype)
```

```python
import functools
import math

import jax
import jax.numpy as jnp
from jax import lax
from jax.experimental import pallas as pl
from jax.experimental.pallas import tpu as pltpu

ATTN_HEADS = 8
HEAD_DIM = 64
ATTN_WIDTH = ATTN_HEADS * HEAD_DIM
LRU_BLOCKS = 8
LRU_BLOCK_W = 64
LRU_WIDTH = LRU_BLOCKS * LRU_BLOCK_W
DILATED_PAIRS = ((128, 1), (512, 4), (2048, 16))
Q_BLOCK = 128
ROT_DIM = HEAD_DIM // 4
ROT_HALF = ROT_DIM // 2
ROPE_THETA = 500000.0
ATTN_SCALE = 1.0 / math.sqrt(HEAD_DIM)
NEG_INF = -1e30
CONV_W = 4
LRU_C = 8.0
TOP_K = 2
RMS_EPS = 1e-6

LANES = 128
SUBLANES = 8
SMEM_BLOCK_WORDS = 1024
VMEM_LIMIT_BYTES = 56 * 1024 * 1024

F32 = jnp.float32
BF16 = jnp.bfloat16


def _params(semantics):
    return pltpu.CompilerParams(dimension_semantics=semantics,
                                vmem_limit_bytes=VMEM_LIMIT_BYTES)


def _resident(block_shape, index_map):
    return pl.BlockSpec(block_shape, index_map, pipeline_mode=pl.Buffered(1))


def _rms(x, g):
    var = jnp.mean(x * x, axis=-1, keepdims=True)
    return x * lax.rsqrt(var + RMS_EPS) * g


def _tile_rows(r):
    return pl.ds(pl.multiple_of(r * SUBLANES, SUBLANES), SUBLANES)


def _store_row_tiles(ref, x):
    rows = x.shape[0]
    for c in range(x.shape[1] // LANES):
        ref[pl.ds(c, rows, stride=SUBLANES), :] = x[:, c * LANES:(c + 1) * LANES]


def _load_row_tiles(ref):
    rows = ref.shape[0] // SUBLANES
    return jnp.concatenate([ref[pl.ds(c, rows, stride=SUBLANES), :] for c in range(SUBLANES)],
                           axis=1)


def _lru_block(u, gate, cw_ref, cb_ref, wr_ref, br_ref, wi_ref, bi_ref, lam_ref,
               ubuf, hcarry, a_scr, b_scr):
    ts, c = u.shape
    ubuf[SUBLANES:SUBLANES + ts, :] = u
    ext = ubuf[...]
    uc = cb_ref[...]
    for tap in range(CONV_W):
        back = CONV_W - 1 - tap
        shifted = pltpu.roll(ext, back, 0) if back else ext
        uc = uc + shifted[SUBLANES:SUBLANES + ts, :] * cw_ref[tap:tap + 1, :]
    ubuf[0:SUBLANES, :] = ubuf[ts:ts + SUBLANES, :]

    ucb = uc.astype(BF16)
    r = jax.nn.sigmoid(jnp.dot(ucb, wr_ref[...], preferred_element_type=F32) + br_ref[...])
    ig = jax.nn.sigmoid(jnp.dot(ucb, wi_ref[...], preferred_element_type=F32) + bi_ref[...])
    nl = -lam_ref[...]
    softplus = jnp.maximum(nl, 0.0) + jnp.log1p(jnp.exp(-jnp.abs(nl)))
    log_a = -LRU_C * r * softplus
    a = jnp.exp(log_a)
    mult = jnp.sqrt(1.0 - a * a)
    b = mult * ig * uc

    groups = ts // SUBLANES
    a3 = a.reshape(groups, SUBLANES, c)
    b3 = b.reshape(groups, SUBLANES, c)
    sub = lax.broadcasted_iota(jnp.int32, (groups, SUBLANES, c), 1)
    shift = 1
    while shift < SUBLANES:
        valid = sub >= shift
        a_sh = jnp.where(valid, pltpu.roll(a3, shift, 1), 1.0)
        b_sh = jnp.where(valid, pltpu.roll(b3, shift, 1), 0.0)
        b3 = a3 * b_sh + b3
        a3 = a3 * a_sh
        shift *= 2
    a_scr[...] = a3.reshape(ts, c)
    b_scr[...] = b3.reshape(ts, c)

    h_prev = hcarry[...]
    for g in range(groups):
        rows = slice(g * SUBLANES, (g + 1) * SUBLANES)
        h = a_scr[rows, :] * h_prev + b_scr[rows, :]
        b_scr[rows, :] = h
        h_prev = jnp.broadcast_to(h[SUBLANES - 1:SUBLANES, :], (SUBLANES, c))
    hcarry[...] = h_prev
    return jax.nn.gelu(gate) * b_scr[...]


def _in_proj_kernel(x_ref, pos_ref, g_ref, w_ref, rope_ref,
                    cw_ref, cb_ref, wr_ref, br_ref, wi_ref, bi_ref, lam_ref,
                    q_ref, k_ref, v_ref, rec_ref, ubuf, hcarry, a_scr, b_scr):
    @pl.when(pl.program_id(1) == 0)
    def _():
        ubuf[0:SUBLANES, :] = jnp.zeros((SUBLANES, ubuf.shape[1]), F32)
        hcarry[...] = jnp.zeros_like(hcarry)

    ang = pos_ref[...].astype(F32) * rope_ref[0:1, :]
    cs = jnp.cos(ang)
    sn = jnp.sin(ang)
    s_lo = sn * rope_ref[1:2, :]
    s_hi = sn * rope_ref[2:3, :]

    def rotary(z):
        return (z * cs + pltpu.roll(z, ROT_HALF, 1) * s_lo
                + pltpu.roll(z, LANES - ROT_HALF, 1) * s_hi)

    h = _rms(x_ref[...], g_ref[...]).astype(BF16)
    w = ATTN_WIDTH
    u = jnp.dot(h, w_ref[:, 3 * w:3 * w + LRU_WIDTH], preferred_element_type=F32)
    gate = jnp.dot(h, w_ref[:, 3 * w + LRU_WIDTH:], preferred_element_type=F32)
    rec_ref[...] = _lru_block(u, gate, cw_ref, cb_ref, wr_ref, br_ref, wi_ref, bi_ref, lam_ref,
                              ubuf, hcarry, a_scr, b_scr)
    for sec, out in ((0, q_ref), (1, k_ref)):
        z = jnp.dot(h, w_ref[:, sec * w:(sec + 1) * w], preferred_element_type=F32)
        for cb in range(w // LANES):
            sl = slice(cb * LANES, (cb + 1) * LANES)
            out[:, sl] = rotary(z[:, sl]).astype(BF16)
    v_ref[...] = jnp.dot(h, w_ref[:, 2 * w:3 * w], preferred_element_type=F32).astype(BF16)


def _rope_table():
    lane = jnp.arange(LANES, dtype=jnp.int32) % HEAD_DIM
    inv_freq = ROPE_THETA ** (-jnp.arange(ROT_HALF, dtype=F32) / ROT_HALF)
    freq = jnp.where(lane < ROT_DIM, inv_freq[lane % ROT_HALF], 0.0)
    lo = jnp.where((lane >= ROT_HALF) & (lane < ROT_DIM), 1.0, 0.0)
    hi = jnp.where(lane < ROT_HALF, -1.0, 0.0)
    return jnp.stack([freq, lo, hi]).astype(F32)


def _block_diag(w):
    nb, bw, _ = w.shape
    eye = jnp.eye(nb, dtype=w.dtype)
    return jnp.einsum("gij,gh->gihj", w, eye).reshape(nb * bw, nb * bw)


def _in_proj(x, pos, g, w_in, conv_w, conv_b, w_r, b_r, w_i, b_i, lam, batch, seq, ts):
    t, d = x.shape
    ncol = w_in.shape[1]
    c = LRU_WIDTH
    nblk = seq // ts
    row = lambda b, j: (b * nblk + j, 0)
    const = lambda b, j: (0, 0)
    vec = lambda a: a.reshape(1, c).astype(F32)
    return pl.pallas_call(
        _in_proj_kernel,
        grid=(batch, nblk),
        in_specs=[
            pl.BlockSpec((ts, d), row),
            pl.BlockSpec((ts, 1), row),
            pl.BlockSpec((1, d), const),
            _resident((d, ncol), const),
            pl.BlockSpec((3, LANES), const),
            pl.BlockSpec((CONV_W, c), const),
            pl.BlockSpec((1, c), const),
            pl.BlockSpec((c, c), const),
            pl.BlockSpec((1, c), const),
            pl.BlockSpec((c, c), const),
            pl.BlockSpec((1, c), const),
            pl.BlockSpec((1, c), const),
        ],
        out_specs=[pl.BlockSpec((ts, ATTN_WIDTH), row)] * 3 + [pl.BlockSpec((ts, c), row)],
        out_shape=[jax.ShapeDtypeStruct((t, ATTN_WIDTH), BF16)] * 3
        + [jax.ShapeDtypeStruct((t, c), F32)],
        scratch_shapes=[pltpu.VMEM((ts + SUBLANES, c), F32), pltpu.VMEM((SUBLANES, c), F32),
                        pltpu.VMEM((ts, c), F32), pltpu.VMEM((ts, c), F32)],
        compiler_params=_params(("parallel", "arbitrary")),
        name="in_proj_lru",
    )(x, pos, g, w_in, _rope_table(), conv_w.astype(F32), vec(conv_b),
      _block_diag(w_r).astype(BF16), vec(b_r), _block_diag(w_i).astype(BF16), vec(b_i), vec(lam))


ATTN_TILE = DILATED_PAIRS[-1][1] * Q_BLOCK


def _rows(start, size, stride):
    return pl.ds(start, size) if stride == 1 else pl.ds(start, size, stride=stride)


ATTN_PIPE_WIDTH = 4


def _attn_kernel(q_ref, kc_ref, vc_ref, kp_ref, vp_ref, bias_ref, o_ref,
                 q0_scr, q1_scr, k_scr, v_scr, ob0, ob1, ob2, ls0, ls1, ls2, *slots):
    s_scrs, m_scrs = slots[:len(slots) // 2], slots[len(slots) // 2:]
    j = pl.program_id(2)
    ts = q_ref.shape[0]
    lane = lax.broadcasted_iota(jnp.int32, (1, LANES), 1)
    head0 = lane < HEAD_DIM

    qf = q_ref[...].astype(F32) * ATTN_SCALE
    q0_scr[...] = jnp.where(head0, qf, 0.0)
    q1_scr[...] = jnp.where(head0, 0.0, qf)
    k_scr[0:ts, :] = kp_ref[...].astype(F32)
    k_scr[ts:2 * ts, :] = kc_ref[...].astype(F32)
    v_scr[0:ts, :] = vp_ref[...].astype(F32)
    v_scr[ts:2 * ts, :] = vc_ref[...].astype(F32)

    nt = (((1,), (1,)), ((), ()))
    units = ts // Q_BLOCK
    width = len(s_scrs) // 2
    groups = units // width
    assert groups % 2 == 0 and groups >= 2
    for (_, dil), ob, ls in zip(DILATED_PAIRS, (ob0, ob1, ob2), (ls0, ls1, ls2)):
        shift = dil.bit_length() - 1

        def rows_of(u, dil=dil, shift=shift):
            cls = jnp.bitwise_and(u, dil - 1)
            blk = lax.shift_right_logical(u, shift)
            q_start = cls + dil * Q_BLOCK * blk
            q_rows = _rows(q_start, Q_BLOCK, dil)
            kv_rows = _rows(ts + q_start - dil * Q_BLOCK, 2 * Q_BLOCK, dil)
            return blk, q_rows, kv_rows

        def scores(u, slot):
            blk, q_rows, kv_rows = rows_of(u)
            lhs = jnp.concatenate([q0_scr[q_rows, :], q1_scr[q_rows, :]], axis=0).astype(BF16)
            k2 = k_scr[kv_rows, :].astype(BF16)
            no_prev = jnp.logical_and(j == 0, blk == 0).astype(jnp.int32)
            bias = bias_ref[pl.ds(no_prev * 2 * Q_BLOCK, 2 * Q_BLOCK), :]
            s = lax.dot_general(lhs, k2, nt, preferred_element_type=F32) + bias
            s_scrs[slot][...] = s
            m_scrs[slot][...] = jnp.broadcast_to(jnp.max(s, axis=-1, keepdims=True),
                                                 (2 * Q_BLOCK, LANES))

        def finish(u, slot, ob=ob, ls=ls):
            _, q_rows, kv_rows = rows_of(u)
            m = m_scrs[slot][...]
            p = jnp.exp(s_scrs[slot][...] - jnp.concatenate([m, m], axis=1))
            den = jnp.sum(p, axis=-1, keepdims=True)
            v2 = v_scr[kv_rows, :].astype(BF16)
            o = jnp.dot(p.astype(BF16), v2, preferred_element_type=F32) / den
            lse = m + jnp.log(den)
            ob[q_rows, :] = jnp.where(head0, o[0:Q_BLOCK], o[Q_BLOCK:])
            ls[q_rows, :] = jnp.where(head0, lse[0:Q_BLOCK], lse[Q_BLOCK:])

        def scores_group(g, half):
            for w in range(width):
                scores(g * width + w, half * width + w)

        def finish_group(g, half):
            for w in range(width):
                finish(g * width + w, half * width + w)

        scores_group(jnp.int32(0), 0)

        def pair(i, carry):
            scores_group(2 * i + 1, 1)
            finish_group(2 * i, 0)
            scores_group(2 * i + 2, 0)
            finish_group(2 * i + 1, 1)
            return carry

        lax.fori_loop(0, groups // 2 - 1, pair, 0)
        scores_group(jnp.int32(groups - 1), 1)
        finish_group(jnp.int32(groups - 2), 0)
        finish_group(jnp.int32(groups - 1), 1)

    chunk = 2 * Q_BLOCK

    def mix(c, carry):
        r = pl.ds(pl.multiple_of(c * chunk, chunk), chunk)
        la, lb, lc = ls0[r, :], ls1[r, :], ls2[r, :]
        m = jnp.maximum(jnp.maximum(la, lb), lc)
        ea, eb, ec = jnp.exp(la - m), jnp.exp(lb - m), jnp.exp(lc - m)
        o_ref[r, :] = (ea * ob0[r, :] + eb * ob1[r, :] + ec * ob2[r, :]) / (ea + eb + ec)
        return carry

    lax.fori_loop(0, ts // chunk, mix, 0)


def _attn_bias():
    qi = jnp.arange(2 * Q_BLOCK, dtype=jnp.int32)[:, None] % Q_BLOCK
    kj = jnp.arange(2 * Q_BLOCK, dtype=jnp.int32)[None, :]
    delta = Q_BLOCK + qi - kj
    band = (delta >= 0) & (delta <= Q_BLOCK)
    with_prev = jnp.where(band, 0.0, NEG_INF)
    no_prev = jnp.where(band & (kj >= Q_BLOCK), 0.0, NEG_INF)
    return jnp.concatenate([with_prev, no_prev], axis=0).astype(F32)


def _attention(q, k, v, batch, seq):
    ts = ATTN_TILE
    assert seq % ts == 0
    nt = seq // ts
    cur = pl.BlockSpec((ts, LANES), lambda b, c, j: (b * nt + j, c))
    prev = pl.BlockSpec((ts, LANES), lambda b, c, j: (b * nt + jnp.maximum(j - 1, 0), c))
    tile_scr = pltpu.VMEM((ts, LANES), F32)
    pair_scr = pltpu.VMEM((2 * ts, LANES), F32)
    return pl.pallas_call(
        _attn_kernel,
        grid=(batch, ATTN_WIDTH // LANES, nt),
        in_specs=[cur, cur, cur, prev, prev,
                  pl.BlockSpec((4 * Q_BLOCK, 2 * Q_BLOCK), lambda b, c, j: (0, 0))],
        out_specs=cur,
        out_shape=jax.ShapeDtypeStruct((batch * seq, ATTN_WIDTH), F32),
        scratch_shapes=[tile_scr, tile_scr, pair_scr, pair_scr] + [tile_scr] * 6
        + [pltpu.VMEM((2 * Q_BLOCK, 2 * Q_BLOCK), F32)] * (2 * ATTN_PIPE_WIDTH)
        + [pltpu.VMEM((2 * Q_BLOCK, LANES), F32)] * (2 * ATTN_PIPE_WIDTH),
        compiler_params=_params(("parallel", "parallel", "parallel")),
        name="attention",
    )(q, k, v, k, v, _attn_bias())


def _mix(attn_ref, rec_ref, x_ref, ga_ref, gr_ref, wo_ref, gpost_ref, gpre_ref):
    na = _rms(attn_ref[...], ga_ref[...]).astype(BF16)
    nr = _rms(rec_ref[...], gr_ref[...]).astype(BF16)
    mixed = (jnp.dot(na, wo_ref[0:ATTN_WIDTH, :], preferred_element_type=F32)
             + jnp.dot(nr, wo_ref[ATTN_WIDTH:, :], preferred_element_type=F32))
    x1 = x_ref[...] + _rms(mixed, gpost_ref[...])
    return x1, _rms(x1, gpre_ref[...])


def _mix_specs(tm, d, row, const, w_out_shape):
    aw = ATTN_WIDTH
    return ([pl.BlockSpec((tm, aw), row)] * 2 + [pl.BlockSpec((tm, d), row),
            pl.BlockSpec((1, aw), const), pl.BlockSpec((1, LRU_WIDTH), const),
            _resident(w_out_shape, const), pl.BlockSpec((1, d), const),
            pl.BlockSpec((1, d), const)])


def _mix_args(attn, rec, x, ga, gr, w_out, gpost, gpre):
    d = x.shape[1]
    return [attn, rec, x, ga.reshape(1, ATTN_WIDTH), gr.reshape(1, LRU_WIDTH),
            w_out.astype(BF16), gpost.reshape(1, d), gpre.reshape(1, d)]


def _mix_router_kernel(*refs, n_exp):
    rw_ref, x1_ref, h2_ref, ri_ref, rp_ref = refs[8:]
    x1, hn = _mix(*refs[:8])
    x1_ref[...] = x1
    _store_row_tiles(h2_ref, hn)

    rw = rw_ref[...]
    h_hi, r_hi = hn.astype(BF16), rw.astype(BF16)
    h_lo = (hn - h_hi.astype(F32)).astype(BF16)
    r_lo = (rw - r_hi.astype(F32)).astype(BF16)
    logits = (jnp.dot(h_hi, r_hi, preferred_element_type=F32)
              + jnp.dot(h_hi, r_lo, preferred_element_type=F32)
              + jnp.dot(h_lo, r_hi, preferred_element_type=F32))
    tm = logits.shape[0]
    lane = lax.broadcasted_iota(jnp.int32, (tm, LANES), 1)
    logits = jnp.where(lane < n_exp, logits, -jnp.inf)
    m1 = jnp.max(logits, axis=-1, keepdims=True)
    i1 = jnp.min(jnp.where(logits == m1, lane, n_exp), axis=-1, keepdims=True)
    rest = jnp.where(lane == i1, -jnp.inf, logits)
    m2 = jnp.max(rest, axis=-1, keepdims=True)
    i2 = jnp.min(jnp.where(rest == m2, lane, n_exp), axis=-1, keepdims=True)
    e2 = jnp.exp(m2 - m1)
    p1 = 1.0 / (1.0 + e2)
    p2 = e2 / (1.0 + e2)
    ri_ref[...] = jnp.where(lane == 0, i1, jnp.where(lane == 1, i2, 0))
    rp_ref[...] = jnp.where(lane == 0, p1, jnp.where(lane == 1, p2, 0.0))


def _mix_out_router(attn, rec, x, ga, gr, w_out, gpost, gpre, router_w, tm):
    t, d = x.shape
    assert d == SUBLANES * LANES
    row = lambda i: (i, 0)
    const = lambda i: (0, 0)
    n_exp = router_w.shape[1]
    rw = jnp.pad(router_w.astype(F32), ((0, 0), (0, LANES - n_exp)))
    return pl.pallas_call(
        functools.partial(_mix_router_kernel, n_exp=n_exp),
        grid=(t // tm,),
        in_specs=_mix_specs(tm, d, row, const, w_out.shape) + [pl.BlockSpec(rw.shape, const)],
        out_specs=[pl.BlockSpec((tm, d), row), pl.BlockSpec((tm * SUBLANES, LANES), row)]
        + [pl.BlockSpec((tm, LANES), row)] * 2,
        out_shape=[jax.ShapeDtypeStruct((t, d), F32),
                   jax.ShapeDtypeStruct((t * SUBLANES, LANES), F32)]
        + [jax.ShapeDtypeStruct((t, LANES), jnp.int32), jax.ShapeDtypeStruct((t, LANES), F32)],
        compiler_params=_params(("parallel",)),
        name="mix_out_router",
    )(*_mix_args(attn, rec, x, ga, gr, w_out, gpost, gpre), rw)


def _swiglu_into(o_ref, x, wg_ref, wu_ref, wd_ref, f_chunk):
    d_ff = wg_ref.shape[1]
    for c in range(d_ff // f_chunk):
        sl = slice(c * f_chunk, (c + 1) * f_chunk)
        g = jnp.dot(x, wg_ref[:, sl], preferred_element_type=F32)
        u = jnp.dot(x, wu_ref[:, sl], preferred_element_type=F32)
        a = (g * jax.nn.sigmoid(g) * u).astype(BF16)
        y = jnp.dot(a, wd_ref[sl, :], preferred_element_type=F32)
        if c == 0:
            o_ref[...] = y
        else:
            o_ref[...] += y


def _ffn_kernel(te_ref, nv_ref, x_ref, wg_ref, wu_ref, wd_ref, o_ref, acc, *, f_chunk):
    i = pl.program_id(0)

    @pl.when(i < nv_ref[0])
    def _():
        _swiglu_into(acc, _load_row_tiles(x_ref).astype(BF16), wg_ref, wu_ref, wd_ref, f_chunk)
        _store_row_tiles(o_ref, acc[...])

    @pl.when(i >= nv_ref[0])
    def _():
        o_ref[...] = jnp.zeros_like(o_ref)


def _ffn(xs, tile_expert, n_valid, wg, wu, wd, tm, f_chunk):
    d, d_ff = wg.shape[1], wg.shape[2]
    rows = xs.shape[0] // SUBLANES
    assert d == SUBLANES * LANES and d_ff % f_chunk == 0 and rows % tm == 0
    wmap = lambda i, te, nv: (te[i], 0, 0)
    grid_spec = pltpu.PrefetchScalarGridSpec(
        num_scalar_prefetch=2,
        grid=(rows // tm,),
        in_specs=[
            pl.BlockSpec((tm * SUBLANES, LANES), lambda i, te, nv: (jnp.minimum(i, nv[0] - 1), 0)),
            _resident((None, d, d_ff), wmap),
            _resident((None, d, d_ff), wmap),
            _resident((None, d_ff, d), wmap),
        ],
        out_specs=pl.BlockSpec((tm * SUBLANES, LANES), lambda i, te, nv: (i, 0)),
        scratch_shapes=[pltpu.VMEM((tm, d), F32)],
    )
    return pl.pallas_call(
        functools.partial(_ffn_kernel, f_chunk=f_chunk),
        grid_spec=grid_spec,
        out_shape=jax.ShapeDtypeStruct(xs.shape, F32),
        compiler_params=_params(("arbitrary",)),
        name="ffn",
    )(tile_expert, n_valid, xs, wg, wu, wd)


def _mix_ffn_kernel(*refs, f_chunk):
    wg_ref, wu_ref, wd_ref, gffn_ref, o_ref, h2_scr, x1_scr = refs[8:]
    s = pl.program_id(0)

    def mix_into(cur):
        x1, hn = _mix(*refs[:8])
        x1_scr[cur] = x1
        h2_scr[cur] = hn.astype(BF16)

    def ffn_from(prev):
        _swiglu_into(o_ref, h2_scr[prev], wg_ref, wu_ref, wd_ref, f_chunk)
        o_ref[...] = x1_scr[prev] + _rms(o_ref[...], gffn_ref[...])

    @pl.when(s == 0)
    def _():
        mix_into(0)

    for parity in (0, 1):
        @pl.when(jnp.logical_and(lax.rem(s, 2) == parity, s > 0))
        def _():
            mix_into(parity)
            ffn_from(1 - parity)


def _mix_ffn(attn, rec, x, ga, gr, w_out, gpost, gpre, wg, wu, wd, gffn, tm, f_chunk):
    t, d = x.shape
    d_ff = wg.shape[1]
    n = t // tm
    assert d_ff % f_chunk == 0 and t % tm == 0
    row = lambda s: (jnp.minimum(s, n - 1), 0)
    const = lambda s: (0, 0)
    return pl.pallas_call(
        functools.partial(_mix_ffn_kernel, f_chunk=f_chunk),
        grid=(n + 1,),
        in_specs=_mix_specs(tm, d, row, const, w_out.shape)
        + [_resident((d, d_ff), const), _resident((d, d_ff), const), _resident((d_ff, d), const),
           pl.BlockSpec((1, d), const)],
        out_specs=pl.BlockSpec((tm, d), lambda s: (jnp.maximum(s - 1, 0), 0)),
        out_shape=jax.ShapeDtypeStruct((t, d), F32),
        scratch_shapes=[pltpu.VMEM((2, tm, d), BF16), pltpu.VMEM((2, tm, d), F32)],
        compiler_params=_params(("arbitrary",)),
        name="mix_ffn_dense",
    )(*_mix_args(attn, rec, x, ga, gr, w_out, gpost, gpre),
      wg.astype(BF16), wu.astype(BF16), wd.astype(BF16), gffn.reshape(1, d))


def _dispatch_kernel(pad_end_ref, d1_ref, d2_ref, h_ref, out_ref, zbuf, sem, zsem):
    tile = zbuf.shape[0]

    @pl.when(pl.program_id(0) == 0)
    def _():
        zbuf[...] = jnp.zeros_like(zbuf)

        def zero_copy(e):
            start = pl.multiple_of(pad_end_ref[e] * SUBLANES - tile, tile)
            return pltpu.make_async_copy(zbuf, out_ref.at[pl.ds(start, tile)], zsem)

        def has_rows(e):
            prev_end = pad_end_ref[e - 1] if e else 0
            return pad_end_ref[e] > prev_end

        for e in range(pad_end_ref.shape[0]):
            @pl.when(has_rows(e))
            def _():
                zero_copy(e).start()
        for e in range(pad_end_ref.shape[0]):
            @pl.when(has_rows(e))
            def _():
                zero_copy(e).wait()

    def row_copies(j):
        src = h_ref.at[_tile_rows(j)]
        return (pltpu.make_async_copy(src, out_ref.at[_tile_rows(d1_ref[j])], sem),
                pltpu.make_async_copy(src, out_ref.at[_tile_rows(d2_ref[j])], sem))

    def start(j, carry):
        for queue, c in enumerate(row_copies(j)):
            c.start(priority=queue)
        return carry

    def wait(j, carry):
        for c in row_copies(j):
            c.wait()
        return carry

    n = d1_ref.shape[0]
    lax.fori_loop(0, n, start, 0, unroll=8)
    lax.fori_loop(0, n, wait, 0, unroll=8)


def _dispatch(h, dest, pad_end, n_rows, tm, tile):
    t = h.shape[0] // SUBLANES
    assert tm % SMEM_BLOCK_WORDS == 0 and t % tm == 0
    nblk = t // tm
    grid_spec = pltpu.PrefetchScalarGridSpec(
        num_scalar_prefetch=1,
        grid=(nblk,),
        in_specs=[pl.BlockSpec((tm,), lambda i, pe: (i,), memory_space=pltpu.SMEM),
                  pl.BlockSpec((tm,), lambda i, pe: (i + nblk,), memory_space=pltpu.SMEM),
                  pl.BlockSpec((tm * SUBLANES, LANES), lambda i, pe: (i, 0))],
        out_specs=pl.BlockSpec(memory_space=pl.ANY),
        scratch_shapes=[pltpu.VMEM((tile * SUBLANES, LANES), h.dtype),
                        pltpu.SemaphoreType.DMA(()), pltpu.SemaphoreType.DMA(())],
    )
    return pl.pallas_call(
        _dispatch_kernel,
        grid_spec=grid_spec,
        out_shape=jax.ShapeDtypeStruct((n_rows * SUBLANES, LANES), h.dtype),
        compiler_params=_params(("arbitrary",)),
        name="moe_dispatch",
    )(pad_end, dest, dest, h)


def _combine_kernel(d1_ref, d2_ref, ys_ref, rp_ref, x_ref, g_ref, o_ref,
                    ya0, yb0, ya1, yb1, sems):
    s = pl.program_id(0)
    last = pl.num_programs(0) - 1
    bufs = ((ya0, yb0), (ya1, yb1))
    n = d1_ref.shape[0]

    def row_copies(j, pair, src_a, src_b):
        ya, yb = bufs[pair]
        dst = _tile_rows(j)
        return (pltpu.make_async_copy(ys_ref.at[_tile_rows(src_a)], ya.at[dst], sems.at[pair]),
                pltpu.make_async_copy(ys_ref.at[_tile_rows(src_b)], yb.at[dst], sems.at[pair]))

    def gather(pair):
        def start(j, carry):
            for queue, c in enumerate(row_copies(j, pair, d1_ref[j], d2_ref[j])):
                c.start(priority=queue)
            return carry
        lax.fori_loop(0, n, start, 0, unroll=8)

    def finish(pair):
        def wait(j, carry):
            for c in row_copies(j, pair, 0, 0):
                c.wait()
            return carry
        lax.fori_loop(0, n, wait, 0, unroll=8)
        ya, yb = bufs[pair]
        y = rp_ref[:, 0:1] * _load_row_tiles(ya) + rp_ref[:, 1:2] * _load_row_tiles(yb)
        o_ref[...] = x_ref[...] + _rms(y, g_ref[...])

    for parity in (0, 1):
        @pl.when(jnp.logical_and(lax.rem(s, 2) == parity, s < last))
        def _():
            gather(parity)

        @pl.when(jnp.logical_and(lax.rem(s, 2) == parity, s > 0))
        def _():
            finish(1 - parity)


def _combine(ys, dest, route_p, x, g, tm):
    t, d = x.shape
    assert d == SUBLANES * LANES and tm % SMEM_BLOCK_WORDS == 0 and t % tm == 0
    nblk = t // tm
    ahead = lambda off: (lambda s: (jnp.minimum(s, nblk - 1) + off,))
    behind = lambda s: (jnp.maximum(s - 1, 0), 0)
    buf = pltpu.VMEM((tm * SUBLANES, LANES), F32)
    return pl.pallas_call(
        _combine_kernel,
        grid=(nblk + 1,),
        in_specs=[pl.BlockSpec((tm,), ahead(0), memory_space=pltpu.SMEM),
                  pl.BlockSpec((tm,), ahead(nblk), memory_space=pltpu.SMEM),
                  pl.BlockSpec(memory_space=pl.ANY),
                  pl.BlockSpec((tm, LANES), behind), pl.BlockSpec((tm, d), behind),
                  pl.BlockSpec((1, d), lambda s: (0, 0))],
        out_specs=pl.BlockSpec((tm, d), behind),
        out_shape=jax.ShapeDtypeStruct((t, d), F32),
        scratch_shapes=[buf, buf, buf, buf, pltpu.SemaphoreType.DMA((2,))],
        compiler_params=_params(("arbitrary",)),
        name="moe_combine",
    )(dest, dest, ys, route_p, x, g.reshape(1, d))


def _routing_tables(route_i, n_exp, tm):
    t = route_i.shape[0]
    e_flat = jnp.concatenate([route_i[:, 0], route_i[:, 1]])
    onehot = (e_flat[:, None] == jnp.arange(n_exp, dtype=jnp.int32)[None, :]).astype(jnp.int32)
    counts = jnp.sum(onehot, axis=0)
    padded = ((counts + tm - 1) // tm) * tm
    pad_end = jnp.cumsum(padded)
    pad_start = pad_end - padded
    dest = jnp.sum((jnp.cumsum(onehot, axis=0) - onehot + pad_start[None, :]) * onehot, axis=1)

    n_rows = TOP_K * t + n_exp * tm
    n_tiles = n_rows // tm
    tile_first_row = jnp.arange(n_tiles, dtype=jnp.int32) * tm
    n_valid = pad_end[-1] // tm
    tile_first_row = jnp.minimum(tile_first_row, (n_valid - 1) * tm)
    tile_expert = jnp.sum((pad_end[None, :] <= tile_first_row[:, None]).astype(jnp.int32), axis=1)
    return (dest.astype(jnp.int32), pad_end.astype(jnp.int32),
            jnp.minimum(tile_expert, n_exp - 1).astype(jnp.int32),
            n_valid.astype(jnp.int32).reshape(1), n_rows)


def _tile(n, want):
    t = min(n, want)
    while n % t:
        t -= SUBLANES
    return t


def kernel(x, positions, pre_mix_g, w_in, conv_w, conv_b, w_rgate, b_rgate, w_igate, b_igate,
           lru_lambda, attn_out_g, lru_out_g, w_out, post_mix_g, pre_ffn_g, post_ffn_g,
           dense_w_gate, dense_w_up, dense_w_down, router_w, moe_w_gate, moe_w_up, moe_w_down):
    batch, seq, d = x.shape
    depth = w_in.shape[0]
    t = batch * seq
    d_ff = dense_w_gate.shape[-1]
    n_exp = moe_w_gate.shape[1]
    assert w_in.shape[2] == 3 * ATTN_WIDTH + 2 * LRU_WIDTH
    assert seq % (DILATED_PAIRS[-1][1] * Q_BLOCK) == 0

    tm_mix = _tile(t, 256)
    tm_ffn = _tile(t, 512)
    tm_moe = tm_ffn
    ts_proj = _tile(seq, 512)
    tm_rows = SMEM_BLOCK_WORDS
    f_chunk = 1792 if d_ff % 1792 == 0 else (512 if d_ff % 512 == 0 else d_ff)

    xf = x.reshape(t, d).astype(F32)
    pos = positions.reshape(t, 1).astype(jnp.int32)

    for l in range(depth):
        q, k, v, rec = _in_proj(xf, pos, pre_mix_g[l].reshape(1, d), w_in[l].astype(BF16),
                                conv_w[l], conv_b[l], w_rgate[l], b_rgate[l], w_igate[l],
                                b_igate[l], lru_lambda[l], batch, seq, ts_proj)
        attn = _attention(q, k, v, batch, seq)
        j = l // 2
        if l % 2 == 0:
            xf = _mix_ffn(attn, rec, xf, attn_out_g[l], lru_out_g[l], w_out[l], post_mix_g[l],
                          pre_ffn_g[l], dense_w_gate[j], dense_w_up[j], dense_w_down[j],
                          post_ffn_g[l], tm_ffn, f_chunk)
        else:
            x1, h2, route_i, route_p = _mix_out_router(
                attn, rec, xf, attn_out_g[l], lru_out_g[l], w_out[l], post_mix_g[l],
                pre_ffn_g[l], router_w[j], tm_mix)
            dest, pad_end, tile_expert, n_valid, n_rows = _routing_tables(route_i, n_exp, tm_moe)
            xs = _dispatch(h2, dest, pad_end, n_rows, tm_rows, tm_moe)
            ys = _ffn(xs, tile_expert, n_valid, moe_w_gate[j].astype(BF16),
                      moe_w_up[j].astype(BF16), moe_w_down[j].astype(BF16), tm_moe, f_chunk)
            xf = _combine(ys, dest, route_p, x1, post_ffn_g[l], tm_rows)
    return xf.reshape(batch, seq, d).astype(x.dtype)
```

```python
import functools
import math

import jax
import jax.numpy as jnp
from jax import lax
from jax.experimental import pallas as pl
from jax.experimental.pallas import tpu as pltpu

ATTN_HEADS = 8
HEAD_DIM = 64
ATTN_WIDTH = ATTN_HEADS * HEAD_DIM
LRU_BLOCKS = 8
LRU_BLOCK_W = 64
LRU_WIDTH = LRU_BLOCKS * LRU_BLOCK_W
DILATED_PAIRS = ((128, 1), (512, 4), (2048, 16))
Q_BLOCK = 128
ROT_DIM = HEAD_DIM // 4
ROT_HALF = ROT_DIM // 2
ROPE_THETA = 500000.0
ATTN_SCALE = 1.0 / math.sqrt(HEAD_DIM)
NEG_INF = -1e30
CONV_W = 4
LRU_C = 8.0
TOP_K = 2
RMS_EPS = 1e-6

LANES = 128
SUBLANES = 8
SMEM_BLOCK_WORDS = 1024
VMEM_LIMIT_BYTES = 56 * 1024 * 1024

F32 = jnp.float32
BF16 = jnp.bfloat16


def _params(semantics):
    return pltpu.CompilerParams(dimension_semantics=semantics,
                                vmem_limit_bytes=VMEM_LIMIT_BYTES)


def _resident(block_shape, index_map):
    return pl.BlockSpec(block_shape, index_map, pipeline_mode=pl.Buffered(1))


def _rms(x, g):
    var = jnp.mean(x * x, axis=-1, keepdims=True)
    return x * lax.rsqrt(var + RMS_EPS) * g


def _tile_rows(r):
    return pl.ds(pl.multiple_of(r * SUBLANES, SUBLANES), SUBLANES)


def _store_row_tiles(ref, x):
    rows = x.shape[0]
    for c in range(x.shape[1] // LANES):
        ref[pl.ds(c, rows, stride=SUBLANES), :] = x[:, c * LANES:(c + 1) * LANES]


def _load_row_tiles(ref):
    rows = ref.shape[0] // SUBLANES
    return jnp.concatenate([ref[pl.ds(c, rows, stride=SUBLANES), :] for c in range(SUBLANES)],
                           axis=1)


def _lru_block(u, gate, cw_ref, cb_ref, wr_ref, br_ref, wi_ref, bi_ref, lam_ref,
               ubuf, hcarry, a_scr, b_scr):
    ts, c = u.shape
    ubuf[SUBLANES:SUBLANES + ts, :] = u
    ext = ubuf[...]
    uc = cb_ref[...]
    for tap in range(CONV_W):
        back = CONV_W - 1 - tap
        shifted = pltpu.roll(ext, back, 0) if back else ext
        uc = uc + shifted[SUBLANES:SUBLANES + ts, :] * cw_ref[tap:tap + 1, :]
    ubuf[0:SUBLANES, :] = ubuf[ts:ts + SUBLANES, :]

    ucb = uc.astype(BF16)
    r = jax.nn.sigmoid(jnp.dot(ucb, wr_ref[...], preferred_element_type=F32) + br_ref[...])
    ig = jax.nn.sigmoid(jnp.dot(ucb, wi_ref[...], preferred_element_type=F32) + bi_ref[...])
    nl = -lam_ref[...]
    softplus = jnp.maximum(nl, 0.0) + jnp.log1p(jnp.exp(-jnp.abs(nl)))
    log_a = -LRU_C * r * softplus
    a = jnp.exp(log_a)
    mult = jnp.sqrt(1.0 - a * a)
    b = mult * ig * uc

    groups = ts // SUBLANES
    a3 = a.reshape(groups, SUBLANES, c)
    b3 = b.reshape(groups, SUBLANES, c)
    sub = lax.broadcasted_iota(jnp.int32, (groups, SUBLANES, c), 1)
    shift = 1
    while shift < SUBLANES:
        valid = sub >= shift
        a_sh = jnp.where(valid, pltpu.roll(a3, shift, 1), 1.0)
        b_sh = jnp.where(valid, pltpu.roll(b3, shift, 1), 0.0)
        b3 = a3 * b_sh + b3
        a3 = a3 * a_sh
        shift *= 2
    a_scr[...] = a3.reshape(ts, c)
    b_scr[...] = b3.reshape(ts, c)

    h_prev = hcarry[...]
    for g in range(groups):
        rows = slice(g * SUBLANES, (g + 1) * SUBLANES)
        h = a_scr[rows, :] * h_prev + b_scr[rows, :]
        b_scr[rows, :] = h
        h_prev = jnp.broadcast_to(h[SUBLANES - 1:SUBLANES, :], (SUBLANES, c))
    hcarry[...] = h_prev
    return jax.nn.gelu(gate) * b_scr[...]


def _in_proj_kernel(x_ref, pos_ref, g_ref, w_ref, rope_ref,
                    cw_ref, cb_ref, wr_ref, br_ref, wi_ref, bi_ref, lam_ref,
                    q_ref, k_ref, v_ref, rec_ref, ubuf, hcarry, a_scr, b_scr):
    @pl.when(pl.program_id(1) == 0)
    def _():
        ubuf[0:SUBLANES, :] = jnp.zeros((SUBLANES, ubuf.shape[1]), F32)
        hcarry[...] = jnp.zeros_like(hcarry)

    ang = pos_ref[...].astype(F32) * rope_ref[0:1, :]
    cs = jnp.cos(ang)
    sn = jnp.sin(ang)
    s_lo = sn * rope_ref[1:2, :]
    s_hi = sn * rope_ref[2:3, :]

    def rotary(z):
        return (z * cs + pltpu.roll(z, ROT_HALF, 1) * s_lo
                + pltpu.roll(z, LANES - ROT_HALF, 1) * s_hi)

    h = _rms(x_ref[...], g_ref[...]).astype(BF16)
    w = ATTN_WIDTH
    u = jnp.dot(h, w_ref[:, 3 * w:3 * w + LRU_WIDTH], preferred_element_type=F32)
    gate = jnp.dot(h, w_ref[:, 3 * w + LRU_WIDTH:], preferred_element_type=F32)
    rec_ref[...] = _lru_block(u, gate, cw_ref, cb_ref, wr_ref, br_ref, wi_ref, bi_ref, lam_ref,
                              ubuf, hcarry, a_scr, b_scr)
    for sec, out in ((0, q_ref), (1, k_ref)):
        z = jnp.dot(h, w_ref[:, sec * w:(sec + 1) * w], preferred_element_type=F32)
        for cb in range(w // LANES):
            sl = slice(cb * LANES, (cb + 1) * LANES)
            out[:, sl] = rotary(z[:, sl]).astype(BF16)
    v_ref[...] = jnp.dot(h, w_ref[:, 2 * w:3 * w], preferred_element_type=F32).astype(BF16)


def _rope_table():
    lane = jnp.arange(LANES, dtype=jnp.int32) % HEAD_DIM
    inv_freq = ROPE_THETA ** (-jnp.arange(ROT_HALF, dtype=F32) / ROT_HALF)
    freq = jnp.where(lane < ROT_DIM, inv_freq[lane % ROT_HALF], 0.0)
    lo = jnp.where((lane >= ROT_HALF) & (lane < ROT_DIM), 1.0, 0.0)
    hi = jnp.where(lane < ROT_HALF, -1.0, 0.0)
    return jnp.stack([freq, lo, hi]).astype(F32)


def _block_diag(w):
    nb, bw, _ = w.shape
    eye = jnp.eye(nb, dtype=w.dtype)
    return jnp.einsum("gij,gh->gihj", w, eye).reshape(nb * bw, nb * bw)


def _in_proj(x, pos, g, w_in, conv_w, conv_b, w_r, b_r, w_i, b_i, lam, batch, seq, ts):
    t, d = x.shape
    ncol = w_in.shape[1]
    c = LRU_WIDTH
    nblk = seq // ts
    row = lambda b, j: (b * nblk + j, 0)
    const = lambda b, j: (0, 0)
    vec = lambda a: a.reshape(1, c).astype(F32)
    return pl.pallas_call(
        _in_proj_kernel,
        grid=(batch, nblk),
        in_specs=[
            pl.BlockSpec((ts, d), row),
            pl.BlockSpec((ts, 1), row),
            pl.BlockSpec((1, d), const),
            _resident((d, ncol), const),
            pl.BlockSpec((3, LANES), const),
            pl.BlockSpec((CONV_W, c), const),
            pl.BlockSpec((1, c), const),
            pl.BlockSpec((c, c), const),
            pl.BlockSpec((1, c), const),
            pl.BlockSpec((c, c), const),
            pl.BlockSpec((1, c), const),
            pl.BlockSpec((1, c), const),
        ],
        out_specs=[pl.BlockSpec((ts, ATTN_WIDTH), row)] * 3 + [pl.BlockSpec((ts, c), row)],
        out_shape=[jax.ShapeDtypeStruct((t, ATTN_WIDTH), BF16)] * 3
        + [jax.ShapeDtypeStruct((t, c), F32)],
        scratch_shapes=[pltpu.VMEM((ts + SUBLANES, c), F32), pltpu.VMEM((SUBLANES, c), F32),
                        pltpu.VMEM((ts, c), F32), pltpu.VMEM((ts, c), F32)],
        compiler_params=_params(("parallel", "arbitrary")),
        name="in_proj_lru",
    )(x, pos, g, w_in, _rope_table(), conv_w.astype(F32), vec(conv_b),
      _block_diag(w_r).astype(BF16), vec(b_r), _block_diag(w_i).astype(BF16), vec(b_i), vec(lam))


ATTN_TILE = DILATED_PAIRS[-1][1] * Q_BLOCK


def _rows(start, size, stride):
    return pl.ds(start, size) if stride == 1 else pl.ds(start, size, stride=stride)


ATTN_PIPE_WIDTH = 4


def _attn_kernel(q_ref, kc_ref, vc_ref, kp_ref, vp_ref, bias_ref, o_ref,
                 q0_scr, q1_scr, k_scr, v_scr, q0c, q1c, kc, vc,
                 ob0, ob1, ob2, ls0, ls1, ls2, *slots):
    s_scrs, m_scrs = slots[:len(slots) // 2], slots[len(slots) // 2:]
    j = pl.program_id(2)
    ts = q_ref.shape[0]
    lane = lax.broadcasted_iota(jnp.int32, (1, LANES), 1)
    head0 = lane < HEAD_DIM

    qf = q_ref[...].astype(F32) * ATTN_SCALE
    q0_scr[...] = jnp.where(head0, qf, 0.0)
    q1_scr[...] = jnp.where(head0, 0.0, qf)
    k_scr[0:ts, :] = kp_ref[...].astype(F32)
    k_scr[ts:2 * ts, :] = kc_ref[...].astype(F32)
    v_scr[0:ts, :] = vp_ref[...].astype(F32)
    v_scr[ts:2 * ts, :] = vc_ref[...].astype(F32)

    nt = (((1,), (1,)), ((), ()))
    units = ts // Q_BLOCK
    width = len(s_scrs) // 2
    groups = units // width
    assert groups % 2 == 0 and groups >= 2
    pre = DILATED_PAIRS[1][1]
    assert all(dil == 1 or dil % pre == 0 for _, dil in DILATED_PAIRS)

    def regroup(c, carry):
        for src, dst in ((q0_scr, q0c), (q1_scr, q1c), (k_scr, kc), (v_scr, vc)):
            n = src.shape[0] // pre
            dst[pl.ds(pl.multiple_of(c * n, Q_BLOCK), n), :] = src[_rows(c, n, pre), :]
        return carry

    lax.fori_loop(0, pre, regroup, 0)

    for (_, dil), ob, ls in zip(DILATED_PAIRS, (ob0, ob1, ob2), (ls0, ls1, ls2)):
        shift = dil.bit_length() - 1
        if dil == 1:
            q0_src, q1_src, k_src, v_src = q0_scr, q1_scr, k_scr, v_scr
        else:
            q0_src, q1_src, k_src, v_src = q0c, q1c, kc, vc

        def rows_of(u, dil=dil, shift=shift):
            cls = jnp.bitwise_and(u, dil - 1)
            blk = lax.shift_right_logical(u, shift)
            q_start = cls + dil * Q_BLOCK * blk
            out_rows = _rows(q_start, Q_BLOCK, dil)
            if dil == 1:
                q_rows = out_rows
                kv_rows = _rows(ts + q_start - Q_BLOCK, 2 * Q_BLOCK, 1)
            else:
                step = dil // pre
                c4 = jnp.bitwise_and(cls, pre - 1)
                sub = lax.shift_right_logical(cls, pre.bit_length() - 1)
                q_rows = _rows(c4 * (ts // pre) + sub + step * Q_BLOCK * blk, Q_BLOCK, step)
                kv_rows = _rows(c4 * (2 * ts // pre) + sub
                                + step * (ts // dil + (blk - 1) * Q_BLOCK), 2 * Q_BLOCK, step)
            return blk, out_rows, q_rows, kv_rows

        def scores(u, slot, q0_src=q0_src, q1_src=q1_src, k_src=k_src):
            blk, _, q_rows, kv_rows = rows_of(u)
            lhs = jnp.concatenate([q0_src[q_rows, :], q1_src[q_rows, :]], axis=0).astype(BF16)
            k2 = k_src[kv_rows, :].astype(BF16)
            no_prev = jnp.logical_and(j == 0, blk == 0).astype(jnp.int32)
            bias = bias_ref[pl.ds(no_prev * 2 * Q_BLOCK, 2 * Q_BLOCK), :]
            s = lax.dot_general(lhs, k2, nt, preferred_element_type=F32) + bias
            s_scrs[slot][...] = s
            m_scrs[slot][...] = jnp.broadcast_to(jnp.max(s, axis=-1, keepdims=True),
                                                 (2 * Q_BLOCK, LANES))

        def finish(u, slot, ob=ob, ls=ls, v_src=v_src):
            _, out_rows, _, kv_rows = rows_of(u)
            m = m_scrs[slot][...]
            p = jnp.exp(s_scrs[slot][...] - jnp.concatenate([m, m], axis=1))
            den = jnp.sum(p, axis=-1, keepdims=True)
            v2 = v_src[kv_rows, :].astype(BF16)
            o = jnp.dot(p.astype(BF16), v2, preferred_element_type=F32) / den
            lse = m + jnp.log(den)
            ob[out_rows, :] = jnp.where(head0, o[0:Q_BLOCK], o[Q_BLOCK:])
            ls[out_rows, :] = jnp.where(head0, lse[0:Q_BLOCK], lse[Q_BLOCK:])

        def scores_group(g, half):
            for w in range(width):
                scores(g * width + w, half * width + w)

        def finish_group(g, half):
            for w in range(width):
                finish(g * width + w, half * width + w)

        scores_group(jnp.int32(0), 0)

        def pair(i, carry):
            scores_group(2 * i + 1, 1)
            finish_group(2 * i, 0)
            scores_group(2 * i + 2, 0)
            finish_group(2 * i + 1, 1)
            return carry

        lax.fori_loop(0, groups // 2 - 1, pair, 0)
        scores_group(jnp.int32(groups - 1), 1)
        finish_group(jnp.int32(groups - 2), 0)
        finish_group(jnp.int32(groups - 1), 1)

    chunk = 2 * Q_BLOCK

    def mix(c, carry):
        r = pl.ds(pl.multiple_of(c * chunk, chunk), chunk)
        la, lb, lc = ls0[r, :], ls1[r, :], ls2[r, :]
        m = jnp.maximum(jnp.maximum(la, lb), lc)
        ea, eb, ec = jnp.exp(la - m), jnp.exp(lb - m), jnp.exp(lc - m)
        o_ref[r, :] = (ea * ob0[r, :] + eb * ob1[r, :] + ec * ob2[r, :]) / (ea + eb + ec)
        return carry

    lax.fori_loop(0, ts // chunk, mix, 0)


def _attn_bias():
    qi = jnp.arange(2 * Q_BLOCK, dtype=jnp.int32)[:, None] % Q_BLOCK
    kj = jnp.arange(2 * Q_BLOCK, dtype=jnp.int32)[None, :]
    delta = Q_BLOCK + qi - kj
    band = (delta >= 0) & (delta <= Q_BLOCK)
    with_prev = jnp.where(band, 0.0, NEG_INF)
    no_prev = jnp.where(band & (kj >= Q_BLOCK), 0.0, NEG_INF)
    return jnp.concatenate([with_prev, no_prev], axis=0).astype(F32)


def _attention(q, k, v, batch, seq):
    ts = ATTN_TILE
    assert seq % ts == 0
    nt = seq // ts
    cur = pl.BlockSpec((ts, LANES), lambda b, c, j: (b * nt + j, c))
    prev = pl.BlockSpec((ts, LANES), lambda b, c, j: (b * nt + jnp.maximum(j - 1, 0), c))
    tile_scr = pltpu.VMEM((ts, LANES), F32)
    pair_scr = pltpu.VMEM((2 * ts, LANES), F32)
    return pl.pallas_call(
        _attn_kernel,
        grid=(batch, ATTN_WIDTH // LANES, nt),
        in_specs=[cur, cur, cur, prev, prev,
                  pl.BlockSpec((4 * Q_BLOCK, 2 * Q_BLOCK), lambda b, c, j: (0, 0))],
        out_specs=cur,
        out_shape=jax.ShapeDtypeStruct((batch * seq, ATTN_WIDTH), F32),
        scratch_shapes=[tile_scr, tile_scr, pair_scr, pair_scr] * 2 + [tile_scr] * 6
        + [pltpu.VMEM((2 * Q_BLOCK, 2 * Q_BLOCK), F32)] * (2 * ATTN_PIPE_WIDTH)
        + [pltpu.VMEM((2 * Q_BLOCK, LANES), F32)] * (2 * ATTN_PIPE_WIDTH),
        compiler_params=_params(("parallel", "parallel", "parallel")),
        name="attention",
    )(q, k, v, k, v, _attn_bias())


def _mix(attn_ref, rec_ref, x_ref, ga_ref, gr_ref, wo_ref, gpost_ref, gpre_ref):
    na = _rms(attn_ref[...], ga_ref[...]).astype(BF16)
    nr = _rms(rec_ref[...], gr_ref[...]).astype(BF16)
    mixed = (jnp.dot(na, wo_ref[0:ATTN_WIDTH, :], preferred_element_type=F32)
             + jnp.dot(nr, wo_ref[ATTN_WIDTH:, :], preferred_element_type=F32))
    x1 = x_ref[...] + _rms(mixed, gpost_ref[...])
    return x1, _rms(x1, gpre_ref[...])


def _mix_specs(tm, d, row, const, w_out_shape):
    aw = ATTN_WIDTH
    return ([pl.BlockSpec((tm, aw), row)] * 2 + [pl.BlockSpec((tm, d), row),
            pl.BlockSpec((1, aw), const), pl.BlockSpec((1, LRU_WIDTH), const),
            _resident(w_out_shape, const), pl.BlockSpec((1, d), const),
            pl.BlockSpec((1, d), const)])


def _mix_args(attn, rec, x, ga, gr, w_out, gpost, gpre):
    d = x.shape[1]
    return [attn, rec, x, ga.reshape(1, ATTN_WIDTH), gr.reshape(1, LRU_WIDTH),
            w_out.astype(BF16), gpost.reshape(1, d), gpre.reshape(1, d)]


def _mix_router_kernel(*refs, n_exp):
    rw_ref, x1_ref, h2_ref, ri_ref, rp_ref = refs[8:]
    x1, hn = _mix(*refs[:8])
    x1_ref[...] = x1
    _store_row_tiles(h2_ref, hn)

    rw = rw_ref[...]
    h_hi, r_hi = hn.astype(BF16), rw.astype(BF16)
    h_lo = (hn - h_hi.astype(F32)).astype(BF16)
    r_lo = (rw - r_hi.astype(F32)).astype(BF16)
    logits = (jnp.dot(h_hi, r_hi, preferred_element_type=F32)
              + jnp.dot(h_hi, r_lo, preferred_element_type=F32)
              + jnp.dot(h_lo, r_hi, preferred_element_type=F32))
    tm = logits.shape[0]
    lane = lax.broadcasted_iota(jnp.int32, (tm, LANES), 1)
    logits = jnp.where(lane < n_exp, logits, -jnp.inf)
    m1 = jnp.max(logits, axis=-1, keepdims=True)
    i1 = jnp.min(jnp.where(logits == m1, lane, n_exp), axis=-1, keepdims=True)
    rest = jnp.where(lane == i1, -jnp.inf, logits)
    m2 = jnp.max(rest, axis=-1, keepdims=True)
    i2 = jnp.min(jnp.where(rest == m2, lane, n_exp), axis=-1, keepdims=True)
    e2 = jnp.exp(m2 - m1)
    p1 = 1.0 / (1.0 + e2)
    p2 = e2 / (1.0 + e2)
    ri_ref[...] = jnp.where(lane == 0, i1, jnp.where(lane == 1, i2, 0))
    rp_ref[...] = jnp.where(lane == 0, p1, jnp.where(lane == 1, p2, 0.0))


def _mix_out_router(attn, rec, x, ga, gr, w_out, gpost, gpre, router_w, tm):
    t, d = x.shape
    assert d == SUBLANES * LANES
    row = lambda i: (i, 0)
    const = lambda i: (0, 0)
    n_exp = router_w.shape[1]
    rw = jnp.pad(router_w.astype(F32), ((0, 0), (0, LANES - n_exp)))
    return pl.pallas_call(
        functools.partial(_mix_router_kernel, n_exp=n_exp),
        grid=(t // tm,),
        in_specs=_mix_specs(tm, d, row, const, w_out.shape) + [pl.BlockSpec(rw.shape, const)],
        out_specs=[pl.BlockSpec((tm, d), row), pl.BlockSpec((tm * SUBLANES, LANES), row)]
        + [pl.BlockSpec((tm, LANES), row)] * 2,
        out_shape=[jax.ShapeDtypeStruct((t, d), F32),
                   jax.ShapeDtypeStruct((t * SUBLANES, LANES), F32)]
        + [jax.ShapeDtypeStruct((t, LANES), jnp.int32), jax.ShapeDtypeStruct((t, LANES), F32)],
        compiler_params=_params(("parallel",)),
        name="mix_out_router",
    )(*_mix_args(attn, rec, x, ga, gr, w_out, gpost, gpre), rw)


def _swiglu_into(o_ref, x, wg_ref, wu_ref, wd_ref, f_chunk):
    d_ff = wg_ref.shape[1]
    for c in range(d_ff // f_chunk):
        sl = slice(c * f_chunk, (c + 1) * f_chunk)
        g = jnp.dot(x, wg_ref[:, sl], preferred_element_type=F32)
        u = jnp.dot(x, wu_ref[:, sl], preferred_element_type=F32)
        a = (g * jax.nn.sigmoid(g) * u).astype(BF16)
        y = jnp.dot(a, wd_ref[sl, :], preferred_element_type=F32)
        if c == 0:
            o_ref[...] = y
        else:
            o_ref[...] += y


def _ffn_kernel(te_ref, nv_ref, x_ref, wg_ref, wu_ref, wd_ref, o_ref, acc, *, f_chunk):
    i = pl.program_id(0)

    @pl.when(i < nv_ref[0])
    def _():
        _swiglu_into(acc, _load_row_tiles(x_ref).astype(BF16), wg_ref, wu_ref, wd_ref, f_chunk)
        _store_row_tiles(o_ref, acc[...])

    @pl.when(i >= nv_ref[0])
    def _():
        o_ref[...] = jnp.zeros_like(o_ref)


def _ffn(xs, tile_expert, n_valid, wg, wu, wd, tm, f_chunk):
    d, d_ff = wg.shape[1], wg.shape[2]
    rows = xs.shape[0] // SUBLANES
    assert d == SUBLANES * LANES and d_ff % f_chunk == 0 and rows % tm == 0
    wmap = lambda i, te, nv: (te[i], 0, 0)
    grid_spec = pltpu.PrefetchScalarGridSpec(
        num_scalar_prefetch=2,
        grid=(rows // tm,),
        in_specs=[
            pl.BlockSpec((tm * SUBLANES, LANES), lambda i, te, nv: (jnp.minimum(i, nv[0] - 1), 0)),
            _resident((None, d, d_ff), wmap),
            _resident((None, d, d_ff), wmap),
            _resident((None, d_ff, d), wmap),
        ],
        out_specs=pl.BlockSpec((tm * SUBLANES, LANES), lambda i, te, nv: (i, 0)),
        scratch_shapes=[pltpu.VMEM((tm, d), F32)],
    )
    return pl.pallas_call(
        functools.partial(_ffn_kernel, f_chunk=f_chunk),
        grid_spec=grid_spec,
        out_shape=jax.ShapeDtypeStruct(xs.shape, F32),
        compiler_params=_params(("arbitrary",)),
        name="ffn",
    )(tile_expert, n_valid, xs, wg, wu, wd)


def _mix_ffn_kernel(*refs, f_chunk):
    wg_ref, wu_ref, wd_ref, gffn_ref, o_ref, h2_scr, x1_scr = refs[8:]
    s = pl.program_id(0)

    def mix_into(cur):
        x1, hn = _mix(*refs[:8])
        x1_scr[cur] = x1
        h2_scr[cur] = hn.astype(BF16)

    def ffn_from(prev):
        _swiglu_into(o_ref, h2_scr[prev], wg_ref, wu_ref, wd_ref, f_chunk)
        o_ref[...] = x1_scr[prev] + _rms(o_ref[...], gffn_ref[...])

    @pl.when(s == 0)
    def _():
        mix_into(0)

    for parity in (0, 1):
        @pl.when(jnp.logical_and(lax.rem(s, 2) == parity, s > 0))
        def _():
            mix_into(parity)
            ffn_from(1 - parity)


def _mix_ffn(attn, rec, x, ga, gr, w_out, gpost, gpre, wg, wu, wd, gffn, tm, f_chunk):
    t, d = x.shape
    d_ff = wg.shape[1]
    n = t // tm
    assert d_ff % f_chunk == 0 and t % tm == 0
    row = lambda s: (jnp.minimum(s, n - 1), 0)
    const = lambda s: (0, 0)
    return pl.pallas_call(
        functools.partial(_mix_ffn_kernel, f_chunk=f_chunk),
        grid=(n + 1,),
        in_specs=_mix_specs(tm, d, row, const, w_out.shape)
        + [_resident((d, d_ff), const), _resident((d, d_ff), const), _resident((d_ff, d), const),
           pl.BlockSpec((1, d), const)],
        out_specs=pl.BlockSpec((tm, d), lambda s: (jnp.maximum(s - 1, 0), 0)),
        out_shape=jax.ShapeDtypeStruct((t, d), F32),
        scratch_shapes=[pltpu.VMEM((2, tm, d), BF16), pltpu.VMEM((2, tm, d), F32)],
        compiler_params=_params(("arbitrary",)),
        name="mix_ffn_dense",
    )(*_mix_args(attn, rec, x, ga, gr, w_out, gpost, gpre),
      wg.astype(BF16), wu.astype(BF16), wd.astype(BF16), gffn.reshape(1, d))


def _dispatch_kernel(pad_end_ref, d1_ref, d2_ref, h_ref, out_ref, zbuf, sem, zsem):
    tile = zbuf.shape[0]

    @pl.when(pl.program_id(0) == 0)
    def _():
        zbuf[...] = jnp.zeros_like(zbuf)

        def zero_copy(e):
            start = pl.multiple_of(pad_end_ref[e] * SUBLANES - tile, tile)
            return pltpu.make_async_copy(zbuf, out_ref.at[pl.ds(start, tile)], zsem)

        def has_rows(e):
            prev_end = pad_end_ref[e - 1] if e else 0
            return pad_end_ref[e] > prev_end

        for e in range(pad_end_ref.shape[0]):
            @pl.when(has_rows(e))
            def _():
                zero_copy(e).start()
        for e in range(pad_end_ref.shape[0]):
            @pl.when(has_rows(e))
            def _():
                zero_copy(e).wait()

    def row_copies(j):
        src = h_ref.at[_tile_rows(j)]
        return (pltpu.make_async_copy(src, out_ref.at[_tile_rows(d1_ref[j])], sem),
                pltpu.make_async_copy(src, out_ref.at[_tile_rows(d2_ref[j])], sem))

    def start(j, carry):
        for queue, c in enumerate(row_copies(j)):
            c.start(priority=queue)
        return carry

    def wait(j, carry):
        for c in row_copies(j):
            c.wait()
        return carry

    n = d1_ref.shape[0]
    lax.fori_loop(0, n, start, 0, unroll=8)
    lax.fori_loop(0, n, wait, 0, unroll=8)


def _dispatch(h, dest, pad_end, n_rows, tm, tile):
    t = h.shape[0] // SUBLANES
    assert tm % SMEM_BLOCK_WORDS == 0 and t % tm == 0
    nblk = t // tm
    grid_spec = pltpu.PrefetchScalarGridSpec(
        num_scalar_prefetch=1,
        grid=(nblk,),
        in_specs=[pl.BlockSpec((tm,), lambda i, pe: (i,), memory_space=pltpu.SMEM),
                  pl.BlockSpec((tm,), lambda i, pe: (i + nblk,), memory_space=pltpu.SMEM),
                  pl.BlockSpec((tm * SUBLANES, LANES), lambda i, pe: (i, 0))],
        out_specs=pl.BlockSpec(memory_space=pl.ANY),
        scratch_shapes=[pltpu.VMEM((tile * SUBLANES, LANES), h.dtype),
                        pltpu.SemaphoreType.DMA(()), pltpu.SemaphoreType.DMA(())],
    )
    return pl.pallas_call(
        _dispatch_kernel,
        grid_spec=grid_spec,
        out_shape=jax.ShapeDtypeStruct((n_rows * SUBLANES, LANES), h.dtype),
        compiler_params=_params(("arbitrary",)),
        name="moe_dispatch",
    )(pad_end, dest, dest, h)


def _combine_kernel(d1_ref, d2_ref, ys_ref, rp_ref, x_ref, g_ref, o_ref,
                    ya0, yb0, ya1, yb1, sems):
    s = pl.program_id(0)
    last = pl.num_programs(0) - 1
    bufs = ((ya0, yb0), (ya1, yb1))
    n = d1_ref.shape[0]

    def row_copies(j, pair, src_a, src_b):
        ya, yb = bufs[pair]
        dst = _tile_rows(j)
        return (pltpu.make_async_copy(ys_ref.at[_tile_rows(src_a)], ya.at[dst], sems.at[pair]),
                pltpu.make_async_copy(ys_ref.at[_tile_rows(src_b)], yb.at[dst], sems.at[pair]))

    def gather(pair):
        def start(j, carry):
            for queue, c in enumerate(row_copies(j, pair, d1_ref[j], d2_ref[j])):
                c.start(priority=queue)
            return carry
        lax.fori_loop(0, n, start, 0, unroll=8)

    def finish(pair):
        def wait(j, carry):
            for c in row_copies(j, pair, 0, 0):
                c.wait()
            return carry
        lax.fori_loop(0, n, wait, 0, unroll=8)
        ya, yb = bufs[pair]
        y = rp_ref[:, 0:1] * _load_row_tiles(ya) + rp_ref[:, 1:2] * _load_row_tiles(yb)
        o_ref[...] = x_ref[...] + _rms(y, g_ref[...])

    for parity in (0, 1):
        @pl.when(jnp.logical_and(lax.rem(s, 2) == parity, s < last))
        def _():
            gather(parity)

        @pl.when(jnp.logical_and(lax.rem(s, 2) == parity, s > 0))
        def _():
            finish(1 - parity)


def _combine(ys, dest, route_p, x, g, tm):
    t, d = x.shape
    assert d == SUBLANES * LANES and tm % SMEM_BLOCK_WORDS == 0 and t % tm == 0
    nblk = t // tm
    ahead = lambda off: (lambda s: (jnp.minimum(s, nblk - 1) + off,))
    behind = lambda s: (jnp.maximum(s - 1, 0), 0)
    buf = pltpu.VMEM((tm * SUBLANES, LANES), F32)
    return pl.pallas_call(
        _combine_kernel,
        grid=(nblk + 1,),
        in_specs=[pl.BlockSpec((tm,), ahead(0), memory_space=pltpu.SMEM),
                  pl.BlockSpec((tm,), ahead(nblk), memory_space=pltpu.SMEM),
                  pl.BlockSpec(memory_space=pl.ANY),
                  pl.BlockSpec((tm, LANES), behind), pl.BlockSpec((tm, d), behind),
                  pl.BlockSpec((1, d), lambda s: (0, 0))],
        out_specs=pl.BlockSpec((tm, d), behind),
        out_shape=jax.ShapeDtypeStruct((t, d), F32),
        scratch_shapes=[buf, buf, buf, buf, pltpu.SemaphoreType.DMA((2,))],
        compiler_params=_params(("arbitrary",)),
        name="moe_combine",
    )(dest, dest, ys, route_p, x, g.reshape(1, d))


def _routing_tables(route_i, n_exp, tm):
    t = route_i.shape[0]
    e_flat = jnp.concatenate([route_i[:, 0], route_i[:, 1]])
    onehot = (e_flat[:, None] == jnp.arange(n_exp, dtype=jnp.int32)[None, :]).astype(jnp.int32)
    counts = jnp.sum(onehot, axis=0)
    padded = ((counts + tm - 1) // tm) * tm
    pad_end = jnp.cumsum(padded)
    pad_start = pad_end - padded
    dest = jnp.sum((jnp.cumsum(onehot, axis=0) - onehot + pad_start[None, :]) * onehot, axis=1)

    n_rows = TOP_K * t + n_exp * tm
    n_tiles = n_rows // tm
    tile_first_row = jnp.arange(n_tiles, dtype=jnp.int32) * tm
    n_valid = pad_end[-1] // tm
    tile_first_row = jnp.minimum(tile_first_row, (n_valid - 1) * tm)
    tile_expert = jnp.sum((pad_end[None, :] <= tile_first_row[:, None]).astype(jnp.int32), axis=1)
    return (dest.astype(jnp.int32), pad_end.astype(jnp.int32),
            jnp.minimum(tile_expert, n_exp - 1).astype(jnp.int32),
            n_valid.astype(jnp.int32).reshape(1), n_rows)


def _tile(n, want):
    t = min(n, want)
    while n % t:
        t -= SUBLANES
    return t


def kernel(x, positions, pre_mix_g, w_in, conv_w, conv_b, w_rgate, b_rgate, w_igate, b_igate,
           lru_lambda, attn_out_g, lru_out_g, w_out, post_mix_g, pre_ffn_g, post_ffn_g,
           dense_w_gate, dense_w_up, dense_w_down, router_w, moe_w_gate, moe_w_up, moe_w_down):
    batch, seq, d = x.shape
    depth = w_in.shape[0]
    t = batch * seq
    d_ff = dense_w_gate.shape[-1]
    n_exp = moe_w_gate.shape[1]
    assert w_in.shape[2] == 3 * ATTN_WIDTH + 2 * LRU_WIDTH
    assert seq % (DILATED_PAIRS[-1][1] * Q_BLOCK) == 0

    tm_mix = _tile(t, 256)
    tm_ffn = _tile(t, 512)
    tm_moe = tm_ffn
    ts_proj = _tile(seq, 512)
    tm_rows = SMEM_BLOCK_WORDS
    f_chunk = 1792 if d_ff % 1792 == 0 else (512 if d_ff % 512 == 0 else d_ff)

    xf = x.reshape(t, d).astype(F32)
    pos = positions.reshape(t, 1).astype(jnp.int32)

    for l in range(depth):
        q, k, v, rec = _in_proj(xf, pos, pre_mix_g[l].reshape(1, d), w_in[l].astype(BF16),
                                conv_w[l], conv_b[l], w_rgate[l], b_rgate[l], w_igate[l],
                                b_igate[l], lru_lambda[l], batch, seq, ts_proj)
        attn = _attention(q, k, v, batch, seq)
        j = l // 2
        if l % 2 == 0:
            xf = _mix_ffn(attn, rec, xf, attn_out_g[l], lru_out_g[l], w_out[l], post_mix_g[l],
                          pre_ffn_g[l], dense_w_gate[j], dense_w_up[j], dense_w_down[j],
                          post_ffn_g[l], tm_ffn, f_chunk)
        else:
            x1, h2, route_i, route_p = _mix_out_router(
                attn, rec, xf, attn_out_g[l], lru_out_g[l], w_out[l], post_mix_g[l],
                pre_ffn_g[l], router_w[j], tm_mix)
            dest, pad_end, tile_expert, n_valid, n_rows = _routing_tables(route_i, n_exp, tm_moe)
            xs = _dispatch(h2, dest, pad_end, n_rows, tm_rows, tm_moe)
            ys = _ffn(xs, tile_expert, n_valid, moe_w_gate[j].astype(BF16),
                      moe_w_up[j].astype(BF16), moe_w_down[j].astype(BF16), tm_moe, f_chunk)
            xf = _combine(ys, dest, route_p, x1, post_ffn_g[l], tm_rows)
    return xf.reshape(batch, seq, d).astype(x.dtype)
```

```python
import functools
import math

import jax
import jax.numpy as jnp
from jax import lax
from jax.experimental import pallas as pl
from jax.experimental.pallas import tpu as pltpu

ATTN_HEADS = 8
HEAD_DIM = 64
ATTN_WIDTH = ATTN_HEADS * HEAD_DIM
LRU_BLOCKS = 8
LRU_BLOCK_W = 64
LRU_WIDTH = LRU_BLOCKS * LRU_BLOCK_W
DILATED_PAIRS = ((128, 1), (512, 4), (2048, 16))
Q_BLOCK = 128
ROT_DIM = HEAD_DIM // 4
ROT_HALF = ROT_DIM // 2
ROPE_THETA = 500000.0
ATTN_SCALE = 1.0 / math.sqrt(HEAD_DIM)
NEG_INF = -1e30
CONV_W = 4
LRU_C = 8.0
TOP_K = 2
RMS_EPS = 1e-6

LANES = 128
SUBLANES = 8
SMEM_BLOCK_WORDS = 1024
VMEM_LIMIT_BYTES = 56 * 1024 * 1024

F32 = jnp.float32
BF16 = jnp.bfloat16


def _params(semantics):
    return pltpu.CompilerParams(dimension_semantics=semantics,
                                vmem_limit_bytes=VMEM_LIMIT_BYTES)


def _resident(block_shape, index_map):
    return pl.BlockSpec(block_shape, index_map, pipeline_mode=pl.Buffered(1))


def _rms(x, g):
    var = jnp.mean(x * x, axis=-1, keepdims=True)
    return x * lax.rsqrt(var + RMS_EPS) * g


def _tile_rows(r):
    return pl.ds(pl.multiple_of(r * SUBLANES, SUBLANES), SUBLANES)


def _store_row_tiles(ref, x):
    rows = x.shape[0]
    for c in range(x.shape[1] // LANES):
        ref[pl.ds(c, rows, stride=SUBLANES), :] = x[:, c * LANES:(c + 1) * LANES]


def _load_row_tiles(ref):
    rows = ref.shape[0] // SUBLANES
    return jnp.concatenate([ref[pl.ds(c, rows, stride=SUBLANES), :] for c in range(SUBLANES)],
                           axis=1)


def _lru_block(u, gate, cw_ref, cb_ref, wr_ref, br_ref, wi_ref, bi_ref, lam_ref,
               ubuf, hcarry, a_scr, b_scr):
    ts, c = u.shape
    ubuf[SUBLANES:SUBLANES + ts, :] = u
    ext = ubuf[...]
    uc = cb_ref[...]
    for tap in range(CONV_W):
        back = CONV_W - 1 - tap
        shifted = pltpu.roll(ext, back, 0) if back else ext
        uc = uc + shifted[SUBLANES:SUBLANES + ts, :] * cw_ref[tap:tap + 1, :]
    ubuf[0:SUBLANES, :] = ubuf[ts:ts + SUBLANES, :]

    ucb = uc.astype(BF16)
    r = jax.nn.sigmoid(jnp.dot(ucb, wr_ref[...], preferred_element_type=F32) + br_ref[...])
    ig = jax.nn.sigmoid(jnp.dot(ucb, wi_ref[...], preferred_element_type=F32) + bi_ref[...])
    nl = -lam_ref[...]
    softplus = jnp.maximum(nl, 0.0) + jnp.log1p(jnp.exp(-jnp.abs(nl)))
    log_a = -LRU_C * r * softplus
    a = jnp.exp(log_a)
    mult = jnp.sqrt(1.0 - a * a)
    b = mult * ig * uc

    groups = ts // SUBLANES
    a3 = a.reshape(groups, SUBLANES, c)
    b3 = b.reshape(groups, SUBLANES, c)
    sub = lax.broadcasted_iota(jnp.int32, (groups, SUBLANES, c), 1)
    shift = 1
    while shift < SUBLANES:
        valid = sub >= shift
        a_sh = jnp.where(valid, pltpu.roll(a3, shift, 1), 1.0)
        b_sh = jnp.where(valid, pltpu.roll(b3, shift, 1), 0.0)
        b3 = a3 * b_sh + b3
        a3 = a3 * a_sh
        shift *= 2
    a_scr[...] = a3.reshape(ts, c)
    b_scr[...] = b3.reshape(ts, c)

    h_prev = hcarry[...]
    for g in range(groups):
        rows = slice(g * SUBLANES, (g + 1) * SUBLANES)
        h = a_scr[rows, :] * h_prev + b_scr[rows, :]
        b_scr[rows, :] = h
        h_prev = jnp.broadcast_to(h[SUBLANES - 1:SUBLANES, :], (SUBLANES, c))
    hcarry[...] = h_prev
    return jax.nn.gelu(gate) * b_scr[...]


def _in_proj_kernel(x_ref, pos_ref, g_ref, w_ref, rope_ref,
                    cw_ref, cb_ref, wr_ref, br_ref, wi_ref, bi_ref, lam_ref,
                    q_ref, k_ref, v_ref, rec_ref, ubuf, hcarry, a_scr, b_scr):
    @pl.when(pl.program_id(1) == 0)
    def _():
        ubuf[0:SUBLANES, :] = jnp.zeros((SUBLANES, ubuf.shape[1]), F32)
        hcarry[...] = jnp.zeros_like(hcarry)

    ang = pos_ref[...].astype(F32) * rope_ref[0:1, :]
    cs = jnp.cos(ang)
    sn = jnp.sin(ang)
    s_lo = sn * rope_ref[1:2, :]
    s_hi = sn * rope_ref[2:3, :]

    def rotary(z):
        return (z * cs + pltpu.roll(z, ROT_HALF, 1) * s_lo
                + pltpu.roll(z, LANES - ROT_HALF, 1) * s_hi)

    h = _rms(x_ref[...], g_ref[...]).astype(BF16)
    w = ATTN_WIDTH
    u = jnp.dot(h, w_ref[:, 3 * w:3 * w + LRU_WIDTH], preferred_element_type=F32)
    gate = jnp.dot(h, w_ref[:, 3 * w + LRU_WIDTH:], preferred_element_type=F32)
    rec_ref[...] = _lru_block(u, gate, cw_ref, cb_ref, wr_ref, br_ref, wi_ref, bi_ref, lam_ref,
                              ubuf, hcarry, a_scr, b_scr)
    for sec, out in ((0, q_ref), (1, k_ref)):
        z = jnp.dot(h, w_ref[:, sec * w:(sec + 1) * w], preferred_element_type=F32)
        for cb in range(w // LANES):
            sl = slice(cb * LANES, (cb + 1) * LANES)
            out[:, sl] = rotary(z[:, sl]).astype(BF16)
    v_ref[...] = jnp.dot(h, w_ref[:, 2 * w:3 * w], preferred_element_type=F32).astype(BF16)


def _rope_table():
    lane = jnp.arange(LANES, dtype=jnp.int32) % HEAD_DIM
    inv_freq = ROPE_THETA ** (-jnp.arange(ROT_HALF, dtype=F32) / ROT_HALF)
    freq = jnp.where(lane < ROT_DIM, inv_freq[lane % ROT_HALF], 0.0)
    lo = jnp.where((lane >= ROT_HALF) & (lane < ROT_DIM), 1.0, 0.0)
    hi = jnp.where(lane < ROT_HALF, -1.0, 0.0)
    return jnp.stack([freq, lo, hi]).astype(F32)


def _block_diag(w):
    nb, bw, _ = w.shape
    eye = jnp.eye(nb, dtype=w.dtype)
    return jnp.einsum("gij,gh->gihj", w, eye).reshape(nb * bw, nb * bw)


def _in_proj(x, pos, g, w_in, conv_w, conv_b, w_r, b_r, w_i, b_i, lam, batch, seq, ts):
    t, d = x.shape
    ncol = w_in.shape[1]
    c = LRU_WIDTH
    nblk = seq // ts
    row = lambda b, j: (b * nblk + j, 0)
    const = lambda b, j: (0, 0)
    vec = lambda a: a.reshape(1, c).astype(F32)
    return pl.pallas_call(
        _in_proj_kernel,
        grid=(batch, nblk),
        in_specs=[
            pl.BlockSpec((ts, d), row),
            pl.BlockSpec((ts, 1), row),
            pl.BlockSpec((1, d), const),
            _resident((d, ncol), const),
            pl.BlockSpec((3, LANES), const),
            pl.BlockSpec((CONV_W, c), const),
            pl.BlockSpec((1, c), const),
            pl.BlockSpec((c, c), const),
            pl.BlockSpec((1, c), const),
            pl.BlockSpec((c, c), const),
            pl.BlockSpec((1, c), const),
            pl.BlockSpec((1, c), const),
        ],
        out_specs=[pl.BlockSpec((ts, ATTN_WIDTH), row)] * 3 + [pl.BlockSpec((ts, c), row)],
        out_shape=[jax.ShapeDtypeStruct((t, ATTN_WIDTH), BF16)] * 3
        + [jax.ShapeDtypeStruct((t, c), F32)],
        scratch_shapes=[pltpu.VMEM((ts + SUBLANES, c), F32), pltpu.VMEM((SUBLANES, c), F32),
                        pltpu.VMEM((ts, c), F32), pltpu.VMEM((ts, c), F32)],
        compiler_params=_params(("parallel", "arbitrary")),
        name="in_proj_lru",
    )(x, pos, g, w_in, _rope_table(), conv_w.astype(F32), vec(conv_b),
      _block_diag(w_r).astype(BF16), vec(b_r), _block_diag(w_i).astype(BF16), vec(b_i), vec(lam))


ATTN_TILE = DILATED_PAIRS[-1][1] * Q_BLOCK


def _rows(start, size, stride):
    return pl.ds(start, size) if stride == 1 else pl.ds(start, size, stride=stride)


ATTN_PIPE_WIDTH = 2


def _attn_kernel(q_ref, kc_ref, vc_ref, kp_ref, vp_ref, bias_ref, o_ref,
                 q0_scr, q1_scr, k_scr, v_scr, q0c, q1c, kc, vc,
                 ob0, ob1, ob2, ls0, ls1, ls2, *slots):
    s_scrs, m_scrs = slots[:len(slots) // 2], slots[len(slots) // 2:]
    j = pl.program_id(2)
    ts = q_ref.shape[0]
    lane = lax.broadcasted_iota(jnp.int32, (1, LANES), 1)
    head0 = lane < HEAD_DIM

    qf = q_ref[...].astype(F32) * ATTN_SCALE
    q0_scr[...] = jnp.where(head0, qf, 0.0)
    q1_scr[...] = jnp.where(head0, 0.0, qf)
    k_scr[0:ts, :] = kp_ref[...].astype(F32)
    k_scr[ts:2 * ts, :] = kc_ref[...].astype(F32)
    v_scr[0:ts, :] = vp_ref[...].astype(F32)
    v_scr[ts:2 * ts, :] = vc_ref[...].astype(F32)

    nt = (((1,), (1,)), ((), ()))
    units = ts // Q_BLOCK
    width = len(s_scrs) // 2
    groups = units // width
    assert groups % 2 == 0 and groups >= 2
    pre = DILATED_PAIRS[1][1]
    assert all(dil == 1 or dil % pre == 0 for _, dil in DILATED_PAIRS)

    def regroup(c, carry):
        for src, dst in ((q0_scr, q0c), (q1_scr, q1c), (k_scr, kc), (v_scr, vc)):
            n = src.shape[0] // pre
            dst[pl.ds(pl.multiple_of(c * n, Q_BLOCK), n), :] = src[_rows(c, n, pre), :]
        return carry

    lax.fori_loop(0, pre, regroup, 0)

    for (_, dil), ob, ls in zip(DILATED_PAIRS, (ob0, ob1, ob2), (ls0, ls1, ls2)):
        shift = dil.bit_length() - 1
        if dil == 1:
            q0_src, q1_src, k_src, v_src = q0_scr, q1_scr, k_scr, v_scr
        else:
            q0_src, q1_src, k_src, v_src = q0c, q1c, kc, vc

        def rows_of(u, dil=dil, shift=shift):
            cls = jnp.bitwise_and(u, dil - 1)
            blk = lax.shift_right_logical(u, shift)
            q_start = cls + dil * Q_BLOCK * blk
            out_rows = _rows(q_start, Q_BLOCK, dil)
            if dil == 1:
                q_rows = out_rows
                kv_rows = _rows(ts + q_start - Q_BLOCK, 2 * Q_BLOCK, 1)
            else:
                step = dil // pre
                c4 = jnp.bitwise_and(cls, pre - 1)
                sub = lax.shift_right_logical(cls, pre.bit_length() - 1)
                q_rows = _rows(c4 * (ts // pre) + sub + step * Q_BLOCK * blk, Q_BLOCK, step)
                kv_rows = _rows(c4 * (2 * ts // pre) + sub
                                + step * (ts // dil + (blk - 1) * Q_BLOCK), 2 * Q_BLOCK, step)
            return blk, out_rows, q_rows, kv_rows

        def scores(u, slot, q0_src=q0_src, q1_src=q1_src, k_src=k_src):
            blk, _, q_rows, kv_rows = rows_of(u)
            lhs = jnp.concatenate([q0_src[q_rows, :], q1_src[q_rows, :]], axis=0).astype(BF16)
            k2 = k_src[kv_rows, :].astype(BF16)
            no_prev = jnp.logical_and(j == 0, blk == 0).astype(jnp.int32)
            bias = bias_ref[pl.ds(no_prev * 2 * Q_BLOCK, 2 * Q_BLOCK), :]
            s = lax.dot_general(lhs, k2, nt, preferred_element_type=F32) + bias
            s_scrs[slot][...] = s
            m_scrs[slot][...] = jnp.broadcast_to(jnp.max(s, axis=-1, keepdims=True),
                                                 (2 * Q_BLOCK, LANES))

        def finish(u, slot, ob=ob, ls=ls, v_src=v_src):
            _, out_rows, _, kv_rows = rows_of(u)
            m = m_scrs[slot][...]
            p = jnp.exp(s_scrs[slot][...] - jnp.concatenate([m, m], axis=1))
            den = jnp.sum(p, axis=-1, keepdims=True)
            v2 = v_src[kv_rows, :].astype(BF16)
            o = jnp.dot(p.astype(BF16), v2, preferred_element_type=F32) / den
            lse = m + jnp.log(den)
            ob[out_rows, :] = jnp.where(head0, o[0:Q_BLOCK], o[Q_BLOCK:])
            ls[out_rows, :] = jnp.where(head0, lse[0:Q_BLOCK], lse[Q_BLOCK:])

        def scores_group(g, half):
            for w in range(width):
                scores(g * width + w, half * width + w)

        def finish_group(g, half):
            for w in range(width):
                finish(g * width + w, half * width + w)

        scores_group(jnp.int32(0), 0)
        for g in range(groups - 1):
            scores_group(jnp.int32(g + 1), (g + 1) % 2)
            finish_group(jnp.int32(g), g % 2)
        finish_group(jnp.int32(groups - 1), (groups - 1) % 2)

    chunk = 2 * Q_BLOCK

    def mix(c, carry):
        r = pl.ds(pl.multiple_of(c * chunk, chunk), chunk)
        la, lb, lc = ls0[r, :], ls1[r, :], ls2[r, :]
        m = jnp.maximum(jnp.maximum(la, lb), lc)
        ea, eb, ec = jnp.exp(la - m), jnp.exp(lb - m), jnp.exp(lc - m)
        o_ref[r, :] = (ea * ob0[r, :] + eb * ob1[r, :] + ec * ob2[r, :]) / (ea + eb + ec)
        return carry

    lax.fori_loop(0, ts // chunk, mix, 0)


def _attn_bias():
    qi = jnp.arange(2 * Q_BLOCK, dtype=jnp.int32)[:, None] % Q_BLOCK
    kj = jnp.arange(2 * Q_BLOCK, dtype=jnp.int32)[None, :]
    delta = Q_BLOCK + qi - kj
    band = (delta >= 0) & (delta <= Q_BLOCK)
    with_prev = jnp.where(band, 0.0, NEG_INF)
    no_prev = jnp.where(band & (kj >= Q_BLOCK), 0.0, NEG_INF)
    return jnp.concatenate([with_prev, no_prev], axis=0).astype(F32)


def _attention(q, k, v, batch, seq):
    ts = ATTN_TILE
    assert seq % ts == 0
    nt = seq // ts
    cur = pl.BlockSpec((ts, LANES), lambda b, c, j: (b * nt + j, c))
    prev = pl.BlockSpec((ts, LANES), lambda b, c, j: (b * nt + jnp.maximum(j - 1, 0), c))
    tile_scr = pltpu.VMEM((ts, LANES), F32)
    pair_scr = pltpu.VMEM((2 * ts, LANES), F32)
    return pl.pallas_call(
        _attn_kernel,
        grid=(batch, ATTN_WIDTH // LANES, nt),
        in_specs=[cur, cur, cur, prev, prev,
                  pl.BlockSpec((4 * Q_BLOCK, 2 * Q_BLOCK), lambda b, c, j: (0, 0))],
        out_specs=cur,
        out_shape=jax.ShapeDtypeStruct((batch * seq, ATTN_WIDTH), F32),
        scratch_shapes=[tile_scr, tile_scr, pair_scr, pair_scr] * 2 + [tile_scr] * 6
        + [pltpu.VMEM((2 * Q_BLOCK, 2 * Q_BLOCK), F32)] * (2 * ATTN_PIPE_WIDTH)
        + [pltpu.VMEM((2 * Q_BLOCK, LANES), F32)] * (2 * ATTN_PIPE_WIDTH),
        compiler_params=_params(("parallel", "parallel", "parallel")),
        name="attention",
    )(q, k, v, k, v, _attn_bias())


def _mix(attn_ref, rec_ref, x_ref, ga_ref, gr_ref, wo_ref, gpost_ref, gpre_ref):
    na = _rms(attn_ref[...], ga_ref[...]).astype(BF16)
    nr = _rms(rec_ref[...], gr_ref[...]).astype(BF16)
    mixed = (jnp.dot(na, wo_ref[0:ATTN_WIDTH, :], preferred_element_type=F32)
             + jnp.dot(nr, wo_ref[ATTN_WIDTH:, :], preferred_element_type=F32))
    x1 = x_ref[...] + _rms(mixed, gpost_ref[...])
    return x1, _rms(x1, gpre_ref[...])


def _mix_specs(tm, d, row, const, w_out_shape):
    aw = ATTN_WIDTH
    return ([pl.BlockSpec((tm, aw), row)] * 2 + [pl.BlockSpec((tm, d), row),
            pl.BlockSpec((1, aw), const), pl.BlockSpec((1, LRU_WIDTH), const),
            _resident(w_out_shape, const), pl.BlockSpec((1, d), const),
            pl.BlockSpec((1, d), const)])


def _mix_args(attn, rec, x, ga, gr, w_out, gpost, gpre):
    d = x.shape[1]
    return [attn, rec, x, ga.reshape(1, ATTN_WIDTH), gr.reshape(1, LRU_WIDTH),
            w_out.astype(BF16), gpost.reshape(1, d), gpre.reshape(1, d)]


def _mix_router_kernel(*refs, n_exp):
    rw_ref, x1_ref, h2_ref, ri_ref, rp_ref = refs[8:]
    x1, hn = _mix(*refs[:8])
    x1_ref[...] = x1
    _store_row_tiles(h2_ref, hn)

    rw = rw_ref[...]
    h_hi, r_hi = hn.astype(BF16), rw.astype(BF16)
    h_lo = (hn - h_hi.astype(F32)).astype(BF16)
    r_lo = (rw - r_hi.astype(F32)).astype(BF16)
    logits = (jnp.dot(h_hi, r_hi, preferred_element_type=F32)
              + jnp.dot(h_hi, r_lo, preferred_element_type=F32)
              + jnp.dot(h_lo, r_hi, preferred_element_type=F32))
    tm = logits.shape[0]
    lane = lax.broadcasted_iota(jnp.int32, (tm, LANES), 1)
    logits = jnp.where(lane < n_exp, logits, -jnp.inf)
    m1 = jnp.max(logits, axis=-1, keepdims=True)
    i1 = jnp.min(jnp.where(logits == m1, lane, n_exp), axis=-1, keepdims=True)
    rest = jnp.where(lane == i1, -jnp.inf, logits)
    m2 = jnp.max(rest, axis=-1, keepdims=True)
    i2 = jnp.min(jnp.where(rest == m2, lane, n_exp), axis=-1, keepdims=True)
    e2 = jnp.exp(m2 - m1)
    p1 = 1.0 / (1.0 + e2)
    p2 = e2 / (1.0 + e2)
    ri_ref[...] = jnp.where(lane == 0, i1, jnp.where(lane == 1, i2, 0))
    rp_ref[...] = jnp.where(lane == 0, p1, jnp.where(lane == 1, p2, 0.0))


def _mix_out_router(attn, rec, x, ga, gr, w_out, gpost, gpre, router_w, tm):
    t, d = x.shape
    assert d == SUBLANES * LANES
    row = lambda i: (i, 0)
    const = lambda i: (0, 0)
    n_exp = router_w.shape[1]
    rw = jnp.pad(router_w.astype(F32), ((0, 0), (0, LANES - n_exp)))
    return pl.pallas_call(
        functools.partial(_mix_router_kernel, n_exp=n_exp),
        grid=(t // tm,),
        in_specs=_mix_specs(tm, d, row, const, w_out.shape) + [pl.BlockSpec(rw.shape, const)],
        out_specs=[pl.BlockSpec((tm, d), row), pl.BlockSpec((tm * SUBLANES, LANES), row)]
        + [pl.BlockSpec((tm, LANES), row)] * 2,
        out_shape=[jax.ShapeDtypeStruct((t, d), F32),
                   jax.ShapeDtypeStruct((t * SUBLANES, LANES), F32)]
        + [jax.ShapeDtypeStruct((t, LANES), jnp.int32), jax.ShapeDtypeStruct((t, LANES), F32)],
        compiler_params=_params(("parallel",)),
        name="mix_out_router",
    )(*_mix_args(attn, rec, x, ga, gr, w_out, gpost, gpre), rw)


def _swiglu_into(o_ref, x, wg_ref, wu_ref, wd_ref, f_chunk):
    d_ff = wg_ref.shape[1]
    for c in range(d_ff // f_chunk):
        sl = slice(c * f_chunk, (c + 1) * f_chunk)
        g = jnp.dot(x, wg_ref[:, sl], preferred_element_type=F32)
        u = jnp.dot(x, wu_ref[:, sl], preferred_element_type=F32)
        a = (g * jax.nn.sigmoid(g) * u).astype(BF16)
        y = jnp.dot(a, wd_ref[sl, :], preferred_element_type=F32)
        if c == 0:
            o_ref[...] = y
        else:
            o_ref[...] += y


def _ffn_kernel(te_ref, nv_ref, x_ref, wg_ref, wu_ref, wd_ref, o_ref, acc, *, f_chunk):
    i = pl.program_id(0)

    @pl.when(i < nv_ref[0])
    def _():
        _swiglu_into(acc, _load_row_tiles(x_ref).astype(BF16), wg_ref, wu_ref, wd_ref, f_chunk)
        _store_row_tiles(o_ref, acc[...])

    @pl.when(i >= nv_ref[0])
    def _():
        o_ref[...] = jnp.zeros_like(o_ref)


def _ffn(xs, tile_expert, n_valid, wg, wu, wd, tm, f_chunk):
    d, d_ff = wg.shape[1], wg.shape[2]
    rows = xs.shape[0] // SUBLANES
    assert d == SUBLANES * LANES and d_ff % f_chunk == 0 and rows % tm == 0
    wmap = lambda i, te, nv: (te[i], 0, 0)
    grid_spec = pltpu.PrefetchScalarGridSpec(
        num_scalar_prefetch=2,
        grid=(rows // tm,),
        in_specs=[
            pl.BlockSpec((tm * SUBLANES, LANES), lambda i, te, nv: (jnp.minimum(i, nv[0] - 1), 0)),
            _resident((None, d, d_ff), wmap),
            _resident((None, d, d_ff), wmap),
            _resident((None, d_ff, d), wmap),
        ],
        out_specs=pl.BlockSpec((tm * SUBLANES, LANES), lambda i, te, nv: (i, 0)),
        scratch_shapes=[pltpu.VMEM((tm, d), F32)],
    )
    return pl.pallas_call(
        functools.partial(_ffn_kernel, f_chunk=f_chunk),
        grid_spec=grid_spec,
        out_shape=jax.ShapeDtypeStruct(xs.shape, F32),
        compiler_params=_params(("arbitrary",)),
        name="ffn",
    )(tile_expert, n_valid, xs, wg, wu, wd)


def _mix_ffn_kernel(*refs, f_chunk):
    wg_ref, wu_ref, wd_ref, gffn_ref, o_ref, h2_scr, x1_scr = refs[8:]
    s = pl.program_id(0)

    def mix_into(cur):
        x1, hn = _mix(*refs[:8])
        x1_scr[cur] = x1
        h2_scr[cur] = hn.astype(BF16)

    def ffn_from(prev):
        _swiglu_into(o_ref, h2_scr[prev], wg_ref, wu_ref, wd_ref, f_chunk)
        o_ref[...] = x1_scr[prev] + _rms(o_ref[...], gffn_ref[...])

    @pl.when(s == 0)
    def _():
        mix_into(0)

    for parity in (0, 1):
        @pl.when(jnp.logical_and(lax.rem(s, 2) == parity, s > 0))
        def _():
            mix_into(parity)
            ffn_from(1 - parity)


def _mix_ffn(attn, rec, x, ga, gr, w_out, gpost, gpre, wg, wu, wd, gffn, tm, f_chunk):
    t, d = x.shape
    d_ff = wg.shape[1]
    n = t // tm
    assert d_ff % f_chunk == 0 and t % tm == 0
    row = lambda s: (jnp.minimum(s, n - 1), 0)
    const = lambda s: (0, 0)
    return pl.pallas_call(
        functools.partial(_mix_ffn_kernel, f_chunk=f_chunk),
        grid=(n + 1,),
        in_specs=_mix_specs(tm, d, row, const, w_out.shape)
        + [_resident((d, d_ff), const), _resident((d, d_ff), const), _resident((d_ff, d), const),
           pl.BlockSpec((1, d), const)],
        out_specs=pl.BlockSpec((tm, d), lambda s: (jnp.maximum(s - 1, 0), 0)),
        out_shape=jax.ShapeDtypeStruct((t, d), F32),
        scratch_shapes=[pltpu.VMEM((2, tm, d), BF16), pltpu.VMEM((2, tm, d), F32)],
        compiler_params=_params(("arbitrary",)),
        name="mix_ffn_dense",
    )(*_mix_args(attn, rec, x, ga, gr, w_out, gpost, gpre),
      wg.astype(BF16), wu.astype(BF16), wd.astype(BF16), gffn.reshape(1, d))


def _dispatch_kernel(pad_end_ref, d1_ref, d2_ref, h_ref, out_ref, zbuf, sem, zsem):
    tile = zbuf.shape[0]

    @pl.when(pl.program_id(0) == 0)
    def _():
        zbuf[...] = jnp.zeros_like(zbuf)

        def zero_copy(e):
            start = pl.multiple_of(pad_end_ref[e] * SUBLANES - tile, tile)
            return pltpu.make_async_copy(zbuf, out_ref.at[pl.ds(start, tile)], zsem)

        def has_rows(e):
            prev_end = pad_end_ref[e - 1] if e else 0
            return pad_end_ref[e] > prev_end

        for e in range(pad_end_ref.shape[0]):
            @pl.when(has_rows(e))
            def _():
                zero_copy(e).start()
        for e in range(pad_end_ref.shape[0]):
            @pl.when(has_rows(e))
            def _():
                zero_copy(e).wait()

    def row_copies(j):
        src = h_ref.at[_tile_rows(j)]
        return (pltpu.make_async_copy(src, out_ref.at[_tile_rows(d1_ref[j])], sem),
                pltpu.make_async_copy(src, out_ref.at[_tile_rows(d2_ref[j])], sem))

    def start(j, carry):
        for queue, c in enumerate(row_copies(j)):
            c.start(priority=queue)
        return carry

    def wait(j, carry):
        for c in row_copies(j):
            c.wait()
        return carry

    n = d1_ref.shape[0]
    lax.fori_loop(0, n, start, 0, unroll=8)
    lax.fori_loop(0, n, wait, 0, unroll=8)


def _dispatch(h, dest, pad_end, n_rows, tm, tile):
    t = h.shape[0] // SUBLANES
    assert tm % SMEM_BLOCK_WORDS == 0 and t % tm == 0
    nblk = t // tm
    grid_spec = pltpu.PrefetchScalarGridSpec(
        num_scalar_prefetch=1,
        grid=(nblk,),
        in_specs=[pl.BlockSpec((tm,), lambda i, pe: (i,), memory_space=pltpu.SMEM),
                  pl.BlockSpec((tm,), lambda i, pe: (i + nblk,), memory_space=pltpu.SMEM),
                  pl.BlockSpec((tm * SUBLANES, LANES), lambda i, pe: (i, 0))],
        out_specs=pl.BlockSpec(memory_space=pl.ANY),
        scratch_shapes=[pltpu.VMEM((tile * SUBLANES, LANES), h.dtype),
                        pltpu.SemaphoreType.DMA(()), pltpu.SemaphoreType.DMA(())],
    )
    return pl.pallas_call(
        _dispatch_kernel,
        grid_spec=grid_spec,
        out_shape=jax.ShapeDtypeStruct((n_rows * SUBLANES, LANES), h.dtype),
        compiler_params=_params(("arbitrary",)),
        name="moe_dispatch",
    )(pad_end, dest, dest, h)


def _combine_kernel(d1_ref, d2_ref, ys_ref, rp_ref, x_ref, g_ref, o_ref,
                    ya0, yb0, ya1, yb1, sems):
    s = pl.program_id(0)
    last = pl.num_programs(0) - 1
    bufs = ((ya0, yb0), (ya1, yb1))
    n = d1_ref.shape[0]

    def row_copies(j, pair, src_a, src_b):
        ya, yb = bufs[pair]
        dst = _tile_rows(j)
        return (pltpu.make_async_copy(ys_ref.at[_tile_rows(src_a)], ya.at[dst], sems.at[pair]),
                pltpu.make_async_copy(ys_ref.at[_tile_rows(src_b)], yb.at[dst], sems.at[pair]))

    def gather(pair):
        def start(j, carry):
            for queue, c in enumerate(row_copies(j, pair, d1_ref[j], d2_ref[j])):
                c.start(priority=queue)
            return carry
        lax.fori_loop(0, n, start, 0, unroll=8)

    def finish(pair):
        def wait(j, carry):
            for c in row_copies(j, pair, 0, 0):
                c.wait()
            return carry
        lax.fori_loop(0, n, wait, 0, unroll=8)
        ya, yb = bufs[pair]
        y = rp_ref[:, 0:1] * _load_row_tiles(ya) + rp_ref[:, 1:2] * _load_row_tiles(yb)
        o_ref[...] = x_ref[...] + _rms(y, g_ref[...])

    for parity in (0, 1):
        @pl.when(jnp.logical_and(lax.rem(s, 2) == parity, s < last))
        def _():
            gather(parity)

        @pl.when(jnp.logical_and(lax.rem(s, 2) == parity, s > 0))
        def _():
            finish(1 - parity)


def _combine(ys, dest, route_p, x, g, tm):
    t, d = x.shape
    assert d == SUBLANES * LANES and tm % SMEM_BLOCK_WORDS == 0 and t % tm == 0
    nblk = t // tm
    ahead = lambda off: (lambda s: (jnp.minimum(s, nblk - 1) + off,))
    behind = lambda s: (jnp.maximum(s - 1, 0), 0)
    buf = pltpu.VMEM((tm * SUBLANES, LANES), F32)
    return pl.pallas_call(
        _combine_kernel,
        grid=(nblk + 1,),
        in_specs=[pl.BlockSpec((tm,), ahead(0), memory_space=pltpu.SMEM),
                  pl.BlockSpec((tm,), ahead(nblk), memory_space=pltpu.SMEM),
                  pl.BlockSpec(memory_space=pl.ANY),
                  pl.BlockSpec((tm, LANES), behind), pl.BlockSpec((tm, d), behind),
                  pl.BlockSpec((1, d), lambda s: (0, 0))],
        out_specs=pl.BlockSpec((tm, d), behind),
        out_shape=jax.ShapeDtypeStruct((t, d), F32),
        scratch_shapes=[buf, buf, buf, buf, pltpu.SemaphoreType.DMA((2,))],
        compiler_params=_params(("arbitrary",)),
        name="moe_combine",
    )(dest, dest, ys, route_p, x, g.reshape(1, d))


def _routing_tables(route_i, n_exp, tm):
    t = route_i.shape[0]
    e_flat = jnp.concatenate([route_i[:, 0], route_i[:, 1]])
    onehot = (e_flat[:, None] == jnp.arange(n_exp, dtype=jnp.int32)[None, :]).astype(jnp.int32)
    counts = jnp.sum(onehot, axis=0)
    padded = ((counts + tm - 1) // tm) * tm
    pad_end = jnp.cumsum(padded)
    pad_start = pad_end - padded
    dest = jnp.sum((jnp.cumsum(onehot, axis=0) - onehot + pad_start[None, :]) * onehot, axis=1)

    n_rows = TOP_K * t + n_exp * tm
    n_tiles = n_rows // tm
    tile_first_row = jnp.arange(n_tiles, dtype=jnp.int32) * tm
    n_valid = pad_end[-1] // tm
    tile_first_row = jnp.minimum(tile_first_row, (n_valid - 1) * tm)
    tile_expert = jnp.sum((pad_end[None, :] <= tile_first_row[:, None]).astype(jnp.int32), axis=1)
    return (dest.astype(jnp.int32), pad_end.astype(jnp.int32),
            jnp.minimum(tile_expert, n_exp - 1).astype(jnp.int32),
            n_valid.astype(jnp.int32).reshape(1), n_rows)


def _tile(n, want):
    t = min(n, want)
    while n % t:
        t -= SUBLANES
    return t


def kernel(x, positions, pre_mix_g, w_in, conv_w, conv_b, w_rgate, b_rgate, w_igate, b_igate,
           lru_lambda, attn_out_g, lru_out_g, w_out, post_mix_g, pre_ffn_g, post_ffn_g,
           dense_w_gate, dense_w_up, dense_w_down, router_w, moe_w_gate, moe_w_up, moe_w_down):
    batch, seq, d = x.shape
    depth = w_in.shape[0]
    t = batch * seq
    d_ff = dense_w_gate.shape[-1]
    n_exp = moe_w_gate.shape[1]
    assert w_in.shape[2] == 3 * ATTN_WIDTH + 2 * LRU_WIDTH
    assert seq % (DILATED_PAIRS[-1][1] * Q_BLOCK) == 0

    tm_mix = _tile(t, 256)
    tm_ffn = _tile(t, 512)
    tm_moe = tm_ffn
    ts_proj = _tile(seq, 512)
    tm_rows = SMEM_BLOCK_WORDS
    f_chunk = 1792 if d_ff % 1792 == 0 else (512 if d_ff % 512 == 0 else d_ff)

    xf = x.reshape(t, d).astype(F32)
    pos = positions.reshape(t, 1).astype(jnp.int32)

    for l in range(depth):
        q, k, v, rec = _in_proj(xf, pos, pre_mix_g[l].reshape(1, d), w_in[l].astype(BF16),
                                conv_w[l], conv_b[l], w_rgate[l], b_rgate[l], w_igate[l],
                                b_igate[l], lru_lambda[l], batch, seq, ts_proj)
        attn = _attention(q, k, v, batch, seq)
        j = l // 2
        if l % 2 == 0:
            xf = _mix_ffn(attn, rec, xf, attn_out_g[l], lru_out_g[l], w_out[l], post_mix_g[l],
                          pre_ffn_g[l], dense_w_gate[j], dense_w_up[j], dense_w_down[j],
                          post_ffn_g[l], tm_ffn, f_chunk)
        else:
            x1, h2, route_i, route_p = _mix_out_router(
                attn, rec, xf, attn_out_g[l], lru_out_g[l], w_out[l], post_mix_g[l],
                pre_ffn_g[l], router_w[j], tm_mix)
            dest, pad_end, tile_expert, n_valid, n_rows = _routing_tables(route_i, n_exp, tm_moe)
            xs = _dispatch(h2, dest, pad_end, n_rows, tm_rows, tm_moe)
            ys = _ffn(xs, tile_expert, n_valid, moe_w_gate[j].astype(BF16),
                      moe_w_up[j].astype(BF16), moe_w_down[j].astype(BF16), tm_moe, f_chunk)
            xf = _combine(ys, dest, route_p, x1, post_ffn_g[l], tm_rows)
    return xf.reshape(batch, seq, d).astype(x.dtype)
```

```python
import functools
import math

import jax
import jax.numpy as jnp
from jax import lax
from jax.experimental import pallas as pl
from jax.experimental.pallas import tpu as pltpu

ATTN_HEADS = 8
HEAD_DIM = 64
ATTN_WIDTH = ATTN_HEADS * HEAD_DIM
LRU_BLOCKS = 8
LRU_BLOCK_W = 64
LRU_WIDTH = LRU_BLOCKS * LRU_BLOCK_W
DILATED_PAIRS = ((128, 1), (512, 4), (2048, 16))
Q_BLOCK = 128
ROT_DIM = HEAD_DIM // 4
ROT_HALF = ROT_DIM // 2
ROPE_THETA = 500000.0
ATTN_SCALE = 1.0 / math.sqrt(HEAD_DIM)
NEG_INF = -1e30
CONV_W = 4
LRU_C = 8.0
TOP_K = 2
RMS_EPS = 1e-6

LANES = 128
SUBLANES = 8
SMEM_BLOCK_WORDS = 1024
VMEM_LIMIT_BYTES = 56 * 1024 * 1024

F32 = jnp.float32
BF16 = jnp.bfloat16


def _params(semantics):
    return pltpu.CompilerParams(dimension_semantics=semantics,
                                vmem_limit_bytes=VMEM_LIMIT_BYTES)


def _resident(block_shape, index_map):
    return pl.BlockSpec(block_shape, index_map, pipeline_mode=pl.Buffered(1))


def _rms(x, g):
    var = jnp.mean(x * x, axis=-1, keepdims=True)
    return x * lax.rsqrt(var + RMS_EPS) * g


def _tile_rows(r):
    return pl.ds(pl.multiple_of(r * SUBLANES, SUBLANES), SUBLANES)


def _store_row_tiles(ref, x):
    rows = x.shape[0]
    for c in range(x.shape[1] // LANES):
        ref[pl.ds(c, rows, stride=SUBLANES), :] = x[:, c * LANES:(c + 1) * LANES]


def _load_row_tiles(ref):
    rows = ref.shape[0] // SUBLANES
    return jnp.concatenate([ref[pl.ds(c, rows, stride=SUBLANES), :] for c in range(SUBLANES)],
                           axis=1)


def _lru_block(u, gate, cw_ref, cb_ref, wr_ref, br_ref, wi_ref, bi_ref, lam_ref,
               ubuf, hcarry, a_scr, b_scr):
    ts, c = u.shape
    ubuf[SUBLANES:SUBLANES + ts, :] = u
    ext = ubuf[...]
    uc = cb_ref[...]
    for tap in range(CONV_W):
        back = CONV_W - 1 - tap
        shifted = pltpu.roll(ext, back, 0) if back else ext
        uc = uc + shifted[SUBLANES:SUBLANES + ts, :] * cw_ref[tap:tap + 1, :]
    ubuf[0:SUBLANES, :] = ubuf[ts:ts + SUBLANES, :]

    ucb = uc.astype(BF16)
    r = jax.nn.sigmoid(jnp.dot(ucb, wr_ref[...], preferred_element_type=F32) + br_ref[...])
    ig = jax.nn.sigmoid(jnp.dot(ucb, wi_ref[...], preferred_element_type=F32) + bi_ref[...])
    nl = -lam_ref[...]
    softplus = jnp.maximum(nl, 0.0) + jnp.log1p(jnp.exp(-jnp.abs(nl)))
    log_a = -LRU_C * r * softplus
    a = jnp.exp(log_a)
    mult = jnp.sqrt(1.0 - a * a)
    b = mult * ig * uc

    groups = ts // SUBLANES
    a3 = a.reshape(groups, SUBLANES, c)
    b3 = b.reshape(groups, SUBLANES, c)
    sub = lax.broadcasted_iota(jnp.int32, (groups, SUBLANES, c), 1)
    shift = 1
    while shift < SUBLANES:
        valid = sub >= shift
        a_sh = jnp.where(valid, pltpu.roll(a3, shift, 1), 1.0)
        b_sh = jnp.where(valid, pltpu.roll(b3, shift, 1), 0.0)
        b3 = a3 * b_sh + b3
        a3 = a3 * a_sh
        shift *= 2
    a_scr[...] = a3.reshape(ts, c)
    b_scr[...] = b3.reshape(ts, c)

    h_prev = hcarry[...]
    for g in range(groups):
        rows = slice(g * SUBLANES, (g + 1) * SUBLANES)
        h = a_scr[rows, :] * h_prev + b_scr[rows, :]
        b_scr[rows, :] = h
        h_prev = jnp.broadcast_to(h[SUBLANES - 1:SUBLANES, :], (SUBLANES, c))
    hcarry[...] = h_prev
    return jax.nn.gelu(gate) * b_scr[...]


def _in_proj_kernel(x_ref, pos_ref, g_ref, w_ref, rope_ref,
                    cw_ref, cb_ref, wr_ref, br_ref, wi_ref, bi_ref, lam_ref,
                    q_ref, k_ref, v_ref, rec_ref, ubuf, hcarry, a_scr, b_scr):
    @pl.when(pl.program_id(1) == 0)
    def _():
        ubuf[0:SUBLANES, :] = jnp.zeros((SUBLANES, ubuf.shape[1]), F32)
        hcarry[...] = jnp.zeros_like(hcarry)

    ang = pos_ref[...].astype(F32) * rope_ref[0:1, :]
    cs = jnp.cos(ang)
    sn = jnp.sin(ang)
    s_lo = sn * rope_ref[1:2, :]
    s_hi = sn * rope_ref[2:3, :]

    def rotary(z):
        return (z * cs + pltpu.roll(z, ROT_HALF, 1) * s_lo
                + pltpu.roll(z, LANES - ROT_HALF, 1) * s_hi)

    h = _rms(x_ref[...], g_ref[...]).astype(BF16)
    w = ATTN_WIDTH
    u = jnp.dot(h, w_ref[:, 3 * w:3 * w + LRU_WIDTH], preferred_element_type=F32)
    gate = jnp.dot(h, w_ref[:, 3 * w + LRU_WIDTH:], preferred_element_type=F32)
    rec_ref[...] = _lru_block(u, gate, cw_ref, cb_ref, wr_ref, br_ref, wi_ref, bi_ref, lam_ref,
                              ubuf, hcarry, a_scr, b_scr)
    for sec, out in ((0, q_ref), (1, k_ref)):
        z = jnp.dot(h, w_ref[:, sec * w:(sec + 1) * w], preferred_element_type=F32)
        for cb in range(w // LANES):
            sl = slice(cb * LANES, (cb + 1) * LANES)
            out[:, sl] = rotary(z[:, sl]).astype(BF16)
    v_ref[...] = jnp.dot(h, w_ref[:, 2 * w:3 * w], preferred_element_type=F32).astype(BF16)


def _rope_table():
    lane = jnp.arange(LANES, dtype=jnp.int32) % HEAD_DIM
    inv_freq = ROPE_THETA ** (-jnp.arange(ROT_HALF, dtype=F32) / ROT_HALF)
    freq = jnp.where(lane < ROT_DIM, inv_freq[lane % ROT_HALF], 0.0)
    lo = jnp.where((lane >= ROT_HALF) & (lane < ROT_DIM), 1.0, 0.0)
    hi = jnp.where(lane < ROT_HALF, -1.0, 0.0)
    return jnp.stack([freq, lo, hi]).astype(F32)


def _block_diag(w):
    nb, bw, _ = w.shape
    eye = jnp.eye(nb, dtype=w.dtype)
    return jnp.einsum("gij,gh->gihj", w, eye).reshape(nb * bw, nb * bw)


def _in_proj(x, pos, g, w_in, conv_w, conv_b, w_r, b_r, w_i, b_i, lam, batch, seq, ts):
    t, d = x.shape
    ncol = w_in.shape[1]
    c = LRU_WIDTH
    nblk = seq // ts
    row = lambda b, j: (b * nblk + j, 0)
    const = lambda b, j: (0, 0)
    vec = lambda a: a.reshape(1, c).astype(F32)
    return pl.pallas_call(
        _in_proj_kernel,
        grid=(batch, nblk),
        in_specs=[
            pl.BlockSpec((ts, d), row),
            pl.BlockSpec((ts, 1), row),
            pl.BlockSpec((1, d), const),
            _resident((d, ncol), const),
            pl.BlockSpec((3, LANES), const),
            pl.BlockSpec((CONV_W, c), const),
            pl.BlockSpec((1, c), const),
            pl.BlockSpec((c, c), const),
            pl.BlockSpec((1, c), const),
            pl.BlockSpec((c, c), const),
            pl.BlockSpec((1, c), const),
            pl.BlockSpec((1, c), const),
        ],
        out_specs=[pl.BlockSpec((ts, ATTN_WIDTH), row)] * 3 + [pl.BlockSpec((ts, c), row)],
        out_shape=[jax.ShapeDtypeStruct((t, ATTN_WIDTH), BF16)] * 3
        + [jax.ShapeDtypeStruct((t, c), F32)],
        scratch_shapes=[pltpu.VMEM((ts + SUBLANES, c), F32), pltpu.VMEM((SUBLANES, c), F32),
                        pltpu.VMEM((ts, c), F32), pltpu.VMEM((ts, c), F32)],
        compiler_params=_params(("parallel", "arbitrary")),
        name="in_proj_lru",
    )(x, pos, g, w_in, _rope_table(), conv_w.astype(F32), vec(conv_b),
      _block_diag(w_r).astype(BF16), vec(b_r), _block_diag(w_i).astype(BF16), vec(b_i), vec(lam))


ATTN_TILE = DILATED_PAIRS[-1][1] * Q_BLOCK


def _rows(start, size, stride):
    return pl.ds(start, size) if stride == 1 else pl.ds(start, size, stride=stride)


ATTN_PIPE_WIDTH = 2


def _attn_kernel(q_ref, kc_ref, vc_ref, kp_ref, vp_ref, bias_ref, o_ref,
                 q0_scr, q1_scr, k_scr, v_scr, q0c, q1c, kc, vc,
                 ob0, ob1, ob2, ls0, ls1, ls2, *slots):
    s_scrs, m_scrs = slots[:len(slots) // 2], slots[len(slots) // 2:]
    j = pl.program_id(2)
    ts = q_ref.shape[0]
    lane = lax.broadcasted_iota(jnp.int32, (1, LANES), 1)
    head0 = lane < HEAD_DIM

    qf = q_ref[...].astype(F32) * ATTN_SCALE
    q0_scr[...] = jnp.where(head0, qf, 0.0)
    q1_scr[...] = jnp.where(head0, 0.0, qf)
    k_scr[0:ts, :] = kp_ref[...].astype(F32)
    k_scr[ts:2 * ts, :] = kc_ref[...].astype(F32)
    v_scr[0:ts, :] = vp_ref[...].astype(F32)
    v_scr[ts:2 * ts, :] = vc_ref[...].astype(F32)

    nt = (((1,), (1,)), ((), ()))
    units = ts // Q_BLOCK
    width = len(s_scrs) // 2
    groups = units // width
    assert groups % 2 == 0 and groups >= 2
    pre = DILATED_PAIRS[1][1]
    assert all(dil == 1 or dil % pre == 0 for _, dil in DILATED_PAIRS)

    for c in range(pre):
        for src, dst in ((q0_scr, q0c), (q1_scr, q1c), (k_scr, kc), (v_scr, vc)):
            n = src.shape[0] // pre
            dst[c * n:(c + 1) * n, :] = src[_rows(c, n, pre), :]

    for (_, dil), ob, ls in zip(DILATED_PAIRS, (ob0, ob1, ob2), (ls0, ls1, ls2)):
        shift = dil.bit_length() - 1
        if dil == 1:
            q0_src, q1_src, k_src, v_src = q0_scr, q1_scr, k_scr, v_scr
        else:
            q0_src, q1_src, k_src, v_src = q0c, q1c, kc, vc

        def rows_of(u, dil=dil, shift=shift):
            cls = jnp.bitwise_and(u, dil - 1)
            blk = lax.shift_right_logical(u, shift)
            q_start = cls + dil * Q_BLOCK * blk
            out_rows = _rows(q_start, Q_BLOCK, dil)
            if dil == 1:
                q_rows = out_rows
                kv_rows = _rows(ts + q_start - Q_BLOCK, 2 * Q_BLOCK, 1)
            else:
                step = dil // pre
                c4 = jnp.bitwise_and(cls, pre - 1)
                sub = lax.shift_right_logical(cls, pre.bit_length() - 1)
                q_rows = _rows(c4 * (ts // pre) + sub + step * Q_BLOCK * blk, Q_BLOCK, step)
                kv_rows = _rows(c4 * (2 * ts // pre) + sub
                                + step * (ts // dil + (blk - 1) * Q_BLOCK), 2 * Q_BLOCK, step)
            return blk, out_rows, q_rows, kv_rows

        def scores(u, slot, q0_src=q0_src, q1_src=q1_src, k_src=k_src):
            blk, _, q_rows, kv_rows = rows_of(u)
            lhs = jnp.concatenate([q0_src[q_rows, :], q1_src[q_rows, :]], axis=0).astype(BF16)
            k2 = k_src[kv_rows, :].astype(BF16)
            no_prev = jnp.logical_and(j == 0, blk == 0).astype(jnp.int32)
            bias = bias_ref[pl.ds(no_prev * 2 * Q_BLOCK, 2 * Q_BLOCK), :]
            s = lax.dot_general(lhs, k2, nt, preferred_element_type=F32) + bias
            s_scrs[slot][...] = s
            m_scrs[slot][...] = jnp.broadcast_to(jnp.max(s, axis=-1, keepdims=True),
                                                 (2 * Q_BLOCK, LANES))

        def finish(u, slot, ob=ob, ls=ls, v_src=v_src):
            _, out_rows, _, kv_rows = rows_of(u)
            m = m_scrs[slot][...]
            p = jnp.exp(s_scrs[slot][...] - jnp.concatenate([m, m], axis=1))
            den = jnp.sum(p, axis=-1, keepdims=True)
            v2 = v_src[kv_rows, :].astype(BF16)
            o = jnp.dot(p.astype(BF16), v2, preferred_element_type=F32) / den
            lse = m + jnp.log(den)
            ob[out_rows, :] = jnp.where(head0, o[0:Q_BLOCK], o[Q_BLOCK:])
            ls[out_rows, :] = jnp.where(head0, lse[0:Q_BLOCK], lse[Q_BLOCK:])

        def scores_group(g, half):
            for w in range(width):
                scores(g * width + w, half * width + w)

        def finish_group(g, half):
            for w in range(width):
                finish(g * width + w, half * width + w)

        scores_group(jnp.int32(0), 0)
        for g in range(groups - 1):
            scores_group(jnp.int32(g + 1), (g + 1) % 2)
            finish_group(jnp.int32(g), g % 2)
        finish_group(jnp.int32(groups - 1), (groups - 1) % 2)

    chunk = 2 * Q_BLOCK

    def mix(c, carry):
        r = pl.ds(pl.multiple_of(c * chunk, chunk), chunk)
        la, lb, lc = ls0[r, :], ls1[r, :], ls2[r, :]
        m = jnp.maximum(jnp.maximum(la, lb), lc)
        ea, eb, ec = jnp.exp(la - m), jnp.exp(lb - m), jnp.exp(lc - m)
        o_ref[r, :] = (ea * ob0[r, :] + eb * ob1[r, :] + ec * ob2[r, :]) / (ea + eb + ec)
        return carry

    lax.fori_loop(0, ts // chunk, mix, 0)


def _attn_bias():
    qi = jnp.arange(2 * Q_BLOCK, dtype=jnp.int32)[:, None] % Q_BLOCK
    kj = jnp.arange(2 * Q_BLOCK, dtype=jnp.int32)[None, :]
    delta = Q_BLOCK + qi - kj
    band = (delta >= 0) & (delta <= Q_BLOCK)
    with_prev = jnp.where(band, 0.0, NEG_INF)
    no_prev = jnp.where(band & (kj >= Q_BLOCK), 0.0, NEG_INF)
    return jnp.concatenate([with_prev, no_prev], axis=0).astype(F32)


def _attention(q, k, v, batch, seq):
    ts = ATTN_TILE
    assert seq % ts == 0
    nt = seq // ts
    cur = pl.BlockSpec((ts, LANES), lambda b, c, j: (b * nt + j, c))
    prev = pl.BlockSpec((ts, LANES), lambda b, c, j: (b * nt + jnp.maximum(j - 1, 0), c))
    tile_scr = pltpu.VMEM((ts, LANES), F32)
    pair_scr = pltpu.VMEM((2 * ts, LANES), F32)
    return pl.pallas_call(
        _attn_kernel,
        grid=(batch, ATTN_WIDTH // LANES, nt),
        in_specs=[cur, cur, cur, prev, prev,
                  pl.BlockSpec((4 * Q_BLOCK, 2 * Q_BLOCK), lambda b, c, j: (0, 0))],
        out_specs=cur,
        out_shape=jax.ShapeDtypeStruct((batch * seq, ATTN_WIDTH), F32),
        scratch_shapes=[tile_scr, tile_scr, pair_scr, pair_scr] * 2 + [tile_scr] * 6
        + [pltpu.VMEM((2 * Q_BLOCK, 2 * Q_BLOCK), F32)] * (2 * ATTN_PIPE_WIDTH)
        + [pltpu.VMEM((2 * Q_BLOCK, LANES), F32)] * (2 * ATTN_PIPE_WIDTH),
        compiler_params=_params(("parallel", "parallel", "parallel")),
        name="attention",
    )(q, k, v, k, v, _attn_bias())


def _mix(attn_ref, rec_ref, x_ref, ga_ref, gr_ref, wo_ref, gpost_ref, gpre_ref):
    na = _rms(attn_ref[...], ga_ref[...]).astype(BF16)
    nr = _rms(rec_ref[...], gr_ref[...]).astype(BF16)
    mixed = (jnp.dot(na, wo_ref[0:ATTN_WIDTH, :], preferred_element_type=F32)
             + jnp.dot(nr, wo_ref[ATTN_WIDTH:, :], preferred_element_type=F32))
    x1 = x_ref[...] + _rms(mixed, gpost_ref[...])
    return x1, _rms(x1, gpre_ref[...])


def _mix_specs(tm, d, row, const, w_out_shape):
    aw = ATTN_WIDTH
    return ([pl.BlockSpec((tm, aw), row)] * 2 + [pl.BlockSpec((tm, d), row),
            pl.BlockSpec((1, aw), const), pl.BlockSpec((1, LRU_WIDTH), const),
            _resident(w_out_shape, const), pl.BlockSpec((1, d), const),
            pl.BlockSpec((1, d), const)])


def _mix_args(attn, rec, x, ga, gr, w_out, gpost, gpre):
    d = x.shape[1]
    return [attn, rec, x, ga.reshape(1, ATTN_WIDTH), gr.reshape(1, LRU_WIDTH),
            w_out.astype(BF16), gpost.reshape(1, d), gpre.reshape(1, d)]


def _mix_router_kernel(*refs, n_exp):
    rw_ref, x1_ref, h2_ref, ri_ref, rp_ref = refs[8:]
    x1, hn = _mix(*refs[:8])
    x1_ref[...] = x1
    _store_row_tiles(h2_ref, hn)

    rw = rw_ref[...]
    h_hi, r_hi = hn.astype(BF16), rw.astype(BF16)
    h_lo = (hn - h_hi.astype(F32)).astype(BF16)
    r_lo = (rw - r_hi.astype(F32)).astype(BF16)
    logits = (jnp.dot(h_hi, r_hi, preferred_element_type=F32)
              + jnp.dot(h_hi, r_lo, preferred_element_type=F32)
              + jnp.dot(h_lo, r_hi, preferred_element_type=F32))
    tm = logits.shape[0]
    lane = lax.broadcasted_iota(jnp.int32, (tm, LANES), 1)
    logits = jnp.where(lane < n_exp, logits, -jnp.inf)
    m1 = jnp.max(logits, axis=-1, keepdims=True)
    i1 = jnp.min(jnp.where(logits == m1, lane, n_exp), axis=-1, keepdims=True)
    rest = jnp.where(lane == i1, -jnp.inf, logits)
    m2 = jnp.max(rest, axis=-1, keepdims=True)
    i2 = jnp.min(jnp.where(rest == m2, lane, n_exp), axis=-1, keepdims=True)
    e2 = jnp.exp(m2 - m1)
    p1 = 1.0 / (1.0 + e2)
    p2 = e2 / (1.0 + e2)
    ri_ref[...] = jnp.where(lane == 0, i1, jnp.where(lane == 1, i2, 0))
    rp_ref[...] = jnp.where(lane == 0, p1, jnp.where(lane == 1, p2, 0.0))


def _mix_out_router(attn, rec, x, ga, gr, w_out, gpost, gpre, router_w, tm):
    t, d = x.shape
    assert d == SUBLANES * LANES
    row = lambda i: (i, 0)
    const = lambda i: (0, 0)
    n_exp = router_w.shape[1]
    rw = jnp.pad(router_w.astype(F32), ((0, 0), (0, LANES - n_exp)))
    return pl.pallas_call(
        functools.partial(_mix_router_kernel, n_exp=n_exp),
        grid=(t // tm,),
        in_specs=_mix_specs(tm, d, row, const, w_out.shape) + [pl.BlockSpec(rw.shape, const)],
        out_specs=[pl.BlockSpec((tm, d), row), pl.BlockSpec((tm * SUBLANES, LANES), row)]
        + [pl.BlockSpec((tm, LANES), row)] * 2,
        out_shape=[jax.ShapeDtypeStruct((t, d), F32),
                   jax.ShapeDtypeStruct((t * SUBLANES, LANES), F32)]
        + [jax.ShapeDtypeStruct((t, LANES), jnp.int32), jax.ShapeDtypeStruct((t, LANES), F32)],
        compiler_params=_params(("parallel",)),
        name="mix_out_router",
    )(*_mix_args(attn, rec, x, ga, gr, w_out, gpost, gpre), rw)


def _swiglu_into(o_ref, x, wg_ref, wu_ref, wd_ref, f_chunk):
    d_ff = wg_ref.shape[1]
    for c in range(d_ff // f_chunk):
        sl = slice(c * f_chunk, (c + 1) * f_chunk)
        g = jnp.dot(x, wg_ref[:, sl], preferred_element_type=F32)
        u = jnp.dot(x, wu_ref[:, sl], preferred_element_type=F32)
        a = (g * jax.nn.sigmoid(g) * u).astype(BF16)
        y = jnp.dot(a, wd_ref[sl, :], preferred_element_type=F32)
        if c == 0:
            o_ref[...] = y
        else:
            o_ref[...] += y


def _ffn_kernel(te_ref, nv_ref, x_ref, wg_ref, wu_ref, wd_ref, o_ref, acc, *, f_chunk):
    i = pl.program_id(0)

    @pl.when(i < nv_ref[0])
    def _():
        _swiglu_into(acc, _load_row_tiles(x_ref).astype(BF16), wg_ref, wu_ref, wd_ref, f_chunk)
        _store_row_tiles(o_ref, acc[...])

    @pl.when(i >= nv_ref[0])
    def _():
        o_ref[...] = jnp.zeros_like(o_ref)


def _ffn(xs, tile_expert, n_valid, wg, wu, wd, tm, f_chunk):
    d, d_ff = wg.shape[1], wg.shape[2]
    rows = xs.shape[0] // SUBLANES
    assert d == SUBLANES * LANES and d_ff % f_chunk == 0 and rows % tm == 0
    wmap = lambda i, te, nv: (te[i], 0, 0)
    grid_spec = pltpu.PrefetchScalarGridSpec(
        num_scalar_prefetch=2,
        grid=(rows // tm,),
        in_specs=[
            pl.BlockSpec((tm * SUBLANES, LANES), lambda i, te, nv: (jnp.minimum(i, nv[0] - 1), 0)),
            _resident((None, d, d_ff), wmap),
            _resident((None, d, d_ff), wmap),
            _resident((None, d_ff, d), wmap),
        ],
        out_specs=pl.BlockSpec((tm * SUBLANES, LANES), lambda i, te, nv: (i, 0)),
        scratch_shapes=[pltpu.VMEM((tm, d), F32)],
    )
    return pl.pallas_call(
        functools.partial(_ffn_kernel, f_chunk=f_chunk),
        grid_spec=grid_spec,
        out_shape=jax.ShapeDtypeStruct(xs.shape, F32),
        compiler_params=_params(("arbitrary",)),
        name="ffn",
    )(tile_expert, n_valid, xs, wg, wu, wd)


def _mix_ffn_kernel(*refs, f_chunk):
    wg_ref, wu_ref, wd_ref, gffn_ref, o_ref, h2_scr, x1_scr = refs[8:]
    s = pl.program_id(0)

    def mix_into(cur):
        x1, hn = _mix(*refs[:8])
        x1_scr[cur] = x1
        h2_scr[cur] = hn.astype(BF16)

    def ffn_from(prev):
        _swiglu_into(o_ref, h2_scr[prev], wg_ref, wu_ref, wd_ref, f_chunk)
        o_ref[...] = x1_scr[prev] + _rms(o_ref[...], gffn_ref[...])

    @pl.when(s == 0)
    def _():
        mix_into(0)

    for parity in (0, 1):
        @pl.when(jnp.logical_and(lax.rem(s, 2) == parity, s > 0))
        def _():
            mix_into(parity)
            ffn_from(1 - parity)


def _mix_ffn(attn, rec, x, ga, gr, w_out, gpost, gpre, wg, wu, wd, gffn, tm, f_chunk):
    t, d = x.shape
    d_ff = wg.shape[1]
    n = t // tm
    assert d_ff % f_chunk == 0 and t % tm == 0
    row = lambda s: (jnp.minimum(s, n - 1), 0)
    const = lambda s: (0, 0)
    return pl.pallas_call(
        functools.partial(_mix_ffn_kernel, f_chunk=f_chunk),
        grid=(n + 1,),
        in_specs=_mix_specs(tm, d, row, const, w_out.shape)
        + [_resident((d, d_ff), const), _resident((d, d_ff), const), _resident((d_ff, d), const),
           pl.BlockSpec((1, d), const)],
        out_specs=pl.BlockSpec((tm, d), lambda s: (jnp.maximum(s - 1, 0), 0)),
        out_shape=jax.ShapeDtypeStruct((t, d), F32),
        scratch_shapes=[pltpu.VMEM((2, tm, d), BF16), pltpu.VMEM((2, tm, d), F32)],
        compiler_params=_params(("arbitrary",)),
        name="mix_ffn_dense",
    )(*_mix_args(attn, rec, x, ga, gr, w_out, gpost, gpre),
      wg.astype(BF16), wu.astype(BF16), wd.astype(BF16), gffn.reshape(1, d))


def _dispatch_kernel(pad_end_ref, d1_ref, d2_ref, h_ref, out_ref, zbuf, sem, zsem):
    tile = zbuf.shape[0]

    @pl.when(pl.program_id(0) == 0)
    def _():
        zbuf[...] = jnp.zeros_like(zbuf)

        def zero_copy(e):
            start = pl.multiple_of(pad_end_ref[e] * SUBLANES - tile, tile)
            return pltpu.make_async_copy(zbuf, out_ref.at[pl.ds(start, tile)], zsem)

        def has_rows(e):
            prev_end = pad_end_ref[e - 1] if e else 0
            return pad_end_ref[e] > prev_end

        for e in range(pad_end_ref.shape[0]):
            @pl.when(has_rows(e))
            def _():
                zero_copy(e).start()
        for e in range(pad_end_ref.shape[0]):
            @pl.when(has_rows(e))
            def _():
                zero_copy(e).wait()

    def row_copies(j):
        src = h_ref.at[_tile_rows(j)]
        return (pltpu.make_async_copy(src, out_ref.at[_tile_rows(d1_ref[j])], sem),
                pltpu.make_async_copy(src, out_ref.at[_tile_rows(d2_ref[j])], sem))

    def start(j, carry):
        for queue, c in enumerate(row_copies(j)):
            c.start(priority=queue)
        return carry

    def wait(j, carry):
        for c in row_copies(j):
            c.wait()
        return carry

    n = d1_ref.shape[0]
    lax.fori_loop(0, n, start, 0, unroll=8)
    lax.fori_loop(0, n, wait, 0, unroll=8)


def _dispatch(h, dest, pad_end, n_rows, tm, tile):
    t = h.shape[0] // SUBLANES
    assert tm % SMEM_BLOCK_WORDS == 0 and t % tm == 0
    nblk = t // tm
    grid_spec = pltpu.PrefetchScalarGridSpec(
        num_scalar_prefetch=1,
        grid=(nblk,),
        in_specs=[pl.BlockSpec((tm,), lambda i, pe: (i,), memory_space=pltpu.SMEM),
                  pl.BlockSpec((tm,), lambda i, pe: (i + nblk,), memory_space=pltpu.SMEM),
                  pl.BlockSpec((tm * SUBLANES, LANES), lambda i, pe: (i, 0))],
        out_specs=pl.BlockSpec(memory_space=pl.ANY),
        scratch_shapes=[pltpu.VMEM((tile * SUBLANES, LANES), h.dtype),
                        pltpu.SemaphoreType.DMA(()), pltpu.SemaphoreType.DMA(())],
    )
    return pl.pallas_call(
        _dispatch_kernel,
        grid_spec=grid_spec,
        out_shape=jax.ShapeDtypeStruct((n_rows * SUBLANES, LANES), h.dtype),
        compiler_params=_params(("arbitrary",)),
        name="moe_dispatch",
    )(pad_end, dest, dest, h)


def _combine_kernel(d1_ref, d2_ref, ys_ref, rp_ref, x_ref, g_ref, o_ref,
                    ya0, yb0, ya1, yb1, sems):
    s = pl.program_id(0)
    last = pl.num_programs(0) - 1
    bufs = ((ya0, yb0), (ya1, yb1))
    n = d1_ref.shape[0]

    def row_copies(j, pair, src_a, src_b):
        ya, yb = bufs[pair]
        dst = _tile_rows(j)
        return (pltpu.make_async_copy(ys_ref.at[_tile_rows(src_a)], ya.at[dst], sems.at[pair]),
                pltpu.make_async_copy(ys_ref.at[_tile_rows(src_b)], yb.at[dst], sems.at[pair]))

    def gather(pair):
        def start(j, carry):
            for queue, c in enumerate(row_copies(j, pair, d1_ref[j], d2_ref[j])):
                c.start(priority=queue)
            return carry
        lax.fori_loop(0, n, start, 0, unroll=8)

    def finish(pair):
        def wait(j, carry):
            for c in row_copies(j, pair, 0, 0):
                c.wait()
            return carry
        lax.fori_loop(0, n, wait, 0, unroll=8)
        ya, yb = bufs[pair]
        y = rp_ref[:, 0:1] * _load_row_tiles(ya) + rp_ref[:, 1:2] * _load_row_tiles(yb)
        o_ref[...] = x_ref[...] + _rms(y, g_ref[...])

    for parity in (0, 1):
        @pl.when(jnp.logical_and(lax.rem(s, 2) == parity, s < last))
        def _():
            gather(parity)

        @pl.when(jnp.logical_and(lax.rem(s, 2) == parity, s > 0))
        def _():
            finish(1 - parity)


def _combine(ys, dest, route_p, x, g, tm):
    t, d = x.shape
    assert d == SUBLANES * LANES and tm % SMEM_BLOCK_WORDS == 0 and t % tm == 0
    nblk = t // tm
    ahead = lambda off: (lambda s: (jnp.minimum(s, nblk - 1) + off,))
    behind = lambda s: (jnp.maximum(s - 1, 0), 0)
    buf = pltpu.VMEM((tm * SUBLANES, LANES), F32)
    return pl.pallas_call(
        _combine_kernel,
        grid=(nblk + 1,),
        in_specs=[pl.BlockSpec((tm,), ahead(0), memory_space=pltpu.SMEM),
                  pl.BlockSpec((tm,), ahead(nblk), memory_space=pltpu.SMEM),
                  pl.BlockSpec(memory_space=pl.ANY),
                  pl.BlockSpec((tm, LANES), behind), pl.BlockSpec((tm, d), behind),
                  pl.BlockSpec((1, d), lambda s: (0, 0))],
        out_specs=pl.BlockSpec((tm, d), behind),
        out_shape=jax.ShapeDtypeStruct((t, d), F32),
        scratch_shapes=[buf, buf, buf, buf, pltpu.SemaphoreType.DMA((2,))],
        compiler_params=_params(("arbitrary",)),
        name="moe_combine",
    )(dest, dest, ys, route_p, x, g.reshape(1, d))


def _routing_tables(route_i, n_exp, tm):
    t = route_i.shape[0]
    e_flat = jnp.concatenate([route_i[:, 0], route_i[:, 1]])
    onehot = (e_flat[:, None] == jnp.arange(n_exp, dtype=jnp.int32)[None, :]).astype(jnp.int32)
    counts = jnp.sum(onehot, axis=0)
    padded = ((counts + tm - 1) // tm) * tm
    pad_end = jnp.cumsum(padded)
    pad_start = pad_end - padded
    dest = jnp.sum((jnp.cumsum(onehot, axis=0) - onehot + pad_start[None, :]) * onehot, axis=1)

    n_rows = TOP_K * t + n_exp * tm
    n_tiles = n_rows // tm
    tile_first_row = jnp.arange(n_tiles, dtype=jnp.int32) * tm
    n_valid = pad_end[-1] // tm
    tile_first_row = jnp.minimum(tile_first_row, (n_valid - 1) * tm)
    tile_expert = jnp.sum((pad_end[None, :] <= tile_first_row[:, None]).astype(jnp.int32), axis=1)
    return (dest.astype(jnp.int32), pad_end.astype(jnp.int32),
            jnp.minimum(tile_expert, n_exp - 1).astype(jnp.int32),
            n_valid.astype(jnp.int32).reshape(1), n_rows)


def _tile(n, want):
    t = min(n, want)
    while n % t:
        t -= SUBLANES
    return t


def kernel(x, positions, pre_mix_g, w_in, conv_w, conv_b, w_rgate, b_rgate, w_igate, b_igate,
           lru_lambda, attn_out_g, lru_out_g, w_out, post_mix_g, pre_ffn_g, post_ffn_g,
           dense_w_gate, dense_w_up, dense_w_down, router_w, moe_w_gate, moe_w_up, moe_w_down):
    batch, seq, d = x.shape
    depth = w_in.shape[0]
    t = batch * seq
    d_ff = dense_w_gate.shape[-1]
    n_exp = moe_w_gate.shape[1]
    assert w_in.shape[2] == 3 * ATTN_WIDTH + 2 * LRU_WIDTH
    assert seq % (DILATED_PAIRS[-1][1] * Q_BLOCK) == 0

    tm_mix = _tile(t, 256)
    tm_ffn = _tile(t, 512)
    tm_moe = tm_ffn
    ts_proj = _tile(seq, 512)
    tm_rows = SMEM_BLOCK_WORDS
    f_chunk = 1792 if d_ff % 1792 == 0 else (512 if d_ff % 512 == 0 else d_ff)

    xf = x.reshape(t, d).astype(F32)
    pos = positions.reshape(t, 1).astype(jnp.int32)

    for l in range(depth):
        q, k, v, rec = _in_proj(xf, pos, pre_mix_g[l].reshape(1, d), w_in[l].astype(BF16),
                                conv_w[l], conv_b[l], w_rgate[l], b_rgate[l], w_igate[l],
                                b_igate[l], lru_lambda[l], batch, seq, ts_proj)
        attn = _attention(q, k, v, batch, seq)
        j = l // 2
        if l % 2 == 0:
            xf = _mix_ffn(attn, rec, xf, attn_out_g[l], lru_out_g[l], w_out[l], post_mix_g[l],
                          pre_ffn_g[l], dense_w_gate[j], dense_w_up[j], dense_w_down[j],
                          post_ffn_g[l], tm_ffn, f_chunk)
        else:
            x1, h2, route_i, route_p = _mix_out_router(
                attn, rec, xf, attn_out_g[l], lru_out_g[l], w_out[l], post_mix_g[l],
                pre_ffn_g[l], router_w[j], tm_mix)
            dest, pad_end, tile_expert, n_valid, n_rows = _routing_tables(route_i, n_exp, tm_moe)
            xs = _dispatch(h2, dest, pad_end, n_rows, tm_rows, tm_moe)
            ys = _ffn(xs, tile_expert, n_valid, moe_w_gate[j].astype(BF16),
                      moe_w_up[j].astype(BF16), moe_w_down[j].astype(BF16), tm_moe, f_chunk)
            xf = _combine(ys, dest, route_p, x1, post_ffn_g[l], tm_rows)
    return xf.reshape(batch, seq, d).astype(x.dtype)
```

```python
import functools
import math

import jax
import jax.numpy as jnp
from jax import lax
from jax.experimental import pallas as pl
from jax.experimental.pallas import tpu as pltpu

ATTN_HEADS = 8
HEAD_DIM = 64
ATTN_WIDTH = ATTN_HEADS * HEAD_DIM
LRU_BLOCKS = 8
LRU_BLOCK_W = 64
LRU_WIDTH = LRU_BLOCKS * LRU_BLOCK_W
DILATED_PAIRS = ((128, 1), (512, 4), (2048, 16))
Q_BLOCK = 128
ROT_DIM = HEAD_DIM // 4
ROT_HALF = ROT_DIM // 2
ROPE_THETA = 500000.0
ATTN_SCALE = 1.0 / math.sqrt(HEAD_DIM)
NEG_INF = -1e30
CONV_W = 4
LRU_C = 8.0
TOP_K = 2
RMS_EPS = 1e-6

LANES = 128
SUBLANES = 8
SMEM_BLOCK_WORDS = 1024
VMEM_LIMIT_BYTES = 56 * 1024 * 1024

F32 = jnp.float32
BF16 = jnp.bfloat16


def _params(semantics):
    return pltpu.CompilerParams(dimension_semantics=semantics,
                                vmem_limit_bytes=VMEM_LIMIT_BYTES)


def _resident(block_shape, index_map):
    return pl.BlockSpec(block_shape, index_map, pipeline_mode=pl.Buffered(1))


def _rms(x, g):
    var = jnp.mean(x * x, axis=-1, keepdims=True)
    return x * lax.rsqrt(var + RMS_EPS) * g


def _tile_rows(r):
    return pl.ds(pl.multiple_of(r * SUBLANES, SUBLANES), SUBLANES)


def _store_row_tiles(ref, x):
    rows = x.shape[0]
    for c in range(x.shape[1] // LANES):
        ref[pl.ds(c, rows, stride=SUBLANES), :] = x[:, c * LANES:(c + 1) * LANES]


def _load_row_tiles(ref):
    rows = ref.shape[0] // SUBLANES
    return jnp.concatenate([ref[pl.ds(c, rows, stride=SUBLANES), :] for c in range(SUBLANES)],
                           axis=1)


def _lru_block(u, gate, cw_ref, cb_ref, wr_ref, br_ref, wi_ref, bi_ref, lam_ref,
               ubuf, hcarry, a_scr, b_scr):
    ts, c = u.shape
    ubuf[SUBLANES:SUBLANES + ts, :] = u
    ext = ubuf[...]
    uc = cb_ref[...]
    for tap in range(CONV_W):
        back = CONV_W - 1 - tap
        shifted = pltpu.roll(ext, back, 0) if back else ext
        uc = uc + shifted[SUBLANES:SUBLANES + ts, :] * cw_ref[tap:tap + 1, :]
    ubuf[0:SUBLANES, :] = ubuf[ts:ts + SUBLANES, :]

    ucb = uc.astype(BF16)
    r = jax.nn.sigmoid(jnp.dot(ucb, wr_ref[...], preferred_element_type=F32) + br_ref[...])
    ig = jax.nn.sigmoid(jnp.dot(ucb, wi_ref[...], preferred_element_type=F32) + bi_ref[...])
    nl = -lam_ref[...]
    softplus = jnp.maximum(nl, 0.0) + jnp.log1p(jnp.exp(-jnp.abs(nl)))
    log_a = -LRU_C * r * softplus
    a = jnp.exp(log_a)
    mult = jnp.sqrt(1.0 - a * a)
    b = mult * ig * uc

    groups = ts // SUBLANES
    a3 = a.reshape(groups, SUBLANES, c)
    b3 = b.reshape(groups, SUBLANES, c)
    sub = lax.broadcasted_iota(jnp.int32, (groups, SUBLANES, c), 1)
    shift = 1
    while shift < SUBLANES:
        valid = sub >= shift
        a_sh = jnp.where(valid, pltpu.roll(a3, shift, 1), 1.0)
        b_sh = jnp.where(valid, pltpu.roll(b3, shift, 1), 0.0)
        b3 = a3 * b_sh + b3
        a3 = a3 * a_sh
        shift *= 2
    a_scr[...] = a3.reshape(ts, c)
    b_scr[...] = b3.reshape(ts, c)

    h_prev = hcarry[...]
    for g in range(groups):
        rows = slice(g * SUBLANES, (g + 1) * SUBLANES)
        h = a_scr[rows, :] * h_prev + b_scr[rows, :]
        b_scr[rows, :] = h
        h_prev = jnp.broadcast_to(h[SUBLANES - 1:SUBLANES, :], (SUBLANES, c))
    hcarry[...] = h_prev
    return jax.nn.gelu(gate) * b_scr[...]


def _in_proj_kernel(x_ref, pos_ref, g_ref, w_ref, rope_ref,
                    cw_ref, cb_ref, wr_ref, br_ref, wi_ref, bi_ref, lam_ref,
                    q_ref, k_ref, v_ref, rec_ref, ubuf, hcarry, a_scr, b_scr):
    @pl.when(pl.program_id(1) == 0)
    def _():
        ubuf[0:SUBLANES, :] = jnp.zeros((SUBLANES, ubuf.shape[1]), F32)
        hcarry[...] = jnp.zeros_like(hcarry)

    ang = pos_ref[...].astype(F32) * rope_ref[0:1, :]
    cs = jnp.cos(ang)
    sn = jnp.sin(ang)
    s_lo = sn * rope_ref[1:2, :]
    s_hi = sn * rope_ref[2:3, :]

    def rotary(z):
        return (z * cs + pltpu.roll(z, ROT_HALF, 1) * s_lo
                + pltpu.roll(z, LANES - ROT_HALF, 1) * s_hi)

    h = _rms(x_ref[...], g_ref[...]).astype(BF16)
    w = ATTN_WIDTH
    u = jnp.dot(h, w_ref[:, 3 * w:3 * w + LRU_WIDTH], preferred_element_type=F32)
    gate = jnp.dot(h, w_ref[:, 3 * w + LRU_WIDTH:], preferred_element_type=F32)
    rec_ref[...] = _lru_block(u, gate, cw_ref, cb_ref, wr_ref, br_ref, wi_ref, bi_ref, lam_ref,
                              ubuf, hcarry, a_scr, b_scr)
    for sec, out in ((0, q_ref), (1, k_ref)):
        z = jnp.dot(h, w_ref[:, sec * w:(sec + 1) * w], preferred_element_type=F32)
        for cb in range(w // LANES):
            sl = slice(cb * LANES, (cb + 1) * LANES)
            out[:, sl] = rotary(z[:, sl]).astype(BF16)
    v_ref[...] = jnp.dot(h, w_ref[:, 2 * w:3 * w], preferred_element_type=F32).astype(BF16)


def _rope_table():
    lane = jnp.arange(LANES, dtype=jnp.int32) % HEAD_DIM
    inv_freq = ROPE_THETA ** (-jnp.arange(ROT_HALF, dtype=F32) / ROT_HALF)
    freq = jnp.where(lane < ROT_DIM, inv_freq[lane % ROT_HALF], 0.0)
    lo = jnp.where((lane >= ROT_HALF) & (lane < ROT_DIM), 1.0, 0.0)
    hi = jnp.where(lane < ROT_HALF, -1.0, 0.0)
    return jnp.stack([freq, lo, hi]).astype(F32)


def _block_diag(w):
    nb, bw, _ = w.shape
    eye = jnp.eye(nb, dtype=w.dtype)
    return jnp.einsum("gij,gh->gihj", w, eye).reshape(nb * bw, nb * bw)


def _in_proj(x, pos, g, w_in, conv_w, conv_b, w_r, b_r, w_i, b_i, lam, batch, seq, ts):
    t, d = x.shape
    ncol = w_in.shape[1]
    c = LRU_WIDTH
    nblk = seq // ts
    row = lambda b, j: (b * nblk + j, 0)
    const = lambda b, j: (0, 0)
    vec = lambda a: a.reshape(1, c).astype(F32)
    return pl.pallas_call(
        _in_proj_kernel,
        grid=(batch, nblk),
        in_specs=[
            pl.BlockSpec((ts, d), row),
            pl.BlockSpec((ts, 1), row),
            pl.BlockSpec((1, d), const),
            _resident((d, ncol), const),
            pl.BlockSpec((3, LANES), const),
            pl.BlockSpec((CONV_W, c), const),
            pl.BlockSpec((1, c), const),
            pl.BlockSpec((c, c), const),
            pl.BlockSpec((1, c), const),
            pl.BlockSpec((c, c), const),
            pl.BlockSpec((1, c), const),
            pl.BlockSpec((1, c), const),
        ],
        out_specs=[pl.BlockSpec((ts, ATTN_WIDTH), row)] * 3 + [pl.BlockSpec((ts, c), row)],
        out_shape=[jax.ShapeDtypeStruct((t, ATTN_WIDTH), BF16)] * 3
        + [jax.ShapeDtypeStruct((t, c), F32)],
        scratch_shapes=[pltpu.VMEM((ts + SUBLANES, c), F32), pltpu.VMEM((SUBLANES, c), F32),
                        pltpu.VMEM((ts, c), F32), pltpu.VMEM((ts, c), F32)],
        compiler_params=_params(("parallel", "arbitrary")),
        name="in_proj_lru",
    )(x, pos, g, w_in, _rope_table(), conv_w.astype(F32), vec(conv_b),
      _block_diag(w_r).astype(BF16), vec(b_r), _block_diag(w_i).astype(BF16), vec(b_i), vec(lam))


ATTN_TILE = DILATED_PAIRS[-1][1] * Q_BLOCK


def _rows(start, size, stride):
    return pl.ds(start, size) if stride == 1 else pl.ds(start, size, stride=stride)


ATTN_PIPE_WIDTH = 2


def _attn_kernel(q_ref, kc_ref, vc_ref, kp_ref, vp_ref, bias_ref, o_ref,
                 q0_scr, q1_scr, k_scr, v_scr, q0c, q1c, kc, vc,
                 ob0, ob1, ob2, ls0, ls1, ls2, obc, lsc, *slots):
    s_scrs, m_scrs = slots[:len(slots) // 2], slots[len(slots) // 2:]
    j = pl.program_id(2)
    ts = q_ref.shape[0]
    lane = lax.broadcasted_iota(jnp.int32, (1, LANES), 1)
    head0 = lane < HEAD_DIM

    qf = q_ref[...].astype(F32) * ATTN_SCALE
    q0_scr[...] = jnp.where(head0, qf, 0.0)
    q1_scr[...] = jnp.where(head0, 0.0, qf)
    k_scr[0:ts, :] = kp_ref[...].astype(F32)
    k_scr[ts:2 * ts, :] = kc_ref[...].astype(F32)
    v_scr[0:ts, :] = vp_ref[...].astype(F32)
    v_scr[ts:2 * ts, :] = vc_ref[...].astype(F32)

    nt = (((1,), (1,)), ((), ()))
    units = ts // Q_BLOCK
    width = len(s_scrs) // 2
    groups = units // width
    assert groups % 2 == 0 and groups >= 2
    pre = DILATED_PAIRS[1][1]
    assert all(dil == 1 or dil % pre == 0 for _, dil in DILATED_PAIRS)

    for c in range(pre):
        for src, dst in ((q0_scr, q0c), (q1_scr, q1c), (k_scr, kc), (v_scr, vc)):
            n = src.shape[0] // pre
            dst[c * n:(c + 1) * n, :] = src[_rows(c, n, pre), :]

    for (_, dil), ob, ls in zip(DILATED_PAIRS, (ob0, ob1, ob2), (ls0, ls1, ls2)):
        shift = dil.bit_length() - 1
        if dil == 1:
            q0_src, q1_src, k_src, v_src = q0_scr, q1_scr, k_scr, v_scr
        else:
            q0_src, q1_src, k_src, v_src = q0c, q1c, kc, vc

        def rows_of(u, dil=dil, shift=shift):
            cls = jnp.bitwise_and(u, dil - 1)
            blk = lax.shift_right_logical(u, shift)
            q_start = cls + dil * Q_BLOCK * blk
            out_rows = _rows(q_start, Q_BLOCK, dil)
            if dil == 1:
                q_rows = out_rows
                kv_rows = _rows(ts + q_start - Q_BLOCK, 2 * Q_BLOCK, 1)
            else:
                step = dil // pre
                c4 = jnp.bitwise_and(cls, pre - 1)
                sub = lax.shift_right_logical(cls, pre.bit_length() - 1)
                q_rows = _rows(c4 * (ts // pre) + sub + step * Q_BLOCK * blk, Q_BLOCK, step)
                kv_rows = _rows(c4 * (2 * ts // pre) + sub
                                + step * (ts // dil + (blk - 1) * Q_BLOCK), 2 * Q_BLOCK, step)
            return blk, out_rows, q_rows, kv_rows

        def scores(u, slot, q0_src=q0_src, q1_src=q1_src, k_src=k_src):
            blk, _, q_rows, kv_rows = rows_of(u)
            lhs = jnp.concatenate([q0_src[q_rows, :], q1_src[q_rows, :]], axis=0).astype(BF16)
            k2 = k_src[kv_rows, :].astype(BF16)
            no_prev = jnp.logical_and(j == 0, blk == 0).astype(jnp.int32)
            bias = bias_ref[pl.ds(no_prev * 2 * Q_BLOCK, 2 * Q_BLOCK), :]
            s = lax.dot_general(lhs, k2, nt, preferred_element_type=F32) + bias
            s_scrs[slot][...] = s
            m_scrs[slot][...] = jnp.broadcast_to(jnp.max(s, axis=-1, keepdims=True),
                                                 (2 * Q_BLOCK, LANES))

        via_regrouped = dil > pre
        ob_dst, ls_dst = (obc, lsc) if via_regrouped else (ob, ls)

        def finish(u, slot, ob=ob_dst, ls=ls_dst, v_src=v_src, via_regrouped=via_regrouped):
            _, out_rows, q_rows, kv_rows = rows_of(u)
            if via_regrouped:
                out_rows = q_rows
            m = m_scrs[slot][...]
            p = jnp.exp(s_scrs[slot][...] - jnp.concatenate([m, m], axis=1))
            den = jnp.sum(p, axis=-1, keepdims=True)
            v2 = v_src[kv_rows, :].astype(BF16)
            o = jnp.dot(p.astype(BF16), v2, preferred_element_type=F32) / den
            lse = m + jnp.log(den)
            ob[out_rows, :] = jnp.where(head0, o[0:Q_BLOCK], o[Q_BLOCK:])
            ls[out_rows, :] = jnp.where(head0, lse[0:Q_BLOCK], lse[Q_BLOCK:])

        def scores_group(g, half):
            for w in range(width):
                scores(g * width + w, half * width + w)

        def finish_group(g, half):
            for w in range(width):
                finish(g * width + w, half * width + w)

        scores_group(jnp.int32(0), 0)
        for g in range(groups - 1):
            scores_group(jnp.int32(g + 1), (g + 1) % 2)
            finish_group(jnp.int32(g), g % 2)
        finish_group(jnp.int32(groups - 1), (groups - 1) % 2)
        if via_regrouped:
            n = ts // pre
            for c in range(pre):
                ob[_rows(c, n, pre), :] = obc[c * n:(c + 1) * n, :]
                ls[_rows(c, n, pre), :] = lsc[c * n:(c + 1) * n, :]

    chunk = 2 * Q_BLOCK

    def mix(c, carry):
        r = pl.ds(pl.multiple_of(c * chunk, chunk), chunk)
        la, lb, lc = ls0[r, :], ls1[r, :], ls2[r, :]
        m = jnp.maximum(jnp.maximum(la, lb), lc)
        ea, eb, ec = jnp.exp(la - m), jnp.exp(lb - m), jnp.exp(lc - m)
        o_ref[r, :] = (ea * ob0[r, :] + eb * ob1[r, :] + ec * ob2[r, :]) / (ea + eb + ec)
        return carry

    lax.fori_loop(0, ts // chunk, mix, 0)


def _attn_bias():
    qi = jnp.arange(2 * Q_BLOCK, dtype=jnp.int32)[:, None] % Q_BLOCK
    kj = jnp.arange(2 * Q_BLOCK, dtype=jnp.int32)[None, :]
    delta = Q_BLOCK + qi - kj
    band = (delta >= 0) & (delta <= Q_BLOCK)
    with_prev = jnp.where(band, 0.0, NEG_INF)
    no_prev = jnp.where(band & (kj >= Q_BLOCK), 0.0, NEG_INF)
    return jnp.concatenate([with_prev, no_prev], axis=0).astype(F32)


def _attention(q, k, v, batch, seq):
    ts = ATTN_TILE
    assert seq % ts == 0
    nt = seq // ts
    cur = pl.BlockSpec((ts, LANES), lambda b, c, j: (b * nt + j, c))
    prev = pl.BlockSpec((ts, LANES), lambda b, c, j: (b * nt + jnp.maximum(j - 1, 0), c))
    tile_scr = pltpu.VMEM((ts, LANES), F32)
    pair_scr = pltpu.VMEM((2 * ts, LANES), F32)
    return pl.pallas_call(
        _attn_kernel,
        grid=(batch, ATTN_WIDTH // LANES, nt),
        in_specs=[cur, cur, cur, prev, prev,
                  pl.BlockSpec((4 * Q_BLOCK, 2 * Q_BLOCK), lambda b, c, j: (0, 0))],
        out_specs=cur,
        out_shape=jax.ShapeDtypeStruct((batch * seq, ATTN_WIDTH), F32),
        scratch_shapes=[tile_scr, tile_scr, pair_scr, pair_scr] * 2 + [tile_scr] * 8
        + [pltpu.VMEM((2 * Q_BLOCK, 2 * Q_BLOCK), F32)] * (2 * ATTN_PIPE_WIDTH)
        + [pltpu.VMEM((2 * Q_BLOCK, LANES), F32)] * (2 * ATTN_PIPE_WIDTH),
        compiler_params=_params(("parallel", "parallel", "parallel")),
        name="attention",
    )(q, k, v, k, v, _attn_bias())


def _mix(attn_ref, rec_ref, x_ref, ga_ref, gr_ref, wo_ref, gpost_ref, gpre_ref):
    na = _rms(attn_ref[...], ga_ref[...]).astype(BF16)
    nr = _rms(rec_ref[...], gr_ref[...]).astype(BF16)
    mixed = (jnp.dot(na, wo_ref[0:ATTN_WIDTH, :], preferred_element_type=F32)
             + jnp.dot(nr, wo_ref[ATTN_WIDTH:, :], preferred_element_type=F32))
    x1 = x_ref[...] + _rms(mixed, gpost_ref[...])
    return x1, _rms(x1, gpre_ref[...])


def _mix_specs(tm, d, row, const, w_out_shape):
    aw = ATTN_WIDTH
    return ([pl.BlockSpec((tm, aw), row)] * 2 + [pl.BlockSpec((tm, d), row),
            pl.BlockSpec((1, aw), const), pl.BlockSpec((1, LRU_WIDTH), const),
            _resident(w_out_shape, const), pl.BlockSpec((1, d), const),
            pl.BlockSpec((1, d), const)])


def _mix_args(attn, rec, x, ga, gr, w_out, gpost, gpre):
    d = x.shape[1]
    return [attn, rec, x, ga.reshape(1, ATTN_WIDTH), gr.reshape(1, LRU_WIDTH),
            w_out.astype(BF16), gpost.reshape(1, d), gpre.reshape(1, d)]


def _mix_router_kernel(*refs, n_exp):
    rw_ref, x1_ref, h2_ref, ri_ref, rp_ref = refs[8:]
    x1, hn = _mix(*refs[:8])
    x1_ref[...] = x1
    _store_row_tiles(h2_ref, hn)

    rw = rw_ref[...]
    h_hi, r_hi = hn.astype(BF16), rw.astype(BF16)
    h_lo = (hn - h_hi.astype(F32)).astype(BF16)
    r_lo = (rw - r_hi.astype(F32)).astype(BF16)
    logits = (jnp.dot(h_hi, r_hi, preferred_element_type=F32)
              + jnp.dot(h_hi, r_lo, preferred_element_type=F32)
              + jnp.dot(h_lo, r_hi, preferred_element_type=F32))
    tm = logits.shape[0]
    lane = lax.broadcasted_iota(jnp.int32, (tm, LANES), 1)
    logits = jnp.where(lane < n_exp, logits, -jnp.inf)
    m1 = jnp.max(logits, axis=-1, keepdims=True)
    i1 = jnp.min(jnp.where(logits == m1, lane, n_exp), axis=-1, keepdims=True)
    rest = jnp.where(lane == i1, -jnp.inf, logits)
    m2 = jnp.max(rest, axis=-1, keepdims=True)
    i2 = jnp.min(jnp.where(rest == m2, lane, n_exp), axis=-1, keepdims=True)
    e2 = jnp.exp(m2 - m1)
    p1 = 1.0 / (1.0 + e2)
    p2 = e2 / (1.0 + e2)
    ri_ref[...] = jnp.where(lane == 0, i1, jnp.where(lane == 1, i2, 0))
    rp_ref[...] = jnp.where(lane == 0, p1, jnp.where(lane == 1, p2, 0.0))


def _mix_out_router(attn, rec, x, ga, gr, w_out, gpost, gpre, router_w, tm):
    t, d = x.shape
    assert d == SUBLANES * LANES
    row = lambda i: (i, 0)
    const = lambda i: (0, 0)
    n_exp = router_w.shape[1]
    rw = jnp.pad(router_w.astype(F32), ((0, 0), (0, LANES - n_exp)))
    return pl.pallas_call(
        functools.partial(_mix_router_kernel, n_exp=n_exp),
        grid=(t // tm,),
        in_specs=_mix_specs(tm, d, row, const, w_out.shape) + [pl.BlockSpec(rw.shape, const)],
        out_specs=[pl.BlockSpec((tm, d), row), pl.BlockSpec((tm * SUBLANES, LANES), row)]
        + [pl.BlockSpec((tm, LANES), row)] * 2,
        out_shape=[jax.ShapeDtypeStruct((t, d), F32),
                   jax.ShapeDtypeStruct((t * SUBLANES, LANES), F32)]
        + [jax.ShapeDtypeStruct((t, LANES), jnp.int32), jax.ShapeDtypeStruct((t, LANES), F32)],
        compiler_params=_params(("parallel",)),
        name="mix_out_router",
    )(*_mix_args(attn, rec, x, ga, gr, w_out, gpost, gpre), rw)


def _swiglu_into(o_ref, x, wg_ref, wu_ref, wd_ref, f_chunk):
    d_ff = wg_ref.shape[1]
    for c in range(d_ff // f_chunk):
        sl = slice(c * f_chunk, (c + 1) * f_chunk)
        g = jnp.dot(x, wg_ref[:, sl], preferred_element_type=F32)
        u = jnp.dot(x, wu_ref[:, sl], preferred_element_type=F32)
        a = (g * jax.nn.sigmoid(g) * u).astype(BF16)
        y = jnp.dot(a, wd_ref[sl, :], preferred_element_type=F32)
        if c == 0:
            o_ref[...] = y
        else:
            o_ref[...] += y


def _ffn_kernel(te_ref, nv_ref, x_ref, wg_ref, wu_ref, wd_ref, o_ref, acc, *, f_chunk):
    i = pl.program_id(0)

    @pl.when(i < nv_ref[0])
    def _():
        _swiglu_into(acc, _load_row_tiles(x_ref).astype(BF16), wg_ref, wu_ref, wd_ref, f_chunk)
        _store_row_tiles(o_ref, acc[...])

    @pl.when(i >= nv_ref[0])
    def _():
        o_ref[...] = jnp.zeros_like(o_ref)


def _ffn(xs, tile_expert, n_valid, wg, wu, wd, tm, f_chunk):
    d, d_ff = wg.shape[1], wg.shape[2]
    rows = xs.shape[0] // SUBLANES
    assert d == SUBLANES * LANES and d_ff % f_chunk == 0 and rows % tm == 0
    wmap = lambda i, te, nv: (te[i], 0, 0)
    grid_spec = pltpu.PrefetchScalarGridSpec(
        num_scalar_prefetch=2,
        grid=(rows // tm,),
        in_specs=[
            pl.BlockSpec((tm * SUBLANES, LANES), lambda i, te, nv: (jnp.minimum(i, nv[0] - 1), 0)),
            _resident((None, d, d_ff), wmap),
            _resident((None, d, d_ff), wmap),
            _resident((None, d_ff, d), wmap),
        ],
        out_specs=pl.BlockSpec((tm * SUBLANES, LANES), lambda i, te, nv: (i, 0)),
        scratch_shapes=[pltpu.VMEM((tm, d), F32)],
    )
    return pl.pallas_call(
        functools.partial(_ffn_kernel, f_chunk=f_chunk),
        grid_spec=grid_spec,
        out_shape=jax.ShapeDtypeStruct(xs.shape, F32),
        compiler_params=_params(("arbitrary",)),
        name="ffn",
    )(tile_expert, n_valid, xs, wg, wu, wd)


def _mix_ffn_kernel(*refs, f_chunk):
    wg_ref, wu_ref, wd_ref, gffn_ref, o_ref, h2_scr, x1_scr = refs[8:]
    s = pl.program_id(0)

    def mix_into(cur):
        x1, hn = _mix(*refs[:8])
        x1_scr[cur] = x1
        h2_scr[cur] = hn.astype(BF16)

    def ffn_from(prev):
        _swiglu_into(o_ref, h2_scr[prev], wg_ref, wu_ref, wd_ref, f_chunk)
        o_ref[...] = x1_scr[prev] + _rms(o_ref[...], gffn_ref[...])

    @pl.when(s == 0)
    def _():
        mix_into(0)

    for parity in (0, 1):
        @pl.when(jnp.logical_and(lax.rem(s, 2) == parity, s > 0))
        def _():
            mix_into(parity)
            ffn_from(1 - parity)


def _mix_ffn(attn, rec, x, ga, gr, w_out, gpost, gpre, wg, wu, wd, gffn, tm, f_chunk):
    t, d = x.shape
    d_ff = wg.shape[1]
    n = t // tm
    assert d_ff % f_chunk == 0 and t % tm == 0
    row = lambda s: (jnp.minimum(s, n - 1), 0)
    const = lambda s: (0, 0)
    return pl.pallas_call(
        functools.partial(_mix_ffn_kernel, f_chunk=f_chunk),
        grid=(n + 1,),
        in_specs=_mix_specs(tm, d, row, const, w_out.shape)
        + [_resident((d, d_ff), const), _resident((d, d_ff), const), _resident((d_ff, d), const),
           pl.BlockSpec((1, d), const)],
        out_specs=pl.BlockSpec((tm, d), lambda s: (jnp.maximum(s - 1, 0), 0)),
        out_shape=jax.ShapeDtypeStruct((t, d), F32),
        scratch_shapes=[pltpu.VMEM((2, tm, d), BF16), pltpu.VMEM((2, tm, d), F32)],
        compiler_params=_params(("arbitrary",)),
        name="mix_ffn_dense",
    )(*_mix_args(attn, rec, x, ga, gr, w_out, gpost, gpre),
      wg.astype(BF16), wu.astype(BF16), wd.astype(BF16), gffn.reshape(1, d))


def _dispatch_kernel(pad_end_ref, d1_ref, d2_ref, h_ref, out_ref, zbuf, sem, zsem):
    tile = zbuf.shape[0]

    @pl.when(pl.program_id(0) == 0)
    def _():
        zbuf[...] = jnp.zeros_like(zbuf)

        def zero_copy(e):
            start = pl.multiple_of(pad_end_ref[e] * SUBLANES - tile, tile)
            return pltpu.make_async_copy(zbuf, out_ref.at[pl.ds(start, tile)], zsem)

        def has_rows(e):
            prev_end = pad_end_ref[e - 1] if e else 0
            return pad_end_ref[e] > prev_end

        for e in range(pad_end_ref.shape[0]):
            @pl.when(has_rows(e))
            def _():
                zero_copy(e).start()
        for e in range(pad_end_ref.shape[0]):
            @pl.when(has_rows(e))
            def _():
                zero_copy(e).wait()

    def row_copies(j):
        src = h_ref.at[_tile_rows(j)]
        return (pltpu.make_async_copy(src, out_ref.at[_tile_rows(d1_ref[j])], sem),
                pltpu.make_async_copy(src, out_ref.at[_tile_rows(d2_ref[j])], sem))

    def start(j, carry):
        for queue, c in enumerate(row_copies(j)):
            c.start(priority=queue)
        return carry

    def wait(j, carry):
        for c in row_copies(j):
            c.wait()
        return carry

    n = d1_ref.shape[0]
    lax.fori_loop(0, n, start, 0, unroll=8)
    lax.fori_loop(0, n, wait, 0, unroll=8)


def _dispatch(h, dest, pad_end, n_rows, tm, tile):
    t = h.shape[0] // SUBLANES
    assert tm % SMEM_BLOCK_WORDS == 0 and t % tm == 0
    nblk = t // tm
    grid_spec = pltpu.PrefetchScalarGridSpec(
        num_scalar_prefetch=1,
        grid=(nblk,),
        in_specs=[pl.BlockSpec((tm,), lambda i, pe: (i,), memory_space=pltpu.SMEM),
                  pl.BlockSpec((tm,), lambda i, pe: (i + nblk,), memory_space=pltpu.SMEM),
                  pl.BlockSpec((tm * SUBLANES, LANES), lambda i, pe: (i, 0))],
        out_specs=pl.BlockSpec(memory_space=pl.ANY),
        scratch_shapes=[pltpu.VMEM((tile * SUBLANES, LANES), h.dtype),
                        pltpu.SemaphoreType.DMA(()), pltpu.SemaphoreType.DMA(())],
    )
    return pl.pallas_call(
        _dispatch_kernel,
        grid_spec=grid_spec,
        out_shape=jax.ShapeDtypeStruct((n_rows * SUBLANES, LANES), h.dtype),
        compiler_params=_params(("arbitrary",)),
        name="moe_dispatch",
    )(pad_end, dest, dest, h)


def _combine_kernel(d1_ref, d2_ref, ys_ref, rp_ref, x_ref, g_ref, o_ref,
                    ya0, yb0, ya1, yb1, sems):
    s = pl.program_id(0)
    last = pl.num_programs(0) - 1
    bufs = ((ya0, yb0), (ya1, yb1))
    n = d1_ref.shape[0]

    def row_copies(j, pair, src_a, src_b):
        ya, yb = bufs[pair]
        dst = _tile_rows(j)
        return (pltpu.make_async_copy(ys_ref.at[_tile_rows(src_a)], ya.at[dst], sems.at[pair]),
                pltpu.make_async_copy(ys_ref.at[_tile_rows(src_b)], yb.at[dst], sems.at[pair]))

    def gather(pair):
        def start(j, carry):
            for queue, c in enumerate(row_copies(j, pair, d1_ref[j], d2_ref[j])):
                c.start(priority=queue)
            return carry
        lax.fori_loop(0, n, start, 0, unroll=8)

    def finish(pair):
        def wait(j, carry):
            for c in row_copies(j, pair, 0, 0):
                c.wait()
            return carry
        lax.fori_loop(0, n, wait, 0, unroll=8)
        ya, yb = bufs[pair]
        y = rp_ref[:, 0:1] * _load_row_tiles(ya) + rp_ref[:, 1:2] * _load_row_tiles(yb)
        o_ref[...] = x_ref[...] + _rms(y, g_ref[...])

    for parity in (0, 1):
        @pl.when(jnp.logical_and(lax.rem(s, 2) == parity, s < last))
        def _():
            gather(parity)

        @pl.when(jnp.logical_and(lax.rem(s, 2) == parity, s > 0))
        def _():
            finish(1 - parity)


def _combine(ys, dest, route_p, x, g, tm):
    t, d = x.shape
    assert d == SUBLANES * LANES and tm % SMEM_BLOCK_WORDS == 0 and t % tm == 0
    nblk = t // tm
    ahead = lambda off: (lambda s: (jnp.minimum(s, nblk - 1) + off,))
    behind = lambda s: (jnp.maximum(s - 1, 0), 0)
    buf = pltpu.VMEM((tm * SUBLANES, LANES), F32)
    return pl.pallas_call(
        _combine_kernel,
        grid=(nblk + 1,),
        in_specs=[pl.BlockSpec((tm,), ahead(0), memory_space=pltpu.SMEM),
                  pl.BlockSpec((tm,), ahead(nblk), memory_space=pltpu.SMEM),
                  pl.BlockSpec(memory_space=pl.ANY),
                  pl.BlockSpec((tm, LANES), behind), pl.BlockSpec((tm, d), behind),
                  pl.BlockSpec((1, d), lambda s: (0, 0))],
        out_specs=pl.BlockSpec((tm, d), behind),
        out_shape=jax.ShapeDtypeStruct((t, d), F32),
        scratch_shapes=[buf, buf, buf, buf, pltpu.SemaphoreType.DMA((2,))],
        compiler_params=_params(("arbitrary",)),
        name="moe_combine",
    )(dest, dest, ys, route_p, x, g.reshape(1, d))


def _routing_tables(route_i, n_exp, tm):
    t = route_i.shape[0]
    e_flat = jnp.concatenate([route_i[:, 0], route_i[:, 1]])
    onehot = (e_flat[:, None] == jnp.arange(n_exp, dtype=jnp.int32)[None, :]).astype(jnp.int32)
    counts = jnp.sum(onehot, axis=0)
    padded = ((counts + tm - 1) // tm) * tm
    pad_end = jnp.cumsum(padded)
    pad_start = pad_end - padded
    dest = jnp.sum((jnp.cumsum(onehot, axis=0) - onehot + pad_start[None, :]) * onehot, axis=1)

    n_rows = TOP_K * t + n_exp * tm
    n_tiles = n_rows // tm
    tile_first_row = jnp.arange(n_tiles, dtype=jnp.int32) * tm
    n_valid = pad_end[-1] // tm
    tile_first_row = jnp.minimum(tile_first_row, (n_valid - 1) * tm)
    tile_expert = jnp.sum((pad_end[None, :] <= tile_first_row[:, None]).astype(jnp.int32), axis=1)
    return (dest.astype(jnp.int32), pad_end.astype(jnp.int32),
            jnp.minimum(tile_expert, n_exp - 1).astype(jnp.int32),
            n_valid.astype(jnp.int32).reshape(1), n_rows)


def _tile(n, want):
    t = min(n, want)
    while n % t:
        t -= SUBLANES
    return t


def kernel(x, positions, pre_mix_g, w_in, conv_w, conv_b, w_rgate, b_rgate, w_igate, b_igate,
           lru_lambda, attn_out_g, lru_out_g, w_out, post_mix_g, pre_ffn_g, post_ffn_g,
           dense_w_gate, dense_w_up, dense_w_down, router_w, moe_w_gate, moe_w_up, moe_w_down):
    batch, seq, d = x.shape
    depth = w_in.shape[0]
    t = batch * seq
    d_ff = dense_w_gate.shape[-1]
    n_exp = moe_w_gate.shape[1]
    assert w_in.shape[2] == 3 * ATTN_WIDTH + 2 * LRU_WIDTH
    assert seq % (DILATED_PAIRS[-1][1] * Q_BLOCK) == 0

    tm_mix = _tile(t, 256)
    tm_ffn = _tile(t, 512)
    tm_moe = tm_ffn
    ts_proj = _tile(seq, 512)
    tm_rows = SMEM_BLOCK_WORDS
    f_chunk = 1792 if d_ff % 1792 == 0 else (512 if d_ff % 512 == 0 else d_ff)

    xf = x.reshape(t, d).astype(F32)
    pos = positions.reshape(t, 1).astype(jnp.int32)

    for l in range(depth):
        q, k, v, rec = _in_proj(xf, pos, pre_mix_g[l].reshape(1, d), w_in[l].astype(BF16),
                                conv_w[l], conv_b[l], w_rgate[l], b_rgate[l], w_igate[l],
                                b_igate[l], lru_lambda[l], batch, seq, ts_proj)
        attn = _attention(q, k, v, batch, seq)
        j = l // 2
        if l % 2 == 0:
            xf = _mix_ffn(attn, rec, xf, attn_out_g[l], lru_out_g[l], w_out[l], post_mix_g[l],
                          pre_ffn_g[l], dense_w_gate[j], dense_w_up[j], dense_w_down[j],
                          post_ffn_g[l], tm_ffn, f_chunk)
        else:
            x1, h2, route_i, route_p = _mix_out_router(
                attn, rec, xf, attn_out_g[l], lru_out_g[l], w_out[l], post_mix_g[l],
                pre_ffn_g[l], router_w[j], tm_mix)
            dest, pad_end, tile_expert, n_valid, n_rows = _routing_tables(route_i, n_exp, tm_moe)
            xs = _dispatch(h2, dest, pad_end, n_rows, tm_rows, tm_moe)
            ys = _ffn(xs, tile_expert, n_valid, moe_w_gate[j].astype(BF16),
                      moe_w_up[j].astype(BF16), moe_w_down[j].astype(BF16), tm_moe, f_chunk)
            xf = _combine(ys, dest, route_p, x1, post_ffn_g[l], tm_rows)
    return xf.reshape(batch, seq, d).astype(x.dtype)
```

```python
import functools
import math

import jax
import jax.numpy as jnp
from jax import lax
from jax.experimental import pallas as pl
from jax.experimental.pallas import tpu as pltpu

ATTN_HEADS = 8
HEAD_DIM = 64
ATTN_WIDTH = ATTN_HEADS * HEAD_DIM
LRU_BLOCKS = 8
LRU_BLOCK_W = 64
LRU_WIDTH = LRU_BLOCKS * LRU_BLOCK_W
DILATED_PAIRS = ((128, 1), (512, 4), (2048, 16))
Q_BLOCK = 128
ROT_DIM = HEAD_DIM // 4
ROT_HALF = ROT_DIM // 2
ROPE_THETA = 500000.0
ATTN_SCALE = 1.0 / math.sqrt(HEAD_DIM)
NEG_INF = -1e30
CONV_W = 4
LRU_C = 8.0
TOP_K = 2
RMS_EPS = 1e-6

LANES = 128
SUBLANES = 8
SMEM_BLOCK_WORDS = 1024
VMEM_LIMIT_BYTES = 56 * 1024 * 1024

F32 = jnp.float32
BF16 = jnp.bfloat16


def _params(semantics):
    return pltpu.CompilerParams(dimension_semantics=semantics,
                                vmem_limit_bytes=VMEM_LIMIT_BYTES)


def _resident(block_shape, index_map):
    return pl.BlockSpec(block_shape, index_map, pipeline_mode=pl.Buffered(1))


def _rms(x, g):
    var = jnp.mean(x * x, axis=-1, keepdims=True)
    return x * lax.rsqrt(var + RMS_EPS) * g


def _tile_rows(r):
    return pl.ds(pl.multiple_of(r * SUBLANES, SUBLANES), SUBLANES)


def _store_row_tiles(ref, x):
    rows = x.shape[0]
    for c in range(x.shape[1] // LANES):
        ref[pl.ds(c, rows, stride=SUBLANES), :] = x[:, c * LANES:(c + 1) * LANES]


def _load_row_tiles(ref):
    rows = ref.shape[0] // SUBLANES
    return jnp.concatenate([ref[pl.ds(c, rows, stride=SUBLANES), :] for c in range(SUBLANES)],
                           axis=1)


def _lru_block(u, gate, cw_ref, cb_ref, wr_ref, br_ref, wi_ref, bi_ref, lam_ref,
               ubuf, hcarry, a_scr, b_scr):
    ts, c = u.shape
    ubuf[SUBLANES:SUBLANES + ts, :] = u
    ext = ubuf[...]
    uc = cb_ref[...]
    for tap in range(CONV_W):
        back = CONV_W - 1 - tap
        shifted = pltpu.roll(ext, back, 0) if back else ext
        uc = uc + shifted[SUBLANES:SUBLANES + ts, :] * cw_ref[tap:tap + 1, :]
    ubuf[0:SUBLANES, :] = ubuf[ts:ts + SUBLANES, :]

    ucb = uc.astype(BF16)
    r = jax.nn.sigmoid(jnp.dot(ucb, wr_ref[...], preferred_element_type=F32) + br_ref[...])
    ig = jax.nn.sigmoid(jnp.dot(ucb, wi_ref[...], preferred_element_type=F32) + bi_ref[...])
    nl = -lam_ref[...]
    softplus = jnp.maximum(nl, 0.0) + jnp.log1p(jnp.exp(-jnp.abs(nl)))
    log_a = -LRU_C * r * softplus
    a = jnp.exp(log_a)
    mult = jnp.sqrt(1.0 - a * a)
    b = mult * ig * uc

    groups = ts // SUBLANES
    a3 = a.reshape(groups, SUBLANES, c)
    b3 = b.reshape(groups, SUBLANES, c)
    sub = lax.broadcasted_iota(jnp.int32, (groups, SUBLANES, c), 1)
    shift = 1
    while shift < SUBLANES:
        valid = sub >= shift
        a_sh = jnp.where(valid, pltpu.roll(a3, shift, 1), 1.0)
        b_sh = jnp.where(valid, pltpu.roll(b3, shift, 1), 0.0)
        b3 = a3 * b_sh + b3
        a3 = a3 * a_sh
        shift *= 2
    a_scr[...] = a3.reshape(ts, c)
    b_scr[...] = b3.reshape(ts, c)

    h_prev = hcarry[...]
    for g in range(groups):
        rows = slice(g * SUBLANES, (g + 1) * SUBLANES)
        h = a_scr[rows, :] * h_prev + b_scr[rows, :]
        b_scr[rows, :] = h
        h_prev = jnp.broadcast_to(h[SUBLANES - 1:SUBLANES, :], (SUBLANES, c))
    hcarry[...] = h_prev
    return jax.nn.gelu(gate) * b_scr[...]


def _in_proj_kernel(x_ref, pos_ref, g_ref, w_ref, rope_ref,
                    cw_ref, cb_ref, wr_ref, br_ref, wi_ref, bi_ref, lam_ref,
                    q_ref, k_ref, v_ref, rec_ref, *rest, reuse_trig):
    trig_out = None if reuse_trig else rest[0]
    ubuf, hcarry, a_scr, b_scr = rest[-4:]

    @pl.when(pl.program_id(1) == 0)
    def _():
        ubuf[0:SUBLANES, :] = jnp.zeros((SUBLANES, ubuf.shape[1]), F32)
        hcarry[...] = jnp.zeros_like(hcarry)

    if reuse_trig:
        cs = pos_ref[:, 0:LANES]
        sn = pos_ref[:, LANES:]
    else:
        ang = pos_ref[...].astype(F32) * rope_ref[0:1, :]
        cs = jnp.cos(ang)
        sn = jnp.sin(ang)
        trig_out[:, 0:LANES] = cs
        trig_out[:, LANES:] = sn
    s_lo = sn * rope_ref[1:2, :]
    s_hi = sn * rope_ref[2:3, :]

    def rotary(z):
        return (z * cs + pltpu.roll(z, ROT_HALF, 1) * s_lo
                + pltpu.roll(z, LANES - ROT_HALF, 1) * s_hi)

    h = _rms(x_ref[...], g_ref[...]).astype(BF16)
    w = ATTN_WIDTH
    u = jnp.dot(h, w_ref[:, 3 * w:3 * w + LRU_WIDTH], preferred_element_type=F32)
    gate = jnp.dot(h, w_ref[:, 3 * w + LRU_WIDTH:], preferred_element_type=F32)
    rec_ref[...] = _lru_block(u, gate, cw_ref, cb_ref, wr_ref, br_ref, wi_ref, bi_ref, lam_ref,
                              ubuf, hcarry, a_scr, b_scr)
    for sec, out in ((0, q_ref), (1, k_ref)):
        z = jnp.dot(h, w_ref[:, sec * w:(sec + 1) * w], preferred_element_type=F32)
        for cb in range(w // LANES):
            sl = slice(cb * LANES, (cb + 1) * LANES)
            out[:, sl] = rotary(z[:, sl]).astype(BF16)
    v_ref[...] = jnp.dot(h, w_ref[:, 2 * w:3 * w], preferred_element_type=F32).astype(BF16)


def _rope_table():
    lane = jnp.arange(LANES, dtype=jnp.int32) % HEAD_DIM
    inv_freq = ROPE_THETA ** (-jnp.arange(ROT_HALF, dtype=F32) / ROT_HALF)
    freq = jnp.where(lane < ROT_DIM, inv_freq[lane % ROT_HALF], 0.0)
    lo = jnp.where((lane >= ROT_HALF) & (lane < ROT_DIM), 1.0, 0.0)
    hi = jnp.where(lane < ROT_HALF, -1.0, 0.0)
    return jnp.stack([freq, lo, hi]).astype(F32)


def _block_diag(w):
    nb, bw, _ = w.shape
    eye = jnp.eye(nb, dtype=w.dtype)
    return jnp.einsum("gij,gh->gihj", w, eye).reshape(nb * bw, nb * bw)


def _in_proj(x, pos, g, w_in, conv_w, conv_b, w_r, b_r, w_i, b_i, lam, batch, seq, ts):
    reuse_trig = pos.shape[1] == 2 * LANES
    t, d = x.shape
    ncol = w_in.shape[1]
    c = LRU_WIDTH
    nblk = seq // ts
    row = lambda b, j: (b * nblk + j, 0)
    const = lambda b, j: (0, 0)
    vec = lambda a: a.reshape(1, c).astype(F32)
    trig_spec = pl.BlockSpec((ts, 2 * LANES), row)
    trig_out = [] if reuse_trig else [trig_spec]
    outs = pl.pallas_call(
        functools.partial(_in_proj_kernel, reuse_trig=reuse_trig),
        grid=(batch, nblk),
        in_specs=[
            pl.BlockSpec((ts, d), row),
            trig_spec if reuse_trig else pl.BlockSpec((ts, 1), row),
            pl.BlockSpec((1, d), const),
            _resident((d, ncol), const),
            pl.BlockSpec((3, LANES), const),
            pl.BlockSpec((CONV_W, c), const),
            pl.BlockSpec((1, c), const),
            pl.BlockSpec((c, c), const),
            pl.BlockSpec((1, c), const),
            pl.BlockSpec((c, c), const),
            pl.BlockSpec((1, c), const),
            pl.BlockSpec((1, c), const),
        ],
        out_specs=[pl.BlockSpec((ts, ATTN_WIDTH), row)] * 3 + [pl.BlockSpec((ts, c), row)]
        + trig_out,
        out_shape=[jax.ShapeDtypeStruct((t, ATTN_WIDTH), BF16)] * 3
        + [jax.ShapeDtypeStruct((t, c), F32)]
        + [jax.ShapeDtypeStruct((t, 2 * LANES), F32)] * len(trig_out),
        scratch_shapes=[pltpu.VMEM((ts + SUBLANES, c), F32), pltpu.VMEM((SUBLANES, c), F32),
                        pltpu.VMEM((ts, c), F32), pltpu.VMEM((ts, c), F32)],
        compiler_params=_params(("parallel", "arbitrary")),
        name="in_proj_lru",
    )(x, pos, g, w_in, _rope_table(), conv_w.astype(F32), vec(conv_b),
      _block_diag(w_r).astype(BF16), vec(b_r), _block_diag(w_i).astype(BF16), vec(b_i), vec(lam))
    return tuple(outs) if not reuse_trig else (*outs, pos)


ATTN_TILE = DILATED_PAIRS[-1][1] * Q_BLOCK


def _rows(start, size, stride):
    return pl.ds(start, size) if stride == 1 else pl.ds(start, size, stride=stride)


ATTN_PIPE_WIDTH = 2


def _attn_kernel(q_ref, kc_ref, vc_ref, kp_ref, vp_ref, bias_ref, o_ref,
                 q0_scr, q1_scr, k_scr, v_scr, q0c, q1c, kc, vc,
                 ob0, ob1, ob2, ls0, ls1, ls2, obc, lsc, *slots):
    s_scrs, m_scrs = slots[:len(slots) // 2], slots[len(slots) // 2:]
    j = pl.program_id(2)
    ts = q_ref.shape[0]
    lane = lax.broadcasted_iota(jnp.int32, (1, LANES), 1)
    head0 = lane < HEAD_DIM

    qf = q_ref[...].astype(F32) * ATTN_SCALE
    q0_scr[...] = jnp.where(head0, qf, 0.0)
    q1_scr[...] = jnp.where(head0, 0.0, qf)
    k_scr[0:ts, :] = kp_ref[...].astype(F32)
    k_scr[ts:2 * ts, :] = kc_ref[...].astype(F32)
    v_scr[0:ts, :] = vp_ref[...].astype(F32)
    v_scr[ts:2 * ts, :] = vc_ref[...].astype(F32)

    nt = (((1,), (1,)), ((), ()))
    units = ts // Q_BLOCK
    width = len(s_scrs) // 2
    groups = units // width
    assert groups % 2 == 0 and groups >= 2
    pre = DILATED_PAIRS[1][1]
    assert all(dil == 1 or dil % pre == 0 for _, dil in DILATED_PAIRS)

    for c in range(pre):
        for src, dst in ((q0_scr, q0c), (q1_scr, q1c), (k_scr, kc), (v_scr, vc)):
            n = src.shape[0] // pre
            dst[c * n:(c + 1) * n, :] = src[_rows(c, n, pre), :]

    for (_, dil), ob, ls in zip(DILATED_PAIRS, (ob0, ob1, ob2), (ls0, ls1, ls2)):
        shift = dil.bit_length() - 1
        if dil == 1:
            q0_src, q1_src, k_src, v_src = q0_scr, q1_scr, k_scr, v_scr
        else:
            q0_src, q1_src, k_src, v_src = q0c, q1c, kc, vc

        def rows_of(u, dil=dil, shift=shift):
            cls = jnp.bitwise_and(u, dil - 1)
            blk = lax.shift_right_logical(u, shift)
            q_start = cls + dil * Q_BLOCK * blk
            out_rows = _rows(q_start, Q_BLOCK, dil)
            if dil == 1:
                q_rows = out_rows
                kv_rows = _rows(ts + q_start - Q_BLOCK, 2 * Q_BLOCK, 1)
            else:
                step = dil // pre
                c4 = jnp.bitwise_and(cls, pre - 1)
                sub = lax.shift_right_logical(cls, pre.bit_length() - 1)
                q_rows = _rows(c4 * (ts // pre) + sub + step * Q_BLOCK * blk, Q_BLOCK, step)
                kv_rows = _rows(c4 * (2 * ts // pre) + sub
                                + step * (ts // dil + (blk - 1) * Q_BLOCK), 2 * Q_BLOCK, step)
            return blk, out_rows, q_rows, kv_rows

        def scores(u, slot, q0_src=q0_src, q1_src=q1_src, k_src=k_src):
            blk, _, q_rows, kv_rows = rows_of(u)
            lhs = jnp.concatenate([q0_src[q_rows, :], q1_src[q_rows, :]], axis=0).astype(BF16)
            k2 = k_src[kv_rows, :].astype(BF16)
            no_prev = jnp.logical_and(j == 0, blk == 0).astype(jnp.int32)
            bias = bias_ref[pl.ds(no_prev * 2 * Q_BLOCK, 2 * Q_BLOCK), :]
            s = lax.dot_general(lhs, k2, nt, preferred_element_type=F32) + bias
            s_scrs[slot][...] = s
            m_scrs[slot][...] = jnp.broadcast_to(jnp.max(s, axis=-1, keepdims=True),
                                                 (2 * Q_BLOCK, LANES))

        via_regrouped = dil > pre
        ob_dst, ls_dst = (obc, lsc) if via_regrouped else (ob, ls)

        def finish(u, slot, ob=ob_dst, ls=ls_dst, v_src=v_src, via_regrouped=via_regrouped):
            _, out_rows, q_rows, kv_rows = rows_of(u)
            if via_regrouped:
                out_rows = q_rows
            m = m_scrs[slot][...]
            p = jnp.exp(s_scrs[slot][...] - jnp.concatenate([m, m], axis=1))
            den = jnp.sum(p, axis=-1, keepdims=True)
            v2 = v_src[kv_rows, :].astype(BF16)
            o = jnp.dot(p.astype(BF16), v2, preferred_element_type=F32) / den
            lse = m + jnp.log(den)
            ob[out_rows, :] = jnp.where(head0, o[0:Q_BLOCK], o[Q_BLOCK:])
            ls[out_rows, :] = jnp.where(head0, lse[0:Q_BLOCK], lse[Q_BLOCK:])

        def scores_group(g, half):
            for w in range(width):
                scores(g * width + w, half * width + w)

        def finish_group(g, half):
            for w in range(width):
                finish(g * width + w, half * width + w)

        scores_group(jnp.int32(0), 0)
        for g in range(groups - 1):
            scores_group(jnp.int32(g + 1), (g + 1) % 2)
            finish_group(jnp.int32(g), g % 2)
        finish_group(jnp.int32(groups - 1), (groups - 1) % 2)
        if via_regrouped:
            n = ts // pre
            for c in range(pre):
                ob[_rows(c, n, pre), :] = obc[c * n:(c + 1) * n, :]
                ls[_rows(c, n, pre), :] = lsc[c * n:(c + 1) * n, :]

    chunk = 2 * Q_BLOCK

    def mix(c, carry):
        r = pl.ds(pl.multiple_of(c * chunk, chunk), chunk)
        la, lb, lc = ls0[r, :], ls1[r, :], ls2[r, :]
        m = jnp.maximum(jnp.maximum(la, lb), lc)
        ea, eb, ec = jnp.exp(la - m), jnp.exp(lb - m), jnp.exp(lc - m)
        o_ref[r, :] = (ea * ob0[r, :] + eb * ob1[r, :] + ec * ob2[r, :]) / (ea + eb + ec)
        return carry

    lax.fori_loop(0, ts // chunk, mix, 0)


def _attn_bias():
    qi = jnp.arange(2 * Q_BLOCK, dtype=jnp.int32)[:, None] % Q_BLOCK
    kj = jnp.arange(2 * Q_BLOCK, dtype=jnp.int32)[None, :]
    delta = Q_BLOCK + qi - kj
    band = (delta >= 0) & (delta <= Q_BLOCK)
    with_prev = jnp.where(band, 0.0, NEG_INF)
    no_prev = jnp.where(band & (kj >= Q_BLOCK), 0.0, NEG_INF)
    return jnp.concatenate([with_prev, no_prev], axis=0).astype(F32)


def _attention(q, k, v, batch, seq):
    ts = ATTN_TILE
    assert seq % ts == 0
    nt = seq // ts
    cur = pl.BlockSpec((ts, LANES), lambda b, c, j: (b * nt + j, c))
    prev = pl.BlockSpec((ts, LANES), lambda b, c, j: (b * nt + jnp.maximum(j - 1, 0), c))
    tile_scr = pltpu.VMEM((ts, LANES), F32)
    pair_scr = pltpu.VMEM((2 * ts, LANES), F32)
    return pl.pallas_call(
        _attn_kernel,
        grid=(batch, ATTN_WIDTH // LANES, nt),
        in_specs=[cur, cur, cur, prev, prev,
                  pl.BlockSpec((4 * Q_BLOCK, 2 * Q_BLOCK), lambda b, c, j: (0, 0))],
        out_specs=cur,
        out_shape=jax.ShapeDtypeStruct((batch * seq, ATTN_WIDTH), F32),
        scratch_shapes=[tile_scr, tile_scr, pair_scr, pair_scr] * 2 + [tile_scr] * 8
        + [pltpu.VMEM((2 * Q_BLOCK, 2 * Q_BLOCK), F32)] * (2 * ATTN_PIPE_WIDTH)
        + [pltpu.VMEM((2 * Q_BLOCK, LANES), F32)] * (2 * ATTN_PIPE_WIDTH),
        compiler_params=_params(("parallel", "parallel", "parallel")),
        name="attention",
    )(q, k, v, k, v, _attn_bias())


def _mix(attn_ref, rec_ref, x_ref, ga_ref, gr_ref, wo_ref, gpost_ref, gpre_ref):
    na = _rms(attn_ref[...], ga_ref[...]).astype(BF16)
    nr = _rms(rec_ref[...], gr_ref[...]).astype(BF16)
    mixed = (jnp.dot(na, wo_ref[0:ATTN_WIDTH, :], preferred_element_type=F32)
             + jnp.dot(nr, wo_ref[ATTN_WIDTH:, :], preferred_element_type=F32))
    x1 = x_ref[...] + _rms(mixed, gpost_ref[...])
    return x1, _rms(x1, gpre_ref[...])


def _mix_specs(tm, d, row, const, w_out_shape):
    aw = ATTN_WIDTH
    return ([pl.BlockSpec((tm, aw), row)] * 2 + [pl.BlockSpec((tm, d), row),
            pl.BlockSpec((1, aw), const), pl.BlockSpec((1, LRU_WIDTH), const),
            _resident(w_out_shape, const), pl.BlockSpec((1, d), const),
            pl.BlockSpec((1, d), const)])


def _mix_args(attn, rec, x, ga, gr, w_out, gpost, gpre):
    d = x.shape[1]
    return [attn, rec, x, ga.reshape(1, ATTN_WIDTH), gr.reshape(1, LRU_WIDTH),
            w_out.astype(BF16), gpost.reshape(1, d), gpre.reshape(1, d)]


def _mix_router_kernel(*refs, n_exp):
    rw_ref, x1_ref, h2_ref, ri_ref, rp_ref = refs[8:]
    x1, hn = _mix(*refs[:8])
    x1_ref[...] = x1
    _store_row_tiles(h2_ref, hn)

    rw = rw_ref[...]
    h_hi, r_hi = hn.astype(BF16), rw.astype(BF16)
    h_lo = (hn - h_hi.astype(F32)).astype(BF16)
    r_lo = (rw - r_hi.astype(F32)).astype(BF16)
    logits = (jnp.dot(h_hi, r_hi, preferred_element_type=F32)
              + jnp.dot(h_hi, r_lo, preferred_element_type=F32)
              + jnp.dot(h_lo, r_hi, preferred_element_type=F32))
    tm = logits.shape[0]
    lane = lax.broadcasted_iota(jnp.int32, (tm, LANES), 1)
    logits = jnp.where(lane < n_exp, logits, -jnp.inf)
    m1 = jnp.max(logits, axis=-1, keepdims=True)
    i1 = jnp.min(jnp.where(logits == m1, lane, n_exp), axis=-1, keepdims=True)
    rest = jnp.where(lane == i1, -jnp.inf, logits)
    m2 = jnp.max(rest, axis=-1, keepdims=True)
    i2 = jnp.min(jnp.where(rest == m2, lane, n_exp), axis=-1, keepdims=True)
    e2 = jnp.exp(m2 - m1)
    p1 = 1.0 / (1.0 + e2)
    p2 = e2 / (1.0 + e2)
    ri_ref[...] = jnp.where(lane == 0, i1, jnp.where(lane == 1, i2, 0))
    rp_ref[...] = jnp.where(lane == 0, p1, jnp.where(lane == 1, p2, 0.0))


def _mix_out_router(attn, rec, x, ga, gr, w_out, gpost, gpre, router_w, tm):
    t, d = x.shape
    assert d == SUBLANES * LANES
    row = lambda i: (i, 0)
    const = lambda i: (0, 0)
    n_exp = router_w.shape[1]
    rw = jnp.pad(router_w.astype(F32), ((0, 0), (0, LANES - n_exp)))
    return pl.pallas_call(
        functools.partial(_mix_router_kernel, n_exp=n_exp),
        grid=(t // tm,),
        in_specs=_mix_specs(tm, d, row, const, w_out.shape) + [pl.BlockSpec(rw.shape, const)],
        out_specs=[pl.BlockSpec((tm, d), row), pl.BlockSpec((tm * SUBLANES, LANES), row)]
        + [pl.BlockSpec((tm, LANES), row)] * 2,
        out_shape=[jax.ShapeDtypeStruct((t, d), F32),
                   jax.ShapeDtypeStruct((t * SUBLANES, LANES), F32)]
        + [jax.ShapeDtypeStruct((t, LANES), jnp.int32), jax.ShapeDtypeStruct((t, LANES), F32)],
        compiler_params=_params(("parallel",)),
        name="mix_out_router",
    )(*_mix_args(attn, rec, x, ga, gr, w_out, gpost, gpre), rw)


def _swiglu_into(o_ref, x, wg_ref, wu_ref, wd_ref, f_chunk):
    d_ff = wg_ref.shape[1]
    for c in range(d_ff // f_chunk):
        sl = slice(c * f_chunk, (c + 1) * f_chunk)
        g = jnp.dot(x, wg_ref[:, sl], preferred_element_type=F32)
        u = jnp.dot(x, wu_ref[:, sl], preferred_element_type=F32)
        a = (g * jax.nn.sigmoid(g) * u).astype(BF16)
        y = jnp.dot(a, wd_ref[sl, :], preferred_element_type=F32)
        if c == 0:
            o_ref[...] = y
        else:
            o_ref[...] += y


def _ffn_kernel(te_ref, nv_ref, x_ref, wg_ref, wu_ref, wd_ref, o_ref, acc, *, f_chunk):
    i = pl.program_id(0)

    @pl.when(i < nv_ref[0])
    def _():
        _swiglu_into(acc, _load_row_tiles(x_ref).astype(BF16), wg_ref, wu_ref, wd_ref, f_chunk)
        _store_row_tiles(o_ref, acc[...])

    @pl.when(i >= nv_ref[0])
    def _():
        o_ref[...] = jnp.zeros_like(o_ref)


def _ffn(xs, tile_expert, n_valid, wg, wu, wd, tm, f_chunk):
    d, d_ff = wg.shape[1], wg.shape[2]
    rows = xs.shape[0] // SUBLANES
    assert d == SUBLANES * LANES and d_ff % f_chunk == 0 and rows % tm == 0
    wmap = lambda i, te, nv: (te[i], 0, 0)
    grid_spec = pltpu.PrefetchScalarGridSpec(
        num_scalar_prefetch=2,
        grid=(rows // tm,),
        in_specs=[
            pl.BlockSpec((tm * SUBLANES, LANES), lambda i, te, nv: (jnp.minimum(i, nv[0] - 1), 0)),
            _resident((None, d, d_ff), wmap),
            _resident((None, d, d_ff), wmap),
            _resident((None, d_ff, d), wmap),
        ],
        out_specs=pl.BlockSpec((tm * SUBLANES, LANES), lambda i, te, nv: (i, 0)),
        scratch_shapes=[pltpu.VMEM((tm, d), F32)],
    )
    return pl.pallas_call(
        functools.partial(_ffn_kernel, f_chunk=f_chunk),
        grid_spec=grid_spec,
        out_shape=jax.ShapeDtypeStruct(xs.shape, F32),
        compiler_params=_params(("arbitrary",)),
        name="ffn",
    )(tile_expert, n_valid, xs, wg, wu, wd)


def _mix_ffn_kernel(*refs, f_chunk):
    wg_ref, wu_ref, wd_ref, gffn_ref, o_ref, h2_scr, x1_scr = refs[8:]
    s = pl.program_id(0)

    def mix_into(cur):
        x1, hn = _mix(*refs[:8])
        x1_scr[cur] = x1
        h2_scr[cur] = hn.astype(BF16)

    def ffn_from(prev):
        _swiglu_into(o_ref, h2_scr[prev], wg_ref, wu_ref, wd_ref, f_chunk)
        o_ref[...] = x1_scr[prev] + _rms(o_ref[...], gffn_ref[...])

    @pl.when(s == 0)
    def _():
        mix_into(0)

    for parity in (0, 1):
        @pl.when(jnp.logical_and(lax.rem(s, 2) == parity, s > 0))
        def _():
            mix_into(parity)
            ffn_from(1 - parity)


def _mix_ffn(attn, rec, x, ga, gr, w_out, gpost, gpre, wg, wu, wd, gffn, tm, f_chunk):
    t, d = x.shape
    d_ff = wg.shape[1]
    n = t // tm
    assert d_ff % f_chunk == 0 and t % tm == 0
    row = lambda s: (jnp.minimum(s, n - 1), 0)
    const = lambda s: (0, 0)
    return pl.pallas_call(
        functools.partial(_mix_ffn_kernel, f_chunk=f_chunk),
        grid=(n + 1,),
        in_specs=_mix_specs(tm, d, row, const, w_out.shape)
        + [_resident((d, d_ff), const), _resident((d, d_ff), const), _resident((d_ff, d), const),
           pl.BlockSpec((1, d), const)],
        out_specs=pl.BlockSpec((tm, d), lambda s: (jnp.maximum(s - 1, 0), 0)),
        out_shape=jax.ShapeDtypeStruct((t, d), F32),
        scratch_shapes=[pltpu.VMEM((2, tm, d), BF16), pltpu.VMEM((2, tm, d), F32)],
        compiler_params=_params(("arbitrary",)),
        name="mix_ffn_dense",
    )(*_mix_args(attn, rec, x, ga, gr, w_out, gpost, gpre),
      wg.astype(BF16), wu.astype(BF16), wd.astype(BF16), gffn.reshape(1, d))


def _dispatch_kernel(pad_end_ref, d1_ref, d2_ref, h_ref, out_ref, zbuf, sem, zsem):
    tile = zbuf.shape[0]

    @pl.when(pl.program_id(0) == 0)
    def _():
        zbuf[...] = jnp.zeros_like(zbuf)

        def zero_copy(e):
            start = pl.multiple_of(pad_end_ref[e] * SUBLANES - tile, tile)
            return pltpu.make_async_copy(zbuf, out_ref.at[pl.ds(start, tile)], zsem)

        def has_rows(e):
            prev_end = pad_end_ref[e - 1] if e else 0
            return pad_end_ref[e] > prev_end

        for e in range(pad_end_ref.shape[0]):
            @pl.when(has_rows(e))
            def _():
                zero_copy(e).start()
        for e in range(pad_end_ref.shape[0]):
            @pl.when(has_rows(e))
            def _():
                zero_copy(e).wait()

    def row_copies(j):
        src = h_ref.at[_tile_rows(j)]
        return (pltpu.make_async_copy(src, out_ref.at[_tile_rows(d1_ref[j])], sem),
                pltpu.make_async_copy(src, out_ref.at[_tile_rows(d2_ref[j])], sem))

    def start(j, carry):
        for queue, c in enumerate(row_copies(j)):
            c.start(priority=queue)
        return carry

    def wait(j, carry):
        for c in row_copies(j):
            c.wait()
        return carry

    n = d1_ref.shape[0]
    lax.fori_loop(0, n, start, 0, unroll=8)
    lax.fori_loop(0, n, wait, 0, unroll=8)


def _dispatch(h, dest, pad_end, n_rows, tm, tile):
    t = h.shape[0] // SUBLANES
    assert tm % SMEM_BLOCK_WORDS == 0 and t % tm == 0
    nblk = t // tm
    grid_spec = pltpu.PrefetchScalarGridSpec(
        num_scalar_prefetch=1,
        grid=(nblk,),
        in_specs=[pl.BlockSpec((tm,), lambda i, pe: (i,), memory_space=pltpu.SMEM),
                  pl.BlockSpec((tm,), lambda i, pe: (i + nblk,), memory_space=pltpu.SMEM),
                  pl.BlockSpec((tm * SUBLANES, LANES), lambda i, pe: (i, 0))],
        out_specs=pl.BlockSpec(memory_space=pl.ANY),
        scratch_shapes=[pltpu.VMEM((tile * SUBLANES, LANES), h.dtype),
                        pltpu.SemaphoreType.DMA(()), pltpu.SemaphoreType.DMA(())],
    )
    return pl.pallas_call(
        _dispatch_kernel,
        grid_spec=grid_spec,
        out_shape=jax.ShapeDtypeStruct((n_rows * SUBLANES, LANES), h.dtype),
        compiler_params=_params(("arbitrary",)),
        name="moe_dispatch",
    )(pad_end, dest, dest, h)


def _combine_kernel(d1_ref, d2_ref, ys_ref, rp_ref, x_ref, g_ref, o_ref,
                    ya0, yb0, ya1, yb1, sems):
    s = pl.program_id(0)
    last = pl.num_programs(0) - 1
    bufs = ((ya0, yb0), (ya1, yb1))
    n = d1_ref.shape[0]

    def row_copies(j, pair, src_a, src_b):
        ya, yb = bufs[pair]
        dst = _tile_rows(j)
        return (pltpu.make_async_copy(ys_ref.at[_tile_rows(src_a)], ya.at[dst], sems.at[pair]),
                pltpu.make_async_copy(ys_ref.at[_tile_rows(src_b)], yb.at[dst], sems.at[pair]))

    def gather(pair):
        def start(j, carry):
            for queue, c in enumerate(row_copies(j, pair, d1_ref[j], d2_ref[j])):
                c.start(priority=queue)
            return carry
        lax.fori_loop(0, n, start, 0, unroll=8)

    def finish(pair):
        def wait(j, carry):
            for c in row_copies(j, pair, 0, 0):
                c.wait()
            return carry
        lax.fori_loop(0, n, wait, 0, unroll=8)
        ya, yb = bufs[pair]
        y = rp_ref[:, 0:1] * _load_row_tiles(ya) + rp_ref[:, 1:2] * _load_row_tiles(yb)
        o_ref[...] = x_ref[...] + _rms(y, g_ref[...])

    for parity in (0, 1):
        @pl.when(jnp.logical_and(lax.rem(s, 2) == parity, s < last))
        def _():
            gather(parity)

        @pl.when(jnp.logical_and(lax.rem(s, 2) == parity, s > 0))
        def _():
            finish(1 - parity)


def _combine(ys, dest, route_p, x, g, tm):
    t, d = x.shape
    assert d == SUBLANES * LANES and tm % SMEM_BLOCK_WORDS == 0 and t % tm == 0
    nblk = t // tm
    ahead = lambda off: (lambda s: (jnp.minimum(s, nblk - 1) + off,))
    behind = lambda s: (jnp.maximum(s - 1, 0), 0)
    buf = pltpu.VMEM((tm * SUBLANES, LANES), F32)
    return pl.pallas_call(
        _combine_kernel,
        grid=(nblk + 1,),
        in_specs=[pl.BlockSpec((tm,), ahead(0), memory_space=pltpu.SMEM),
                  pl.BlockSpec((tm,), ahead(nblk), memory_space=pltpu.SMEM),
                  pl.BlockSpec(memory_space=pl.ANY),
                  pl.BlockSpec((tm, LANES), behind), pl.BlockSpec((tm, d), behind),
                  pl.BlockSpec((1, d), lambda s: (0, 0))],
        out_specs=pl.BlockSpec((tm, d), behind),
        out_shape=jax.ShapeDtypeStruct((t, d), F32),
        scratch_shapes=[buf, buf, buf, buf, pltpu.SemaphoreType.DMA((2,))],
        compiler_params=_params(("arbitrary",)),
        name="moe_combine",
    )(dest, dest, ys, route_p, x, g.reshape(1, d))


def _routing_tables(route_i, n_exp, tm):
    t = route_i.shape[0]
    e_flat = jnp.concatenate([route_i[:, 0], route_i[:, 1]])
    onehot = (e_flat[:, None] == jnp.arange(n_exp, dtype=jnp.int32)[None, :]).astype(jnp.int32)
    counts = jnp.sum(onehot, axis=0)
    padded = ((counts + tm - 1) // tm) * tm
    pad_end = jnp.cumsum(padded)
    pad_start = pad_end - padded
    dest = jnp.sum((jnp.cumsum(onehot, axis=0) - onehot + pad_start[None, :]) * onehot, axis=1)

    n_rows = TOP_K * t + n_exp * tm
    n_tiles = n_rows // tm
    tile_first_row = jnp.arange(n_tiles, dtype=jnp.int32) * tm
    n_valid = pad_end[-1] // tm
    tile_first_row = jnp.minimum(tile_first_row, (n_valid - 1) * tm)
    tile_expert = jnp.sum((pad_end[None, :] <= tile_first_row[:, None]).astype(jnp.int32), axis=1)
    return (dest.astype(jnp.int32), pad_end.astype(jnp.int32),
            jnp.minimum(tile_expert, n_exp - 1).astype(jnp.int32),
            n_valid.astype(jnp.int32).reshape(1), n_rows)


def _tile(n, want):
    t = min(n, want)
    while n % t:
        t -= SUBLANES
    return t


def kernel(x, positions, pre_mix_g, w_in, conv_w, conv_b, w_rgate, b_rgate, w_igate, b_igate,
           lru_lambda, attn_out_g, lru_out_g, w_out, post_mix_g, pre_ffn_g, post_ffn_g,
           dense_w_gate, dense_w_up, dense_w_down, router_w, moe_w_gate, moe_w_up, moe_w_down):
    batch, seq, d = x.shape
    depth = w_in.shape[0]
    t = batch * seq
    d_ff = dense_w_gate.shape[-1]
    n_exp = moe_w_gate.shape[1]
    assert w_in.shape[2] == 3 * ATTN_WIDTH + 2 * LRU_WIDTH
    assert seq % (DILATED_PAIRS[-1][1] * Q_BLOCK) == 0

    tm_mix = _tile(t, 256)
    tm_ffn = _tile(t, 512)
    tm_moe = tm_ffn
    ts_proj = _tile(seq, 512)
    tm_rows = SMEM_BLOCK_WORDS
    f_chunk = 1792 if d_ff % 1792 == 0 else (512 if d_ff % 512 == 0 else d_ff)

    xf = x.reshape(t, d).astype(F32)
    pos = positions.reshape(t, 1).astype(jnp.int32)

    for l in range(depth):
        q, k, v, rec, pos = _in_proj(xf, pos, pre_mix_g[l].reshape(1, d), w_in[l].astype(BF16),
                                     conv_w[l], conv_b[l], w_rgate[l], b_rgate[l], w_igate[l],
                                     b_igate[l], lru_lambda[l], batch, seq, ts_proj)
        attn = _attention(q, k, v, batch, seq)
        j = l // 2
        if l % 2 == 0:
            xf = _mix_ffn(attn, rec, xf, attn_out_g[l], lru_out_g[l], w_out[l], post_mix_g[l],
                          pre_ffn_g[l], dense_w_gate[j], dense_w_up[j], dense_w_down[j],
                          post_ffn_g[l], tm_ffn, f_chunk)
        else:
            x1, h2, route_i, route_p = _mix_out_router(
                attn, rec, xf, attn_out_g[l], lru_out_g[l], w_out[l], post_mix_g[l],
                pre_ffn_g[l], router_w[j], tm_mix)
            dest, pad_end, tile_expert, n_valid, n_rows = _routing_tables(route_i, n_exp, tm_moe)
            xs = _dispatch(h2, dest, pad_end, n_rows, tm_rows, tm_moe)
            ys = _ffn(xs, tile_expert, n_valid, moe_w_gate[j].astype(BF16),
                      moe_w_up[j].astype(BF16), moe_w_down[j].astype(BF16), tm_moe, f_chunk)
            xf = _combine(ys, dest, route_p, x1, post_ffn_g[l], tm_rows)
    return xf.reshape(batch, seq, d).astype(x.dtype)
```
